```python
import math
import jax, jax.numpy as jnp
from jax import lax
import numpy as np

D_MODEL = 1024
BATCH = 8
SEQ = 4096
DEPTH = 2

HEAD_DIM = 64
NSA_HEADS = 8
NSA_GROUPS = 2
NSA_HPG = NSA_HEADS // NSA_GROUPS
CMP_BLOCK = 32
CMP_STRIDE = 16
CMP_HIDDEN = 256
SEL_BLOCK = 64
N_SEL = 8
WINDOW = 512
FORCE_BONUS = 1e4
DSA_HEADS = 8
IDX_HEADS = 4
IDX_DIM = 64
DSA_TOPK_MAX = 256
Q_BLOCK = 128
N_BUCKETS = 32
T5_MAX_DIST = 128
N_ATTN_HEADS = NSA_HEADS + DSA_HEADS
LRU_WIDTH = 512
LRU_BLOCKS = 8
LRU_BLOCK_DIM = LRU_WIDTH // LRU_BLOCKS
CONV_WIDTH = 4
LRU_C = 8.0
MLSTM_HEADS = 4
MLSTM_DIM = 128
MLSTM_WIDTH = MLSTM_HEADS * MLSTM_DIM
MLSTM_CHUNK = 64
PLE_DIM = 256
RMS_EPS = 1e-6
NEG = -1e30
N_EVEN = (DEPTH + 1) // 2
N_ODD = DEPTH // 2
NSA_WIDTH = NSA_HEADS * HEAD_DIM
DSA_WIDTH = DSA_HEADS * HEAD_DIM
NSA_KV = NSA_GROUPS * HEAD_DIM
ATTN_SPLITS = (NSA_WIDTH, 6 * NSA_KV, 3 * NSA_HEADS, NSA_WIDTH,
               DSA_WIDTH, HEAD_DIM, HEAD_DIM, IDX_HEADS * IDX_DIM, IDX_DIM, IDX_HEADS, DSA_WIDTH)
ATTN_IN = sum(ATTN_SPLITS)
ATTN_OUT = NSA_WIDTH + DSA_WIDTH
REC_SPLITS = (LRU_WIDTH, LRU_WIDTH, MLSTM_WIDTH, MLSTM_WIDTH, MLSTM_WIDTH,
              MLSTM_HEADS, MLSTM_HEADS, MLSTM_WIDTH, MLSTM_WIDTH)
REC_IN = sum(REC_SPLITS)
REC_OUT = LRU_WIDTH + MLSTM_WIDTH

kernel_name = "hybrid_nsa_dsa_rglru_mlstm_trunk"


def rmsnorm(x, g):
    xf = x.astype(jnp.float32)
    y = xf * lax.rsqrt(jnp.mean(xf * xf, axis=-1, keepdims=True) + RMS_EPS)
    return (y * g.astype(jnp.float32)).astype(x.dtype)


def split_cols(u, widths):
    return jnp.split(u, [int(c) for c in np.cumsum(widths)[:-1]], axis=-1)


def masked_softmax(s, mask):
    return jax.nn.softmax(jnp.where(mask, s, NEG), axis=-1)


def t5_bucket(dist):
    n = jnp.maximum(dist, 0)
    max_exact = N_BUCKETS // 2
    nf = jnp.maximum(n, 1).astype(jnp.float32)
    large = max_exact + (jnp.log(nf / max_exact) / math.log(T5_MAX_DIST / max_exact)
                         * (N_BUCKETS - max_exact)).astype(jnp.int32)
    large = jnp.minimum(large, N_BUCKETS - 1)
    return jnp.where(n < max_exact, n, large)


def causal_dwconv(x, w, b):
    y = lax.conv_general_dilated(x, w[:, None, :], window_strides=(1,),
                                 padding=[(CONV_WIDTH - 1, 0)],
                                 dimension_numbers=('NWC', 'WIO', 'NWC'),
                                 feature_group_count=x.shape[-1])
    return y + b


def compress_blocks(k, pos, w1, w2):
    B, S = k.shape[0], k.shape[1]
    n_cmp = (S - CMP_BLOCK) // CMP_STRIDE + 1
    idx = np.arange(n_cmp)[:, None] * CMP_STRIDE + np.arange(CMP_BLOCK)[None, :]
    blk = k[:, idx] + pos[None, None, :, None, :]
    blk = blk.transpose(0, 1, 3, 2, 4).reshape(B, n_cmp, NSA_GROUPS, CMP_BLOCK * HEAD_DIM)
    return jax.nn.silu(blk @ w1) @ w2


def sparse_attention_layer(h, w_in, w_out, cmp_pos_k, cmp_w1_k, cmp_w2_k,
                           cmp_pos_v, cmp_w1_v, cmp_w2_v, t5_table):
    f32 = jnp.float32
    B, S, _ = h.shape
    u = h @ w_in
    (a_q, a_kv, a_g, a_z, b_q, b_k, b_v, b_qi, b_ki, b_wi, b_z) = split_cols(u, ATTN_SPLITS)
    scale = HEAD_DIM ** -0.5

    a_q = a_q.astype(f32).reshape(B, S, NSA_GROUPS, NSA_HPG, HEAD_DIM)
    gates = jax.nn.sigmoid(a_g.astype(f32)).reshape(B, S, NSA_GROUPS, NSA_HPG, 3)
    kc, vc, ks, vs, kw, vw = [t.reshape(B, S, NSA_GROUPS, HEAD_DIM) for t in jnp.split(a_kv, 6, axis=-1)]
    k_cmp = compress_blocks(kc, cmp_pos_k, cmp_w1_k, cmp_w2_k).astype(f32)
    v_cmp = compress_blocks(vc, cmp_pos_v, cmp_w1_v, cmp_w2_v).astype(f32)
    n_cmp = k_cmp.shape[1]
    cmp_end = jnp.asarray(np.arange(n_cmp) * CMP_STRIDE + CMP_BLOCK - 1, jnp.int32)
    n_blocks = S // SEL_BLOCK
    n_pick = min(N_SEL, n_blocks)
    ci = np.arange(n_cmp)[:, None]
    sj = np.arange(n_blocks)[None, :]
    overlap = jnp.asarray(((ci * CMP_STRIDE < (sj + 1) * SEL_BLOCK)
                           & (ci * CMP_STRIDE + CMP_BLOCK > sj * SEL_BLOCK)).astype(np.float32))
    blk_start = jnp.arange(n_blocks, dtype=jnp.int32) * SEL_BLOCK
    nb_idx = jnp.arange(n_blocks, dtype=jnp.int32)
    ks_blocks = ks.astype(f32).transpose(0, 2, 1, 3).reshape(B, NSA_GROUPS, n_blocks, SEL_BLOCK, HEAD_DIM)
    vs_blocks = vs.astype(f32).transpose(0, 2, 1, 3).reshape(B, NSA_GROUPS, n_blocks, SEL_BLOCK, HEAD_DIM)
    pad = ((0, 0), (WINDOW, 0), (0, 0), (0, 0))
    kw_pad = jnp.pad(kw.astype(f32), pad)
    vw_pad = jnp.pad(vw.astype(f32), pad)
    tbl_a = t5_table[:, :NSA_HEADS].astype(f32).reshape(N_BUCKETS, NSA_GROUPS, NSA_HPG).transpose(1, 2,0)
    g_ar = jnp.arange(NSA_GROUPS)[None, :, None, None, None]
    h_ar = jnp.arange(NSA_HPG)[None, None, :, None, None]

    b_q = b_q.astype(f32).reshape(B, S, DSA_HEADS, HEAD_DIM)
    b_k = b_k.astype(f32)
    b_v = b_v.astype(f32)
    b_qi = b_qi.astype(f32).reshape(B, S, IDX_HEADS, IDX_DIM)
    b_ki = b_ki.astype(f32)
    b_wi = b_wi.astype(f32) * (IDX_DIM ** -0.5 * IDX_HEADS ** -0.5)
    k_top = min(DSA_TOPK_MAX, S // 4)
    tbl_b = t5_table[:, NSA_HEADS:].astype(f32).T
    key_pos = jnp.arange(S, dtype=jnp.int32)

    def block_fn(qb):
        q0 = qb * Q_BLOCK
        t = q0 + jnp.arange(Q_BLOCK, dtype=jnp.int32)
        qa = lax.dynamic_slice_in_dim(a_q, q0, Q_BLOCK, axis=1)
        dist_c = t[:, None] - cmp_end[None, :]
        mask_c = dist_c >= 0
        s_c = jnp.einsum('bqghd,bcgd->bghqc', qa, k_cmp) * scale + tbl_a[:, :, t5_bucket(dist_c)]
        p_c = masked_softmax(s_c, mask_c) * jnp.any(mask_c, axis=-1)[:, None].astype(f32)
        o_c = jnp.einsum('bghqc,bcgd->bqghd', p_c, v_cmp)
        imp = jnp.einsum('bghqc,cn->bgqn', p_c, overlap)
        cur = t // SEL_BLOCK
        forced = ((nb_idx[None, :] == 0) | (nb_idx[None, :] == cur[:, None])
                  | (nb_idx[None, :] == cur[:, None] - 1)).astype(f32)
        admissible = blk_start[None, :] <= t[:, None]
        score = jnp.where(admissible, imp + FORCE_BONUS * forced, NEG)
        _, sel = lax.top_k(score, n_pick)
        gather = jax.vmap(jax.vmap(lambda kb, ix: kb[ix]))
        k_sel = gather(ks_blocks, sel).reshape(B, NSA_GROUPS, Q_BLOCK, n_pick * SEL_BLOCK, HEAD_DIM)
        v_sel = gather(vs_blocks, sel).reshape(B, NSA_GROUPS, Q_BLOCK, n_pick * SEL_BLOCK, HEAD_DIM)
        pos_sel = (sel[..., None] * SEL_BLOCK + jnp.arange(SEL_BLOCK, dtype=jnp.int32)).reshape(
            B, NSA_GROUPS, Q_BLOCK, n_pick * SEL_BLOCK)
        dist_s = t[None, None, :, None] - pos_sel
        bias_s = tbl_a[g_ar, h_ar, t5_bucket(dist_s)[:, :, None]]
        s_s = jnp.einsum('bqghd,bgqkd->bghqk', qa, k_sel) * scale + bias_s
        p_s = masked_softmax(s_s, (dist_s >= 0)[:, :, None])
        o_s = jnp.einsum('bghqk,bgqkd->bqghd', p_s, v_sel)
        k_win = lax.dynamic_slice_in_dim(kw_pad, q0, Q_BLOCK + WINDOW, axis=1)
        v_win = lax.dynamic_slice_in_dim(vw_pad, q0, Q_BLOCK + WINDOW, axis=1)
        pos_w = q0 - WINDOW + jnp.arange(Q_BLOCK + WINDOW, dtype=jnp.int32)
        dist_w = t[:, None] - pos_w[None, :]
        mask_w = (dist_w >= 0) & (dist_w < WINDOW) & (pos_w >= 0)[None, :]
        s_w = jnp.einsum('bqghd,bkgd->bghqk', qa, k_win) * scale + tbl_a[:, :, t5_bucket(dist_w)]
        p_w = masked_softmax(s_w, mask_w)
        o_w = jnp.einsum('bghqk,bkgd->bqghd', p_w, v_win)
        g = lax.dynamic_slice_in_dim(gates, q0, Q_BLOCK, axis=1)
        o_a = (g[..., 0:1] * o_c + g[..., 1:2] * o_s + g[..., 2:3] * o_w).reshape(B, Q_BLOCK, NSA_WIDTH)
        qi = lax.dynamic_slice_in_dim(b_qi, q0, Q_BLOCK, axis=1)
        wi = lax.dynamic_slice_in_dim(b_wi, q0, Q_BLOCK, axis=1)
        idx_score = jnp.einsum('bqh,bqhs->bqs', wi, jax.nn.relu(jnp.einsum('bqhd,bsd->bqhs', qi, b_ki)))
        idx_score = jnp.where(key_pos[None, :] <= t[:, None], idx_score, NEG)
        _, sel_b = lax.top_k(idx_score, k_top)
        take = jax.vmap(lambda kk, ix: kk[ix])
        k_b = take(b_k, sel_b)
        v_b = take(b_v, sel_b)
        dist_b = t[None, :, None] - sel_b
        bias_b = tbl_b[:, t5_bucket(dist_b)].transpose(1, 0, 2, 3)
        qd = lax.dynamic_slice_in_dim(b_q, q0, Q_BLOCK, axis=1)
        s_b = jnp.einsum('bqhd,bqkd->bhqk', qd, k_b) * scale + bias_b
        p_b = masked_softmax(s_b, (dist_b >= 0)[:, None])
        o_b = jnp.einsum('bhqk,bqkd->bqhd', p_b, v_b).reshape(B, Q_BLOCK, DSA_WIDTH)
        return o_a, o_b

    o_a, o_b = lax.map(block_fn, jnp.arange(S // Q_BLOCK, dtype=jnp.int32))
    o_a = o_a.transpose(1, 0, 2, 3).reshape(B, S, NSA_WIDTH)
    o_b = o_b.transpose(1, 0, 2, 3).reshape(B, S, DSA_WIDTH)
    y = jnp.concatenate([o_a * jax.nn.silu(a_z.astype(f32)), o_b * jax.nn.silu(b_z.astype(f32))], axis=-1)
    return y.astype(h.dtype) @ w_out


def rglru(x, conv_w, conv_b, wa, ba, wx, bx, lam):
    B, S, _ = x.shape
    xc = causal_dwconv(x, conv_w, conv_b)
    xb = xc.reshape(B, S, LRU_BLOCKS, LRU_BLOCK_DIM)
    r = jax.nn.sigmoid(jnp.einsum('bsgi,gij->bsgj', xb, wa).reshape(B, S, LRU_WIDTH) + ba)
    ig = jax.nn.sigmoid(jnp.einsum('bsgi,gij->bsgj', xb, wx).reshape(B, S, LRU_WIDTH) + bx)
    log_a = -LRU_C * r.astype(jnp.float32) * jax.nn.softplus(-lam.astype(jnp.float32))
    a = jnp.exp(log_a)
    b = jnp.sqrt(-jnp.expm1(2.0 * log_a)) * (ig * xc).astype(jnp.float32)

    def combine(lhs, rhs):
        a1, b1 = lhs
        a2, b2 = rhs
        return a1 * a2, a2 * b1 + b2

    _, hs = lax.associative_scan(combine, (a, b), axis=1)
    return hs


def mlstm_chunkwise(q, k, v, i_pre, f_pre):
    f32 = jnp.float32
    B, S, _ = q.shape
    L = MLSTM_CHUNK
    nc = S // L

    def to_chunks(t):
        return t.astype(f32).reshape(B, nc, L, MLSTM_HEADS, MLSTM_DIM).transpose(1, 0, 3, 2, 4)

    def gate_chunks(t):
        return t.reshape(B, nc, L, MLSTM_HEADS).transpose(1, 0, 3, 2)

    qc = to_chunks(q)
    kc = to_chunks(k) * (MLSTM_DIM ** -0.5)
    vc = to_chunks(v)
    li = gate_chunks(i_pre.astype(f32))
    lf = gate_chunks(jax.nn.log_sigmoid(f_pre.astype(f32)))
    causal = jnp.asarray(np.tril(np.ones((L, L), dtype=bool)))

    def step(carry, xs):
        C, n, m = carry
        q_, k_, v_, li_, lf_ = xs
        b = jnp.cumsum(lf_, axis=-1)
        dmat = jnp.where(causal, b[..., :, None] - b[..., None, :] + li_[..., None, :], -jnp.inf)
        inter = b + m[..., None]
        m_t = jnp.maximum(inter, jnp.max(dmat, axis=-1))
        w = jnp.einsum('bhld,bhsd->bhls', q_, k_) * jnp.exp(dmat - m_t[..., None])
        prev = jnp.exp(inter - m_t)
        num = prev[..., None] * jnp.einsum('bhld,bhde->bhle', q_, C) + jnp.einsum('bhls,bhse->bhle', w, v_)
        den = prev * jnp.einsum('bhld,bhd->bhl', q_, n) + jnp.sum(w, axis=-1)
        out = num / jnp.maximum(jnp.abs(den), jnp.exp(-m_t))[..., None]
        b_last = b[..., -1]
        decay = b_last[..., None] - b + li_
        m_new = jnp.maximum(b_last + m, jnp.max(decay, axis=-1))
        wk = jnp.exp(decay - m_new[..., None])
        keep = jnp.exp(b_last + m - m_new)
        C_new = keep[..., None, None] * C + jnp.einsum('bhs,bhsd,bhse->bhde', wk, k_, v_)
        n_new = keep[..., None] * n + jnp.einsum('bhs,bhsd->bhd', wk, k_)
        return (C_new, n_new, m_new), out

    init = (jnp.zeros((B, MLSTM_HEADS, MLSTM_DIM, MLSTM_DIM), f32),
            jnp.zeros((B, MLSTM_HEADS, MLSTM_DIM), f32),
            jnp.zeros((B, MLSTM_HEADS), f32))
    _, hs = lax.scan(step, init, (qc, kc, vc, li, lf))
    return hs.transpose(1, 0, 3, 2, 4).reshape(B, S, MLSTM_WIDTH)


def recurrent_layer(h, w_in, w_out, conv_c_w, conv_c_b, wa, ba, wx, bx, lam,
                    conv_d_w, conv_d_b, b_i, b_f):
    f32 = jnp.float32
    u = h @ w_in
    c_x, c_z, d_q, d_k, d_v, d_i, d_f, d_o, d_z = split_cols(u, REC_SPLITS)
    y_c = rglru(c_x, conv_c_w, conv_c_b, wa, ba, wx, bx, lam) * jax.nn.silu(c_z.astype(f32))
    qk = jax.nn.silu(causal_dwconv(jnp.concatenate([d_q, d_k], axis=-1), conv_d_w, conv_d_b))
    q, k = jnp.split(qk, 2, axis=-1)
    h_d = mlstm_chunkwise(q, k, d_v, d_i + b_i, d_f + b_f)
    y_d = jax.nn.sigmoid(d_o.astype(f32)) * h_d * jax.nn.silu(d_z.astype(f32))
    y = jnp.concatenate([y_c, y_d], axis=-1)
    return y.astype(h.dtype) @ w_out


def setup_inputs(seed: int = 0) -> dict:
    key = jax.random.key(seed)
    keys = iter(jax.random.split(key, 40))

    def nrm(shape, scale):
        return jax.random.normal(next(keys), shape, jnp.float32) * scale

    a0 = jax.random.uniform(next(keys), (N_ODD, LRU_WIDTH), jnp.float32, minval=0.9, maxval=0.999)
    return {
        "x": nrm((BATCH, SEQ, D_MODEL), 1.0),
        "p": nrm((DEPTH, BATCH, SEQ, PLE_DIM), 1.0),
        "norm_g": 1.0 + nrm((DEPTH, D_MODEL), 0.02),
        "final_g": 1.0 + nrm((D_MODEL,), 0.02),
        "ple_w": nrm((DEPTH, PLE_DIM, D_MODEL), PLE_DIM ** -0.5),
        "ple_gate_w": nrm((DEPTH, D_MODEL, D_MODEL), D_MODEL ** -0.5),
        "t5_table": nrm((N_BUCKETS, N_ATTN_HEADS), 0.3),
        "attn_w_in": nrm((N_EVEN, D_MODEL, ATTN_IN), D_MODEL ** -0.5),
        "attn_w_out": nrm((N_EVEN, ATTN_OUT, D_MODEL), ATTN_OUT ** -0.5),
        "cmp_pos_k": nrm((N_EVEN, CMP_BLOCK, HEAD_DIM), 0.1),
        "cmp_w1_k": nrm((N_EVEN, CMP_BLOCK * HEAD_DIM, CMP_HIDDEN), (CMP_BLOCK * HEAD_DIM) ** -0.5),
        "cmp_w2_k": nrm((N_EVEN, CMP_HIDDEN, HEAD_DIM), CMP_HIDDEN ** -0.5),
        "cmp_pos_v": nrm((N_EVEN, CMP_BLOCK, HEAD_DIM), 0.1),
        "cmp_w1_v": nrm((N_EVEN, CMP_BLOCK * HEAD_DIM, CMP_HIDDEN), (CMP_BLOCK * HEAD_DIM) ** -0.5),
        "cmp_w2_v": nrm((N_EVEN, CMP_HIDDEN, HEAD_DIM), CMP_HIDDEN ** -0.5),
        "rec_w_in": nrm((N_ODD, D_MODEL, REC_IN), D_MODEL ** -0.5),
        "rec_w_out": nrm((N_ODD, REC_OUT, D_MODEL), REC_OUT ** -0.5),
        "lru_conv_w": nrm((N_ODD, CONV_WIDTH, LRU_WIDTH), CONV_WIDTH ** -0.5),
        "lru_conv_b": nrm((N_ODD, LRU_WIDTH), 0.01),
        "lru_wa": nrm((N_ODD, LRU_BLOCKS, LRU_BLOCK_DIM, LRU_BLOCK_DIM), LRU_BLOCK_DIM ** -0.5),
        "lru_ba": nrm((N_ODD, LRU_WIDTH), 0.01),
        "lru_wx": nrm((N_ODD, LRU_BLOCKS, LRU_BLOCK_DIM, LRU_BLOCK_DIM), LRU_BLOCK_DIM ** -0.5),
        "lru_bx": nrm((N_ODD, LRU_WIDTH), 0.01),
        "lru_lambda": jnp.log(a0) - jnp.log1p(-a0),
        "mlstm_conv_w": nrm((N_ODD, CONV_WIDTH, 2 * MLSTM_WIDTH), CONV_WIDTH ** -0.5),
        "mlstm_conv_b": nrm((N_ODD, 2 * MLSTM_WIDTH), 0.01),
        "mlstm_b_i": nrm((N_ODD, MLSTM_HEADS), 0.1),
        "mlstm_b_f": jnp.linspace(3.0, 6.0, MLSTM_HEADS, dtype=jnp.float32)[None, :] + nrm((N_ODD, MLSTM_HEADS), 0.1),
    }


def reference(x, p, norm_g, final_g, ple_w, ple_gate_w, t5_table,
              attn_w_in, attn_w_out, cmp_pos_k, cmp_w1_k, cmp_w2_k,
              cmp_pos_v, cmp_w1_v, cmp_w2_v,
              rec_w_in, rec_w_out, lru_conv_w, lru_conv_b, lru_wa, lru_ba,
              lru_wx, lru_bx, lru_lambda, mlstm_conv_w, mlstm_conv_b, mlstm_b_i, mlstm_b_f):
    for i in range(DEPTH):
        hn = rmsnorm(x, norm_g[i])
        j = i // 2
        if i % 2 == 0:
            y = sparse_attention_layer(hn, attn_w_in[j], attn_w_out[j],
                                       cmp_pos_k[j], cmp_w1_k[j], cmp_w2_k[j],
                                       cmp_pos_v[j], cmp_w1_v[j], cmp_w2_v[j], t5_table)
        else:
            y = recurrent_layer(hn, rec_w_in[j], rec_w_out[j], lru_conv_w[j], lru_conv_b[j],
                                lru_wa[j], lru_ba[j], lru_wx[j], lru_bx[j], lru_lambda[j],
                                mlstm_conv_w[j], mlstm_conv_b[j], mlstm_b_i[j], mlstm_b_f[j])
        x = x + y
        x = x + (p[i] @ ple_w[i]) * jax.nn.sigmoid(x @ ple_gate_w[i])
    return rmsnorm(x, final_g)
```

```python
import functools
import math

import numpy as np
import jax
import jax.numpy as jnp
from jax import lax
from jax.experimental import pallas as pl
from jax.experimental.pallas import tpu as pltpu

F32 = jnp.float32
MXU_DTYPE = jnp.bfloat16

HEAD_DIM = 64
NSA_HEADS = 8
NSA_GROUPS = 2
NSA_HPG = NSA_HEADS // NSA_GROUPS
CMP_BLOCK = 32
CMP_STRIDE = 16
CMP_HIDDEN = 256
SEL_BLOCK = 64
N_SEL = 8
WINDOW = 512
FORCE_BONUS = 1e4
DSA_HEADS = 8
IDX_HEADS = 4
IDX_DIM = 64
DSA_TOPK_MAX = 256
N_BUCKETS = 32
T5_MAX_DIST = 128
LRU_WIDTH = 512
LRU_BLOCKS = 8
CONV_WIDTH = 4
LRU_C = 8.0
MLSTM_HEADS = 4
MLSTM_DIM = 128
MLSTM_WIDTH = MLSTM_HEADS * MLSTM_DIM
MLSTM_CHUNK = 64
RMS_EPS = 1e-6
NEG = -1e30

LANES = 128
TQ = 128
INT_MIN = -2 ** 31
VMEM_LIMIT = 48 * 1024 * 1024

A_NSA_Q = 0
A_DSA_Q = 1024
A_AZ = 2048
A_BZ = 2560
A_IDX_Q = 3072
A_KV = 3584
A_GATE = 4352
A_DSA_KV = 4480
A_IDX_K = 4608
A_IDX_W = 4736
A_TOTAL = 4864

R_CX = 0
R_CZ = 512
R_QK = 1024
R_V = 2048
R_O = 2560
R_Z = 3072
R_GATE = 3584
R_TOTAL = 3840


def _dot(a, b):
    return jnp.dot(a, b, preferred_element_type=F32)


def _dot_nt(a, b):
    return lax.dot_general(a, b, (((1,), (1,)), ((), ())), preferred_element_type=F32)


def _dot_tn(a, b):
    return lax.dot_general(a, b, (((0,), (0,)), ((), ())), preferred_element_type=F32)


def _mx(a):
    return a.astype(MXU_DTYPE)


def _params(n_grid):
    return pltpu.CompilerParams(dimension_semantics=("arbitrary",) * n_grid,
                                vmem_limit_bytes=VMEM_LIMIT)


def _norm_proj_kernel(x_ref, g_ref, w_ref, o_ref):
    x = x_ref[...]
    ms = jnp.mean(x * x, axis=-1, keepdims=True)
    y = x * lax.rsqrt(ms + RMS_EPS) * g_ref[...]
    o_ref[...] = _dot(_mx(y), w_ref[...])


def _norm_proj(x2d, g, w, tm=256):
    m, d = x2d.shape
    n = w.shape[1]
    return pl.pallas_call(
        _norm_proj_kernel,
        grid=(m // tm,),
        in_specs=[pl.BlockSpec((tm, d), lambda i: (i, 0)),
                  pl.BlockSpec((1, d), lambda i: (0, 0)),
                  pl.BlockSpec((d, n), lambda i: (0, 0))],
        out_specs=pl.BlockSpec((tm, n), lambda i: (i, 0)),
        out_shape=jax.ShapeDtypeStruct((m, n), F32),
        compiler_params=_params(1),
        name="norm_proj",
    )(x2d, g.reshape(1, d), w)


def _out_proj_kernel(x_ref, ya_ref, yb_ref, p_ref, wo_ref, pw_ref, gw_ref, fg_ref, o_ref, *, final):
    half = ya_ref.shape[-1]
    y = _dot(_mx(ya_ref[...]), wo_ref[:half, :]) + _dot(_mx(yb_ref[...]), wo_ref[half:, :])
    x1 = x_ref[...] + y
    gate = jax.nn.sigmoid(_dot(_mx(x1), gw_ref[...]))
    x2 = x1 + _dot(_mx(p_ref[...]), pw_ref[...]) * gate
    if final:
        ms = jnp.mean(x2 * x2, axis=-1, keepdims=True)
        x2 = x2 * lax.rsqrt(ms + RMS_EPS) * fg_ref[...]
    o_ref[...] = x2


def _out_proj(x2d, ya, yb, p2d, w_out, ple_w, gate_w, final_g, final, tm=512):
    m, d = x2d.shape
    half = ya.shape[1]
    pd = p2d.shape[1]
    row = lambda i: (i, 0)
    whole = lambda i: (0, 0)
    return pl.pallas_call(
        functools.partial(_out_proj_kernel, final=final),
        grid=(m // tm,),
        in_specs=[pl.BlockSpec((tm, d), row), pl.BlockSpec((tm, half), row),
                  pl.BlockSpec((tm, half), row), pl.BlockSpec((tm, pd), row),
                  pl.BlockSpec((2 * half, d), whole), pl.BlockSpec((pd, d), whole),
                  pl.BlockSpec((d, d), whole), pl.BlockSpec((1, d), whole)],
        out_specs=pl.BlockSpec((tm, d), row),
        out_shape=jax.ShapeDtypeStruct((m, d), F32),
        compiler_params=_params(1),
        name="out_proj",
    )(x2d, ya, yb, p2d, _mx(w_out), _mx(ple_w), _mx(gate_w), final_g.reshape(1, d))


def _bucket_np(n):
    n = np.asarray(n)
    max_exact = N_BUCKETS // 2
    nf = np.maximum(n, 1).astype(np.float32)
    large = max_exact + (np.log(nf / np.float32(max_exact)) / np.float32(math.log(T5_MAX_DIST / max_exact))
                         * np.float32(N_BUCKETS - max_exact)).astype(np.int32)
    large = np.minimum(large, N_BUCKETS - 1)
    return np.where(n < max_exact, n, large)


def _bias_index(dist):
    dist = np.asarray(dist)
    return np.where(dist >= 0, _bucket_np(np.maximum(dist, 0)), N_BUCKETS).astype(np.int32)


def _near_tiles(tbl):
    far = tbl[N_BUCKETS - 1]
    ext = jnp.concatenate([tbl, jnp.full((1, tbl.shape[1]), NEG, F32)], axis=0)
    qi = np.arange(TQ)[:, None]
    kj = np.arange(TQ)[None, :]
    d0 = ext[_bias_index(qi - kj)] - far
    d1 = ext[_bias_index(TQ + qi - kj)] - far
    return d0.transpose(2, 0, 1), d1.transpose(2, 0, 1)


def _cmp_bias(tbl, nq, n16):
    ext = jnp.concatenate([tbl, jnp.full((1, tbl.shape[1]), NEG, F32)], axis=0)
    per_tile = TQ // CMP_STRIDE
    off = per_tile * (nq - 1)
    width = n16 + off
    qi = np.arange(TQ)[:, None]
    cc = np.arange(width)[None, :] - off
    g = ext[_bias_index(qi - CMP_STRIDE * cc - (CMP_BLOCK - 1))].transpose(2, 0, 1)
    return jnp.stack([g[:, :, off - per_tile * i: off - per_tile * i + n16] for i in range(nq)], axis=0)


def _compress_kernel(h_ref, pos_ref, w1_ref, w2_ref, o_ref):
    n16 = h_ref.shape[1]
    half = h_ref.shape[2]
    acc = None
    for g in range(NSA_GROUPS):
        h = h_ref[g]
        pa = _dot(_mx(h + pos_ref[0]), w1_ref[:half, :])
        pb = _dot(_mx(h + pos_ref[1]), w1_ref[half:, :])
        pre = pa + pltpu.roll(pb, n16 - 1, 0)
        t = _dot(_mx(jax.nn.silu(pre)), w2_ref[g])
        acc = t if acc is None else acc + t
    o_ref[...] = acc


def _compress(hkv, pos, w1, w2):
    _, b, g, n16, half = hkv.shape
    return pl.pallas_call(
        _compress_kernel,
        grid=(2, b),
        in_specs=[pl.BlockSpec((None, None, g, n16, half), lambda k, i: (k, i, 0, 0, 0)),
                  pl.BlockSpec((None, 2, 1, half), lambda k, i: (k, 0, 0, 0)),
                  pl.BlockSpec((None, 2 * half, CMP_HIDDEN), lambda k, i: (k, 0, 0)),
                  pl.BlockSpec((None, g, CMP_HIDDEN, LANES), lambda k, i: (k, 0, 0, 0))],
        out_specs=pl.BlockSpec((None, None, n16, LANES), lambda k, i: (k, i, 0, 0)),
        out_shape=jax.ShapeDtypeStruct((2, b, n16, LANES), F32),
        compiler_params=_params(2),
        name="compress",
    )(hkv, pos, w1, w2)


def _softmax_init(m_s, l_s, acc_s):
    m_s[...] = jnp.full(m_s.shape, -jnp.inf, F32)
    l_s[...] = jnp.zeros(l_s.shape, F32)
    acc_s[...] = jnp.zeros(acc_s.shape, F32)


def _softmax_update(s, v, m_s, l_s, acc_s):
    m_prev = m_s[...]
    m_new = jnp.maximum(m_prev, jnp.max(s, axis=-1, keepdims=True))
    alpha = jnp.exp(m_prev - m_new)
    p = jnp.exp(s - m_new)
    l_s[...] = alpha * l_s[...] + jnp.sum(p, axis=-1, keepdims=True)
    acc_s[...] = alpha * acc_s[...] + _dot(_mx(p), v)
    m_s[...] = m_new


def _stack_heads(q_ref, n_heads):
    return jnp.concatenate([q_ref[:, h * LANES:(h + 1) * LANES] for h in range(n_heads)], axis=0)


def _interleave_heads(o, valid_hi):
    lane = lax.broadcasted_iota(jnp.int32, (TQ, LANES), 1)
    out = []
    for m in range(o.shape[0] // (2 * TQ)):
        even = o[(2 * m) * TQ:(2 * m + 1) * TQ]
        odd = o[(2 * m + 1) * TQ:(2 * m + 2) * TQ]
        if valid_hi:
            even = pltpu.roll(even, HEAD_DIM, 1)
        else:
            odd = pltpu.roll(odd, HEAD_DIM, 1)
        out.append(jnp.where(lane < HEAD_DIM, even, odd))
    return out


def _nsa_kernel(q_ref, kc_ref, vc_ref, ks_ref, vs_ref, kw_ref, vw_ref, g_ref, z_ref,
                bc_ref, d_ref, u_ref, e_ref, ovt_ref, o_ref,
                qs_s, m_s, l_s, acc_s, mb_s, osel_s, *, n_pick):
    i = pl.program_id(1)
    rows = NSA_HEADS * TQ
    grows = NSA_HPG * TQ
    n16 = kc_ref.shape[0]
    nb = ovt_ref.shape[0]
    nkt = ks_ref.shape[0] // TQ
    q0 = i * TQ

    qs = _mx(_stack_heads(q_ref, NSA_HEADS))
    qs_s[...] = qs

    s = _dot_nt(qs, _mx(kc_ref[...])) + bc_ref[...].reshape(rows, n16)
    mx = jnp.max(s, axis=-1, keepdims=True)
    p = jnp.exp(s - mx)
    t_row = q0 + lax.rem(lax.broadcasted_iota(jnp.int32, (rows, 1), 0), TQ)
    any_valid = jnp.where(t_row >= CMP_BLOCK - 1, 1.0, 0.0)
    p = p * (any_valid / jnp.sum(p, axis=-1, keepdims=True))
    o_c = _dot(_mx(p), _mx(vc_ref[...]))

    blk = lax.broadcasted_iota(jnp.int32, (nb, TQ), 0)
    t = q0 + lax.broadcasted_iota(jnp.int32, (nb, TQ), 1)
    cur = lax.shift_right_logical(t, int(math.log2(SEL_BLOCK)))
    forced = jnp.where(blk == 0, 1.0, jnp.where(blk == cur, 1.0, jnp.where(blk == cur - 1, 1.0, 0.0)))
    admissible = blk * SEL_BLOCK <= t
    ovt = ovt_ref[...]
    for g in range(NSA_GROUPS):
        ps = p[g * grows:g * grows + TQ]
        for h in range(1, NSA_HPG):
            ps = ps + p[g * grows + h * TQ:g * grows + (h + 1) * TQ]
        hi = _mx(ps)
        lo = _mx(ps - hi.astype(F32))
        imp_t = _dot_nt(ovt, hi) + _dot_nt(ovt, lo)
        score = jnp.where(admissible, imp_t + FORCE_BONUS * forced, NEG)
        sel = jnp.zeros((nb, TQ), F32)
        for _ in range(n_pick):
            best = jnp.max(score, axis=0, keepdims=True)
            first = jnp.min(jnp.where(score == best, blk, nb), axis=0, keepdims=True)
            hit = blk == first
            sel = jnp.where(hit, 1.0, sel)
            score = jnp.where(hit, -jnp.inf, score)
        if nb < LANES:
            sel = jnp.concatenate([sel, jnp.zeros((LANES - nb, TQ), F32)], axis=0)
        keymask = _dot(_mx(sel.T), e_ref[...])
        addmask = (keymask - 1.0) * (-NEG)
        for kt in range(nkt):
            mb_s[g, kt] = addmask[:, kt * TQ:(kt + 1) * TQ]

    _softmax_init(m_s, l_s, acc_s)

    def sel_tile(j, carry):
        koff = pl.multiple_of(j * TQ, TQ)
        s = _dot_nt(qs_s[...], _mx(ks_ref[pl.ds(koff, TQ), :]))
        bias = d_ref[jnp.clip(j - (i - 2), 0, 2)].reshape(rows, TQ)
        mb = jnp.concatenate([mb_s[0, j]] * NSA_HPG + [mb_s[1, j]] * NSA_HPG, axis=0)
        _softmax_update(s + bias + mb, _mx(vs_ref[pl.ds(koff, TQ), :]), m_s, l_s, acc_s)
        return carry

    lax.fori_loop(0, i + 1, sel_tile, 0)
    osel_s[...] = acc_s[...] / l_s[...]

    _softmax_init(m_s, l_s, acc_s)
    n_win = WINDOW // TQ
    for k in range(n_win + 1):
        j = i - n_win + k

        @pl.when(j >= 0)
        def _():
            koff = pl.multiple_of(j * TQ, TQ)
            s = _dot_nt(qs_s[...], _mx(kw_ref[pl.ds(koff, TQ), :]))
            if k == 0:
                s = s + jnp.concatenate([u_ref[...]] * NSA_HEADS, axis=0)
            elif k == n_win - 1:
                s = s + d_ref[1].reshape(rows, TQ)
            elif k == n_win:
                s = s + d_ref[2].reshape(rows, TQ)
            _softmax_update(s, _mx(vw_ref[pl.ds(koff, TQ), :]), m_s, l_s, acc_s)

    o_w = acc_s[...] / l_s[...]
    o_s = osel_s[...]

    gate = jax.nn.sigmoid(g_ref[...])
    lane = lax.broadcasted_iota(jnp.int32, (TQ, LANES), 1)
    per_head = []
    for hh in range(NSA_HEADS):
        r = slice(hh * TQ, (hh + 1) * TQ)
        o = (gate[:, 3 * hh:3 * hh + 1] * o_c[r] + gate[:, 3 * hh + 1:3 * hh + 2] * o_s[r]
             + gate[:, 3 * hh + 2:3 * hh + 3] * o_w[r])
        per_head.append(o)
    for m in range(NSA_HEADS // 2):
        even, odd = per_head[2 * m], per_head[2 * m + 1]
        if 2 * m < NSA_HPG:
            odd = pltpu.roll(odd, HEAD_DIM, 1)
        else:
            even = pltpu.roll(even, HEAD_DIM, 1)
        slab = jnp.where(lane < HEAD_DIM, even, odd)
        o_ref[:, m * LANES:(m + 1) * LANES] = slab * jax.nn.silu(z_ref[:, m * LANES:(m + 1) * LANES])


def _nsa(u, cmp_kv, bias_c, d_tiles, u_tile, e_mat, ovt, n_pick):
    b, s, _ = u.shape
    nq = s // TQ
    n16 = cmp_kv.shape[2]
    nb = ovt.shape[0]
    rows = NSA_HEADS * TQ
    seq = lambda col: pl.BlockSpec((None, s, LANES), lambda bi, i, col=col: (bi, 0, col // LANES))
    return pl.pallas_call(
        functools.partial(_nsa_kernel, n_pick=n_pick),
        grid=(b, nq),
        in_specs=[
            pl.BlockSpec((None, TQ, NSA_HEADS * LANES), lambda bi, i: (bi, i, A_NSA_Q // (NSA_HEADS * LANES))),
            pl.BlockSpec((None, None, n16, LANES), lambda bi, i: (0, bi, 0, 0)),
            pl.BlockSpec((None, None, n16, LANES), lambda bi, i: (1, bi, 0, 0)),
            seq(A_KV + 2 * LANES), seq(A_KV + 3 * LANES), seq(A_KV + 4 * LANES), seq(A_KV + 5 * LANES),
            pl.BlockSpec((None, TQ, LANES), lambda bi, i: (bi, i, A_GATE // LANES)),
            pl.BlockSpec((None, TQ, 512), lambda bi, i: (bi, i, A_AZ // 512)),
            pl.BlockSpec((None, NSA_HEADS, TQ, n16), lambda bi, i: (i, 0, 0, 0)),
            pl.BlockSpec((3, NSA_HEADS, TQ, TQ), lambda bi, i: (0, 0, 0, 0)),
            pl.BlockSpec((TQ, TQ), lambda bi, i: (0, 0)),
            pl.BlockSpec((LANES, s), lambda bi, i: (0, 0)),
            pl.BlockSpec((nb, n16), lambda bi, i: (0, 0)),
        ],
        out_specs=pl.BlockSpec((None, TQ, 512), lambda bi, i: (bi, i, 0)),
        out_shape=jax.ShapeDtypeStruct((b, s, 512), F32),
        scratch_shapes=[pltpu.VMEM((rows, LANES), MXU_DTYPE),
                        pltpu.VMEM((rows, LANES), F32), pltpu.VMEM((rows, LANES), F32),
                        pltpu.VMEM((rows, LANES), F32),
                        pltpu.VMEM((NSA_GROUPS, s // TQ, TQ, TQ), F32),
                        pltpu.VMEM((rows, LANES), F32)],
        compiler_params=_params(2),
        name="nsa",
    )(u, cmp_kv, cmp_kv, u, u, u, u, u, u, bias_c, d_tiles, u_tile, e_mat, ovt)


def _dsa_kernel(q_ref, kv_ref, qi_ref, ki_ref, wi_ref, z_ref, d_ref, tri_ref, o_ref,
                qs_s, sc_s, m_s, l_s, acc_s, carry_s, *, k_top):
    i = pl.program_id(1)
    rows = DSA_HEADS * TQ
    qs_s[...] = _mx(_stack_heads(q_ref, DSA_HEADS))
    qis = _mx(_stack_heads(qi_ref, IDX_HEADS))
    wi = wi_ref[...]
    wcols = [jnp.broadcast_to(wi[:, h:h + 1], (TQ, TQ)) for h in range(IDX_HEADS)]
    qi_idx = lax.broadcasted_iota(jnp.int32, (TQ, TQ), 0)
    kj_idx = lax.broadcasted_iota(jnp.int32, (TQ, TQ), 1)

    def score_tile(j, carry):
        koff = pl.multiple_of(j * TQ, TQ)
        r = jnp.maximum(_dot_nt(qis, _mx(ki_ref[pl.ds(koff, TQ), :])), 0.0)
        sc = wcols[0] * r[0:TQ]
        for h in range(1, IDX_HEADS):
            sc = sc + wcols[h] * r[h * TQ:(h + 1) * TQ]
        sc = jnp.where((j == i) & (kj_idx > qi_idx), NEG, sc)
        bits = pltpu.bitcast(sc, jnp.int32)
        key = bits ^ (lax.shift_right_arithmetic(bits, 31) & jnp.int32(0x7FFFFFFF))
        sc_s[j] = jnp.where(sc == 0.0, 0, key)
        return carry

    lax.fori_loop(0, i + 1, score_tile, 0)

    def count(pred):
        def body(j, acc):
            return acc + jnp.where(pred(sc_s[j]), 1.0, 0.0)
        acc = lax.fori_loop(0, i + 1, body, jnp.zeros((TQ, TQ), F32))
        return jnp.sum(acc, axis=-1, keepdims=True)

    kf = float(k_top)
    c0 = count(lambda k: k >= 0)
    thr0 = jnp.where(c0 >= kf, 0, INT_MIN) + jnp.zeros((TQ, TQ), jnp.int32)

    def bisect(it, thr):
        cand = thr | lax.shift_left(jnp.int32(1), 30 - it)
        c = count(lambda k: k >= cand)
        return jnp.where(c >= kf, cand, thr)

    thr = lax.fori_loop(0, 31, bisect, thr0)
    need = kf - count(lambda k: k > thr)

    _softmax_init(m_s, l_s, acc_s)
    carry_s[...] = jnp.zeros((TQ, TQ), F32)

    def att_tile(j, carry):
        koff = pl.multiple_of(j * TQ, TQ)
        key = sc_s[j]
        eq = key == thr
        pt = _dot(_mx(jnp.where(eq, 1.0, 0.0)), tri_ref[...])
        seen = carry_s[...]
        take = jnp.where(eq, jnp.where(seen + pt[:, :TQ] <= need, 1.0, 0.0), 0.0)
        carry_s[...] = seen + pt[:, TQ:]
        mb = jnp.where(key > thr, 0.0, (take - 1.0) * (-NEG))
        kv = _mx(kv_ref[pl.ds(koff, TQ), :])
        s = _dot_nt(qs_s[...], kv)
        bias = d_ref[jnp.clip(j - (i - 2), 0, 2)].reshape(rows, TQ)
        _softmax_update(s + bias + jnp.concatenate([mb] * DSA_HEADS, axis=0), kv, m_s, l_s, acc_s)
        return carry

    lax.fori_loop(0, i + 1, att_tile, 0)
    o = acc_s[...] / l_s[...]
    for m, slab in enumerate(_interleave_heads(o, valid_hi=True)):
        o_ref[:, m * LANES:(m + 1) * LANES] = slab * jax.nn.silu(z_ref[:, m * LANES:(m + 1) * LANES])


def _dsa(u, d_tiles, tri, k_top):
    b, s, _ = u.shape
    nq = s // TQ
    rows = DSA_HEADS * TQ
    seq = lambda col: pl.BlockSpec((None, s, LANES), lambda bi, i, col=col: (bi, 0, col // LANES))
    return pl.pallas_call(
        functools.partial(_dsa_kernel, k_top=k_top),
        grid=(b, nq),
        in_specs=[
            pl.BlockSpec((None, TQ, DSA_HEADS * LANES), lambda bi, i: (bi, i, A_DSA_Q // (DSA_HEADS * LANES))),
            seq(A_DSA_KV),
            pl.BlockSpec((None, TQ, IDX_HEADS * LANES), lambda bi, i: (bi, i, A_IDX_Q // (IDX_HEADS * LANES))),
            seq(A_IDX_K),
            pl.BlockSpec((None, TQ, LANES), lambda bi, i: (bi, i, A_IDX_W // LANES)),
            pl.BlockSpec((None, TQ, 512), lambda bi, i: (bi, i, A_BZ // 512)),
            pl.BlockSpec((3, DSA_HEADS, TQ, TQ), lambda bi, i: (0, 0, 0, 0)),
            pl.BlockSpec((TQ, 2 * TQ), lambda bi, i: (0, 0)),
        ],
        out_specs=pl.BlockSpec((None, TQ, 512), lambda bi, i: (bi, i, 0)),
        out_shape=jax.ShapeDtypeStruct((b, s, 512), F32),
        scratch_shapes=[pltpu.VMEM((rows, LANES), MXU_DTYPE),
                        pltpu.VMEM((nq, TQ, TQ), jnp.int32),
                        pltpu.VMEM((rows, LANES), F32), pltpu.VMEM((rows, LANES), F32),
                        pltpu.VMEM((rows, LANES), F32),
                        pltpu.VMEM((TQ, TQ), F32)],
        compiler_params=_params(2),
        name="dsa",
    )(u, u, u, u, u, u, d_tiles, tri)


def _causal_conv(x, xe_s, w_ref, b_ref):
    t = x.shape[0]
    xe_s[8:8 + t, :] = x
    y = b_ref[...] + w_ref[CONV_WIDTH - 1:CONV_WIDTH, :] * x
    for k in range(1, CONV_WIDTH):
        y = y + w_ref[CONV_WIDTH - 1 - k:CONV_WIDTH - k, :] * xe_s[8 - k:8 - k + t, :]
    xe_s[0:8, :] = x[t - 8:t, :]
    return y


def _rglru_kernel(x_ref, z_ref, cw_ref, cb_ref, wa_ref, ba_ref, wx_ref, bx_ref, lam_ref, o_ref,
                  xe_s, h_s):
    t, c = x_ref.shape

    @pl.when(pl.program_id(1) == 0)
    def _():
        xe_s[0:8, :] = jnp.zeros((8, c), F32)
        h_s[...] = jnp.zeros(h_s.shape, F32)

    xc = _causal_conv(x_ref[...], xe_s, cw_ref, cb_ref)
    xcm = _mx(xc)
    r = jax.nn.sigmoid(_dot(xcm, wa_ref[...]) + ba_ref[...])
    ig = jax.nn.sigmoid(_dot(xcm, wx_ref[...]) + bx_ref[...])
    nl = -lam_ref[...]
    softplus = jnp.maximum(nl, 0.0) + jnp.log1p(jnp.exp(-jnp.abs(nl)))
    log_a = (-LRU_C * r) * softplus
    a = jnp.exp(log_a)
    bb = jnp.sqrt(-jnp.tanh(log_a) * (a * a + 1.0)) * (ig * xc)
    row = lax.broadcasted_iota(jnp.int32, (t, c), 0)
    d = 1
    while d < t:
        live = row >= d
        a_sh = jnp.where(live, pltpu.roll(a, d, 0), 1.0)
        b_sh = jnp.where(live, pltpu.roll(bb, d, 0), 0.0)
        bb = a * b_sh + bb
        a = a * a_sh
        d *= 2
    h = a * h_s[0:1, :] + bb
    h_s[0:1, :] = h[t - 1:t, :]
    o_ref[...] = h * jax.nn.silu(z_ref[...])


def _rglru(u, conv_w, conv_b, wa_bd, ba, wx_bd, bx, lam, t=256):
    b, s, _ = u.shape
    c = LRU_WIDTH
    whole = lambda bi, i: (0, 0)
    return pl.pallas_call(
        _rglru_kernel,
        grid=(b, s // t),
        in_specs=[pl.BlockSpec((None, t, c), lambda bi, i: (bi, i, R_CX // c)),
                  pl.BlockSpec((None, t, c), lambda bi, i: (bi, i, R_CZ // c)),
                  pl.BlockSpec((CONV_WIDTH, c), whole), pl.BlockSpec((1, c), whole),
                  pl.BlockSpec((c, c), whole), pl.BlockSpec((1, c), whole),
                  pl.BlockSpec((c, c), whole), pl.BlockSpec((1, c), whole),
                  pl.BlockSpec((1, c), whole)],
        out_specs=pl.BlockSpec((None, t, c), lambda bi, i: (bi, i, 0)),
        out_shape=jax.ShapeDtypeStruct((b, s, c), F32),
        scratch_shapes=[pltpu.VMEM((t + 8, c), F32), pltpu.VMEM((8, c), F32)],
        compiler_params=_params(2),
        name="rglru",
    )(u, u, conv_w, conv_b.reshape(1, c), wa_bd, ba.reshape(1, c), wx_bd, bx.reshape(1, c),
      lam.reshape(1, c))


def _mlstm_kernel(qk_ref, v_ref, g_ref, og_ref, z_ref, cw_ref, cb_ref, gb_ref, o_ref,
                  xe_s, c_s, n_s, m_s):
    t = qk_ref.shape[0]
    L = MLSTM_CHUNK
    H = MLSTM_HEADS
    D = MLSTM_DIM

    @pl.when(pl.program_id(1) == 0)
    def _():
        xe_s[0:8, :] = jnp.zeros((8, xe_s.shape[1]), F32)
        c_s[...] = jnp.zeros(c_s.shape, F32)
        n_s[...] = jnp.zeros(n_s.shape, F32)
        m_s[...] = jnp.zeros(m_s.shape, F32)

    qk = jax.nn.silu(_causal_conv(qk_ref[...], xe_s, cw_ref, cb_ref))

    lane = lax.broadcasted_iota(jnp.int32, (t, LANES), 1)
    row_in = lax.broadcasted_iota(jnp.int32, (t, LANES), 0) & (L - 1)
    gs = g_ref[...] + gb_ref[...]
    log_sig = jnp.minimum(gs, 0.0) - jnp.log1p(jnp.exp(-jnp.abs(gs)))
    pre = jnp.where(lane < H, gs, log_sig)
    cum = pre
    d = 1
    while d < L:
        cum = cum + jnp.where(row_in >= d, pltpu.roll(cum, d, 0), 0.0)
        d *= 2
    comb = jnp.where(lane < H, pre, cum)
    comb_t = comb.T

    causal = (lax.broadcasted_iota(jnp.int32, (L, L), 0) >= lax.broadcasted_iota(jnp.int32, (L, L), 1))
    for c in range(t // L):
        r0 = c * L
        for h in range(H):
            q_ = qk[r0:r0 + L, h * D:(h + 1) * D]
            k_ = qk[r0:r0 + L, (H + h) * D:(H + h + 1) * D] * (D ** -0.5)
            v_ = v_ref[r0:r0 + L, h * D:(h + 1) * D]
            li_col = comb[r0:r0 + L, h:h + 1]
            b_col = comb[r0:r0 + L, H + h:H + h + 1]
            li_row = comb_t[h:h + 1, r0:r0 + L]
            b_row = comb_t[H + h:H + h + 1, r0:r0 + L]
            m_prev = m_s[h:h + 1, 0:1]
            c_prev = c_s[h]
            n_prev = n_s[h:h + 1, :]

            dmat = jnp.where(causal, b_col - b_row + li_row, -jnp.inf)
            inter = b_col + m_prev
            m_t = jnp.maximum(inter, jnp.max(dmat, axis=-1, keepdims=True))
            w = _dot_nt(_mx(q_), _mx(k_)) * jnp.exp(dmat - m_t)
            prev = jnp.exp(inter - m_t)
            num = prev * _dot(_mx(q_), _mx(c_prev)) + _dot(_mx(w), _mx(v_))
            den = prev * jnp.sum(q_ * n_prev, axis=-1, keepdims=True) + jnp.sum(w, axis=-1, keepdims=True)
            out = num / jnp.maximum(jnp.abs(den), jnp.exp(-m_t))

            b_last = b_row[:, L - 1:L]
            m_new = jnp.maximum(b_last + m_prev, jnp.max(b_last - b_row + li_row, axis=-1, keepdims=True))
            wk = jnp.exp(b_last - b_col + li_col - m_new)
            keep = jnp.exp(b_last + m_prev - m_new)
            kw_ = wk * k_
            c_s[h] = keep * c_prev + _dot_tn(_mx(kw_), _mx(v_))
            n_s[h:h + 1, :] = keep * n_prev + jnp.sum(kw_, axis=0, keepdims=True)
            m_s[h:h + 1, :] = jnp.broadcast_to(m_new, (1, LANES))

            cols = slice(h * D, (h + 1) * D)
            o_ref[r0:r0 + L, cols] = (jax.nn.sigmoid(og_ref[r0:r0 + L, cols]) * out
                                      * jax.nn.silu(z_ref[r0:r0 + L, cols]))


def _mlstm(u, conv_w, conv_b, gate_bias, t=256):
    b, s, _ = u.shape
    w = MLSTM_WIDTH
    whole = lambda bi, i: (0, 0)
    return pl.pallas_call(
        _mlstm_kernel,
        grid=(b, s // t),
        in_specs=[pl.BlockSpec((None, t, 2 * w), lambda bi, i: (bi, i, R_QK // (2 * w))),
                  pl.BlockSpec((None, t, w), lambda bi, i: (bi, i, R_V // w)),
                  pl.BlockSpec((None, t, LANES), lambda bi, i: (bi, i, R_GATE // LANES)),
                  pl.BlockSpec((None, t, w), lambda bi, i: (bi, i, R_O // w)),
                  pl.BlockSpec((None, t, w), lambda bi, i: (bi, i, R_Z // w)),
                  pl.BlockSpec((CONV_WIDTH, 2 * w), whole), pl.BlockSpec((1, 2 * w), whole),
                  pl.BlockSpec((1, LANES), whole)],
        out_specs=pl.BlockSpec((None, t, w), lambda bi, i: (bi, i, 0)),
        out_shape=jax.ShapeDtypeStruct((b, s, w), F32),
        scratch_shapes=[pltpu.VMEM((t + 8, 2 * w), F32),
                        pltpu.VMEM((MLSTM_HEADS, MLSTM_DIM, MLSTM_DIM), F32),
                        pltpu.VMEM((8, MLSTM_DIM), F32), pltpu.VMEM((8, LANES), F32)],
        compiler_params=_params(2),
        name="mlstm",
    )(u, u, u, u, u, conv_w, conv_b.reshape(1, 2 * w), gate_bias)


def _pad_heads(w, n_heads, hi, scale=1.0):
    d = w.shape[0]
    w = (w * scale).reshape(d, n_heads, HEAD_DIM)
    z = jnp.zeros_like(w)
    hi = jnp.asarray(hi)[None, :, None]
    lo_half = jnp.where(hi, z, w)
    hi_half = jnp.where(hi, w, z)
    return jnp.concatenate([lo_half, hi_half], axis=-1).reshape(d, n_heads * LANES)


def _attn_weights(w_in):
    d = w_in.shape[0]
    widths = (512, 768, 24, 512, 512, 64, 64, 256, 64, 4, 512)
    offs = np.concatenate([[0], np.cumsum(widths)])
    a_q, a_kv, a_g, a_z, b_q, b_k, b_v, b_qi, b_ki, b_wi, b_z = [
        w_in[:, offs[k]:offs[k + 1]] for k in range(len(widths))]
    scale = HEAD_DIM ** -0.5
    zeros = lambda n: jnp.zeros((d, n), w_in.dtype)
    cols = [
        _pad_heads(a_q, NSA_HEADS, np.arange(NSA_HEADS) >= NSA_HPG, scale),
        _pad_heads(b_q, DSA_HEADS, np.zeros(DSA_HEADS, bool), scale),
        a_z, b_z,
        _pad_heads(b_qi, IDX_HEADS, np.zeros(IDX_HEADS, bool)),
        a_kv,
        a_g, zeros(LANES - 24),
        b_k, b_v,
        b_ki, zeros(LANES - IDX_DIM),
        b_wi * (IDX_DIM ** -0.5 * IDX_HEADS ** -0.5), zeros(LANES - IDX_HEADS),
    ]
    w = jnp.concatenate(cols, axis=1)
    assert w.shape[1] == A_TOTAL
    return _mx(w)


def _rec_weights(w_in):
    d = w_in.shape[0]
    widths = (512, 512, 512, 512, 512, 4, 4, 512, 512)
    offs = np.concatenate([[0], np.cumsum(widths)])
    c_x, c_z, d_q, d_k, d_v, d_i, d_f, d_o, d_z = [w_in[:, offs[k]:offs[k + 1]] for k in range(len(widths))]
    w = jnp.concatenate([c_x, c_z, d_q, d_k, d_v, d_o, d_z, d_i, d_f,
                         jnp.zeros((d, R_TOTAL - R_GATE - 2 * MLSTM_HEADS), w_in.dtype)], axis=1)
    assert w.shape[1] == R_TOTAL
    return _mx(w)


def _block_diag(w):
    g, n, _ = w.shape
    eye = jnp.eye(g, dtype=w.dtype)
    return (eye[:, None, :, None] * w[:, :, None, :]).reshape(g * n, g * n)


def _attention_layer(x2d, b, s, norm_g, w_in, cmp_pos_k, cmp_w1_k, cmp_w2_k, cmp_pos_v, cmp_w1_v, cmp_w2_v,
                     t5_table):
    nq = s // TQ
    n16 = s // CMP_STRIDE
    nb = s // SEL_BLOCK
    u = _norm_proj(x2d, norm_g, _attn_weights(w_in)).reshape(b, s, A_TOTAL)

    half = CMP_STRIDE * HEAD_DIM
    kcvc = u[:, :, A_KV:A_KV + 2 * LANES].reshape(b, n16, CMP_STRIDE, 2, NSA_GROUPS, HEAD_DIM)
    hkv = kcvc.transpose(3, 0, 4, 1, 2, 5).reshape(2, b, NSA_GROUPS, n16, half)
    pos = jnp.stack([cmp_pos_k, cmp_pos_v]).reshape(2, 2, 1, half)
    w1 = _mx(jnp.stack([cmp_w1_k, cmp_w1_v]))
    w2 = jnp.stack([cmp_w2_k, cmp_w2_v])
    zpad = jnp.zeros_like(w2)
    w2 = _mx(jnp.stack([jnp.concatenate([w2, zpad], -1), jnp.concatenate([zpad, w2], -1)], axis=1))
    cmp_kv = _compress(hkv, pos, w1, w2)

    tbl = t5_table.astype(F32)
    tbl_a, tbl_b = tbl[:, :NSA_HEADS], tbl[:, NSA_HEADS:]
    zeros = jnp.zeros((1, NSA_HEADS, TQ, TQ), F32)
    d_a = jnp.concatenate([zeros, jnp.stack(_near_tiles(tbl_a)[::-1])], axis=0)
    d_b = jnp.concatenate([zeros, jnp.stack(_near_tiles(tbl_b)[::-1])], axis=0)
    qi = np.arange(TQ)[:, None]
    kj = np.arange(TQ)[None, :]
    u_tile = jnp.asarray(np.where(kj > qi, 0.0, NEG), F32)
    e_mat = jnp.asarray(np.arange(LANES)[:, None] == (np.arange(s)[None, :] // SEL_BLOCK), MXU_DTYPE)
    ci = np.arange(n16)[None, :]
    sj = np.arange(nb)[:, None]
    ovt = jnp.asarray((ci * CMP_STRIDE < (sj + 1) * SEL_BLOCK) & (ci * CMP_STRIDE + CMP_BLOCK > sj * SEL_BLOCK)
                      & (ci < n16 - 1), MXU_DTYPE)
    tri = jnp.asarray(np.concatenate([kj >= qi, np.ones((TQ, TQ), bool)], axis=1), MXU_DTYPE)

    ya = _nsa(u, cmp_kv, _cmp_bias(tbl_a, nq, n16), d_a, u_tile, e_mat, ovt, min(N_SEL, nb))
    yb = _dsa(u, d_b, tri, min(DSA_TOPK_MAX, s // 4))
    return ya.reshape(b * s, -1), yb.reshape(b * s, -1)


def _recurrent_layer(x2d, b, s, norm_g, w_in, conv_c_w, conv_c_b, wa, ba, wx, bx, lam,
                     conv_d_w, conv_d_b, b_i, b_f):
    u = _norm_proj(x2d, norm_g, _rec_weights(w_in)).reshape(b, s, R_TOTAL)
    yc = _rglru(u, conv_c_w, conv_c_b, _mx(_block_diag(wa)), ba, _mx(_block_diag(wx)), bx, lam)
    gate_bias = jnp.concatenate([b_i, b_f, jnp.zeros((LANES - 2 * MLSTM_HEADS,), F32)]).reshape(1, LANES)
    yd = _mlstm(u, conv_d_w, conv_d_b, gate_bias)
    return yc.reshape(b * s, -1), yd.reshape(b * s, -1)


def kernel(x, p, norm_g, final_g, ple_w, ple_gate_w, t5_table, attn_w_in, attn_w_out, cmp_pos_k, cmp_w1_k, cmp_w2_k, cmp_pos_v, cmp_w1_v, cmp_w2_v, rec_w_in, rec_w_out, lru_conv_w, lru_conv_b, lru_wa, lru_ba, lru_wx, lru_bx, lru_lambda, mlstm_conv_w, mlstm_conv_b, mlstm_b_i, mlstm_b_f):
    b, s, d = x.shape
    depth = p.shape[0]
    x2d = x.reshape(b * s, d)
    for i in range(depth):
        j = i // 2
        if i % 2 == 0:
            ya, yb = _attention_layer(x2d, b, s, norm_g[i], attn_w_in[j], cmp_pos_k[j], cmp_w1_k[j], cmp_w2_k[j],
                                      cmp_pos_v[j], cmp_w1_v[j], cmp_w2_v[j], t5_table)
            w_out = attn_w_out[j]
        else:
            ya, yb = _recurrent_layer(x2d, b, s, norm_g[i], rec_w_in[j], lru_conv_w[j], lru_conv_b[j],
                                      lru_wa[j], lru_ba[j], lru_wx[j], lru_bx[j], lru_lambda[j],
                                      mlstm_conv_w[j], mlstm_conv_b[j], mlstm_b_i[j], mlstm_b_f[j])
            w_out = rec_w_out[j]
        x2d = _out_proj(x2d, ya, yb, p[i].reshape(b * s, -1), w_out, ple_w[i], ple_gate_w[i], final_g,
                        final=(i == depth - 1))
    return x2d.reshape(b, s, d)
```

```python
import functools
import math

import numpy as np
import jax
import jax.numpy as jnp
from jax import lax
from jax.experimental import pallas as pl
from jax.experimental.pallas import tpu as pltpu

F32 = jnp.float32
MXU_DTYPE = jnp.bfloat16

HEAD_DIM = 64
NSA_HEADS = 8
NSA_GROUPS = 2
NSA_HPG = NSA_HEADS // NSA_GROUPS
CMP_BLOCK = 32
CMP_STRIDE = 16
CMP_HIDDEN = 256
SEL_BLOCK = 64
N_SEL = 8
WINDOW = 512
FORCE_BONUS = 1e4
DSA_HEADS = 8
IDX_HEADS = 4
IDX_DIM = 64
DSA_TOPK_MAX = 256
N_BUCKETS = 32
T5_MAX_DIST = 128
LRU_WIDTH = 512
LRU_BLOCKS = 8
CONV_WIDTH = 4
LRU_C = 8.0
MLSTM_HEADS = 4
MLSTM_DIM = 128
MLSTM_WIDTH = MLSTM_HEADS * MLSTM_DIM
MLSTM_CHUNK = 64
RMS_EPS = 1e-6
NEG = -1e30

LANES = 128
TQ = 128
INT_MIN = -2 ** 31
VMEM_LIMIT = 48 * 1024 * 1024

A_NSA_Q = 0
A_DSA_Q = 1024
A_AZ = 2048
A_BZ = 2560
A_IDX_Q = 3072
A_KV = 3584
A_GATE = 4352
A_DSA_KV = 4480
A_IDX_K = 4608
A_IDX_W = 4736
A_TOTAL = 4864

R_CX = 0
R_CZ = 512
R_QK = 1024
R_V = 2048
R_O = 2560
R_Z = 3072
R_GATE = 3584
R_TOTAL = 3840


def _dot(a, b):
    return jnp.dot(a, b, preferred_element_type=F32)


def _dot_nt(a, b):
    return lax.dot_general(a, b, (((1,), (1,)), ((), ())), preferred_element_type=F32)


def _dot_tn(a, b):
    return lax.dot_general(a, b, (((0,), (0,)), ((), ())), preferred_element_type=F32)


def _mx(a):
    return a.astype(MXU_DTYPE)


def _params(n_grid):
    return pltpu.CompilerParams(dimension_semantics=("arbitrary",) * n_grid,
                                vmem_limit_bytes=VMEM_LIMIT)


def _norm_proj_kernel(x_ref, g_ref, w_ref, o_ref):
    x = x_ref[...]
    ms = jnp.mean(x * x, axis=-1, keepdims=True)
    y = x * lax.rsqrt(ms + RMS_EPS) * g_ref[...]
    o_ref[...] = _dot(_mx(y), w_ref[...])


def _norm_proj(x2d, g, w, tm=256):
    m, d = x2d.shape
    n = w.shape[1]
    return pl.pallas_call(
        _norm_proj_kernel,
        grid=(m // tm,),
        in_specs=[pl.BlockSpec((tm, d), lambda i: (i, 0)),
                  pl.BlockSpec((1, d), lambda i: (0, 0)),
                  pl.BlockSpec((d, n), lambda i: (0, 0))],
        out_specs=pl.BlockSpec((tm, n), lambda i: (i, 0)),
        out_shape=jax.ShapeDtypeStruct((m, n), F32),
        compiler_params=_params(1),
        name="norm_proj",
    )(x2d, g.reshape(1, d), w)


def _out_proj_kernel(x_ref, ya_ref, yb_ref, p_ref, wo_ref, pw_ref, gw_ref, fg_ref, o_ref, *, final):
    half = ya_ref.shape[-1]
    y = _dot(_mx(ya_ref[...]), wo_ref[:half, :]) + _dot(_mx(yb_ref[...]), wo_ref[half:, :])
    x1 = x_ref[...] + y
    gate = jax.nn.sigmoid(_dot(_mx(x1), gw_ref[...]))
    x2 = x1 + _dot(_mx(p_ref[...]), pw_ref[...]) * gate
    if final:
        ms = jnp.mean(x2 * x2, axis=-1, keepdims=True)
        x2 = x2 * lax.rsqrt(ms + RMS_EPS) * fg_ref[...]
    o_ref[...] = x2


def _out_proj(x2d, ya, yb, p2d, w_out, ple_w, gate_w, final_g, final, tm=512):
    m, d = x2d.shape
    half = ya.shape[1]
    pd = p2d.shape[1]
    row = lambda i: (i, 0)
    whole = lambda i: (0, 0)
    return pl.pallas_call(
        functools.partial(_out_proj_kernel, final=final),
        grid=(m // tm,),
        in_specs=[pl.BlockSpec((tm, d), row), pl.BlockSpec((tm, half), row),
                  pl.BlockSpec((tm, half), row), pl.BlockSpec((tm, pd), row),
                  pl.BlockSpec((2 * half, d), whole), pl.BlockSpec((pd, d), whole),
                  pl.BlockSpec((d, d), whole), pl.BlockSpec((1, d), whole)],
        out_specs=pl.BlockSpec((tm, d), row),
        out_shape=jax.ShapeDtypeStruct((m, d), F32),
        compiler_params=_params(1),
        name="out_proj",
    )(x2d, ya, yb, p2d, _mx(w_out), _mx(ple_w), _mx(gate_w), final_g.reshape(1, d))


def _bucket_np(n):
    n = np.asarray(n)
    max_exact = N_BUCKETS // 2
    nf = np.maximum(n, 1).astype(np.float32)
    large = max_exact + (np.log(nf / np.float32(max_exact)) / np.float32(math.log(T5_MAX_DIST / max_exact))
                         * np.float32(N_BUCKETS - max_exact)).astype(np.int32)
    large = np.minimum(large, N_BUCKETS - 1)
    return np.where(n < max_exact, n, large)


def _bias_index(dist):
    dist = np.asarray(dist)
    return np.where(dist >= 0, _bucket_np(np.maximum(dist, 0)), N_BUCKETS).astype(np.int32)


def _near_tiles(tbl):
    h = tbl.shape[1]
    far = tbl[N_BUCKETS - 1]
    ext = jnp.concatenate([tbl, jnp.full((1, h), NEG, F32)], axis=0)
    kj = np.arange(TQ)[:, None]
    qi = np.arange(TQ)[None, :]
    d0 = (ext[_bias_index(qi - kj)] - far).transpose(0, 2, 1).reshape(TQ, h * TQ)
    d1 = (ext[_bias_index(TQ + qi - kj)] - far).transpose(0, 2, 1).reshape(TQ, h * TQ)
    return jnp.stack([jnp.zeros_like(d0), d1, d0])


def _cmp_bias(tbl, nq, n16):
    h = tbl.shape[1]
    ext = jnp.concatenate([tbl, jnp.full((1, h), NEG, F32)], axis=0)
    per_tile = TQ // CMP_STRIDE
    off = per_tile * (nq - 1)
    width = n16 + off
    cc = np.arange(width)[:, None] - off
    qi = np.arange(TQ)[None, :]
    g = ext[_bias_index(qi - CMP_STRIDE * cc - (CMP_BLOCK - 1))]
    g = g.transpose(0, 2, 1).reshape(width, h * TQ)
    return jnp.stack([g[off - per_tile * i: off - per_tile * i + n16] for i in range(nq)], axis=0)


def _compress_kernel(h_ref, pos_ref, w1_ref, w2_ref, o_ref):
    n16 = h_ref.shape[1]
    half = h_ref.shape[2]
    acc = None
    for g in range(NSA_GROUPS):
        h = h_ref[g]
        pa = _dot(_mx(h + pos_ref[0]), w1_ref[:half, :])
        pb = _dot(_mx(h + pos_ref[1]), w1_ref[half:, :])
        pre = pa + pltpu.roll(pb, n16 - 1, 0)
        t = _dot(_mx(jax.nn.silu(pre)), w2_ref[g])
        acc = t if acc is None else acc + t
    o_ref[...] = acc


def _compress(hkv, pos, w1, w2):
    _, b, g, n16, half = hkv.shape
    return pl.pallas_call(
        _compress_kernel,
        grid=(2, b),
        in_specs=[pl.BlockSpec((None, None, g, n16, half), lambda k, i: (k, i, 0, 0, 0)),
                  pl.BlockSpec((None, 2, 1, half), lambda k, i: (k, 0, 0, 0)),
                  pl.BlockSpec((None, 2 * half, CMP_HIDDEN), lambda k, i: (k, 0, 0)),
                  pl.BlockSpec((None, g, CMP_HIDDEN, LANES), lambda k, i: (k, 0, 0, 0))],
        out_specs=pl.BlockSpec((None, None, n16, LANES), lambda k, i: (k, i, 0, 0)),
        out_shape=jax.ShapeDtypeStruct((2, b, n16, LANES), F32),
        compiler_params=_params(2),
        name="compress",
    )(hkv, pos, w1, w2)


def _softmax_reset(m_s, l_s, acc_s):
    m_s[...] = jnp.full(m_s.shape, -jnp.inf, F32)
    l_s[...] = jnp.zeros(l_s.shape, F32)
    acc_s[...] = jnp.zeros(acc_s.shape, F32)


def _softmax_step(s, v, m_s, l_s, acc_s):
    m_prev = m_s[...]
    m_new = jnp.maximum(m_prev, jnp.max(s, axis=0, keepdims=True))
    alpha = jnp.exp(m_prev - m_new)
    p = jnp.exp(s - m_new)
    l_s[...] = alpha * l_s[...] + jnp.sum(p, axis=0, keepdims=True)
    acc_s[...] = alpha * acc_s[...] + _dot_tn(v, _mx(p))
    m_s[...] = m_new


def _stack_heads(q_ref, n_heads):
    return jnp.concatenate([q_ref[:, h * LANES:(h + 1) * LANES] for h in range(n_heads)], axis=0)


def _tiles_per_chunk(nq):
    for ch in (4, 2):
        if nq % ch == 0:
            return ch
    return 1


def _nsa_kernel(q_ref, kc_ref, vc_ref, ks_ref, vs_ref, kw_ref, vw_ref, g_ref, z_ref,
                bc_ref, d_ref, u_ref, et_ref, ov_ref, o_ref,
                qs_s, m_s, l_s, acc_s, mb_s, osel_s, *, n_pick, ch):
    i = pl.program_id(1)
    rows = NSA_HEADS * TQ
    nb = ov_ref.shape[0]
    q0 = i * TQ
    half = LANES // NSA_GROUPS

    qs = _mx(_stack_heads(q_ref, NSA_HEADS))
    qs_s[...] = qs

    s = _dot_nt(_mx(kc_ref[...]), qs) + bc_ref[...]
    p = jnp.exp(s - jnp.max(s, axis=0, keepdims=True))
    t_lane = q0 + (lax.broadcasted_iota(jnp.int32, (1, rows), 1) & (TQ - 1))
    any_valid = jnp.where(t_lane >= CMP_BLOCK - 1, 1.0, 0.0)
    p = p * (any_valid / jnp.sum(p, axis=0, keepdims=True))
    oc_t = _dot_tn(_mx(vc_ref[...]), _mx(p))

    blk = lax.broadcasted_iota(jnp.int32, (nb, TQ), 0)
    t = q0 + lax.broadcasted_iota(jnp.int32, (nb, TQ), 1)
    cur = lax.shift_right_logical(t, int(math.log2(SEL_BLOCK)))
    forced = jnp.where(blk == 0, 1.0, jnp.where(blk == cur, 1.0, jnp.where(blk == cur - 1, 1.0, 0.0)))
    admissible = blk * SEL_BLOCK <= t
    ov = ov_ref[...]
    for g in range(NSA_GROUPS):
        c0 = g * NSA_HPG * TQ
        ps = p[:, c0:c0 + TQ]
        for h in range(1, NSA_HPG):
            ps = ps + p[:, c0 + h * TQ:c0 + (h + 1) * TQ]
        hi = _mx(ps)
        lo = _mx(ps - hi.astype(F32))
        imp_t = _dot(ov, hi) + _dot(ov, lo)
        score = jnp.where(admissible, imp_t + FORCE_BONUS * forced, NEG)
        sel = jnp.zeros((nb, TQ), F32)
        for _ in range(n_pick):
            best = jnp.max(score, axis=0, keepdims=True)
            first = jnp.min(jnp.where(score == best, blk, nb), axis=0, keepdims=True)
            hit = blk == first
            sel = jnp.where(hit, 1.0, sel)
            score = jnp.where(hit, -jnp.inf, score)
        keymask = _dot(et_ref[...], _mx(sel))
        mb_s[g] = (keymask - 1.0) * (-NEG)

    _softmax_reset(m_s, l_s, acc_s)

    def sel_block(j0, ntile, near):
        tk = ntile * TQ
        koff = pl.multiple_of(j0 * TQ, TQ)
        s = _dot_nt(_mx(ks_ref[pl.ds(koff, tk), :]), qs_s[...])
        mb0 = mb_s[0, pl.ds(koff, tk), :]
        mb1 = mb_s[1, pl.ds(koff, tk), :]
        s = s + jnp.concatenate([mb0] * NSA_HPG + [mb1] * NSA_HPG, axis=1)
        if near:
            s = s + d_ref[jnp.clip(j0 - (i - 2), 0, 2)]
        _softmax_step(s, _mx(vs_ref[pl.ds(koff, tk), :]), m_s, l_s, acc_s)

    n_far_chunks = jnp.maximum(i - 1, 0) // ch

    def far_body(c, carry):
        sel_block(c * ch, ch, False)
        return carry

    def near_body(j, carry):
        sel_block(j, 1, True)
        return carry

    lax.fori_loop(0, n_far_chunks, far_body, 0)
    lax.fori_loop(n_far_chunks * ch, i + 1, near_body, 0)
    osel_s[...] = acc_s[...] / l_s[...]

    _softmax_reset(m_s, l_s, acc_s)
    n_win = WINDOW // TQ
    for k in range(n_win + 1):
        j = i - n_win + k

        @pl.when(j >= 0)
        def _():
            koff = pl.multiple_of(j * TQ, TQ)
            s = _dot_nt(_mx(kw_ref[pl.ds(koff, TQ), :]), qs_s[...])
            if k == 0:
                s = s + u_ref[...]
            elif k == n_win - 1:
                s = s + d_ref[1]
            elif k == n_win:
                s = s + d_ref[2]
            _softmax_step(s, _mx(vw_ref[pl.ds(koff, TQ), :]), m_s, l_s, acc_s)

    ow_t = acc_s[...] / l_s[...]
    os_t = osel_s[...]

    gate_t = jax.nn.sigmoid(g_ref[...]).T
    for m in range(NSA_HEADS // 2):
        parts = []
        for hh in (2 * m, 2 * m + 1):
            g = hh // NSA_HPG
            r = slice(g * half, (g + 1) * half)
            c = slice(hh * TQ, (hh + 1) * TQ)
            parts.append(gate_t[3 * hh:3 * hh + 1, :] * oc_t[r, c] + gate_t[3 * hh + 1:3 * hh + 2, :] * os_t[r, c]
                         + gate_t[3 * hh + 2:3 * hh + 3, :] * ow_t[r, c])
        slab = jnp.concatenate(parts, axis=0).T
        o_ref[:, m * LANES:(m + 1) * LANES] = slab * jax.nn.silu(z_ref[:, m * LANES:(m + 1) * LANES])


def _nsa(u, cmp_kv, bias_c, d_tiles, u_tile, e_t, ov, n_pick):
    b, s, _ = u.shape
    nq = s // TQ
    n16 = cmp_kv.shape[2]
    nb = ov.shape[0]
    rows = NSA_HEADS * TQ
    seq = lambda col: pl.BlockSpec((None, s, LANES), lambda bi, i, col=col: (bi, 0, col // LANES))
    return pl.pallas_call(
        functools.partial(_nsa_kernel, n_pick=n_pick, ch=_tiles_per_chunk(nq)),
        grid=(b, nq),
        in_specs=[
            pl.BlockSpec((None, TQ, rows), lambda bi, i: (bi, i, A_NSA_Q // rows)),
            pl.BlockSpec((None, None, n16, LANES), lambda bi, i: (0, bi, 0, 0)),
            pl.BlockSpec((None, None, n16, LANES), lambda bi, i: (1, bi, 0, 0)),
            seq(A_KV + 2 * LANES), seq(A_KV + 3 * LANES), seq(A_KV + 4 * LANES), seq(A_KV + 5 * LANES),
            pl.BlockSpec((None, TQ, LANES), lambda bi, i: (bi, i, A_GATE // LANES)),
            pl.BlockSpec((None, TQ, 512), lambda bi, i: (bi, i, A_AZ // 512)),
            pl.BlockSpec((None, n16, rows), lambda bi, i: (i, 0, 0)),
            pl.BlockSpec((3, TQ, rows), lambda bi, i: (0, 0, 0)),
            pl.BlockSpec((TQ, rows), lambda bi, i: (0, 0)),
            pl.BlockSpec((s, nb), lambda bi, i: (0, 0)),
            pl.BlockSpec((nb, n16), lambda bi, i: (0, 0)),
        ],
        out_specs=pl.BlockSpec((None, TQ, 512), lambda bi, i: (bi, i, 0)),
        out_shape=jax.ShapeDtypeStruct((b, s, 512), F32),
        scratch_shapes=[pltpu.VMEM((rows, LANES), MXU_DTYPE),
                        pltpu.VMEM((1, rows), F32), pltpu.VMEM((1, rows), F32),
                        pltpu.VMEM((LANES, rows), F32),
                        pltpu.VMEM((NSA_GROUPS, s, TQ), F32),
                        pltpu.VMEM((LANES, rows), F32)],
        compiler_params=_params(2),
        name="nsa",
    )(u, cmp_kv, cmp_kv, u, u, u, u, u, u, bias_c, d_tiles, u_tile, e_t, ov)


def _dsa_kernel(q_ref, kv_ref, qi_ref, ki_ref, wi_ref, z_ref, d_ref, tri_ref, o_ref,
                qs_s, qis_s, sc_s, m_s, l_s, acc_s, seen_s, *, k_top, ch):
    i = pl.program_id(1)
    qs_s[...] = _mx(_stack_heads(q_ref, DSA_HEADS))
    qis_s[...] = _mx(_stack_heads(qi_ref, IDX_HEADS))
    wi_t = wi_ref[...].T
    w_rows = [jnp.broadcast_to(wi_t[h:h + 1, :], (TQ, TQ)) for h in range(IDX_HEADS)]
    key_idx = lax.broadcasted_iota(jnp.int32, (TQ, TQ), 0)
    q_idx = lax.broadcasted_iota(jnp.int32, (TQ, TQ), 1)
    n_chunks = (i + ch) // ch

    def score_chunk(c, carry):
        koff = pl.multiple_of(c * (ch * TQ), ch * TQ)
        r = jnp.maximum(_dot_nt(_mx(ki_ref[pl.ds(koff, ch * TQ), :]), qis_s[...]), 0.0)
        for t in range(ch):
            jt = c * ch + t
            rt = r[t * TQ:(t + 1) * TQ]
            sc = w_rows[0] * rt[:, 0:TQ]
            for h in range(1, IDX_HEADS):
                sc = sc + w_rows[h] * rt[:, h * TQ:(h + 1) * TQ]
            sc = jnp.where((jt == i) & (key_idx > q_idx), NEG, sc)
            bits = pltpu.bitcast(sc, jnp.int32)
            key = bits ^ (lax.shift_right_arithmetic(bits, 31) & jnp.int32(0x7FFFFFFF))
            key = jnp.where(sc == 0.0, 0, key)
            sc_s[jt] = jnp.where(jt > i, INT_MIN, key)
        return carry

    lax.fori_loop(0, n_chunks, score_chunk, 0)

    def count(pred):
        def body(c, acc):
            for t in range(ch):
                acc = acc + jnp.where(pred(sc_s[c * ch + t]), 1.0, 0.0)
            return acc
        acc = lax.fori_loop(0, n_chunks, body, jnp.zeros((TQ, TQ), F32))
        return jnp.sum(acc, axis=0, keepdims=True)

    kf = float(k_top)
    c0 = count(lambda k: k >= 0)
    thr0 = jnp.where(c0 >= kf, 0, INT_MIN) + jnp.zeros((TQ, TQ), jnp.int32)

    def bisect(it, thr):
        cand = thr | lax.shift_left(jnp.int32(1), 30 - it)
        c = count(lambda k: k >= cand)
        return jnp.where(c >= kf, cand, thr)

    thr = lax.fori_loop(0, 31, bisect, thr0)
    need = kf - count(lambda k: k > thr)

    _softmax_reset(m_s, l_s, acc_s)
    seen_s[...] = jnp.zeros((TQ, TQ), F32)

    def att_block(j0, ntile, near):
        tk = ntile * TQ
        koff = pl.multiple_of(j0 * TQ, TQ)
        masks = []
        for t in range(ntile):
            key = sc_s[j0 + t]
            eq = key == thr
            pt = _dot(tri_ref[...], _mx(jnp.where(eq, 1.0, 0.0)))
            seen = seen_s[...]
            take = jnp.where(eq, jnp.where(seen + pt[:TQ] <= need, 1.0, 0.0), 0.0)
            seen_s[...] = seen + pt[TQ:]
            masks.append(jnp.where(key > thr, 0.0, (take - 1.0) * (-NEG)))
        mb = masks[0] if ntile == 1 else jnp.concatenate(masks, axis=0)
        kv = _mx(kv_ref[pl.ds(koff, tk), :])
        s = _dot_nt(kv, qs_s[...]) + jnp.concatenate([mb] * DSA_HEADS, axis=1)
        if near:
            s = s + d_ref[jnp.clip(j0 - (i - 2), 0, 2)]
        _softmax_step(s, kv, m_s, l_s, acc_s)

    n_far_chunks = jnp.maximum(i - 1, 0) // ch

    def far_body(c, carry):
        att_block(c * ch, ch, False)
        return carry

    def near_body(j, carry):
        att_block(j, 1, True)
        return carry

    lax.fori_loop(0, n_far_chunks, far_body, 0)
    lax.fori_loop(n_far_chunks * ch, i + 1, near_body, 0)

    o_t = acc_s[...] / l_s[...]
    for m in range(DSA_HEADS // 2):
        pair = [o_t[HEAD_DIM:, hh * TQ:(hh + 1) * TQ] for hh in (2 * m, 2 * m + 1)]
        slab = jnp.concatenate(pair, axis=0).T
        o_ref[:, m * LANES:(m + 1) * LANES] = slab * jax.nn.silu(z_ref[:, m * LANES:(m + 1) * LANES])


def _dsa(u, d_tiles, tri, k_top):
    b, s, _ = u.shape
    nq = s // TQ
    rows = DSA_HEADS * TQ
    irows = IDX_HEADS * TQ
    seq = lambda col: pl.BlockSpec((None, s, LANES), lambda bi, i, col=col: (bi, 0, col // LANES))
    return pl.pallas_call(
        functools.partial(_dsa_kernel, k_top=k_top, ch=_tiles_per_chunk(nq)),
        grid=(b, nq),
        in_specs=[
            pl.BlockSpec((None, TQ, rows), lambda bi, i: (bi, i, A_DSA_Q // rows)),
            seq(A_DSA_KV),
            pl.BlockSpec((None, TQ, irows), lambda bi, i: (bi, i, A_IDX_Q // irows)),
            seq(A_IDX_K),
            pl.BlockSpec((None, TQ, LANES), lambda bi, i: (bi, i, A_IDX_W // LANES)),
            pl.BlockSpec((None, TQ, 512), lambda bi, i: (bi, i, A_BZ // 512)),
            pl.BlockSpec((3, TQ, rows), lambda bi, i: (0, 0, 0)),
            pl.BlockSpec((2 * TQ, TQ), lambda bi, i: (0, 0)),
        ],
        out_specs=pl.BlockSpec((None, TQ, 512), lambda bi, i: (bi, i, 0)),
        out_shape=jax.ShapeDtypeStruct((b, s, 512), F32),
        scratch_shapes=[pltpu.VMEM((rows, LANES), MXU_DTYPE),
                        pltpu.VMEM((irows, LANES), MXU_DTYPE),
                        pltpu.VMEM((nq, TQ, TQ), jnp.int32),
                        pltpu.VMEM((1, rows), F32), pltpu.VMEM((1, rows), F32),
                        pltpu.VMEM((LANES, rows), F32),
                        pltpu.VMEM((TQ, TQ), F32)],
        compiler_params=_params(2),
        name="dsa",
    )(u, u, u, u, u, u, d_tiles, tri)


def _causal_conv(x, xe_s, w_ref, b_ref):
    t = x.shape[0]
    xe_s[8:8 + t, :] = x
    y = b_ref[...] + w_ref[CONV_WIDTH - 1:CONV_WIDTH, :] * x
    for k in range(1, CONV_WIDTH):
        y = y + w_ref[CONV_WIDTH - 1 - k:CONV_WIDTH - k, :] * xe_s[8 - k:8 - k + t, :]
    xe_s[0:8, :] = x[t - 8:t, :]
    return y


def _rglru_kernel(x_ref, z_ref, cw_ref, cb_ref, wa_ref, ba_ref, wx_ref, bx_ref, lam_ref, o_ref,
                  xe_s, h_s):
    t, c = x_ref.shape

    @pl.when(pl.program_id(1) == 0)
    def _():
        xe_s[0:8, :] = jnp.zeros((8, c), F32)
        h_s[...] = jnp.zeros(h_s.shape, F32)

    xc = _causal_conv(x_ref[...], xe_s, cw_ref, cb_ref)
    xcm = _mx(xc)
    r = jax.nn.sigmoid(_dot(xcm, wa_ref[...]) + ba_ref[...])
    ig = jax.nn.sigmoid(_dot(xcm, wx_ref[...]) + bx_ref[...])
    nl = -lam_ref[...]
    softplus = jnp.maximum(nl, 0.0) + jnp.log1p(jnp.exp(-jnp.abs(nl)))
    log_a = (-LRU_C * r) * softplus
    a = jnp.exp(log_a)
    bb = jnp.sqrt(-jnp.tanh(log_a) * (a * a + 1.0)) * (ig * xc)
    row = lax.broadcasted_iota(jnp.int32, (t, c), 0)
    d = 1
    while d < t:
        live = row >= d
        a_sh = jnp.where(live, pltpu.roll(a, d, 0), 1.0)
        b_sh = jnp.where(live, pltpu.roll(bb, d, 0), 0.0)
        bb = a * b_sh + bb
        a = a * a_sh
        d *= 2
    h = a * h_s[0:1, :] + bb
    h_s[0:1, :] = h[t - 1:t, :]
    o_ref[...] = h * jax.nn.silu(z_ref[...])


def _rglru(u, conv_w, conv_b, wa_bd, ba, wx_bd, bx, lam, t=256):
    b, s, _ = u.shape
    c = LRU_WIDTH
    whole = lambda bi, i: (0, 0)
    return pl.pallas_call(
        _rglru_kernel,
        grid=(b, s // t),
        in_specs=[pl.BlockSpec((None, t, c), lambda bi, i: (bi, i, R_CX // c)),
                  pl.BlockSpec((None, t, c), lambda bi, i: (bi, i, R_CZ // c)),
                  pl.BlockSpec((CONV_WIDTH, c), whole), pl.BlockSpec((1, c), whole),
                  pl.BlockSpec((c, c), whole), pl.BlockSpec((1, c), whole),
                  pl.BlockSpec((c, c), whole), pl.BlockSpec((1, c), whole),
                  pl.BlockSpec((1, c), whole)],
        out_specs=pl.BlockSpec((None, t, c), lambda bi, i: (bi, i, 0)),
        out_shape=jax.ShapeDtypeStruct((b, s, c), F32),
        scratch_shapes=[pltpu.VMEM((t + 8, c), F32), pltpu.VMEM((8, c), F32)],
        compiler_params=_params(2),
        name="rglru",
    )(u, u, conv_w, conv_b.reshape(1, c), wa_bd, ba.reshape(1, c), wx_bd, bx.reshape(1, c),
      lam.reshape(1, c))


def _mlstm_kernel(qk_ref, v_ref, g_ref, og_ref, z_ref, cw_ref, cb_ref, gb_ref, o_ref,
                  xe_s, c_s, n_s, m_s):
    t = qk_ref.shape[0]
    L = MLSTM_CHUNK
    H = MLSTM_HEADS
    D = MLSTM_DIM

    @pl.when(pl.program_id(1) == 0)
    def _():
        xe_s[0:8, :] = jnp.zeros((8, xe_s.shape[1]), F32)
        c_s[...] = jnp.zeros(c_s.shape, F32)
        n_s[...] = jnp.zeros(n_s.shape, F32)
        m_s[...] = jnp.zeros(m_s.shape, F32)

    qk = jax.nn.silu(_causal_conv(qk_ref[...], xe_s, cw_ref, cb_ref))

    lane = lax.broadcasted_iota(jnp.int32, (t, LANES), 1)
    row_in = lax.broadcasted_iota(jnp.int32, (t, LANES), 0) & (L - 1)
    gs = g_ref[...] + gb_ref[...]
    log_sig = jnp.minimum(gs, 0.0) - jnp.log1p(jnp.exp(-jnp.abs(gs)))
    pre = jnp.where(lane < H, gs, log_sig)
    cum = pre
    d = 1
    while d < L:
        cum = cum + jnp.where(row_in >= d, pltpu.roll(cum, d, 0), 0.0)
        d *= 2
    comb = jnp.where(lane < H, pre, cum)
    comb_t = comb.T

    causal = (lax.broadcasted_iota(jnp.int32, (L, L), 0) >= lax.broadcasted_iota(jnp.int32, (L, L), 1))
    for c in range(t // L):
        r0 = c * L
        for h in range(H):
            q_ = qk[r0:r0 + L, h * D:(h + 1) * D]
            k_ = qk[r0:r0 + L, (H + h) * D:(H + h + 1) * D] * (D ** -0.5)
            v_ = v_ref[r0:r0 + L, h * D:(h + 1) * D]
            li_col = comb[r0:r0 + L, h:h + 1]
            b_col = comb[r0:r0 + L, H + h:H + h + 1]
            li_row = comb_t[h:h + 1, r0:r0 + L]
            b_row = comb_t[H + h:H + h + 1, r0:r0 + L]
            m_prev = m_s[h:h + 1, 0:1]
            c_prev = c_s[h]
            n_prev = n_s[h:h + 1, :]

            dmat = jnp.where(causal, b_col - b_row + li_row, -jnp.inf)
            inter = b_col + m_prev
            m_t = jnp.maximum(inter, jnp.max(dmat, axis=-1, keepdims=True))
            w = _dot_nt(_mx(q_), _mx(k_)) * jnp.exp(dmat - m_t)
            prev = jnp.exp(inter - m_t)
            num = prev * _dot(_mx(q_), _mx(c_prev)) + _dot(_mx(w), _mx(v_))
            den = prev * jnp.sum(q_ * n_prev, axis=-1, keepdims=True) + jnp.sum(w, axis=-1, keepdims=True)
            out = num / jnp.maximum(jnp.abs(den), jnp.exp(-m_t))

            b_last = b_row[:, L - 1:L]
            m_new = jnp.maximum(b_last + m_prev, jnp.max(b_last - b_row + li_row, axis=-1, keepdims=True))
            wk = jnp.exp(b_last - b_col + li_col - m_new)
            keep = jnp.exp(b_last + m_prev - m_new)
            kw_ = wk * k_
            c_s[h] = keep * c_prev + _dot_tn(_mx(kw_), _mx(v_))
            n_s[h:h + 1, :] = keep * n_prev + jnp.sum(kw_, axis=0, keepdims=True)
            m_s[h:h + 1, :] = jnp.broadcast_to(m_new, (1, LANES))

            cols = slice(h * D, (h + 1) * D)
            o_ref[r0:r0 + L, cols] = (jax.nn.sigmoid(og_ref[r0:r0 + L, cols]) * out
                                      * jax.nn.silu(z_ref[r0:r0 + L, cols]))


def _mlstm(u, conv_w, conv_b, gate_bias, t=256):
    b, s, _ = u.shape
    w = MLSTM_WIDTH
    whole = lambda bi, i: (0, 0)
    return pl.pallas_call(
        _mlstm_kernel,
        grid=(b, s // t),
        in_specs=[pl.BlockSpec((None, t, 2 * w), lambda bi, i: (bi, i, R_QK // (2 * w))),
                  pl.BlockSpec((None, t, w), lambda bi, i: (bi, i, R_V // w)),
                  pl.BlockSpec((None, t, LANES), lambda bi, i: (bi, i, R_GATE // LANES)),
                  pl.BlockSpec((None, t, w), lambda bi, i: (bi, i, R_O // w)),
                  pl.BlockSpec((None, t, w), lambda bi, i: (bi, i, R_Z // w)),
                  pl.BlockSpec((CONV_WIDTH, 2 * w), whole), pl.BlockSpec((1, 2 * w), whole),
                  pl.BlockSpec((1, LANES), whole)],
        out_specs=pl.BlockSpec((None, t, w), lambda bi, i: (bi, i, 0)),
        out_shape=jax.ShapeDtypeStruct((b, s, w), F32),
        scratch_shapes=[pltpu.VMEM((t + 8, 2 * w), F32),
                        pltpu.VMEM((MLSTM_HEADS, MLSTM_DIM, MLSTM_DIM), F32),
                        pltpu.VMEM((8, MLSTM_DIM), F32), pltpu.VMEM((8, LANES), F32)],
        compiler_params=_params(2),
        name="mlstm",
    )(u, u, u, u, u, conv_w, conv_b.reshape(1, 2 * w), gate_bias)


def _pad_heads(w, n_heads, hi, scale=1.0):
    d = w.shape[0]
    w = (w * scale).reshape(d, n_heads, HEAD_DIM)
    z = jnp.zeros_like(w)
    hi = jnp.asarray(hi)[None, :, None]
    lo_half = jnp.where(hi, z, w)
    hi_half = jnp.where(hi, w, z)
    return jnp.concatenate([lo_half, hi_half], axis=-1).reshape(d, n_heads * LANES)


def _attn_weights(w_in):
    d = w_in.shape[0]
    widths = (512, 768, 24, 512, 512, 64, 64, 256, 64, 4, 512)
    offs = np.concatenate([[0], np.cumsum(widths)])
    a_q, a_kv, a_g, a_z, b_q, b_k, b_v, b_qi, b_ki, b_wi, b_z = [
        w_in[:, offs[k]:offs[k + 1]] for k in range(len(widths))]
    scale = HEAD_DIM ** -0.5
    zeros = lambda n: jnp.zeros((d, n), w_in.dtype)
    cols = [
        _pad_heads(a_q, NSA_HEADS, np.arange(NSA_HEADS) >= NSA_HPG, scale),
        _pad_heads(b_q, DSA_HEADS, np.zeros(DSA_HEADS, bool), scale),
        a_z, b_z,
        _pad_heads(b_qi, IDX_HEADS, np.zeros(IDX_HEADS, bool)),
        a_kv,
        a_g, zeros(LANES - 24),
        b_k, b_v,
        b_ki, zeros(LANES - IDX_DIM),
        b_wi * (IDX_DIM ** -0.5 * IDX_HEADS ** -0.5), zeros(LANES - IDX_HEADS),
    ]
    w = jnp.concatenate(cols, axis=1)
    assert w.shape[1] == A_TOTAL
    return _mx(w)


def _rec_weights(w_in):
    d = w_in.shape[0]
    widths = (512, 512, 512, 512, 512, 4, 4, 512, 512)
    offs = np.concatenate([[0], np.cumsum(widths)])
    c_x, c_z, d_q, d_k, d_v, d_i, d_f, d_o, d_z = [w_in[:, offs[k]:offs[k + 1]] for k in range(len(widths))]
    w = jnp.concatenate([c_x, c_z, d_q, d_k, d_v, d_o, d_z, d_i, d_f,
                         jnp.zeros((d, R_TOTAL - R_GATE - 2 * MLSTM_HEADS), w_in.dtype)], axis=1)
    assert w.shape[1] == R_TOTAL
    return _mx(w)


def _block_diag(w):
    g, n, _ = w.shape
    eye = jnp.eye(g, dtype=w.dtype)
    return (eye[:, None, :, None] * w[:, :, None, :]).reshape(g * n, g * n)


def _attention_layer(x2d, b, s, norm_g, w_in, cmp_pos_k, cmp_w1_k, cmp_w2_k, cmp_pos_v, cmp_w1_v, cmp_w2_v,
                     t5_table):
    nq = s // TQ
    n16 = s // CMP_STRIDE
    nb = s // SEL_BLOCK
    u = _norm_proj(x2d, norm_g, _attn_weights(w_in)).reshape(b, s, A_TOTAL)

    half = CMP_STRIDE * HEAD_DIM
    kcvc = u[:, :, A_KV:A_KV + 2 * LANES].reshape(b, n16, CMP_STRIDE, 2, NSA_GROUPS, HEAD_DIM)
    hkv = kcvc.transpose(3, 0, 4, 1, 2, 5).reshape(2, b, NSA_GROUPS, n16, half)
    pos = jnp.stack([cmp_pos_k, cmp_pos_v]).reshape(2, 2, 1, half)
    w1 = _mx(jnp.stack([cmp_w1_k, cmp_w1_v]))
    w2 = jnp.stack([cmp_w2_k, cmp_w2_v])
    zpad = jnp.zeros_like(w2)
    w2 = _mx(jnp.stack([jnp.concatenate([w2, zpad], -1), jnp.concatenate([zpad, w2], -1)], axis=1))
    cmp_kv = _compress(hkv, pos, w1, w2)

    tbl = t5_table.astype(F32)
    tbl_a, tbl_b = tbl[:, :NSA_HEADS], tbl[:, NSA_HEADS:]
    kj = np.arange(TQ)[:, None]
    qi = np.arange(TQ)[None, :]
    u_tile = jnp.asarray(np.tile(np.where(kj > qi, 0.0, NEG), (1, NSA_HEADS)), F32)
    e_t = jnp.asarray((np.arange(s)[:, None] // SEL_BLOCK) == np.arange(nb)[None, :], MXU_DTYPE)
    ci = np.arange(n16)[None, :]
    sj = np.arange(nb)[:, None]
    ov = jnp.asarray((ci * CMP_STRIDE < (sj + 1) * SEL_BLOCK) & (ci * CMP_STRIDE + CMP_BLOCK > sj * SEL_BLOCK)
                     & (ci < n16 - 1), MXU_DTYPE)
    tri = jnp.asarray(np.concatenate([qi <= kj, np.ones((TQ, TQ), bool)], axis=0), MXU_DTYPE)

    ya = _nsa(u, cmp_kv, _cmp_bias(tbl_a, nq, n16), _near_tiles(tbl_a), u_tile, e_t, ov, min(N_SEL, nb))
    yb = _dsa(u, _near_tiles(tbl_b), tri, min(DSA_TOPK_MAX, s // 4))
    return ya.reshape(b * s, -1), yb.reshape(b * s, -1)


def _recurrent_layer(x2d, b, s, norm_g, w_in, conv_c_w, conv_c_b, wa, ba, wx, bx, lam,
                     conv_d_w, conv_d_b, b_i, b_f):
    u = _norm_proj(x2d, norm_g, _rec_weights(w_in)).reshape(b, s, R_TOTAL)
    yc = _rglru(u, conv_c_w, conv_c_b, _mx(_block_diag(wa)), ba, _mx(_block_diag(wx)), bx, lam)
    gate_bias = jnp.concatenate([b_i, b_f, jnp.zeros((LANES - 2 * MLSTM_HEADS,), F32)]).reshape(1, LANES)
    yd = _mlstm(u, conv_d_w, conv_d_b, gate_bias)
    return yc.reshape(b * s, -1), yd.reshape(b * s, -1)


def kernel(x, p, norm_g, final_g, ple_w, ple_gate_w, t5_table, attn_w_in, attn_w_out, cmp_pos_k, cmp_w1_k, cmp_w2_k, cmp_pos_v, cmp_w1_v, cmp_w2_v, rec_w_in, rec_w_out, lru_conv_w, lru_conv_b, lru_wa, lru_ba, lru_wx, lru_bx, lru_lambda, mlstm_conv_w, mlstm_conv_b, mlstm_b_i, mlstm_b_f):
    b, s, d = x.shape
    depth = p.shape[0]
    x2d = x.reshape(b * s, d)
    for i in range(depth):
        j = i // 2
        if i % 2 == 0:
            ya, yb = _attention_layer(x2d, b, s, norm_g[i], attn_w_in[j], cmp_pos_k[j], cmp_w1_k[j], cmp_w2_k[j],
                                      cmp_pos_v[j], cmp_w1_v[j], cmp_w2_v[j], t5_table)
            w_out = attn_w_out[j]
        else:
            ya, yb = _recurrent_layer(x2d, b, s, norm_g[i], rec_w_in[j], lru_conv_w[j], lru_conv_b[j],
                                      lru_wa[j], lru_ba[j], lru_wx[j], lru_bx[j], lru_lambda[j],
                                      mlstm_conv_w[j], mlstm_conv_b[j], mlstm_b_i[j], mlstm_b_f[j])
            w_out = rec_w_out[j]
        x2d = _out_proj(x2d, ya, yb, p[i].reshape(b * s, -1), w_out, ple_w[i], ple_gate_w[i], final_g,
                        final=(i == depth - 1))
    return x2d.reshape(b, s, d)
```

```python
import functools
import math

import numpy as np
import jax
import jax.numpy as jnp
from jax import lax
from jax.experimental import pallas as pl
from jax.experimental.pallas import tpu as pltpu

F32 = jnp.float32
MXU_DTYPE = jnp.bfloat16

HEAD_DIM = 64
NSA_HEADS = 8
NSA_GROUPS = 2
NSA_HPG = NSA_HEADS // NSA_GROUPS
CMP_BLOCK = 32
CMP_STRIDE = 16
CMP_HIDDEN = 256
SEL_BLOCK = 64
N_SEL = 8
WINDOW = 512
FORCE_BONUS = 1e4
DSA_HEADS = 8
IDX_HEADS = 4
IDX_DIM = 64
DSA_TOPK_MAX = 256
N_BUCKETS = 32
T5_MAX_DIST = 128
LRU_WIDTH = 512
LRU_BLOCKS = 8
CONV_WIDTH = 4
LRU_C = 8.0
MLSTM_HEADS = 4
MLSTM_DIM = 128
MLSTM_WIDTH = MLSTM_HEADS * MLSTM_DIM
MLSTM_CHUNK = 64
RMS_EPS = 1e-6
NEG = -1e30

LANES = 128
COL_BLOCK = 256
TQ = 128
INT_MIN = -2 ** 31
LOG2E = math.log2(math.e)
VMEM_LIMIT = 48 * 1024 * 1024

A_NSA_Q = 0
A_DSA_Q = 1024
A_AZ = 2048
A_BZ = 2560
A_IDX_Q = 3072
A_KV = 3584
A_GATE = 4352
A_DSA_KV = 4480
A_IDX_K = 4608
A_IDX_W = 4736
A_TOTAL = 4864

R_CX = 0
R_CZ = 512
R_QK = 1024
R_V = 2048
R_O = 2560
R_Z = 3072
R_GATE = 3584
R_TOTAL = 3840


def _dot(a, b):
    return jnp.dot(a, b, preferred_element_type=F32)


def _dot_nt(a, b):
    return lax.dot_general(a, b, (((1,), (1,)), ((), ())), preferred_element_type=F32)


def _dot_tn(a, b):
    return lax.dot_general(a, b, (((0,), (0,)), ((), ())), preferred_element_type=F32)


def _mx(a):
    return a.astype(MXU_DTYPE)


def _params(n_grid):
    return pltpu.CompilerParams(dimension_semantics=("arbitrary",) * n_grid,
                                vmem_limit_bytes=VMEM_LIMIT)


def _norm_proj_kernel(x_ref, g_ref, w_ref, o_ref):
    x = x_ref[...]
    ms = jnp.mean(x * x, axis=-1, keepdims=True)
    y = x * lax.rsqrt(ms + RMS_EPS) * g_ref[...]
    o_ref[...] = _dot(_mx(y), w_ref[...])


def _norm_proj(x2d, g, w, tm=256):
    m, d = x2d.shape
    n = w.shape[1]
    return pl.pallas_call(
        _norm_proj_kernel,
        grid=(m // tm,),
        in_specs=[pl.BlockSpec((tm, d), lambda i: (i, 0)),
                  pl.BlockSpec((1, d), lambda i: (0, 0)),
                  pl.BlockSpec((d, n), lambda i: (0, 0))],
        out_specs=pl.BlockSpec((tm, n), lambda i: (i, 0)),
        out_shape=jax.ShapeDtypeStruct((m, n), F32),
        compiler_params=_params(1),
        name="norm_proj",
    )(x2d, g.reshape(1, d), w)


def _out_proj_kernel(x_ref, ya_ref, yb_ref, p_ref, wo_ref, pw_ref, gw_ref, fg_ref, o_ref, *, final):
    half = ya_ref.shape[-1]
    y = _dot(_mx(ya_ref[...]), wo_ref[:half, :]) + _dot(_mx(yb_ref[...]), wo_ref[half:, :])
    x1 = x_ref[...] + y
    gate = jax.nn.sigmoid(_dot(_mx(x1), gw_ref[...]))
    x2 = x1 + _dot(_mx(p_ref[...]), pw_ref[...]) * gate
    if final:
        ms = jnp.mean(x2 * x2, axis=-1, keepdims=True)
        x2 = x2 * lax.rsqrt(ms + RMS_EPS) * fg_ref[...]
    o_ref[...] = x2


def _out_proj(x2d, ya, yb, p2d, w_out, ple_w, gate_w, final_g, final, tm=512):
    m, d = x2d.shape
    half = ya.shape[1]
    pd = p2d.shape[1]
    row = lambda i: (i, 0)
    whole = lambda i: (0, 0)
    return pl.pallas_call(
        functools.partial(_out_proj_kernel, final=final),
        grid=(m // tm,),
        in_specs=[pl.BlockSpec((tm, d), row), pl.BlockSpec((tm, half), row),
                  pl.BlockSpec((tm, half), row), pl.BlockSpec((tm, pd), row),
                  pl.BlockSpec((2 * half, d), whole), pl.BlockSpec((pd, d), whole),
                  pl.BlockSpec((d, d), whole), pl.BlockSpec((1, d), whole)],
        out_specs=pl.BlockSpec((tm, d), row),
        out_shape=jax.ShapeDtypeStruct((m, d), F32),
        compiler_params=_params(1),
        name="out_proj",
    )(x2d, ya, yb, p2d, _mx(w_out), _mx(ple_w), _mx(gate_w), final_g.reshape(1, d))


def _bucket_np(n):
    n = np.asarray(n)
    max_exact = N_BUCKETS // 2
    nf = np.maximum(n, 1).astype(np.float32)
    large = max_exact + (np.log(nf / np.float32(max_exact)) / np.float32(math.log(T5_MAX_DIST / max_exact))
                         * np.float32(N_BUCKETS - max_exact)).astype(np.int32)
    large = np.minimum(large, N_BUCKETS - 1)
    return np.where(n < max_exact, n, large)


def _bias_index(dist):
    dist = np.asarray(dist)
    return np.where(dist >= 0, _bucket_np(np.maximum(dist, 0)), N_BUCKETS).astype(np.int32)


def _lookup(ext, idx):
    idx = jnp.asarray(idx)[..., None]
    out = jnp.zeros(idx.shape[:-1] + (ext.shape[1],), F32)
    for b in range(ext.shape[0]):
        out = jnp.where(idx == b, ext[b], out)
    return out


def _near_tiles(tbl):
    h = tbl.shape[1]
    far = tbl[N_BUCKETS - 1]
    ext = jnp.concatenate([tbl, jnp.full((1, h), NEG, F32)], axis=0)
    kj = np.arange(TQ)[:, None]
    qi = np.arange(TQ)[None, :]
    d0 = ((_lookup(ext, _bias_index(qi - kj)) - far) * LOG2E).transpose(0, 2, 1).reshape(TQ, h * TQ)
    d1 = ((_lookup(ext, _bias_index(TQ + qi - kj)) - far) * LOG2E).transpose(0, 2, 1).reshape(TQ, h * TQ)
    edge = jnp.asarray(np.tile(np.where(kj > qi, 0.0, NEG), (1, h)), F32)
    return jnp.stack([jnp.zeros_like(d0), d1, d0, jnp.full_like(d0, NEG), edge])


def _cmp_bias(tbl, nq, n16):
    h = tbl.shape[1]
    ext = jnp.concatenate([tbl, jnp.full((1, h), NEG, F32)], axis=0)
    per_tile = TQ // CMP_STRIDE
    off = per_tile * (nq - 1)
    width = n16 + off
    cc = np.arange(width)[:, None] - off
    qi = np.arange(TQ)[None, :]
    g = _lookup(ext, _bias_index(qi - CMP_STRIDE * cc - (CMP_BLOCK - 1))) * LOG2E
    g = g.transpose(0, 2, 1).reshape(width, h * TQ)
    return jnp.stack([g[off - per_tile * i: off - per_tile * i + n16] for i in range(nq)], axis=0)


def _compress_kernel(h_ref, pos_ref, w1_ref, w2_ref, o_ref):
    n16 = h_ref.shape[1]
    half = h_ref.shape[2]
    acc = None
    for g in range(NSA_GROUPS):
        h = h_ref[g]
        pa = _dot(_mx(h + pos_ref[0]), w1_ref[:half, :])
        pb = _dot(_mx(h + pos_ref[1]), w1_ref[half:, :])
        pre = pa + pltpu.roll(pb, n16 - 1, 0)
        t = _dot(_mx(jax.nn.silu(pre)), w2_ref[g])
        acc = t if acc is None else acc + t
    o_ref[...] = acc


def _compress(hkv, pos, w1, w2):
    _, b, g, n16, half = hkv.shape
    return pl.pallas_call(
        _compress_kernel,
        grid=(2, b),
        in_specs=[pl.BlockSpec((None, None, g, n16, half), lambda k, i: (k, i, 0, 0, 0)),
                  pl.BlockSpec((None, 2, 1, half), lambda k, i: (k, 0, 0, 0)),
                  pl.BlockSpec((None, 2 * half, CMP_HIDDEN), lambda k, i: (k, 0, 0)),
                  pl.BlockSpec((None, g, CMP_HIDDEN, LANES), lambda k, i: (k, 0, 0, 0))],
        out_specs=pl.BlockSpec((None, None, n16, LANES), lambda k, i: (k, i, 0, 0)),
        out_shape=jax.ShapeDtypeStruct((2, b, n16, LANES), F32),
        compiler_params=_params(2),
        name="compress",
    )(hkv, pos, w1, w2)


def _softmax_reset(m_s, acc_s):
    m_s[...] = jnp.full(m_s.shape, -jnp.inf, F32)
    acc_s[...] = jnp.zeros(acc_s.shape, F32)


def _softmax_step(score_fn, values, m_s, acc_s, col_block):
    r = m_s.shape[1]
    s_next = score_fn(0, col_block)
    for c0 in range(0, r, col_block):
        cols = slice(c0, c0 + col_block)
        s = s_next
        if c0 + col_block < r:
            s_next = score_fn(c0 + col_block, c0 + 2 * col_block)
        m_prev = m_s[:, cols]
        m_new = jnp.maximum(m_prev, jnp.max(s, axis=0, keepdims=True))
        alpha = jnp.exp2(m_prev - m_new)
        p = _mx(jnp.exp2(s - m_new))
        v = values[c0 * len(values) // r]
        acc_s[:, cols] = alpha * acc_s[:, cols] + _dot_tn(v, p)
        m_s[:, cols] = m_new


def _with_ones(v, ones_hi):
    lane = lax.broadcasted_iota(jnp.int32, v.shape, 1)
    return _mx(jnp.where((lane >= HEAD_DIM) == ones_hi, 1.0, v))


def _near_bias(d_ref, idx_of_tile, j0, ntile, c0, c1):
    tiles = [d_ref[idx_of_tile(j0 + t), :, c0:c1] for t in range(ntile)]
    return tiles[0] if ntile == 1 else jnp.concatenate(tiles, axis=0)


def _stack_heads(q_ref, n_heads):
    return jnp.concatenate([q_ref[:, h * LANES:(h + 1) * LANES] for h in range(n_heads)], axis=0)


def _tiles_per_chunk(nq):
    for ch in (4, 2):
        if nq % ch == 0:
            return ch
    return 1


def _nsa_kernel(q_ref, kc_ref, vc_ref, ks_ref, vs_ref, kw_ref, vw_ref, g_ref, z_ref,
                bc_ref, d_ref, et_ref, ov_ref, o_ref,
                qa_s, m_s, acc_s, osel_s, *, n_pick, ch, n_win_tiles):
    i = pl.program_id(1)
    rows = NSA_HEADS * TQ
    nb = ov_ref.shape[0]
    q0 = i * TQ
    half = LANES // NSA_GROUPS

    qs = _mx(_stack_heads(q_ref, NSA_HEADS))
    qa_s[:, :LANES] = qs

    s = _dot_nt(_mx(kc_ref[...]), qs) + bc_ref[...]
    p = jnp.exp2(s - jnp.max(s, axis=0, keepdims=True))
    t_lane = q0 + (lax.broadcasted_iota(jnp.int32, (1, rows), 1) & (TQ - 1))
    any_valid = jnp.where(t_lane >= CMP_BLOCK - 1, 1.0, 0.0)
    p = p * (any_valid / jnp.sum(p, axis=0, keepdims=True))
    oc_t = _dot_tn(_mx(vc_ref[...]), _mx(p))

    blk = lax.broadcasted_iota(jnp.int32, (nb, TQ), 0)
    t = q0 + lax.broadcasted_iota(jnp.int32, (nb, TQ), 1)
    cur = lax.shift_right_logical(t, int(math.log2(SEL_BLOCK)))
    forced = jnp.where(blk == 0, 1.0, jnp.where(blk == cur, 1.0, jnp.where(blk == cur - 1, 1.0, 0.0)))
    admissible = blk * SEL_BLOCK <= t
    ov = ov_ref[...]
    for g in range(NSA_GROUPS):
        c0 = g * NSA_HPG * TQ
        ps = p[:, c0:c0 + TQ]
        for h in range(1, NSA_HPG):
            ps = ps + p[:, c0 + h * TQ:c0 + (h + 1) * TQ]
        hi = _mx(ps)
        lo = _mx(ps - hi.astype(F32))
        imp_t = _dot(ov, hi) + _dot(ov, lo)
        score = jnp.where(admissible, imp_t + FORCE_BONUS * forced, NEG)
        sel = jnp.zeros((nb, TQ), F32)
        for _ in range(n_pick):
            best = jnp.max(score, axis=0, keepdims=True)
            first = jnp.min(jnp.where(score == best, blk, nb), axis=0, keepdims=True)
            hit = blk == first
            sel = jnp.where(hit, 1.0, sel)
            score = jnp.where(hit, -jnp.inf, score)
        if nb < LANES:
            sel = jnp.concatenate([sel, jnp.ones((LANES - nb, TQ), F32)], axis=0)
        block_mask = _mx((sel.T - 1.0) * (-NEG))
        for h in range(NSA_HPG):
            r0 = (g * NSA_HPG + h) * TQ
            qa_s[r0:r0 + TQ, LANES:] = block_mask

    def values(v_ref, koff, tk):
        v = v_ref[pl.ds(koff, tk), :]
        return [_with_ones(v, True), _with_ones(v, False)]

    _softmax_reset(m_s, acc_s)

    def sel_chunk(c, near):
        tk = ch * TQ
        koff = pl.multiple_of(c * tk, tk)
        k_aug = jnp.concatenate([_mx(ks_ref[pl.ds(koff, tk), :]), et_ref[pl.ds(koff, tk), :]], axis=1)

        def scores(c0, c1):
            s = _dot_nt(k_aug, qa_s[c0:c1, :])
            if near:
                s = s + _near_bias(d_ref, lambda j: jnp.clip(j - (i - 2), 0, 3), c * ch, ch, c0, c1)
            return s

        _softmax_step(scores, values(vs_ref, koff, tk), m_s, acc_s, COL_BLOCK)

    first_near = jnp.maximum(i - 1, 0) // ch

    def far_body(c, carry):
        sel_chunk(c, False)
        return carry

    def near_body(c, carry):
        sel_chunk(c, True)
        return carry

    lax.fori_loop(0, first_near, far_body, 0)
    lax.fori_loop(first_near, i // ch + 1, near_body, 0)
    osel_s[...] = acc_s[...]

    _softmax_reset(m_s, acc_s)
    n_win = WINDOW // TQ
    jw = jnp.maximum(i - (n_win_tiles - 1), 0)
    koff = pl.multiple_of(jw * TQ, TQ)
    tk = n_win_tiles * TQ

    def win_idx(j):
        r = j - (i - n_win)
        return jnp.where(r == 0, 4, jnp.clip(r - (n_win - 2), 0, 3))

    kw = _mx(kw_ref[pl.ds(koff, tk), :])
    _softmax_step(lambda c0, c1: (_dot_nt(kw, qa_s[c0:c1, :LANES])
                                  + _near_bias(d_ref, win_idx, jw, n_win_tiles, c0, c1)),
                  values(vw_ref, koff, tk), m_s, acc_s, COL_BLOCK)
    ow_t = acc_s[...]
    os_t = osel_s[...]

    gate_t = jax.nn.sigmoid(g_ref[...]).T
    for m in range(NSA_HEADS // 2):
        parts = []
        for hh in (2 * m, 2 * m + 1):
            g = hh // NSA_HPG
            r = slice(g * half, (g + 1) * half)
            d = (1 - g) * half
            c = slice(hh * TQ, (hh + 1) * TQ)
            parts.append(gate_t[3 * hh:3 * hh + 1, :] * oc_t[r, c]
                         + gate_t[3 * hh + 1:3 * hh + 2, :] * (os_t[r, c] / os_t[d:d + 1, c])
                         + gate_t[3 * hh + 2:3 * hh + 3, :] * (ow_t[r, c] / ow_t[d:d + 1, c]))
        slab = jnp.concatenate(parts, axis=0).T
        o_ref[:, m * LANES:(m + 1) * LANES] = slab * jax.nn.silu(z_ref[:, m * LANES:(m + 1) * LANES])


def _nsa(u, cmp_kv, bias_c, d_tiles, e_t, ov, n_pick):
    b, s, _ = u.shape
    nq = s // TQ
    n16 = cmp_kv.shape[2]
    nb = ov.shape[0]
    rows = NSA_HEADS * TQ
    n_win_tiles = min(WINDOW // TQ + 1, nq)
    seq = lambda col: pl.BlockSpec((None, s, LANES), lambda bi, i, col=col: (bi, 0, col // LANES))
    return pl.pallas_call(
        functools.partial(_nsa_kernel, n_pick=n_pick, ch=_tiles_per_chunk(nq), n_win_tiles=n_win_tiles),
        grid=(b, nq),
        in_specs=[
            pl.BlockSpec((None, TQ, rows), lambda bi, i: (bi, i, A_NSA_Q // rows)),
            pl.BlockSpec((None, None, n16, LANES), lambda bi, i: (0, bi, 0, 0)),
            pl.BlockSpec((None, None, n16, LANES), lambda bi, i: (1, bi, 0, 0)),
            seq(A_KV + 2 * LANES), seq(A_KV + 3 * LANES), seq(A_KV + 4 * LANES), seq(A_KV + 5 * LANES),
            pl.BlockSpec((None, TQ, LANES), lambda bi, i: (bi, i, A_GATE // LANES)),
            pl.BlockSpec((None, TQ, 512), lambda bi, i: (bi, i, A_AZ // 512)),
            pl.BlockSpec((None, n16, rows), lambda bi, i: (i, 0, 0)),
            pl.BlockSpec(d_tiles.shape, lambda bi, i: (0, 0, 0)),
            pl.BlockSpec((s, LANES), lambda bi, i: (0, 0)),
            pl.BlockSpec((nb, n16), lambda bi, i: (0, 0)),
        ],
        out_specs=pl.BlockSpec((None, TQ, 512), lambda bi, i: (bi, i, 0)),
        out_shape=jax.ShapeDtypeStruct((b, s, 512), F32),
        scratch_shapes=[pltpu.VMEM((rows, 2 * LANES), MXU_DTYPE),
                        pltpu.VMEM((1, rows), F32),
                        pltpu.VMEM((LANES, rows), F32),
                        pltpu.VMEM((LANES, rows), F32)],
        compiler_params=_params(2),
        name="nsa",
    )(u, cmp_kv, cmp_kv, u, u, u, u, u, u, bias_c, d_tiles, e_t, ov)


def _dsa_kernel(q_ref, kv_ref, qi_ref, ki_ref, wi_ref, z_ref, d_ref, tri_ref, o_ref,
                qs_s, qis_s, sc_s, m_s, acc_s, seen_s, *, k_top, ch):
    i = pl.program_id(1)
    qs_s[...] = _mx(_stack_heads(q_ref, DSA_HEADS))
    qis_s[...] = _mx(_stack_heads(qi_ref, IDX_HEADS))
    wi_t = wi_ref[...].T
    w_rows = [jnp.broadcast_to(wi_t[h:h + 1, :], (TQ, TQ)) for h in range(IDX_HEADS)]
    key_idx = lax.broadcasted_iota(jnp.int32, (TQ, TQ), 0)
    q_idx = lax.broadcasted_iota(jnp.int32, (TQ, TQ), 1)
    n_chunks = (i + ch) // ch

    def score_chunk(c, carry):
        koff = pl.multiple_of(c * (ch * TQ), ch * TQ)
        r = jnp.maximum(_dot_nt(_mx(ki_ref[pl.ds(koff, ch * TQ), :]), qis_s[...]), 0.0)
        for t in range(ch):
            jt = c * ch + t
            rt = r[t * TQ:(t + 1) * TQ]
            sc = w_rows[0] * rt[:, 0:TQ]
            for h in range(1, IDX_HEADS):
                sc = sc + w_rows[h] * rt[:, h * TQ:(h + 1) * TQ]
            sc = jnp.where((jt == i) & (key_idx > q_idx), NEG, sc)
            bits = pltpu.bitcast(sc, jnp.int32)
            key = bits ^ (lax.shift_right_arithmetic(bits, 31) & jnp.int32(0x7FFFFFFF))
            key = jnp.where(sc == 0.0, 0, key)
            sc_s[jt] = jnp.where(jt > i, INT_MIN, key)
        return carry

    lax.fori_loop(0, n_chunks, score_chunk, 0)

    def count(pred):
        def body(c, acc):
            for t in range(ch):
                acc = acc + jnp.where(pred(sc_s[c * ch + t]), 1.0, 0.0)
            return acc
        acc = lax.fori_loop(0, n_chunks, body, jnp.zeros((TQ, TQ), F32))
        return jnp.sum(acc, axis=0, keepdims=True)

    kf = float(k_top)
    c0 = count(lambda k: k >= 0)
    thr0 = jnp.where(c0 >= kf, 0, INT_MIN) + jnp.zeros((TQ, TQ), jnp.int32)

    def bisect(it, thr):
        cand = thr | lax.shift_left(jnp.int32(1), 30 - it)
        c = count(lambda k: k >= cand)
        return jnp.where(c >= kf, cand, thr)

    thr = lax.fori_loop(0, 31, bisect, thr0)
    need = kf - count(lambda k: k > thr)

    _softmax_reset(m_s, acc_s)
    seen_s[...] = jnp.zeros((TQ, TQ), F32)

    def att_chunk(c, near):
        tk = ch * TQ
        koff = pl.multiple_of(c * tk, tk)
        masks = []
        for t in range(ch):
            key = sc_s[c * ch + t]
            eq = key == thr
            prefix = _dot(tri_ref[...], _mx(jnp.where(eq, 1.0, 0.0)))
            seen = seen_s[...]
            take = jnp.where(eq, jnp.where(seen + prefix <= need, 1.0, 0.0), 0.0)
            seen_s[...] = seen + prefix[TQ - 1:TQ, :]
            masks.append(jnp.where(key > thr, 0.0, (take - 1.0) * (-NEG)))
        mb = masks[0] if ch == 1 else jnp.concatenate(masks, axis=0)
        mb = jnp.concatenate([mb] * DSA_HEADS, axis=1)
        kv = kv_ref[pl.ds(koff, tk), :]
        kvm = _mx(kv)

        def scores(c0, c1):
            s = _dot_nt(kvm, qs_s[c0:c1, :]) + mb[:, c0:c1]
            if near:
                s = s + _near_bias(d_ref, lambda j: jnp.clip(j - (i - 2), 0, 3), c * ch, ch, c0, c1)
            return s

        _softmax_step(scores, [_with_ones(kv, False)], m_s, acc_s, DSA_HEADS * TQ)

    first_near = jnp.maximum(i - 1, 0) // ch

    def far_body(c, carry):
        att_chunk(c, False)
        return carry

    def near_body(c, carry):
        att_chunk(c, True)
        return carry

    lax.fori_loop(0, first_near, far_body, 0)
    lax.fori_loop(first_near, n_chunks, near_body, 0)

    o_t = acc_s[...]
    o_t = o_t / o_t[0:1, :]
    for m in range(DSA_HEADS // 2):
        pair = [o_t[HEAD_DIM:, hh * TQ:(hh + 1) * TQ] for hh in (2 * m, 2 * m + 1)]
        slab = jnp.concatenate(pair, axis=0).T
        o_ref[:, m * LANES:(m + 1) * LANES] = slab * jax.nn.silu(z_ref[:, m * LANES:(m + 1) * LANES])


def _dsa(u, d_tiles, tri, k_top):
    b, s, _ = u.shape
    nq = s // TQ
    rows = DSA_HEADS * TQ
    irows = IDX_HEADS * TQ
    seq = lambda col: pl.BlockSpec((None, s, LANES), lambda bi, i, col=col: (bi, 0, col // LANES))
    return pl.pallas_call(
        functools.partial(_dsa_kernel, k_top=k_top, ch=_tiles_per_chunk(nq)),
        grid=(b, nq),
        in_specs=[
            pl.BlockSpec((None, TQ, rows), lambda bi, i: (bi, i, A_DSA_Q // rows)),
            seq(A_DSA_KV),
            pl.BlockSpec((None, TQ, irows), lambda bi, i: (bi, i, A_IDX_Q // irows)),
            seq(A_IDX_K),
            pl.BlockSpec((None, TQ, LANES), lambda bi, i: (bi, i, A_IDX_W // LANES)),
            pl.BlockSpec((None, TQ, 512), lambda bi, i: (bi, i, A_BZ // 512)),
            pl.BlockSpec(d_tiles.shape, lambda bi, i: (0, 0, 0)),
            pl.BlockSpec((TQ, TQ), lambda bi, i: (0, 0)),
        ],
        out_specs=pl.BlockSpec((None, TQ, 512), lambda bi, i: (bi, i, 0)),
        out_shape=jax.ShapeDtypeStruct((b, s, 512), F32),
        scratch_shapes=[pltpu.VMEM((rows, LANES), MXU_DTYPE),
                        pltpu.VMEM((irows, LANES), MXU_DTYPE),
                        pltpu.VMEM((nq, TQ, TQ), jnp.int32),
                        pltpu.VMEM((1, rows), F32),
                        pltpu.VMEM((LANES, rows), F32),
                        pltpu.VMEM((TQ, TQ), F32)],
        compiler_params=_params(2),
        name="dsa",
    )(u, u, u, u, u, u, d_tiles, tri)


def _causal_conv(x, xe_s, w_ref, b_ref):
    t = x.shape[0]
    xe_s[8:8 + t, :] = x
    y = b_ref[...] + w_ref[CONV_WIDTH - 1:CONV_WIDTH, :] * x
    for k in range(1, CONV_WIDTH):
        y = y + w_ref[CONV_WIDTH - 1 - k:CONV_WIDTH - k, :] * xe_s[8 - k:8 - k + t, :]
    xe_s[0:8, :] = x[t - 8:t, :]
    return y


def _rglru_kernel(x_ref, z_ref, cw_ref, cb_ref, wa_ref, ba_ref, wx_ref, bx_ref, lam_ref, o_ref,
                  xe_s, h_s):
    t, c = x_ref.shape

    @pl.when(pl.program_id(1) == 0)
    def _():
        xe_s[0:8, :] = jnp.zeros((8, c), F32)
        h_s[...] = jnp.zeros(h_s.shape, F32)

    xc = _causal_conv(x_ref[...], xe_s, cw_ref, cb_ref)
    xcm = _mx(xc)
    r = jax.nn.sigmoid(_dot(xcm, wa_ref[...]) + ba_ref[...])
    ig = jax.nn.sigmoid(_dot(xcm, wx_ref[...]) + bx_ref[...])
    nl = -lam_ref[...]
    softplus = jnp.maximum(nl, 0.0) + jnp.log1p(jnp.exp(-jnp.abs(nl)))
    log_a = (-LRU_C * r) * softplus
    a = jnp.exp(log_a)
    bb = jnp.sqrt(-jnp.tanh(log_a) * (a * a + 1.0)) * (ig * xc)
    row = lax.broadcasted_iota(jnp.int32, (t, c), 0)
    d = 1
    while d < t:
        live = row >= d
        a_sh = jnp.where(live, pltpu.roll(a, d, 0), 1.0)
        b_sh = jnp.where(live, pltpu.roll(bb, d, 0), 0.0)
        bb = a * b_sh + bb
        a = a * a_sh
        d *= 2
    h = a * h_s[0:1, :] + bb
    h_s[0:1, :] = h[t - 1:t, :]
    o_ref[...] = h * jax.nn.silu(z_ref[...])


def _rglru(u, conv_w, conv_b, wa_bd, ba, wx_bd, bx, lam, t=256):
    b, s, _ = u.shape
    c = LRU_WIDTH
    whole = lambda bi, i: (0, 0)
    return pl.pallas_call(
        _rglru_kernel,
        grid=(b, s // t),
        in_specs=[pl.BlockSpec((None, t, c), lambda bi, i: (bi, i, R_CX // c)),
                  pl.BlockSpec((None, t, c), lambda bi, i: (bi, i, R_CZ // c)),
                  pl.BlockSpec((CONV_WIDTH, c), whole), pl.BlockSpec((1, c), whole),
                  pl.BlockSpec((c, c), whole), pl.BlockSpec((1, c), whole),
                  pl.BlockSpec((c, c), whole), pl.BlockSpec((1, c), whole),
                  pl.BlockSpec((1, c), whole)],
        out_specs=pl.BlockSpec((None, t, c), lambda bi, i: (bi, i, 0)),
        out_shape=jax.ShapeDtypeStruct((b, s, c), F32),
        scratch_shapes=[pltpu.VMEM((t + 8, c), F32), pltpu.VMEM((8, c), F32)],
        compiler_params=_params(2),
        name="rglru",
    )(u, u, conv_w, conv_b.reshape(1, c), wa_bd, ba.reshape(1, c), wx_bd, bx.reshape(1, c),
      lam.reshape(1, c))


def _mlstm_kernel(qk_ref, v_ref, g_ref, og_ref, z_ref, cw_ref, cb_ref, gb_ref, o_ref,
                  xe_s, c_s, n_s, m_s):
    t = qk_ref.shape[0]
    L = MLSTM_CHUNK
    H = MLSTM_HEADS
    D = MLSTM_DIM

    @pl.when(pl.program_id(1) == 0)
    def _():
        xe_s[0:8, :] = jnp.zeros((8, xe_s.shape[1]), F32)
        c_s[...] = jnp.zeros(c_s.shape, F32)
        n_s[...] = jnp.zeros(n_s.shape, F32)
        m_s[...] = jnp.zeros(m_s.shape, F32)

    qk = jax.nn.silu(_causal_conv(qk_ref[...], xe_s, cw_ref, cb_ref))

    lane = lax.broadcasted_iota(jnp.int32, (t, LANES), 1)
    row_in = lax.broadcasted_iota(jnp.int32, (t, LANES), 0) & (L - 1)
    gs = g_ref[...] + gb_ref[...]
    log_sig = jnp.minimum(gs, 0.0) - jnp.log1p(jnp.exp(-jnp.abs(gs)))
    pre = jnp.where(lane < H, gs, log_sig)
    cum = pre
    d = 1
    while d < L:
        cum = cum + jnp.where(row_in >= d, pltpu.roll(cum, d, 0), 0.0)
        d *= 2
    comb = jnp.where(lane < H, pre, cum)
    comb_t = comb.T

    causal = (lax.broadcasted_iota(jnp.int32, (L, L), 0) >= lax.broadcasted_iota(jnp.int32, (L, L), 1))
    units = [(c, h) for c in range(t // L) for h in range(H)]
    rows = lambda c: slice(c * L, (c + 1) * L)

    pre = {}
    for c, h in units:
        q_ = qk[rows(c), h * D:(h + 1) * D]
        k_ = qk[rows(c), (H + h) * D:(H + h + 1) * D] * (D ** -0.5)
        v_ = _mx(v_ref[rows(c), h * D:(h + 1) * D])
        li_col = comb[rows(c), h:h + 1]
        b_col = comb[rows(c), H + h:H + h + 1]
        li_row = comb_t[h:h + 1, rows(c)]
        b_row = comb_t[H + h:H + h + 1, rows(c)]
        b_last = b_row[:, L - 1:L]
        dmat = jnp.where(causal, b_col - b_row + li_row, -jnp.inf)
        pre[c, h] = dict(
            q=q_, k=k_, v=v_, b_col=b_col, b_last=b_last, dmat=dmat,
            dmat_max=jnp.max(dmat, axis=-1, keepdims=True),
            decay_col=b_last - b_col + li_col,
            decay_max=jnp.max(b_last - b_row + li_row, axis=-1, keepdims=True),
            qk=_dot_nt(_mx(q_), _mx(k_)))

    m_prev, m_new = {}, {}
    for h in range(H):
        m = m_s[h:h + 1, 0:1]
        for c in range(t // L):
            m_prev[c, h] = m
            m = jnp.maximum(pre[c, h]["b_last"] + m, pre[c, h]["decay_max"])
            m_new[c, h] = m
        m_s[h:h + 1, :] = jnp.broadcast_to(m, (1, LANES))

    mid = {}
    for c, h in units:
        u = pre[c, h]
        inter = u["b_col"] + m_prev[c, h]
        m_t = jnp.maximum(inter, u["dmat_max"])
        w = u["qk"] * jnp.exp(u["dmat"] - m_t)
        kw_ = jnp.exp(u["decay_col"] - m_new[c, h]) * u["k"]
        mid[c, h] = dict(
            m_t=m_t, prev=jnp.exp(inter - m_t),
            wv=_dot(_mx(w), u["v"]), wsum=jnp.sum(w, axis=-1, keepdims=True),
            keep=jnp.exp(u["b_last"] + m_prev[c, h] - m_new[c, h]),
            kv=_dot_tn(_mx(kw_), u["v"]), ksum=jnp.sum(kw_, axis=0, keepdims=True))

    qc, qn = {}, {}
    state = {h: (c_s[h], n_s[h:h + 1, :]) for h in range(H)}
    for c, h in units:
        c_prev, n_prev = state[h]
        qc[c, h] = _dot(_mx(pre[c, h]["q"]), _mx(c_prev))
        qn[c, h] = jnp.sum(pre[c, h]["q"] * n_prev, axis=-1, keepdims=True)
        state[h] = (mid[c, h]["keep"] * c_prev + mid[c, h]["kv"],
                    mid[c, h]["keep"] * n_prev + mid[c, h]["ksum"])
    for h in range(H):
        c_s[h] = state[h][0]
        n_s[h:h + 1, :] = state[h][1]

    for c, h in units:
        v = mid[c, h]
        num = v["prev"] * qc[c, h] + v["wv"]
        den = v["prev"] * qn[c, h] + v["wsum"]
        out = num / jnp.maximum(jnp.abs(den), jnp.exp(-v["m_t"]))
        cols = slice(h * D, (h + 1) * D)
        o_ref[rows(c), cols] = (jax.nn.sigmoid(og_ref[rows(c), cols]) * out
                                * jax.nn.silu(z_ref[rows(c), cols]))


def _mlstm(u, conv_w, conv_b, gate_bias, t=256):
    b, s, _ = u.shape
    w = MLSTM_WIDTH
    whole = lambda bi, i: (0, 0)
    return pl.pallas_call(
        _mlstm_kernel,
        grid=(b, s // t),
        in_specs=[pl.BlockSpec((None, t, 2 * w), lambda bi, i: (bi, i, R_QK // (2 * w))),
                  pl.BlockSpec((None, t, w), lambda bi, i: (bi, i, R_V // w)),
                  pl.BlockSpec((None, t, LANES), lambda bi, i: (bi, i, R_GATE // LANES)),
                  pl.BlockSpec((None, t, w), lambda bi, i: (bi, i, R_O // w)),
                  pl.BlockSpec((None, t, w), lambda bi, i: (bi, i, R_Z // w)),
                  pl.BlockSpec((CONV_WIDTH, 2 * w), whole), pl.BlockSpec((1, 2 * w), whole),
                  pl.BlockSpec((1, LANES), whole)],
        out_specs=pl.BlockSpec((None, t, w), lambda bi, i: (bi, i, 0)),
        out_shape=jax.ShapeDtypeStruct((b, s, w), F32),
        scratch_shapes=[pltpu.VMEM((t + 8, 2 * w), F32),
                        pltpu.VMEM((MLSTM_HEADS, MLSTM_DIM, MLSTM_DIM), F32),
                        pltpu.VMEM((8, MLSTM_DIM), F32), pltpu.VMEM((8, LANES), F32)],
        compiler_params=_params(2),
        name="mlstm",
    )(u, u, u, u, u, conv_w, conv_b.reshape(1, 2 * w), gate_bias)


def _pad_heads(w, n_heads, hi, scale=1.0):
    d = w.shape[0]
    w = (w * scale).reshape(d, n_heads, HEAD_DIM)
    z = jnp.zeros_like(w)
    hi = jnp.asarray(hi)[None, :, None]
    lo_half = jnp.where(hi, z, w)
    hi_half = jnp.where(hi, w, z)
    return jnp.concatenate([lo_half, hi_half], axis=-1).reshape(d, n_heads * LANES)


def _attn_weights(w_in):
    d = w_in.shape[0]
    widths = (512, 768, 24, 512, 512, 64, 64, 256, 64, 4, 512)
    offs = np.concatenate([[0], np.cumsum(widths)])
    a_q, a_kv, a_g, a_z, b_q, b_k, b_v, b_qi, b_ki, b_wi, b_z = [
        w_in[:, offs[k]:offs[k + 1]] for k in range(len(widths))]
    scale = HEAD_DIM ** -0.5 * LOG2E
    zeros = lambda n: jnp.zeros((d, n), w_in.dtype)
    cols = [
        _pad_heads(a_q, NSA_HEADS, np.arange(NSA_HEADS) >= NSA_HPG, scale),
        _pad_heads(b_q, DSA_HEADS, np.zeros(DSA_HEADS, bool), scale),
        a_z, b_z,
        _pad_heads(b_qi, IDX_HEADS, np.zeros(IDX_HEADS, bool)),
        a_kv,
        a_g, zeros(LANES - 24),
        b_k, b_v,
        b_ki, zeros(LANES - IDX_DIM),
        b_wi * (IDX_DIM ** -0.5 * IDX_HEADS ** -0.5), zeros(LANES - IDX_HEADS),
    ]
    w = jnp.concatenate(cols, axis=1)
    assert w.shape[1] == A_TOTAL
    return _mx(w)


def _rec_weights(w_in):
    d = w_in.shape[0]
    widths = (512, 512, 512, 512, 512, 4, 4, 512, 512)
    offs = np.concatenate([[0], np.cumsum(widths)])
    c_x, c_z, d_q, d_k, d_v, d_i, d_f, d_o, d_z = [w_in[:, offs[k]:offs[k + 1]] for k in range(len(widths))]
    w = jnp.concatenate([c_x, c_z, d_q, d_k, d_v, d_o, d_z, d_i, d_f,
                         jnp.zeros((d, R_TOTAL - R_GATE - 2 * MLSTM_HEADS), w_in.dtype)], axis=1)
    assert w.shape[1] == R_TOTAL
    return _mx(w)


def _block_diag(w):
    g, n, _ = w.shape
    eye = jnp.eye(g, dtype=w.dtype)
    return (eye[:, None, :, None] * w[:, :, None, :]).reshape(g * n, g * n)


def _attention_layer(x2d, b, s, norm_g, w_in, cmp_pos_k, cmp_w1_k, cmp_w2_k, cmp_pos_v, cmp_w1_v, cmp_w2_v,
                     t5_table):
    nq = s // TQ
    n16 = s // CMP_STRIDE
    nb = s // SEL_BLOCK
    u = _norm_proj(x2d, norm_g, _attn_weights(w_in)).reshape(b, s, A_TOTAL)

    half = CMP_STRIDE * HEAD_DIM
    kcvc = u[:, :, A_KV:A_KV + 2 * LANES].reshape(b, n16, CMP_STRIDE, 2, NSA_GROUPS, HEAD_DIM)
    hkv = kcvc.transpose(3, 0, 4, 1, 2, 5).reshape(2, b, NSA_GROUPS, n16, half)
    pos = jnp.stack([cmp_pos_k, cmp_pos_v]).reshape(2, 2, 1, half)
    w1 = _mx(jnp.stack([cmp_w1_k, cmp_w1_v]))
    w2 = jnp.stack([cmp_w2_k, cmp_w2_v])
    zpad = jnp.zeros_like(w2)
    w2 = _mx(jnp.stack([jnp.concatenate([w2, zpad], -1), jnp.concatenate([zpad, w2], -1)], axis=1))
    cmp_kv = _compress(hkv, pos, w1, w2)

    tbl = t5_table.astype(F32)
    tbl_a, tbl_b = tbl[:, :NSA_HEADS], tbl[:, NSA_HEADS:]
    kj = np.arange(TQ)[:, None]
    qi = np.arange(TQ)[None, :]
    assert nb <= LANES
    e_t = jnp.asarray((np.arange(s)[:, None] // SEL_BLOCK) == np.arange(LANES)[None, :], MXU_DTYPE)
    ci = np.arange(n16)[None, :]
    sj = np.arange(nb)[:, None]
    ov = jnp.asarray((ci * CMP_STRIDE < (sj + 1) * SEL_BLOCK) & (ci * CMP_STRIDE + CMP_BLOCK > sj * SEL_BLOCK)
                     & (ci < n16 - 1), MXU_DTYPE)
    tri = jnp.asarray(qi <= kj, MXU_DTYPE)

    ya = _nsa(u, cmp_kv, _cmp_bias(tbl_a, nq, n16), _near_tiles(tbl_a), e_t, ov, min(N_SEL, nb))
    yb = _dsa(u, _near_tiles(tbl_b), tri, min(DSA_TOPK_MAX, s // 4))
    return ya.reshape(b * s, -1), yb.reshape(b * s, -1)


def _recurrent_layer(x2d, b, s, norm_g, w_in, conv_c_w, conv_c_b, wa, ba, wx, bx, lam,
                     conv_d_w, conv_d_b, b_i, b_f):
    u = _norm_proj(x2d, norm_g, _rec_weights(w_in)).reshape(b, s, R_TOTAL)
    yc = _rglru(u, conv_c_w, conv_c_b, _mx(_block_diag(wa)), ba, _mx(_block_diag(wx)), bx, lam)
    gate_bias = jnp.concatenate([b_i, b_f, jnp.zeros((LANES - 2 * MLSTM_HEADS,), F32)]).reshape(1, LANES)
    yd = _mlstm(u, conv_d_w, conv_d_b, gate_bias)
    return yc.reshape(b * s, -1), yd.reshape(b * s, -1)


def kernel(x, p, norm_g, final_g, ple_w, ple_gate_w, t5_table, attn_w_in, attn_w_out, cmp_pos_k, cmp_w1_k, cmp_w2_k, cmp_pos_v, cmp_w1_v, cmp_w2_v, rec_w_in, rec_w_out, lru_conv_w, lru_conv_b, lru_wa, lru_ba, lru_wx, lru_bx, lru_lambda, mlstm_conv_w, mlstm_conv_b, mlstm_b_i, mlstm_b_f):
    b, s, d = x.shape
    depth = p.shape[0]
    x2d = x.reshape(b * s, d)
    for i in range(depth):
        j = i // 2
        if i % 2 == 0:
            ya, yb = _attention_layer(x2d, b, s, norm_g[i], attn_w_in[j], cmp_pos_k[j], cmp_w1_k[j], cmp_w2_k[j],
                                      cmp_pos_v[j], cmp_w1_v[j], cmp_w2_v[j], t5_table)
            w_out = attn_w_out[j]
        else:
            ya, yb = _recurrent_layer(x2d, b, s, norm_g[i], rec_w_in[j], lru_conv_w[j], lru_conv_b[j],
                                      lru_wa[j], lru_ba[j], lru_wx[j], lru_bx[j], lru_lambda[j],
                                      mlstm_conv_w[j], mlstm_conv_b[j], mlstm_b_i[j], mlstm_b_f[j])
            w_out = rec_w_out[j]
        x2d = _out_proj(x2d, ya, yb, p[i].reshape(b * s, -1), w_out, ple_w[i], ple_gate_w[i], final_g,
                        final=(i == depth - 1))
    return x2d.reshape(b, s, d)
```

```python
import functools
import math

import numpy as np
import jax
import jax.numpy as jnp
from jax import lax
from jax.experimental import pallas as pl
from jax.experimental.pallas import tpu as pltpu

F32 = jnp.float32
MXU_DTYPE = jnp.bfloat16

HEAD_DIM = 64
NSA_HEADS = 8
NSA_GROUPS = 2
NSA_HPG = NSA_HEADS // NSA_GROUPS
CMP_BLOCK = 32
CMP_STRIDE = 16
CMP_HIDDEN = 256
SEL_BLOCK = 64
N_SEL = 8
WINDOW = 512
FORCE_BONUS = 1e4
DSA_HEADS = 8
IDX_HEADS = 4
IDX_DIM = 64
DSA_TOPK_MAX = 256
N_BUCKETS = 32
T5_MAX_DIST = 128
LRU_WIDTH = 512
LRU_BLOCKS = 8
CONV_WIDTH = 4
LRU_C = 8.0
MLSTM_HEADS = 4
MLSTM_DIM = 128
MLSTM_WIDTH = MLSTM_HEADS * MLSTM_DIM
MLSTM_CHUNK = 64
RMS_EPS = 1e-6
NEG = -1e30

LANES = 128
COL_BLOCK = 256
TQ = 128
INT_MIN = -2 ** 31
LOG2E = math.log2(math.e)
VMEM_LIMIT = 48 * 1024 * 1024

A_NSA_Q = 0
A_DSA_Q = 1024
A_AZ = 2048
A_BZ = 2560
A_IDX_Q = 3072
A_KV = 3584
A_GATE = 4352
A_DSA_KV = 4480
A_IDX_K = 4608
A_IDX_W = 4736
A_TOTAL = 4864

R_CX = 0
R_CZ = 512
R_QK = 1024
R_V = 2048
R_O = 2560
R_Z = 3072
R_GATE = 3584
R_TOTAL = 3840


def _dot(a, b):
    return jnp.dot(a, b, preferred_element_type=F32)


def _dot_nt(a, b):
    return lax.dot_general(a, b, (((1,), (1,)), ((), ())), preferred_element_type=F32)


def _dot_tn(a, b):
    return lax.dot_general(a, b, (((0,), (0,)), ((), ())), preferred_element_type=F32)


def _mx(a):
    return a.astype(MXU_DTYPE)


def _params(n_grid):
    return pltpu.CompilerParams(dimension_semantics=("arbitrary",) * n_grid,
                                vmem_limit_bytes=VMEM_LIMIT)


def _norm_proj_kernel(x_ref, g_ref, w_ref, o_ref):
    x = x_ref[...]
    ms = jnp.mean(x * x, axis=-1, keepdims=True)
    y = x * lax.rsqrt(ms + RMS_EPS) * g_ref[...]
    o_ref[...] = _dot(_mx(y), w_ref[...])


def _norm_proj(x2d, g, w, tm=256):
    m, d = x2d.shape
    n = w.shape[1]
    return pl.pallas_call(
        _norm_proj_kernel,
        grid=(m // tm,),
        in_specs=[pl.BlockSpec((tm, d), lambda i: (i, 0)),
                  pl.BlockSpec((1, d), lambda i: (0, 0)),
                  pl.BlockSpec((d, n), lambda i: (0, 0))],
        out_specs=pl.BlockSpec((tm, n), lambda i: (i, 0)),
        out_shape=jax.ShapeDtypeStruct((m, n), F32),
        compiler_params=_params(1),
        name="norm_proj",
    )(x2d, g.reshape(1, d), w)


def _out_proj_kernel(x_ref, ya_ref, yb_ref, p_ref, wo_ref, pw_ref, gw_ref, fg_ref, o_ref, *, final):
    half = ya_ref.shape[-1]
    y = _dot(_mx(ya_ref[...]), wo_ref[:half, :]) + _dot(_mx(yb_ref[...]), wo_ref[half:, :])
    x1 = x_ref[...] + y
    gate = jax.nn.sigmoid(_dot(_mx(x1), gw_ref[...]))
    x2 = x1 + _dot(_mx(p_ref[...]), pw_ref[...]) * gate
    if final:
        ms = jnp.mean(x2 * x2, axis=-1, keepdims=True)
        x2 = x2 * lax.rsqrt(ms + RMS_EPS) * fg_ref[...]
    o_ref[...] = x2


def _out_proj(x2d, ya, yb, p2d, w_out, ple_w, gate_w, final_g, final, tm=512):
    m, d = x2d.shape
    half = ya.shape[1]
    pd = p2d.shape[1]
    row = lambda i: (i, 0)
    whole = lambda i: (0, 0)
    return pl.pallas_call(
        functools.partial(_out_proj_kernel, final=final),
        grid=(m // tm,),
        in_specs=[pl.BlockSpec((tm, d), row), pl.BlockSpec((tm, half), row),
                  pl.BlockSpec((tm, half), row), pl.BlockSpec((tm, pd), row),
                  pl.BlockSpec((2 * half, d), whole), pl.BlockSpec((pd, d), whole),
                  pl.BlockSpec((d, d), whole), pl.BlockSpec((1, d), whole)],
        out_specs=pl.BlockSpec((tm, d), row),
        out_shape=jax.ShapeDtypeStruct((m, d), F32),
        compiler_params=_params(1),
        name="out_proj",
    )(x2d, ya, yb, p2d, _mx(w_out), _mx(ple_w), _mx(gate_w), final_g.reshape(1, d))


def _bucket_np(n):
    n = np.asarray(n)
    max_exact = N_BUCKETS // 2
    nf = np.maximum(n, 1).astype(np.float32)
    large = max_exact + (np.log(nf / np.float32(max_exact)) / np.float32(math.log(T5_MAX_DIST / max_exact))
                         * np.float32(N_BUCKETS - max_exact)).astype(np.int32)
    large = np.minimum(large, N_BUCKETS - 1)
    return np.where(n < max_exact, n, large)


def _bias_index(dist):
    dist = np.asarray(dist)
    return np.where(dist >= 0, _bucket_np(np.maximum(dist, 0)), N_BUCKETS).astype(np.int32)


def _lookup(ext, idx):
    idx = jnp.asarray(idx)[..., None]
    out = jnp.zeros(idx.shape[:-1] + (ext.shape[1],), F32)
    for b in range(ext.shape[0]):
        out = jnp.where(idx == b, ext[b], out)
    return out


def _near_tiles(tbl):
    h = tbl.shape[1]
    far = tbl[N_BUCKETS - 1]
    ext = jnp.concatenate([tbl, jnp.full((1, h), NEG, F32)], axis=0)
    kj = np.arange(TQ)[:, None]
    qi = np.arange(TQ)[None, :]
    d0 = ((_lookup(ext, _bias_index(qi - kj)) - far) * LOG2E).transpose(0, 2, 1).reshape(TQ, h * TQ)
    d1 = ((_lookup(ext, _bias_index(TQ + qi - kj)) - far) * LOG2E).transpose(0, 2, 1).reshape(TQ, h * TQ)
    edge = jnp.asarray(np.tile(np.where(kj > qi, 0.0, NEG), (1, h)), F32)
    return jnp.stack([jnp.zeros_like(d0), d1, d0, jnp.full_like(d0, NEG), edge])


def _cmp_bias(tbl, nq, n16):
    h = tbl.shape[1]
    ext = jnp.concatenate([tbl, jnp.full((1, h), NEG, F32)], axis=0)
    per_tile = TQ // CMP_STRIDE
    off = per_tile * (nq - 1)
    width = n16 + off
    cc = np.arange(width)[:, None] - off
    qi = np.arange(TQ)[None, :]
    g = _lookup(ext, _bias_index(qi - CMP_STRIDE * cc - (CMP_BLOCK - 1))) * LOG2E
    g = g.transpose(0, 2, 1).reshape(width, h * TQ)
    return jnp.stack([g[off - per_tile * i: off - per_tile * i + n16] for i in range(nq)], axis=0)


def _compress_kernel(x_ref, pos_ref, w1_ref, w2_ref, o_ref):
    n16 = o_ref.shape[0]
    acc = None
    for g in range(NSA_GROUPS):
        pa = pb = None
        for l in range(CMP_STRIDE):
            x = x_ref[pl.ds(l, n16, stride=CMP_STRIDE), :]
            a = _dot(_mx(x + pos_ref[l]), w1_ref[g, l])
            c = _dot(_mx(x + pos_ref[CMP_STRIDE + l]), w1_ref[g, CMP_STRIDE + l])
            pa = a if pa is None else pa + a
            pb = c if pb is None else pb + c
        pre = pa + pltpu.roll(pb, n16 - 1, 0)
        t = _dot(_mx(jax.nn.silu(pre)), w2_ref[g])
        acc = t if acc is None else acc + t
    o_ref[...] = acc


def _compress(u, pos, w1, w2):
    b, s, _ = u.shape
    n16 = s // CMP_STRIDE
    return pl.pallas_call(
        _compress_kernel,
        grid=(2, b),
        in_specs=[pl.BlockSpec((None, s, LANES), lambda k, i: (i, 0, A_KV // LANES + k)),
                  pl.BlockSpec((None, CMP_BLOCK, 1, LANES), lambda k, i: (k, 0, 0, 0)),
                  pl.BlockSpec((None, NSA_GROUPS, CMP_BLOCK, LANES, CMP_HIDDEN), lambda k, i: (k, 0, 0, 0, 0)),
                  pl.BlockSpec((None, NSA_GROUPS, CMP_HIDDEN, LANES), lambda k, i: (k, 0, 0, 0))],
        out_specs=pl.BlockSpec((None, None, n16, LANES), lambda k, i: (k, i, 0, 0)),
        out_shape=jax.ShapeDtypeStruct((2, b, n16, LANES), F32),
        compiler_params=_params(2),
        name="compress",
    )(u, pos, w1, w2)


def _softmax_reset(m_s, acc_s):
    m_s[...] = jnp.full(m_s.shape, -jnp.inf, F32)
    acc_s[...] = jnp.zeros(acc_s.shape, F32)


def _softmax_step(score_fn, values, m_s, acc_s, col_block):
    r = m_s.shape[1]
    s_next = score_fn(0, col_block)
    for c0 in range(0, r, col_block):
        cols = slice(c0, c0 + col_block)
        s = s_next
        if c0 + col_block < r:
            s_next = score_fn(c0 + col_block, c0 + 2 * col_block)
        m_prev = m_s[:, cols]
        m_new = jnp.maximum(m_prev, jnp.max(s, axis=0, keepdims=True))
        alpha = jnp.exp2(m_prev - m_new)
        p = _mx(jnp.exp2(s - m_new))
        v = values[c0 * len(values) // r]
        acc_s[:, cols] = alpha * acc_s[:, cols] + _dot_tn(v, p)
        m_s[:, cols] = m_new


def _with_ones(v, ones_hi):
    lane = lax.broadcasted_iota(jnp.int32, v.shape, 1)
    return _mx(jnp.where((lane >= HEAD_DIM) == ones_hi, 1.0, v))


def _near_bias(d_ref, idx_of_tile, j0, ntile, c0, c1):
    tiles = [d_ref[idx_of_tile(j0 + t), :, c0:c1] for t in range(ntile)]
    return tiles[0] if ntile == 1 else jnp.concatenate(tiles, axis=0)


def _stack_heads(q_ref, n_heads):
    return jnp.concatenate([q_ref[:, h * LANES:(h + 1) * LANES] for h in range(n_heads)], axis=0)


def _tiles_per_chunk(nq):
    for ch in (4, 2):
        if nq % ch == 0:
            return ch
    return 1


def _nsa_kernel(q_ref, kc_ref, vc_ref, ks_ref, vs_ref, kw_ref, vw_ref, g_ref, z_ref,
                bc_ref, d_ref, et_ref, ov_ref, o_ref,
                qa_s, m_s, acc_s, osel_s, *, n_pick, ch, n_win_tiles):
    i = pl.program_id(1)
    rows = NSA_HEADS * TQ
    nb = ov_ref.shape[0]
    q0 = i * TQ
    half = LANES // NSA_GROUPS

    qs = _mx(_stack_heads(q_ref, NSA_HEADS))
    qa_s[:, :LANES] = qs

    s = _dot_nt(_mx(kc_ref[...]), qs) + bc_ref[...]
    p = jnp.exp2(s - jnp.max(s, axis=0, keepdims=True))
    t_lane = q0 + (lax.broadcasted_iota(jnp.int32, (1, rows), 1) & (TQ - 1))
    any_valid = jnp.where(t_lane >= CMP_BLOCK - 1, 1.0, 0.0)
    p = p * (any_valid / jnp.sum(p, axis=0, keepdims=True))
    oc_t = _dot_tn(_mx(vc_ref[...]), _mx(p))

    blk = lax.broadcasted_iota(jnp.int32, (nb, TQ), 0)
    t = q0 + lax.broadcasted_iota(jnp.int32, (nb, TQ), 1)
    cur = lax.shift_right_logical(t, int(math.log2(SEL_BLOCK)))
    forced = jnp.where(blk == 0, 1.0, jnp.where(blk == cur, 1.0, jnp.where(blk == cur - 1, 1.0, 0.0)))
    admissible = blk * SEL_BLOCK <= t
    ov = ov_ref[...]
    for g in range(NSA_GROUPS):
        c0 = g * NSA_HPG * TQ
        ps = p[:, c0:c0 + TQ]
        for h in range(1, NSA_HPG):
            ps = ps + p[:, c0 + h * TQ:c0 + (h + 1) * TQ]
        hi = _mx(ps)
        lo = _mx(ps - hi.astype(F32))
        imp_t = _dot(ov, hi) + _dot(ov, lo)
        score = jnp.where(admissible, imp_t + FORCE_BONUS * forced, NEG)
        sel = jnp.zeros((nb, TQ), F32)
        for _ in range(n_pick):
            best = jnp.max(score, axis=0, keepdims=True)
            first = jnp.min(jnp.where(score == best, blk, nb), axis=0, keepdims=True)
            hit = blk == first
            sel = jnp.where(hit, 1.0, sel)
            score = jnp.where(hit, -jnp.inf, score)
        if nb < LANES:
            sel = jnp.concatenate([sel, jnp.ones((LANES - nb, TQ), F32)], axis=0)
        block_mask = _mx((sel.T - 1.0) * (-NEG))
        for h in range(NSA_HPG):
            r0 = (g * NSA_HPG + h) * TQ
            qa_s[r0:r0 + TQ, LANES:] = block_mask

    def values(v_ref, koff, tk):
        v = v_ref[pl.ds(koff, tk), :]
        return [_with_ones(v, True), _with_ones(v, False)]

    _softmax_reset(m_s, acc_s)

    def sel_chunk(c, near):
        tk = ch * TQ
        koff = pl.multiple_of(c * tk, tk)
        k_aug = jnp.concatenate([_mx(ks_ref[pl.ds(koff, tk), :]), et_ref[pl.ds(koff, tk), :]], axis=1)

        def scores(c0, c1):
            s = _dot_nt(k_aug, qa_s[c0:c1, :])
            if near:
                s = s + _near_bias(d_ref, lambda j: jnp.clip(j - (i - 2), 0, 3), c * ch, ch, c0, c1)
            return s

        _softmax_step(scores, values(vs_ref, koff, tk), m_s, acc_s, COL_BLOCK)

    first_near = jnp.maximum(i - 1, 0) // ch

    def far_body(c, carry):
        sel_chunk(c, False)
        return carry

    def near_body(c, carry):
        sel_chunk(c, True)
        return carry

    lax.fori_loop(0, first_near, far_body, 0)
    lax.fori_loop(first_near, i // ch + 1, near_body, 0)
    osel_s[...] = acc_s[...]

    _softmax_reset(m_s, acc_s)
    n_win = WINDOW // TQ
    jw = jnp.maximum(i - (n_win_tiles - 1), 0)
    koff = pl.multiple_of(jw * TQ, TQ)
    tk = n_win_tiles * TQ

    def win_idx(j):
        r = j - (i - n_win)
        return jnp.where(r == 0, 4, jnp.clip(r - (n_win - 2), 0, 3))

    kw = _mx(kw_ref[pl.ds(koff, tk), :])
    _softmax_step(lambda c0, c1: (_dot_nt(kw, qa_s[c0:c1, :LANES])
                                  + _near_bias(d_ref, win_idx, jw, n_win_tiles, c0, c1)),
                  values(vw_ref, koff, tk), m_s, acc_s, COL_BLOCK)
    ow_t = acc_s[...]
    os_t = osel_s[...]

    gate_t = jax.nn.sigmoid(g_ref[...]).T
    for m in range(NSA_HEADS // 2):
        parts = []
        for hh in (2 * m, 2 * m + 1):
            g = hh // NSA_HPG
            r = slice(g * half, (g + 1) * half)
            d = (1 - g) * half
            c = slice(hh * TQ, (hh + 1) * TQ)
            parts.append(gate_t[3 * hh:3 * hh + 1, :] * oc_t[r, c]
                         + gate_t[3 * hh + 1:3 * hh + 2, :] * (os_t[r, c] / os_t[d:d + 1, c])
                         + gate_t[3 * hh + 2:3 * hh + 3, :] * (ow_t[r, c] / ow_t[d:d + 1, c]))
        slab = jnp.concatenate(parts, axis=0).T
        o_ref[:, m * LANES:(m + 1) * LANES] = slab * jax.nn.silu(z_ref[:, m * LANES:(m + 1) * LANES])


def _nsa(u, cmp_kv, bias_c, d_tiles, e_t, ov, n_pick):
    b, s, _ = u.shape
    nq = s // TQ
    n16 = cmp_kv.shape[2]
    nb = ov.shape[0]
    rows = NSA_HEADS * TQ
    n_win_tiles = min(WINDOW // TQ + 1, nq)
    seq = lambda col: pl.BlockSpec((None, s, LANES), lambda bi, i, col=col: (bi, 0, col // LANES))
    return pl.pallas_call(
        functools.partial(_nsa_kernel, n_pick=n_pick, ch=_tiles_per_chunk(nq), n_win_tiles=n_win_tiles),
        grid=(b, nq),
        in_specs=[
            pl.BlockSpec((None, TQ, rows), lambda bi, i: (bi, i, A_NSA_Q // rows)),
            pl.BlockSpec((None, None, n16, LANES), lambda bi, i: (0, bi, 0, 0)),
            pl.BlockSpec((None, None, n16, LANES), lambda bi, i: (1, bi, 0, 0)),
            seq(A_KV + 2 * LANES), seq(A_KV + 3 * LANES), seq(A_KV + 4 * LANES), seq(A_KV + 5 * LANES),
            pl.BlockSpec((None, TQ, LANES), lambda bi, i: (bi, i, A_GATE // LANES)),
            pl.BlockSpec((None, TQ, 512), lambda bi, i: (bi, i, A_AZ // 512)),
            pl.BlockSpec((None, n16, rows), lambda bi, i: (i, 0, 0)),
            pl.BlockSpec(d_tiles.shape, lambda bi, i: (0, 0, 0)),
            pl.BlockSpec((s, LANES), lambda bi, i: (0, 0)),
            pl.BlockSpec((nb, n16), lambda bi, i: (0, 0)),
        ],
        out_specs=pl.BlockSpec((None, TQ, 512), lambda bi, i: (bi, i, 0)),
        out_shape=jax.ShapeDtypeStruct((b, s, 512), F32),
        scratch_shapes=[pltpu.VMEM((rows, 2 * LANES), MXU_DTYPE),
                        pltpu.VMEM((1, rows), F32),
                        pltpu.VMEM((LANES, rows), F32),
                        pltpu.VMEM((LANES, rows), F32)],
        compiler_params=_params(2),
        name="nsa",
    )(u, cmp_kv, cmp_kv, u, u, u, u, u, u, bias_c, d_tiles, e_t, ov)


def _dsa_kernel(q_ref, kv_ref, qi_ref, ki_ref, wi_ref, z_ref, d_ref, tri_ref, o_ref,
                qs_s, qis_s, sc_s, m_s, acc_s, seen_s, *, k_top, ch):
    i = pl.program_id(1)
    qs_s[...] = _mx(_stack_heads(q_ref, DSA_HEADS))
    qis_s[...] = _mx(_stack_heads(qi_ref, IDX_HEADS))
    wi_t = wi_ref[...].T
    w_rows = [jnp.broadcast_to(wi_t[h:h + 1, :], (TQ, TQ)) for h in range(IDX_HEADS)]
    key_idx = lax.broadcasted_iota(jnp.int32, (TQ, TQ), 0)
    q_idx = lax.broadcasted_iota(jnp.int32, (TQ, TQ), 1)
    n_chunks = (i + ch) // ch

    def score_chunk(c, carry):
        koff = pl.multiple_of(c * (ch * TQ), ch * TQ)
        r = jnp.maximum(_dot_nt(_mx(ki_ref[pl.ds(koff, ch * TQ), :]), qis_s[...]), 0.0)
        for t in range(ch):
            jt = c * ch + t
            rt = r[t * TQ:(t + 1) * TQ]
            sc = w_rows[0] * rt[:, 0:TQ]
            for h in range(1, IDX_HEADS):
                sc = sc + w_rows[h] * rt[:, h * TQ:(h + 1) * TQ]
            sc = jnp.where((jt == i) & (key_idx > q_idx), NEG, sc)
            bits = pltpu.bitcast(sc, jnp.int32)
            key = bits ^ (lax.shift_right_arithmetic(bits, 31) & jnp.int32(0x7FFFFFFF))
            key = jnp.where(sc == 0.0, 0, key)
            sc_s[jt] = jnp.where(jt > i, INT_MIN, key)
        return carry

    lax.fori_loop(0, n_chunks, score_chunk, 0)

    def count(pred):
        def body(c, acc):
            for t in range(ch):
                acc = acc + jnp.where(pred(sc_s[c * ch + t]), 1.0, 0.0)
            return acc
        acc = lax.fori_loop(0, n_chunks, body, jnp.zeros((TQ, TQ), F32))
        return jnp.sum(acc, axis=0, keepdims=True)

    kf = float(k_top)
    c0 = count(lambda k: k >= 0)
    thr0 = jnp.where(c0 >= kf, 0, INT_MIN) + jnp.zeros((TQ, TQ), jnp.int32)

    def bisect(it, thr):
        cand = thr | lax.shift_left(jnp.int32(1), 30 - it)
        c = count(lambda k: k >= cand)
        return jnp.where(c >= kf, cand, thr)

    thr = lax.fori_loop(0, 31, bisect, thr0)
    need = kf - count(lambda k: k > thr)

    _softmax_reset(m_s, acc_s)
    seen_s[...] = jnp.zeros((TQ, TQ), F32)

    def att_chunk(c, near):
        tk = ch * TQ
        koff = pl.multiple_of(c * tk, tk)
        masks = []
        for t in range(ch):
            key = sc_s[c * ch + t]
            eq = key == thr
            prefix = _dot(tri_ref[...], _mx(jnp.where(eq, 1.0, 0.0)))
            seen = seen_s[...]
            take = jnp.where(eq, jnp.where(seen + prefix <= need, 1.0, 0.0), 0.0)
            seen_s[...] = seen + prefix[TQ - 1:TQ, :]
            masks.append(jnp.where(key > thr, 0.0, (take - 1.0) * (-NEG)))
        mb = masks[0] if ch == 1 else jnp.concatenate(masks, axis=0)
        mb = jnp.concatenate([mb] * DSA_HEADS, axis=1)
        kv = kv_ref[pl.ds(koff, tk), :]
        kvm = _mx(kv)

        def scores(c0, c1):
            s = _dot_nt(kvm, qs_s[c0:c1, :]) + mb[:, c0:c1]
            if near:
                s = s + _near_bias(d_ref, lambda j: jnp.clip(j - (i - 2), 0, 3), c * ch, ch, c0, c1)
            return s

        _softmax_step(scores, [_with_ones(kv, False)], m_s, acc_s, DSA_HEADS * TQ)

    first_near = jnp.maximum(i - 1, 0) // ch

    def far_body(c, carry):
        att_chunk(c, False)
        return carry

    def near_body(c, carry):
        att_chunk(c, True)
        return carry

    lax.fori_loop(0, first_near, far_body, 0)
    lax.fori_loop(first_near, n_chunks, near_body, 0)

    o_t = acc_s[...]
    o_t = o_t / o_t[0:1, :]
    for m in range(DSA_HEADS // 2):
        pair = [o_t[HEAD_DIM:, hh * TQ:(hh + 1) * TQ] for hh in (2 * m, 2 * m + 1)]
        slab = jnp.concatenate(pair, axis=0).T
        o_ref[:, m * LANES:(m + 1) * LANES] = slab * jax.nn.silu(z_ref[:, m * LANES:(m + 1) * LANES])


def _dsa(u, d_tiles, tri, k_top):
    b, s, _ = u.shape
    nq = s // TQ
    rows = DSA_HEADS * TQ
    irows = IDX_HEADS * TQ
    seq = lambda col: pl.BlockSpec((None, s, LANES), lambda bi, i, col=col: (bi, 0, col // LANES))
    return pl.pallas_call(
        functools.partial(_dsa_kernel, k_top=k_top, ch=_tiles_per_chunk(nq)),
        grid=(b, nq),
        in_specs=[
            pl.BlockSpec((None, TQ, rows), lambda bi, i: (bi, i, A_DSA_Q // rows)),
            seq(A_DSA_KV),
            pl.BlockSpec((None, TQ, irows), lambda bi, i: (bi, i, A_IDX_Q // irows)),
            seq(A_IDX_K),
            pl.BlockSpec((None, TQ, LANES), lambda bi, i: (bi, i, A_IDX_W // LANES)),
            pl.BlockSpec((None, TQ, 512), lambda bi, i: (bi, i, A_BZ // 512)),
            pl.BlockSpec(d_tiles.shape, lambda bi, i: (0, 0, 0)),
            pl.BlockSpec((TQ, TQ), lambda bi, i: (0, 0)),
        ],
        out_specs=pl.BlockSpec((None, TQ, 512), lambda bi, i: (bi, i, 0)),
        out_shape=jax.ShapeDtypeStruct((b, s, 512), F32),
        scratch_shapes=[pltpu.VMEM((rows, LANES), MXU_DTYPE),
                        pltpu.VMEM((irows, LANES), MXU_DTYPE),
                        pltpu.VMEM((nq, TQ, TQ), jnp.int32),
                        pltpu.VMEM((1, rows), F32),
                        pltpu.VMEM((LANES, rows), F32),
                        pltpu.VMEM((TQ, TQ), F32)],
        compiler_params=_params(2),
        name="dsa",
    )(u, u, u, u, u, u, d_tiles, tri)


def _causal_conv(x, xe_s, w_ref, b_ref):
    t = x.shape[0]
    xe_s[8:8 + t, :] = x
    y = b_ref[...] + w_ref[CONV_WIDTH - 1:CONV_WIDTH, :] * x
    for k in range(1, CONV_WIDTH):
        y = y + w_ref[CONV_WIDTH - 1 - k:CONV_WIDTH - k, :] * xe_s[8 - k:8 - k + t, :]
    xe_s[0:8, :] = x[t - 8:t, :]
    return y


def _rglru_kernel(x_ref, z_ref, cw_ref, cb_ref, wa_ref, ba_ref, wx_ref, bx_ref, lam_ref, o_ref,
                  xe_s, h_s):
    t, c = x_ref.shape

    @pl.when(pl.program_id(1) == 0)
    def _():
        xe_s[0:8, :] = jnp.zeros((8, c), F32)
        h_s[...] = jnp.zeros(h_s.shape, F32)

    xc = _causal_conv(x_ref[...], xe_s, cw_ref, cb_ref)
    xcm = _mx(xc)
    r = jax.nn.sigmoid(_dot(xcm, wa_ref[...]) + ba_ref[...])
    ig = jax.nn.sigmoid(_dot(xcm, wx_ref[...]) + bx_ref[...])
    nl = -lam_ref[...]
    softplus = jnp.maximum(nl, 0.0) + jnp.log1p(jnp.exp(-jnp.abs(nl)))
    log_a = (-LRU_C * r) * softplus
    a = jnp.exp(log_a)
    bb = jnp.sqrt(-jnp.tanh(log_a) * (a * a + 1.0)) * (ig * xc)
    row = lax.broadcasted_iota(jnp.int32, (t, c), 0)
    d = 1
    while d < t:
        live = row >= d
        a_sh = jnp.where(live, pltpu.roll(a, d, 0), 1.0)
        b_sh = jnp.where(live, pltpu.roll(bb, d, 0), 0.0)
        bb = a * b_sh + bb
        a = a * a_sh
        d *= 2
    h = a * h_s[0:1, :] + bb
    h_s[0:1, :] = h[t - 1:t, :]
    o_ref[...] = h * jax.nn.silu(z_ref[...])


def _rglru(u, conv_w, conv_b, wa_bd, ba, wx_bd, bx, lam, t=256):
    b, s, _ = u.shape
    c = LRU_WIDTH
    whole = lambda bi, i: (0, 0)
    return pl.pallas_call(
        _rglru_kernel,
        grid=(b, s // t),
        in_specs=[pl.BlockSpec((None, t, c), lambda bi, i: (bi, i, R_CX // c)),
                  pl.BlockSpec((None, t, c), lambda bi, i: (bi, i, R_CZ // c)),
                  pl.BlockSpec((CONV_WIDTH, c), whole), pl.BlockSpec((1, c), whole),
                  pl.BlockSpec((c, c), whole), pl.BlockSpec((1, c), whole),
                  pl.BlockSpec((c, c), whole), pl.BlockSpec((1, c), whole),
                  pl.BlockSpec((1, c), whole)],
        out_specs=pl.BlockSpec((None, t, c), lambda bi, i: (bi, i, 0)),
        out_shape=jax.ShapeDtypeStruct((b, s, c), F32),
        scratch_shapes=[pltpu.VMEM((t + 8, c), F32), pltpu.VMEM((8, c), F32)],
        compiler_params=_params(2),
        name="rglru",
    )(u, u, conv_w, conv_b.reshape(1, c), wa_bd, ba.reshape(1, c), wx_bd, bx.reshape(1, c),
      lam.reshape(1, c))


def _mlstm_kernel(qk_ref, v_ref, g_ref, og_ref, z_ref, cw_ref, cb_ref, gb_ref, o_ref,
                  xe_s, c_s, m_s):
    t = qk_ref.shape[0]
    L = MLSTM_CHUNK
    H = MLSTM_HEADS
    D = MLSTM_DIM

    @pl.when(pl.program_id(1) == 0)
    def _():
        xe_s[0:8, :] = jnp.zeros((8, xe_s.shape[1]), F32)
        c_s[...] = jnp.zeros(c_s.shape, F32)
        m_s[...] = jnp.zeros(m_s.shape, F32)

    qk = jax.nn.silu(_causal_conv(qk_ref[...], xe_s, cw_ref, cb_ref))

    lane = lax.broadcasted_iota(jnp.int32, (t, LANES), 1)
    row_in = lax.broadcasted_iota(jnp.int32, (t, LANES), 0) & (L - 1)
    gs = g_ref[...] + gb_ref[...]
    log_sig = jnp.minimum(gs, 0.0) - jnp.log1p(jnp.exp(-jnp.abs(gs)))
    pre = jnp.where(lane < H, gs, log_sig)
    cum = pre
    d = 1
    while d < L:
        cum = cum + jnp.where(row_in >= d, pltpu.roll(cum, d, 0), 0.0)
        d *= 2
    comb = jnp.where(lane < H, pre, cum)
    comb_t = comb.T

    causal_t = (lax.broadcasted_iota(jnp.int32, (L, L), 0) <= lax.broadcasted_iota(jnp.int32, (L, L), 1))
    pad_rows = jnp.zeros((7, L), F32)
    for h in range(H):
        v_t = v_ref[:, h * D:(h + 1) * D].T
        state = c_s[h]
        m_prev = m_s[h:h + 1, 0:1]
        outs = []
        for c in range(t // L):
            r = slice(c * L, (c + 1) * L)
            q_ = _mx(qk[r, h * D:(h + 1) * D])
            k_ = _mx(qk[r, (H + h) * D:(H + h + 1) * D] * (D ** -0.5))
            li_row = comb_t[h:h + 1, r]
            b_row = comb_t[H + h:H + h + 1, r]
            lb_col = comb[r, h:h + 1] - comb[r, H + h:H + h + 1]
            b_last = b_row[:, L - 1:L]

            dmat_t = jnp.where(causal_t, b_row + lb_col, -jnp.inf)
            decay_row = b_last - b_row + li_row
            m_new = jnp.maximum(b_last + m_prev, jnp.max(decay_row, axis=-1, keepdims=True))
            inter = b_row + m_prev
            m_t = jnp.maximum(inter, jnp.max(dmat_t, axis=0, keepdims=True))
            w_t = _dot_nt(k_, q_) * jnp.exp(dmat_t - m_t)
            prev = jnp.exp(inter - m_t)

            read = _dot_nt(_mx(state), q_)
            num_t = prev * read[:D] + _dot(_mx(v_t[:, r]), _mx(w_t))
            den = prev * read[D:D + 1] + jnp.sum(w_t, axis=0, keepdims=True)
            outs.append(num_t / jnp.maximum(jnp.abs(den), jnp.exp(-m_t)))

            wk = jnp.exp(decay_row - m_new)
            inc = jnp.concatenate([v_t[:, r] * wk, wk, pad_rows], axis=0)
            state = jnp.exp(b_last + m_prev - m_new) * state + _dot(_mx(inc), k_)
            m_prev = m_new

        c_s[h] = state
        m_s[h:h + 1, :] = jnp.broadcast_to(m_prev, (1, LANES))
        out = jnp.concatenate(outs, axis=1).T
        cols = slice(h * D, (h + 1) * D)
        o_ref[:, cols] = jax.nn.sigmoid(og_ref[:, cols]) * out * jax.nn.silu(z_ref[:, cols])


def _mlstm(u, conv_w, conv_b, gate_bias, t=256):
    b, s, _ = u.shape
    w = MLSTM_WIDTH
    whole = lambda bi, i: (0, 0)
    return pl.pallas_call(
        _mlstm_kernel,
        grid=(b, s // t),
        in_specs=[pl.BlockSpec((None, t, 2 * w), lambda bi, i: (bi, i, R_QK // (2 * w))),
                  pl.BlockSpec((None, t, w), lambda bi, i: (bi, i, R_V // w)),
                  pl.BlockSpec((None, t, LANES), lambda bi, i: (bi, i, R_GATE // LANES)),
                  pl.BlockSpec((None, t, w), lambda bi, i: (bi, i, R_O // w)),
                  pl.BlockSpec((None, t, w), lambda bi, i: (bi, i, R_Z // w)),
                  pl.BlockSpec((CONV_WIDTH, 2 * w), whole), pl.BlockSpec((1, 2 * w), whole),
                  pl.BlockSpec((1, LANES), whole)],
        out_specs=pl.BlockSpec((None, t, w), lambda bi, i: (bi, i, 0)),
        out_shape=jax.ShapeDtypeStruct((b, s, w), F32),
        scratch_shapes=[pltpu.VMEM((t + 8, 2 * w), F32),
                        pltpu.VMEM((MLSTM_HEADS, MLSTM_DIM + 8, MLSTM_DIM), F32),
                        pltpu.VMEM((8, LANES), F32)],
        compiler_params=_params(2),
        name="mlstm",
    )(u, u, u, u, u, conv_w, conv_b.reshape(1, 2 * w), gate_bias)


def _pad_heads(w, n_heads, hi, scale=1.0):
    d = w.shape[0]
    w = (w * scale).reshape(d, n_heads, HEAD_DIM)
    z = jnp.zeros_like(w)
    hi = jnp.asarray(hi)[None, :, None]
    lo_half = jnp.where(hi, z, w)
    hi_half = jnp.where(hi, w, z)
    return jnp.concatenate([lo_half, hi_half], axis=-1).reshape(d, n_heads * LANES)


def _attn_weights(w_in):
    d = w_in.shape[0]
    widths = (512, 768, 24, 512, 512, 64, 64, 256, 64, 4, 512)
    offs = np.concatenate([[0], np.cumsum(widths)])
    a_q, a_kv, a_g, a_z, b_q, b_k, b_v, b_qi, b_ki, b_wi, b_z = [
        w_in[:, offs[k]:offs[k + 1]] for k in range(len(widths))]
    scale = HEAD_DIM ** -0.5 * LOG2E
    zeros = lambda n: jnp.zeros((d, n), w_in.dtype)
    cols = [
        _pad_heads(a_q, NSA_HEADS, np.arange(NSA_HEADS) >= NSA_HPG, scale),
        _pad_heads(b_q, DSA_HEADS, np.zeros(DSA_HEADS, bool), scale),
        a_z, b_z,
        _pad_heads(b_qi, IDX_HEADS, np.zeros(IDX_HEADS, bool)),
        a_kv,
        a_g, zeros(LANES - 24),
        b_k, b_v,
        b_ki, zeros(LANES - IDX_DIM),
        b_wi * (IDX_DIM ** -0.5 * IDX_HEADS ** -0.5), zeros(LANES - IDX_HEADS),
    ]
    w = jnp.concatenate(cols, axis=1)
    assert w.shape[1] == A_TOTAL
    return _mx(w)


def _rec_weights(w_in):
    d = w_in.shape[0]
    widths = (512, 512, 512, 512, 512, 4, 4, 512, 512)
    offs = np.concatenate([[0], np.cumsum(widths)])
    c_x, c_z, d_q, d_k, d_v, d_i, d_f, d_o, d_z = [w_in[:, offs[k]:offs[k + 1]] for k in range(len(widths))]
    w = jnp.concatenate([c_x, c_z, d_q, d_k, d_v, d_o, d_z, d_i, d_f,
                         jnp.zeros((d, R_TOTAL - R_GATE - 2 * MLSTM_HEADS), w_in.dtype)], axis=1)
    assert w.shape[1] == R_TOTAL
    return _mx(w)


def _block_diag(w):
    g, n, _ = w.shape
    eye = jnp.eye(g, dtype=w.dtype)
    return (eye[:, None, :, None] * w[:, :, None, :]).reshape(g * n, g * n)


def _attention_layer(x2d, b, s, norm_g, w_in, cmp_pos_k, cmp_w1_k, cmp_w2_k, cmp_pos_v, cmp_w1_v, cmp_w2_v,
                     t5_table):
    nq = s // TQ
    n16 = s // CMP_STRIDE
    nb = s // SEL_BLOCK
    u = _norm_proj(x2d, norm_g, _attn_weights(w_in)).reshape(b, s, A_TOTAL)

    pos = jnp.stack([cmp_pos_k, cmp_pos_v])
    pos = jnp.concatenate([pos] * NSA_GROUPS, axis=-1).reshape(2, CMP_BLOCK, 1, LANES)
    w1 = jnp.stack([cmp_w1_k, cmp_w1_v]).reshape(2, CMP_BLOCK, HEAD_DIM, CMP_HIDDEN)
    z1 = jnp.zeros_like(w1)
    w1 = _mx(jnp.stack([jnp.concatenate([w1, z1], axis=2), jnp.concatenate([z1, w1], axis=2)], axis=1))
    w2 = jnp.stack([cmp_w2_k, cmp_w2_v])
    zpad = jnp.zeros_like(w2)
    w2 = _mx(jnp.stack([jnp.concatenate([w2, zpad], -1), jnp.concatenate([zpad, w2], -1)], axis=1))
    cmp_kv = _compress(u, pos, w1, w2)

    tbl = t5_table.astype(F32)
    tbl_a, tbl_b = tbl[:, :NSA_HEADS], tbl[:, NSA_HEADS:]
    kj = np.arange(TQ)[:, None]
    qi = np.arange(TQ)[None, :]
    assert nb <= LANES
    e_t = jnp.asarray((np.arange(s)[:, None] // SEL_BLOCK) == np.arange(LANES)[None, :], MXU_DTYPE)
    ci = np.arange(n16)[None, :]
    sj = np.arange(nb)[:, None]
    ov = jnp.asarray((ci * CMP_STRIDE < (sj + 1) * SEL_BLOCK) & (ci * CMP_STRIDE + CMP_BLOCK > sj * SEL_BLOCK)
                     & (ci < n16 - 1), MXU_DTYPE)
    tri = jnp.asarray(qi <= kj, MXU_DTYPE)

    ya = _nsa(u, cmp_kv, _cmp_bias(tbl_a, nq, n16), _near_tiles(tbl_a), e_t, ov, min(N_SEL, nb))
    yb = _dsa(u, _near_tiles(tbl_b), tri, min(DSA_TOPK_MAX, s // 4))
    return ya.reshape(b * s, -1), yb.reshape(b * s, -1)


def _recurrent_layer(x2d, b, s, norm_g, w_in, conv_c_w, conv_c_b, wa, ba, wx, bx, lam,
                     conv_d_w, conv_d_b, b_i, b_f):
    u = _norm_proj(x2d, norm_g, _rec_weights(w_in)).reshape(b, s, R_TOTAL)
    yc = _rglru(u, conv_c_w, conv_c_b, _mx(_block_diag(wa)), ba, _mx(_block_diag(wx)), bx, lam)
    gate_bias = jnp.concatenate([b_i, b_f, jnp.zeros((LANES - 2 * MLSTM_HEADS,), F32)]).reshape(1, LANES)
    yd = _mlstm(u, conv_d_w, conv_d_b, gate_bias)
    return yc.reshape(b * s, -1), yd.reshape(b * s, -1)


def kernel(x, p, norm_g, final_g, ple_w, ple_gate_w, t5_table, attn_w_in, attn_w_out, cmp_pos_k, cmp_w1_k, cmp_w2_k, cmp_pos_v, cmp_w1_v, cmp_w2_v, rec_w_in, rec_w_out, lru_conv_w, lru_conv_b, lru_wa, lru_ba, lru_wx, lru_bx, lru_lambda, mlstm_conv_w, mlstm_conv_b, mlstm_b_i, mlstm_b_f):
    b, s, d = x.shape
    depth = p.shape[0]
    x2d = x.reshape(b * s, d)
    for i in range(depth):
        j = i // 2
        if i % 2 == 0:
            ya, yb = _attention_layer(x2d, b, s, norm_g[i], attn_w_in[j], cmp_pos_k[j], cmp_w1_k[j], cmp_w2_k[j],
                                      cmp_pos_v[j], cmp_w1_v[j], cmp_w2_v[j], t5_table)
            w_out = attn_w_out[j]
        else:
            ya, yb = _recurrent_layer(x2d, b, s, norm_g[i], rec_w_in[j], lru_conv_w[j], lru_conv_b[j],
                                      lru_wa[j], lru_ba[j], lru_wx[j], lru_bx[j], lru_lambda[j],
                                      mlstm_conv_w[j], mlstm_conv_b[j], mlstm_b_i[j], mlstm_b_f[j])
            w_out = rec_w_out[j]
        x2d = _out_proj(x2d, ya, yb, p[i].reshape(b * s, -1), w_out, ple_w[i], ple_gate_w[i], final_g,
                        final=(i == depth - 1))
    return x2d.reshape(b, s, d)
```

```python
import functools
import math

import numpy as np
import jax
import jax.numpy as jnp
from jax import lax
from jax.experimental import pallas as pl
from jax.experimental.pallas import tpu as pltpu

F32 = jnp.float32
MXU_DTYPE = jnp.bfloat16

HEAD_DIM = 64
NSA_HEADS = 8
NSA_GROUPS = 2
NSA_HPG = NSA_HEADS // NSA_GROUPS
CMP_BLOCK = 32
CMP_STRIDE = 16
CMP_HIDDEN = 256
SEL_BLOCK = 64
N_SEL = 8
WINDOW = 512
FORCE_BONUS = 1e4
DSA_HEADS = 8
IDX_HEADS = 4
IDX_DIM = 64
DSA_TOPK_MAX = 256
N_BUCKETS = 32
T5_MAX_DIST = 128
LRU_WIDTH = 512
LRU_BLOCKS = 8
CONV_WIDTH = 4
LRU_C = 8.0
MLSTM_HEADS = 4
MLSTM_DIM = 128
MLSTM_WIDTH = MLSTM_HEADS * MLSTM_DIM
MLSTM_CHUNK = 64
RMS_EPS = 1e-6
NEG = -1e30

LANES = 128
COL_BLOCK = 256
TQ = 128
INT_MIN = -2 ** 31
LOG2E = math.log2(math.e)
VMEM_LIMIT = 48 * 1024 * 1024

A_NSA_Q = 0
A_DSA_Q = 512
A_AZ = 1024
A_BZ = 1536
A_IDX_Q = 2048
A_KV = 2304
A_GATE = 3072
A_DSA_KV = 3200
A_IDX_K = 3328
A_IDX_W = 3456
A_TOTAL = 3584

R_CX = 0
R_CZ = 512
R_QK = 1024
R_V = 2048
R_O = 2560
R_Z = 3072
R_GATE = 3584
R_TOTAL = 3840


def _dot(a, b):
    return jnp.dot(a, b, preferred_element_type=F32)


def _dot_nt(a, b):
    return lax.dot_general(a, b, (((1,), (1,)), ((), ())), preferred_element_type=F32)


def _dot_tn(a, b):
    return lax.dot_general(a, b, (((0,), (0,)), ((), ())), preferred_element_type=F32)


def _mx(a):
    return a.astype(MXU_DTYPE)


def _params(n_grid):
    return pltpu.CompilerParams(dimension_semantics=("arbitrary",) * n_grid,
                                vmem_limit_bytes=VMEM_LIMIT)


def _norm_proj_kernel(x_ref, g_ref, w_ref, o_ref):
    x = x_ref[...]
    ms = jnp.mean(x * x, axis=-1, keepdims=True)
    y = x * lax.rsqrt(ms + RMS_EPS) * g_ref[...]
    o_ref[...] = _dot(_mx(y), w_ref[...])


def _norm_proj(x2d, g, w, tm=256):
    m, d = x2d.shape
    n = w.shape[1]
    return pl.pallas_call(
        _norm_proj_kernel,
        grid=(m // tm,),
        in_specs=[pl.BlockSpec((tm, d), lambda i: (i, 0)),
                  pl.BlockSpec((1, d), lambda i: (0, 0)),
                  pl.BlockSpec((d, n), lambda i: (0, 0))],
        out_specs=pl.BlockSpec((tm, n), lambda i: (i, 0)),
        out_shape=jax.ShapeDtypeStruct((m, n), F32),
        compiler_params=_params(1),
        name="norm_proj",
    )(x2d, g.reshape(1, d), w)


def _out_proj_kernel(x_ref, ya_ref, yb_ref, p_ref, wo_ref, pw_ref, gw_ref, fg_ref, o_ref, *, final):
    half = ya_ref.shape[-1]
    y = _dot(_mx(ya_ref[...]), wo_ref[:half, :]) + _dot(_mx(yb_ref[...]), wo_ref[half:, :])
    x1 = x_ref[...] + y
    gate = jax.nn.sigmoid(_dot(_mx(x1), gw_ref[...]))
    x2 = x1 + _dot(_mx(p_ref[...]), pw_ref[...]) * gate
    if final:
        ms = jnp.mean(x2 * x2, axis=-1, keepdims=True)
        x2 = x2 * lax.rsqrt(ms + RMS_EPS) * fg_ref[...]
    o_ref[...] = x2


def _out_proj(x2d, ya, yb, p2d, w_out, ple_w, gate_w, final_g, final, tm=512):
    m, d = x2d.shape
    half = ya.shape[1]
    pd = p2d.shape[1]
    row = lambda i: (i, 0)
    whole = lambda i: (0, 0)
    return pl.pallas_call(
        functools.partial(_out_proj_kernel, final=final),
        grid=(m // tm,),
        in_specs=[pl.BlockSpec((tm, d), row), pl.BlockSpec((tm, half), row),
                  pl.BlockSpec((tm, half), row), pl.BlockSpec((tm, pd), row),
                  pl.BlockSpec((2 * half, d), whole), pl.BlockSpec((pd, d), whole),
                  pl.BlockSpec((d, d), whole), pl.BlockSpec((1, d), whole)],
        out_specs=pl.BlockSpec((tm, d), row),
        out_shape=jax.ShapeDtypeStruct((m, d), F32),
        compiler_params=_params(1),
        name="out_proj",
    )(x2d, ya, yb, p2d, _mx(w_out), _mx(ple_w), _mx(gate_w), final_g.reshape(1, d))


def _bucket_np(n):
    n = np.asarray(n)
    max_exact = N_BUCKETS // 2
    nf = np.maximum(n, 1).astype(np.float32)
    large = max_exact + (np.log(nf / np.float32(max_exact)) / np.float32(math.log(T5_MAX_DIST / max_exact))
                         * np.float32(N_BUCKETS - max_exact)).astype(np.int32)
    large = np.minimum(large, N_BUCKETS - 1)
    return np.where(n < max_exact, n, large)


def _bias_index(dist):
    dist = np.asarray(dist)
    return np.where(dist >= 0, _bucket_np(np.maximum(dist, 0)), N_BUCKETS).astype(np.int32)


def _lookup(ext, idx):
    idx = jnp.asarray(idx)[..., None]
    out = jnp.zeros(idx.shape[:-1] + (ext.shape[1],), F32)
    for b in range(ext.shape[0]):
        out = jnp.where(idx == b, ext[b], out)
    return out


def _near_tiles(tbl):
    h = tbl.shape[1]
    far = tbl[N_BUCKETS - 1]
    ext = jnp.concatenate([tbl, jnp.full((1, h), NEG, F32)], axis=0)
    kj = np.arange(TQ)[:, None]
    qi = np.arange(TQ)[None, :]
    d0 = ((_lookup(ext, _bias_index(qi - kj)) - far) * LOG2E).transpose(0, 2, 1).reshape(TQ, h * TQ)
    d1 = ((_lookup(ext, _bias_index(TQ + qi - kj)) - far) * LOG2E).transpose(0, 2, 1).reshape(TQ, h * TQ)
    edge = jnp.asarray(np.tile(np.where(kj > qi, 0.0, NEG), (1, h)), F32)
    return jnp.stack([jnp.zeros_like(d0), d1, d0, jnp.full_like(d0, NEG), edge])


def _cmp_bias(tbl, nq, n16):
    h = tbl.shape[1]
    ext = jnp.concatenate([tbl, jnp.full((1, h), NEG, F32)], axis=0)
    per_tile = TQ // CMP_STRIDE
    off = per_tile * (nq - 1)
    width = n16 + off
    cc = np.arange(width)[:, None] - off
    qi = np.arange(TQ)[None, :]
    g = _lookup(ext, _bias_index(qi - CMP_STRIDE * cc - (CMP_BLOCK - 1))) * LOG2E
    g = g.transpose(0, 2, 1).reshape(width, h * TQ)
    return jnp.stack([g[off - per_tile * i: off - per_tile * i + n16] for i in range(nq)], axis=0)


def _compress_kernel(x_ref, pos_ref, w1_ref, w2_ref, o_ref):
    n16 = o_ref.shape[0]
    acc = None
    for g in range(NSA_GROUPS):
        pa = pb = None
        for l in range(CMP_STRIDE):
            x = x_ref[pl.ds(l, n16, stride=CMP_STRIDE), :]
            a = _dot(_mx(x + pos_ref[l]), w1_ref[g, l])
            c = _dot(_mx(x + pos_ref[CMP_STRIDE + l]), w1_ref[g, CMP_STRIDE + l])
            pa = a if pa is None else pa + a
            pb = c if pb is None else pb + c
        pre = pa + pltpu.roll(pb, n16 - 1, 0)
        t = _dot(_mx(jax.nn.silu(pre)), w2_ref[g])
        acc = t if acc is None else acc + t
    o_ref[...] = acc


def _compress(u, pos, w1, w2):
    b, s, _ = u.shape
    n16 = s // CMP_STRIDE
    return pl.pallas_call(
        _compress_kernel,
        grid=(2, b),
        in_specs=[pl.BlockSpec((None, s, LANES), lambda k, i: (i, 0, A_KV // LANES + k)),
                  pl.BlockSpec((None, CMP_BLOCK, 1, LANES), lambda k, i: (k, 0, 0, 0)),
                  pl.BlockSpec((None, NSA_GROUPS, CMP_BLOCK, LANES, CMP_HIDDEN), lambda k, i: (k, 0, 0, 0, 0)),
                  pl.BlockSpec((None, NSA_GROUPS, CMP_HIDDEN, LANES), lambda k, i: (k, 0, 0, 0))],
        out_specs=pl.BlockSpec((None, None, n16, LANES), lambda k, i: (k, i, 0, 0)),
        out_shape=jax.ShapeDtypeStruct((2, b, n16, LANES), F32),
        compiler_params=_params(2),
        name="compress",
    )(u, pos, w1, w2)


def _softmax_reset(m_s, acc_s):
    m_s[...] = jnp.full(m_s.shape, -jnp.inf, F32)
    acc_s[...] = jnp.zeros(acc_s.shape, F32)


def _softmax_step(score_fn, values, m_s, acc_s, col_block):
    r = m_s.shape[1]
    s_next = score_fn(0, col_block)
    for c0 in range(0, r, col_block):
        cols = slice(c0, c0 + col_block)
        s = s_next
        if c0 + col_block < r:
            s_next = score_fn(c0 + col_block, c0 + 2 * col_block)
        m_prev = m_s[:, cols]
        m_new = jnp.maximum(m_prev, jnp.max(s, axis=0, keepdims=True))
        alpha = jnp.exp2(m_prev - m_new)
        p = _mx(jnp.exp2(s - m_new))
        v = values[c0 * len(values) // r]
        acc_s[:, cols] = alpha * acc_s[:, cols] + _dot(v, p)
        m_s[:, cols] = m_new


def _store_value_tiles(v_ref, vt_s, ones_hi):
    row = lax.broadcasted_iota(jnp.int32, (LANES, TQ), 0)

    def body(j, carry):
        vt = v_ref[pl.ds(pl.multiple_of(j * TQ, TQ), TQ), :].T
        for k, hi in enumerate(ones_hi):
            vt_s[k, j] = _mx(jnp.where((row >= HEAD_DIM) == hi, 1.0, vt))
        return carry

    lax.fori_loop(0, vt_s.shape[1], body, 0)


def _value_tiles(vt_s, k, j0, ntile):
    tiles = [vt_s[k, j0 + t] for t in range(ntile)]
    return tiles[0] if ntile == 1 else jnp.concatenate(tiles, axis=1)


def _near_bias(d_ref, idx_of_tile, j0, ntile, c0, c1):
    tiles = [d_ref[idx_of_tile(j0 + t), :, c0:c1] for t in range(ntile)]
    return tiles[0] if ntile == 1 else jnp.concatenate(tiles, axis=0)


def _stack_heads(q_ref, hi):
    lane = lax.broadcasted_iota(jnp.int32, (TQ, LANES), 1)
    out = []
    for h, up in enumerate(hi):
        v = q_ref[:, (h // 2) * LANES:(h // 2 + 1) * LANES]
        if (h % 2 == 1) != up:
            v = pltpu.roll(v, HEAD_DIM, 1)
        out.append(jnp.where((lane >= HEAD_DIM) == up, v, 0.0))
    return jnp.concatenate(out, axis=0)


def _tiles_per_chunk(nq):
    for ch in (4, 2):
        if nq % ch == 0:
            return ch
    return 1


def _nsa_kernel(q_ref, kc_ref, vc_ref, ks_ref, vs_ref, kw_ref, vw_ref, g_ref, z_ref,
                bc_ref, d_ref, et_ref, ov_ref, o_ref,
                qa_s, m_s, acc_s, osel_s, vct_s, vst_s, vwt_s, *, n_pick, ch, n_win_tiles):
    i = pl.program_id(1)
    rows = NSA_HEADS * TQ
    nb = ov_ref.shape[0]
    q0 = i * TQ
    half = LANES // NSA_GROUPS

    @pl.when(i == 0)
    def _():
        vct_s[...] = _mx(vc_ref[...].T)
        _store_value_tiles(vs_ref, vst_s, (True, False))
        _store_value_tiles(vw_ref, vwt_s, (True, False))

    qs = _mx(_stack_heads(q_ref, [h >= NSA_HPG for h in range(NSA_HEADS)]))
    qa_s[:, :LANES] = qs

    s = _dot_nt(_mx(kc_ref[...]), qs) + bc_ref[...]
    p = jnp.exp2(s - jnp.max(s, axis=0, keepdims=True))
    t_lane = q0 + (lax.broadcasted_iota(jnp.int32, (1, rows), 1) & (TQ - 1))
    any_valid = jnp.where(t_lane >= CMP_BLOCK - 1, 1.0, 0.0)
    p = p * (any_valid / jnp.sum(p, axis=0, keepdims=True))
    oc_t = _dot(vct_s[...], _mx(p))

    blk = lax.broadcasted_iota(jnp.int32, (nb, TQ), 0)
    t = q0 + lax.broadcasted_iota(jnp.int32, (nb, TQ), 1)
    cur = lax.shift_right_logical(t, int(math.log2(SEL_BLOCK)))
    forced = jnp.where(blk == 0, 1.0, jnp.where(blk == cur, 1.0, jnp.where(blk == cur - 1, 1.0, 0.0)))
    admissible = blk * SEL_BLOCK <= t
    ov = ov_ref[...]
    for g in range(NSA_GROUPS):
        c0 = g * NSA_HPG * TQ
        ps = p[:, c0:c0 + TQ]
        for h in range(1, NSA_HPG):
            ps = ps + p[:, c0 + h * TQ:c0 + (h + 1) * TQ]
        hi = _mx(ps)
        lo = _mx(ps - hi.astype(F32))
        imp_t = _dot(ov, hi) + _dot(ov, lo)
        score = jnp.where(admissible, imp_t + FORCE_BONUS * forced, NEG)
        sel = jnp.zeros((nb, TQ), F32)
        for _ in range(n_pick):
            best = jnp.max(score, axis=0, keepdims=True)
            first = jnp.min(jnp.where(score == best, blk, nb), axis=0, keepdims=True)
            hit = blk == first
            sel = jnp.where(hit, 1.0, sel)
            score = jnp.where(hit, -jnp.inf, score)
        if nb < LANES:
            sel = jnp.concatenate([sel, jnp.ones((LANES - nb, TQ), F32)], axis=0)
        block_mask = _mx((sel.T - 1.0) * (-NEG))
        for h in range(NSA_HPG):
            r0 = (g * NSA_HPG + h) * TQ
            qa_s[r0:r0 + TQ, LANES:] = block_mask

    def values(vt_s, j0, ntile):
        return [_value_tiles(vt_s, 0, j0, ntile), _value_tiles(vt_s, 1, j0, ntile)]

    _softmax_reset(m_s, acc_s)

    def sel_chunk(c, near):
        tk = ch * TQ
        koff = pl.multiple_of(c * tk, tk)
        k_aug = jnp.concatenate([_mx(ks_ref[pl.ds(koff, tk), :]), et_ref[pl.ds(koff, tk), :]], axis=1)

        def scores(c0, c1):
            s = _dot_nt(k_aug, qa_s[c0:c1, :])
            if near:
                s = s + _near_bias(d_ref, lambda j: jnp.clip(j - (i - 2), 0, 3), c * ch, ch, c0, c1)
            return s

        _softmax_step(scores, values(vst_s, c * ch, ch), m_s, acc_s, COL_BLOCK)

    first_near = jnp.maximum(i - 1, 0) // ch

    def far_body(c, carry):
        sel_chunk(c, False)
        return carry

    def near_body(c, carry):
        sel_chunk(c, True)
        return carry

    lax.fori_loop(0, first_near, far_body, 0)
    lax.fori_loop(first_near, i // ch + 1, near_body, 0)
    osel_s[...] = acc_s[...]

    _softmax_reset(m_s, acc_s)
    n_win = WINDOW // TQ
    jw = jnp.maximum(i - (n_win_tiles - 1), 0)
    koff = pl.multiple_of(jw * TQ, TQ)
    tk = n_win_tiles * TQ

    def win_idx(j):
        r = j - (i - n_win)
        return jnp.where(r == 0, 4, jnp.clip(r - (n_win - 2), 0, 3))

    kw = _mx(kw_ref[pl.ds(koff, tk), :])
    _softmax_step(lambda c0, c1: (_dot_nt(kw, qa_s[c0:c1, :LANES])
                                  + _near_bias(d_ref, win_idx, jw, n_win_tiles, c0, c1)),
                  values(vwt_s, jw, n_win_tiles), m_s, acc_s, COL_BLOCK)
    ow_t = acc_s[...]
    os_t = osel_s[...]

    gate_t = jax.nn.sigmoid(g_ref[...]).T
    for m in range(NSA_HEADS // 2):
        parts = []
        for hh in (2 * m, 2 * m + 1):
            g = hh // NSA_HPG
            r = slice(g * half, (g + 1) * half)
            d = (1 - g) * half
            c = slice(hh * TQ, (hh + 1) * TQ)
            parts.append(gate_t[3 * hh:3 * hh + 1, :] * oc_t[r, c]
                         + gate_t[3 * hh + 1:3 * hh + 2, :] * (os_t[r, c] / os_t[d:d + 1, c])
                         + gate_t[3 * hh + 2:3 * hh + 3, :] * (ow_t[r, c] / ow_t[d:d + 1, c]))
        slab = jnp.concatenate(parts, axis=0).T
        o_ref[:, m * LANES:(m + 1) * LANES] = slab * jax.nn.silu(z_ref[:, m * LANES:(m + 1) * LANES])


def _nsa(u, cmp_kv, bias_c, d_tiles, e_t, ov, n_pick):
    b, s, _ = u.shape
    nq = s // TQ
    n16 = cmp_kv.shape[2]
    nb = ov.shape[0]
    rows = NSA_HEADS * TQ
    n_win_tiles = min(WINDOW // TQ + 1, nq)
    seq = lambda col: pl.BlockSpec((None, s, LANES), lambda bi, i, col=col: (bi, 0, col // LANES))
    return pl.pallas_call(
        functools.partial(_nsa_kernel, n_pick=n_pick, ch=_tiles_per_chunk(nq), n_win_tiles=n_win_tiles),
        grid=(b, nq),
        in_specs=[
            pl.BlockSpec((None, TQ, NSA_HEADS * HEAD_DIM), lambda bi, i: (bi, i, A_NSA_Q // (NSA_HEADS * HEAD_DIM))),
            pl.BlockSpec((None, None, n16, LANES), lambda bi, i: (0, bi, 0, 0)),
            pl.BlockSpec((None, None, n16, LANES), lambda bi, i: (1, bi, 0, 0)),
            seq(A_KV + 2 * LANES), seq(A_KV + 3 * LANES), seq(A_KV + 4 * LANES), seq(A_KV + 5 * LANES),
            pl.BlockSpec((None, TQ, LANES), lambda bi, i: (bi, i, A_GATE // LANES)),
            pl.BlockSpec((None, TQ, 512), lambda bi, i: (bi, i, A_AZ // 512)),
            pl.BlockSpec((None, n16, rows), lambda bi, i: (i, 0, 0)),
            pl.BlockSpec(d_tiles.shape, lambda bi, i: (0, 0, 0)),
            pl.BlockSpec((s, LANES), lambda bi, i: (0, 0)),
            pl.BlockSpec((nb, n16), lambda bi, i: (0, 0)),
        ],
        out_specs=pl.BlockSpec((None, TQ, 512), lambda bi, i: (bi, i, 0)),
        out_shape=jax.ShapeDtypeStruct((b, s, 512), F32),
        scratch_shapes=[pltpu.VMEM((rows, 2 * LANES), MXU_DTYPE),
                        pltpu.VMEM((1, rows), F32),
                        pltpu.VMEM((LANES, rows), F32),
                        pltpu.VMEM((LANES, rows), F32),
                        pltpu.VMEM((LANES, n16), MXU_DTYPE),
                        pltpu.VMEM((NSA_GROUPS, nq, LANES, TQ), MXU_DTYPE),
                        pltpu.VMEM((NSA_GROUPS, nq, LANES, TQ), MXU_DTYPE)],
        compiler_params=_params(2),
        name="nsa",
    )(u, cmp_kv, cmp_kv, u, u, u, u, u, u, bias_c, d_tiles, e_t, ov)


def _dsa_kernel(q_ref, kv_ref, qi_ref, ki_ref, wi_ref, z_ref, d_ref, tri_ref, o_ref,
                qs_s, qis_s, sc_s, hi_s, lo_s, eq_s, m_s, acc_s, seen_s, vt_s, *, k_top, ch):
    i = pl.program_id(1)

    @pl.when(i == 0)
    def _():
        _store_value_tiles(kv_ref, vt_s, (False,))

    qs_s[...] = _mx(_stack_heads(q_ref, [False] * DSA_HEADS))
    qis_s[...] = _mx(_stack_heads(qi_ref, [False] * IDX_HEADS))
    wi_t = wi_ref[...].T
    w_rows = [jnp.broadcast_to(wi_t[h:h + 1, :], (TQ, TQ)) for h in range(IDX_HEADS)]
    key_idx = lax.broadcasted_iota(jnp.int32, (TQ, TQ), 0)
    q_idx = lax.broadcasted_iota(jnp.int32, (TQ, TQ), 1)
    n_chunks = (i + ch) // ch
    pairs = ch // 2

    def score_chunk(c, carry):
        koff = pl.multiple_of(c * (ch * TQ), ch * TQ)
        r = jnp.maximum(_dot_nt(_mx(ki_ref[pl.ds(koff, ch * TQ), :]), qis_s[...]), 0.0)
        keys = []
        for t in range(ch):
            jt = c * ch + t
            rt = r[t * TQ:(t + 1) * TQ]
            sc = w_rows[0] * rt[:, 0:TQ]
            for h in range(1, IDX_HEADS):
                sc = sc + w_rows[h] * rt[:, h * TQ:(h + 1) * TQ]
            sc = jnp.where((jt == i) & (key_idx > q_idx), NEG, sc)
            bits = pltpu.bitcast(sc, jnp.int32)
            key = bits ^ (lax.shift_right_arithmetic(bits, 31) & jnp.int32(0x7FFFFFFF))
            key = jnp.where(sc == 0.0, 0, key)
            keys.append(jnp.where(jt > i, INT_MIN, key))
            sc_s[jt] = keys[-1]
        for pr in range(pairs):
            a, b = keys[2 * pr], keys[2 * pr + 1]
            hi_s[c * pairs + pr] = (a & jnp.int32(-65536)) | lax.shift_right_logical(b, 16)
            lo_s[c * pairs + pr] = (lax.shift_left((a & 0xFFFF) ^ 0x8000, 16)
                                    | ((b & 0xFFFF) ^ 0x8000))
        return carry

    lax.fori_loop(0, n_chunks, score_chunk, 0)

    def halves(words):
        return pltpu.bitcast(words, jnp.int16)

    def both_halves(v):
        w = lax.shift_left(v, 16) | (v & 0xFFFF)
        return halves(jnp.broadcast_to(w, (TQ, TQ)))

    def count16(words_s, pred, weight_s=None):
        one = jnp.ones((2 * TQ, TQ), jnp.int16)

        def body(c, acc):
            for pr in range(pairs):
                w = one if weight_s is None else halves(weight_s[c * pairs + pr])
                acc = acc + jnp.where(pred(halves(words_s[c * pairs + pr])), w, jnp.int16(0))
            return acc
        acc = lax.fori_loop(0, n_chunks, body, jnp.zeros((2 * TQ, TQ), jnp.int16))
        return jnp.sum(acc.astype(jnp.int32), axis=0, keepdims=True)

    def bisect16(words_s, target, weight_s=None):
        c0 = count16(words_s, lambda x: x >= jnp.int16(0), weight_s)
        v0 = jnp.where(c0 >= target, 0, -32768)

        def step(it, v):
            cand = v | lax.shift_left(jnp.int32(1), 14 - it)
            cand16 = both_halves(cand)
            c = count16(words_s, lambda x: x >= cand16, weight_s)
            return jnp.where(c >= target, cand, v)

        return lax.fori_loop(0, 15, step, v0)

    top = bisect16(hi_s, k_top)
    top16 = both_halves(top)
    above = count16(hi_s, lambda x: x > top16)

    def mark(c, carry):
        for pr in range(pairs):
            hit = jnp.where(halves(hi_s[c * pairs + pr]) == top16, jnp.int16(1), jnp.int16(0))
            eq_s[c * pairs + pr] = pltpu.bitcast(hit, jnp.int32)
        return carry

    lax.fori_loop(0, n_chunks, mark, 0)
    low = bisect16(lo_s, k_top - above, eq_s)
    thr = lax.shift_left(top, 16) | ((low ^ 0x8000) & 0xFFFF)
    thr = jnp.broadcast_to(thr, (TQ, TQ))

    def count_above(c, acc):
        for t in range(ch):
            acc = acc + jnp.where(sc_s[c * ch + t] > thr, 1.0, 0.0)
        return acc

    n_above = lax.fori_loop(0, n_chunks, count_above, jnp.zeros((TQ, TQ), F32))
    need = float(k_top) - jnp.sum(n_above, axis=0, keepdims=True)

    _softmax_reset(m_s, acc_s)
    seen_s[...] = jnp.zeros((TQ, TQ), F32)

    def att_chunk(c, near):
        tk = ch * TQ
        koff = pl.multiple_of(c * tk, tk)
        masks = []
        for t in range(ch):
            key = sc_s[c * ch + t]
            eq = key == thr
            prefix = _dot(tri_ref[...], _mx(jnp.where(eq, 1.0, 0.0)))
            seen = seen_s[...]
            take = jnp.where(eq, jnp.where(seen + prefix <= need, 1.0, 0.0), 0.0)
            seen_s[...] = seen + prefix[TQ - 1:TQ, :]
            masks.append(jnp.where(key > thr, 0.0, (take - 1.0) * (-NEG)))
        mb = masks[0] if ch == 1 else jnp.concatenate(masks, axis=0)
        mb = jnp.concatenate([mb] * DSA_HEADS, axis=1)
        kvm = _mx(kv_ref[pl.ds(koff, tk), :])

        def scores(c0, c1):
            s = _dot_nt(kvm, qs_s[c0:c1, :]) + mb[:, c0:c1]
            if near:
                s = s + _near_bias(d_ref, lambda j: jnp.clip(j - (i - 2), 0, 3), c * ch, ch, c0, c1)
            return s

        _softmax_step(scores, [_value_tiles(vt_s, 0, c * ch, ch)], m_s, acc_s, DSA_HEADS * TQ)

    first_near = jnp.maximum(i - 1, 0) // ch

    def far_body(c, carry):
        att_chunk(c, False)
        return carry

    def near_body(c, carry):
        att_chunk(c, True)
        return carry

    lax.fori_loop(0, first_near, far_body, 0)
    lax.fori_loop(first_near, n_chunks, near_body, 0)

    o_t = acc_s[...]
    o_t = o_t / o_t[0:1, :]
    for m in range(DSA_HEADS // 2):
        pair = [o_t[HEAD_DIM:, hh * TQ:(hh + 1) * TQ] for hh in (2 * m, 2 * m + 1)]
        slab = jnp.concatenate(pair, axis=0).T
        o_ref[:, m * LANES:(m + 1) * LANES] = slab * jax.nn.silu(z_ref[:, m * LANES:(m + 1) * LANES])


def _dsa(u, d_tiles, tri, k_top):
    b, s, _ = u.shape
    nq = s // TQ
    assert nq % 2 == 0, "the packed threshold search pairs key tiles"
    rows = DSA_HEADS * TQ
    irows = IDX_HEADS * TQ
    seq = lambda col: pl.BlockSpec((None, s, LANES), lambda bi, i, col=col: (bi, 0, col // LANES))
    return pl.pallas_call(
        functools.partial(_dsa_kernel, k_top=k_top, ch=_tiles_per_chunk(nq)),
        grid=(b, nq),
        in_specs=[
            pl.BlockSpec((None, TQ, DSA_HEADS * HEAD_DIM), lambda bi, i: (bi, i, A_DSA_Q // (DSA_HEADS * HEAD_DIM))),
            seq(A_DSA_KV),
            pl.BlockSpec((None, TQ, IDX_HEADS * IDX_DIM), lambda bi, i: (bi, i, A_IDX_Q // (IDX_HEADS * IDX_DIM))),
            seq(A_IDX_K),
            pl.BlockSpec((None, TQ, LANES), lambda bi, i: (bi, i, A_IDX_W // LANES)),
            pl.BlockSpec((None, TQ, 512), lambda bi, i: (bi, i, A_BZ // 512)),
            pl.BlockSpec(d_tiles.shape, lambda bi, i: (0, 0, 0)),
            pl.BlockSpec((TQ, TQ), lambda bi, i: (0, 0)),
        ],
        out_specs=pl.BlockSpec((None, TQ, 512), lambda bi, i: (bi, i, 0)),
        out_shape=jax.ShapeDtypeStruct((b, s, 512), F32),
        scratch_shapes=[pltpu.VMEM((rows, LANES), MXU_DTYPE),
                        pltpu.VMEM((irows, LANES), MXU_DTYPE),
                        pltpu.VMEM((nq, TQ, TQ), jnp.int32),
                        pltpu.VMEM((nq // 2, TQ, TQ), jnp.int32),
                        pltpu.VMEM((nq // 2, TQ, TQ), jnp.int32),
                        pltpu.VMEM((nq // 2, TQ, TQ), jnp.int32),
                        pltpu.VMEM((1, rows), F32),
                        pltpu.VMEM((LANES, rows), F32),
                        pltpu.VMEM((TQ, TQ), F32),
                        pltpu.VMEM((1, nq, LANES, TQ), MXU_DTYPE)],
        compiler_params=_params(2),
        name="dsa",
    )(u, u, u, u, u, u, d_tiles, tri)


def _causal_conv(x, xe_s, w_ref, b_ref):
    t = x.shape[0]
    xe_s[8:8 + t, :] = x
    y = b_ref[...] + w_ref[CONV_WIDTH - 1:CONV_WIDTH, :] * x
    for k in range(1, CONV_WIDTH):
        y = y + w_ref[CONV_WIDTH - 1 - k:CONV_WIDTH - k, :] * xe_s[8 - k:8 - k + t, :]
    xe_s[0:8, :] = x[t - 8:t, :]
    return y


def _rglru_kernel(x_ref, z_ref, cw_ref, cb_ref, wa_ref, ba_ref, wx_ref, bx_ref, lam_ref, o_ref,
                  xe_s, h_s):
    t, c = x_ref.shape

    @pl.when(pl.program_id(1) == 0)
    def _():
        xe_s[0:8, :] = jnp.zeros((8, c), F32)
        h_s[...] = jnp.zeros(h_s.shape, F32)

    xc = _causal_conv(x_ref[...], xe_s, cw_ref, cb_ref)
    xcm = _mx(xc)
    r = jax.nn.sigmoid(_dot(xcm, wa_ref[...]) + ba_ref[...])
    ig = jax.nn.sigmoid(_dot(xcm, wx_ref[...]) + bx_ref[...])
    nl = -lam_ref[...]
    softplus = jnp.maximum(nl, 0.0) + jnp.log1p(jnp.exp(-jnp.abs(nl)))
    log_a = (-LRU_C * r) * softplus
    a = jnp.exp(log_a)
    bb = jnp.sqrt(-jnp.tanh(log_a) * (a * a + 1.0)) * (ig * xc)
    row = lax.broadcasted_iota(jnp.int32, (t, c), 0)
    d = 1
    while d < t:
        live = row >= d
        a_sh = jnp.where(live, pltpu.roll(a, d, 0), 1.0)
        b_sh = jnp.where(live, pltpu.roll(bb, d, 0), 0.0)
        bb = a * b_sh + bb
        a = a * a_sh
        d *= 2
    h = a * h_s[0:1, :] + bb
    h_s[0:1, :] = h[t - 1:t, :]
    o_ref[...] = h * jax.nn.silu(z_ref[...])


def _rglru(u, conv_w, conv_b, wa_bd, ba, wx_bd, bx, lam, t=256):
    b, s, _ = u.shape
    c = LRU_WIDTH
    whole = lambda bi, i: (0, 0)
    return pl.pallas_call(
        _rglru_kernel,
        grid=(b, s // t),
        in_specs=[pl.BlockSpec((None, t, c), lambda bi, i: (bi, i, R_CX // c)),
                  pl.BlockSpec((None, t, c), lambda bi, i: (bi, i, R_CZ // c)),
                  pl.BlockSpec((CONV_WIDTH, c), whole), pl.BlockSpec((1, c), whole),
                  pl.BlockSpec((c, c), whole), pl.BlockSpec((1, c), whole),
                  pl.BlockSpec((c, c), whole), pl.BlockSpec((1, c), whole),
                  pl.BlockSpec((1, c), whole)],
        out_specs=pl.BlockSpec((None, t, c), lambda bi, i: (bi, i, 0)),
        out_shape=jax.ShapeDtypeStruct((b, s, c), F32),
        scratch_shapes=[pltpu.VMEM((t + 8, c), F32), pltpu.VMEM((8, c), F32)],
        compiler_params=_params(2),
        name="rglru",
    )(u, u, conv_w, conv_b.reshape(1, c), wa_bd, ba.reshape(1, c), wx_bd, bx.reshape(1, c),
      lam.reshape(1, c))


def _mlstm_kernel(qk_ref, v_ref, g_ref, og_ref, z_ref, cw_ref, cb_ref, gb_ref, o_ref,
                  xe_s, c_s, m_s):
    t = qk_ref.shape[0]
    L = MLSTM_CHUNK
    H = MLSTM_HEADS
    D = MLSTM_DIM

    @pl.when(pl.program_id(1) == 0)
    def _():
        xe_s[0:8, :] = jnp.zeros((8, xe_s.shape[1]), F32)
        c_s[...] = jnp.zeros(c_s.shape, F32)
        m_s[...] = jnp.zeros(m_s.shape, F32)

    qk = jax.nn.silu(_causal_conv(qk_ref[...], xe_s, cw_ref, cb_ref))

    lane = lax.broadcasted_iota(jnp.int32, (t, LANES), 1)
    row_in = lax.broadcasted_iota(jnp.int32, (t, LANES), 0) & (L - 1)
    gs = g_ref[...] + gb_ref[...]
    log_sig = jnp.minimum(gs, 0.0) - jnp.log1p(jnp.exp(-jnp.abs(gs)))
    pre = jnp.where(lane < H, gs, log_sig)
    cum = pre
    d = 1
    while d < L:
        cum = cum + jnp.where(row_in >= d, pltpu.roll(cum, d, 0), 0.0)
        d *= 2
    comb = jnp.where(lane < H, pre, cum)
    comb_t = comb.T

    causal_t = (lax.broadcasted_iota(jnp.int32, (L, L), 0) <= lax.broadcasted_iota(jnp.int32, (L, L), 1))
    pad_rows = jnp.zeros((7, L), F32)
    for h in range(H):
        v_t = v_ref[:, h * D:(h + 1) * D].T
        state = c_s[h]
        m_prev = m_s[h:h + 1, 0:1]
        outs = []
        for c in range(t // L):
            r = slice(c * L, (c + 1) * L)
            q_ = _mx(qk[r, h * D:(h + 1) * D])
            k_ = _mx(qk[r, (H + h) * D:(H + h + 1) * D] * (D ** -0.5))
            li_row = comb_t[h:h + 1, r]
            b_row = comb_t[H + h:H + h + 1, r]
            lb_col = comb[r, h:h + 1] - comb[r, H + h:H + h + 1]
            b_last = b_row[:, L - 1:L]

            dmat_t = jnp.where(causal_t, b_row + lb_col, -jnp.inf)
            decay_row = b_last - b_row + li_row
            m_new = jnp.maximum(b_last + m_prev, jnp.max(decay_row, axis=-1, keepdims=True))
            inter = b_row + m_prev
            m_t = jnp.maximum(inter, jnp.max(dmat_t, axis=0, keepdims=True))
            w_t = _dot_nt(k_, q_) * jnp.exp(dmat_t - m_t)
            prev = jnp.exp(inter - m_t)

            read = _dot_nt(_mx(state), q_)
            num_t = prev * read[:D] + _dot(_mx(v_t[:, r]), _mx(w_t))
            den = prev * read[D:D + 1] + jnp.sum(w_t, axis=0, keepdims=True)
            outs.append(num_t / jnp.maximum(jnp.abs(den), jnp.exp(-m_t)))

            wk = jnp.exp(decay_row - m_new)
            inc = jnp.concatenate([v_t[:, r] * wk, wk, pad_rows], axis=0)
            state = jnp.exp(b_last + m_prev - m_new) * state + _dot(_mx(inc), k_)
            m_prev = m_new

        c_s[h] = state
        m_s[h:h + 1, :] = jnp.broadcast_to(m_prev, (1, LANES))
        out = jnp.concatenate(outs, axis=1).T
        cols = slice(h * D, (h + 1) * D)
        o_ref[:, cols] = jax.nn.sigmoid(og_ref[:, cols]) * out * jax.nn.silu(z_ref[:, cols])


def _mlstm(u, conv_w, conv_b, gate_bias, t=256):
    b, s, _ = u.shape
    w = MLSTM_WIDTH
    whole = lambda bi, i: (0, 0)
    return pl.pallas_call(
        _mlstm_kernel,
        grid=(b, s // t),
        in_specs=[pl.BlockSpec((None, t, 2 * w), lambda bi, i: (bi, i, R_QK // (2 * w))),
                  pl.BlockSpec((None, t, w), lambda bi, i: (bi, i, R_V // w)),
                  pl.BlockSpec((None, t, LANES), lambda bi, i: (bi, i, R_GATE // LANES)),
                  pl.BlockSpec((None, t, w), lambda bi, i: (bi, i, R_O // w)),
                  pl.BlockSpec((None, t, w), lambda bi, i: (bi, i, R_Z // w)),
                  pl.BlockSpec((CONV_WIDTH, 2 * w), whole), pl.BlockSpec((1, 2 * w), whole),
                  pl.BlockSpec((1, LANES), whole)],
        out_specs=pl.BlockSpec((None, t, w), lambda bi, i: (bi, i, 0)),
        out_shape=jax.ShapeDtypeStruct((b, s, w), F32),
        scratch_shapes=[pltpu.VMEM((t + 8, 2 * w), F32),
                        pltpu.VMEM((MLSTM_HEADS, MLSTM_DIM + 8, MLSTM_DIM), F32),
                        pltpu.VMEM((8, LANES), F32)],
        compiler_params=_params(2),
        name="mlstm",
    )(u, u, u, u, u, conv_w, conv_b.reshape(1, 2 * w), gate_bias)


def _attn_weights(w_in):
    d = w_in.shape[0]
    widths = (512, 768, 24, 512, 512, 64, 64, 256, 64, 4, 512)
    offs = np.concatenate([[0], np.cumsum(widths)])
    a_q, a_kv, a_g, a_z, b_q, b_k, b_v, b_qi, b_ki, b_wi, b_z = [
        w_in[:, offs[k]:offs[k + 1]] for k in range(len(widths))]
    scale = HEAD_DIM ** -0.5 * LOG2E
    zeros = lambda n: jnp.zeros((d, n), w_in.dtype)
    cols = [
        a_q * scale, b_q * scale, a_z, b_z, b_qi, a_kv,
        a_g, zeros(LANES - 24),
        b_k, b_v,
        b_ki, zeros(LANES - IDX_DIM),
        b_wi * (IDX_DIM ** -0.5 * IDX_HEADS ** -0.5), zeros(LANES - IDX_HEADS),
    ]
    w = jnp.concatenate(cols, axis=1)
    assert w.shape[1] == A_TOTAL
    return _mx(w)


def _rec_weights(w_in):
    d = w_in.shape[0]
    widths = (512, 512, 512, 512, 512, 4, 4, 512, 512)
    offs = np.concatenate([[0], np.cumsum(widths)])
    c_x, c_z, d_q, d_k, d_v, d_i, d_f, d_o, d_z = [w_in[:, offs[k]:offs[k + 1]] for k in range(len(widths))]
    w = jnp.concatenate([c_x, c_z, d_q, d_k, d_v, d_o, d_z, d_i, d_f,
                         jnp.zeros((d, R_TOTAL - R_GATE - 2 * MLSTM_HEADS), w_in.dtype)], axis=1)
    assert w.shape[1] == R_TOTAL
    return _mx(w)


def _block_diag(w):
    g, n, _ = w.shape
    eye = jnp.eye(g, dtype=w.dtype)
    return (eye[:, None, :, None] * w[:, :, None, :]).reshape(g * n, g * n)


def _attention_layer(x2d, b, s, norm_g, w_in, cmp_pos_k, cmp_w1_k, cmp_w2_k, cmp_pos_v, cmp_w1_v, cmp_w2_v,
                     t5_table):
    nq = s // TQ
    n16 = s // CMP_STRIDE
    nb = s // SEL_BLOCK
    u = _norm_proj(x2d, norm_g, _attn_weights(w_in)).reshape(b, s, A_TOTAL)

    pos = jnp.stack([cmp_pos_k, cmp_pos_v])
    pos = jnp.concatenate([pos] * NSA_GROUPS, axis=-1).reshape(2, CMP_BLOCK, 1, LANES)
    w1 = jnp.stack([cmp_w1_k, cmp_w1_v]).reshape(2, CMP_BLOCK, HEAD_DIM, CMP_HIDDEN)
    z1 = jnp.zeros_like(w1)
    w1 = _mx(jnp.stack([jnp.concatenate([w1, z1], axis=2), jnp.concatenate([z1, w1], axis=2)], axis=1))
    w2 = jnp.stack([cmp_w2_k, cmp_w2_v])
    zpad = jnp.zeros_like(w2)
    w2 = _mx(jnp.stack([jnp.concatenate([w2, zpad], -1), jnp.concatenate([zpad, w2], -1)], axis=1))
    cmp_kv = _compress(u, pos, w1, w2)

    tbl = t5_table.astype(F32)
    tbl_a, tbl_b = tbl[:, :NSA_HEADS], tbl[:, NSA_HEADS:]
    kj = np.arange(TQ)[:, None]
    qi = np.arange(TQ)[None, :]
    assert nb <= LANES
    e_t = jnp.asarray((np.arange(s)[:, None] // SEL_BLOCK) == np.arange(LANES)[None, :], MXU_DTYPE)
    ci = np.arange(n16)[None, :]
    sj = np.arange(nb)[:, None]
    ov = jnp.asarray((ci * CMP_STRIDE < (sj + 1) * SEL_BLOCK) & (ci * CMP_STRIDE + CMP_BLOCK > sj * SEL_BLOCK)
                     & (ci < n16 - 1), MXU_DTYPE)
    tri = jnp.asarray(qi <= kj, MXU_DTYPE)

    ya = _nsa(u, cmp_kv, _cmp_bias(tbl_a, nq, n16), _near_tiles(tbl_a), e_t, ov, min(N_SEL, nb))
    yb = _dsa(u, _near_tiles(tbl_b), tri, min(DSA_TOPK_MAX, s // 4))
    return ya.reshape(b * s, -1), yb.reshape(b * s, -1)


def _recurrent_layer(x2d, b, s, norm_g, w_in, conv_c_w, conv_c_b, wa, ba, wx, bx, lam,
                     conv_d_w, conv_d_b, b_i, b_f):
    u = _norm_proj(x2d, norm_g, _rec_weights(w_in)).reshape(b, s, R_TOTAL)
    yc = _rglru(u, conv_c_w, conv_c_b, _mx(_block_diag(wa)), ba, _mx(_block_diag(wx)), bx, lam)
    gate_bias = jnp.concatenate([b_i, b_f, jnp.zeros((LANES - 2 * MLSTM_HEADS,), F32)]).reshape(1, LANES)
    yd = _mlstm(u, conv_d_w, conv_d_b, gate_bias)
    return yc.reshape(b * s, -1), yd.reshape(b * s, -1)


def kernel(x, p, norm_g, final_g, ple_w, ple_gate_w, t5_table, attn_w_in, attn_w_out, cmp_pos_k, cmp_w1_k, cmp_w2_k, cmp_pos_v, cmp_w1_v, cmp_w2_v, rec_w_in, rec_w_out, lru_conv_w, lru_conv_b, lru_wa, lru_ba, lru_wx, lru_bx, lru_lambda, mlstm_conv_w, mlstm_conv_b, mlstm_b_i, mlstm_b_f):
    b, s, d = x.shape
    depth = p.shape[0]
    x2d = x.reshape(b * s, d)
    for i in range(depth):
        j = i // 2
        if i % 2 == 0:
            ya, yb = _attention_layer(x2d, b, s, norm_g[i], attn_w_in[j], cmp_pos_k[j], cmp_w1_k[j], cmp_w2_k[j],
                                      cmp_pos_v[j], cmp_w1_v[j], cmp_w2_v[j], t5_table)
            w_out = attn_w_out[j]
        else:
            ya, yb = _recurrent_layer(x2d, b, s, norm_g[i], rec_w_in[j], lru_conv_w[j], lru_conv_b[j],
                                      lru_wa[j], lru_ba[j], lru_wx[j], lru_bx[j], lru_lambda[j],
                                      mlstm_conv_w[j], mlstm_conv_b[j], mlstm_b_i[j], mlstm_b_f[j])
            w_out = rec_w_out[j]
        x2d = _out_proj(x2d, ya, yb, p[i].reshape(b * s, -1), w_out, ple_w[i], ple_gate_w[i], final_g,
                        final=(i == depth - 1))
    return x2d.reshape(b, s, d)
```

```python
import functools
import math

import numpy as np
import jax
import jax.numpy as jnp
from jax import lax
from jax.experimental import pallas as pl
from jax.experimental.pallas import tpu as pltpu

F32 = jnp.float32
MXU_DTYPE = jnp.bfloat16

HEAD_DIM = 64
NSA_HEADS = 8
NSA_GROUPS = 2
NSA_HPG = NSA_HEADS // NSA_GROUPS
CMP_BLOCK = 32
CMP_STRIDE = 16
CMP_HIDDEN = 256
SEL_BLOCK = 64
N_SEL = 8
WINDOW = 512
FORCE_BONUS = 1e4
DSA_HEADS = 8
IDX_HEADS = 4
IDX_DIM = 64
DSA_TOPK_MAX = 256
N_BUCKETS = 32
T5_MAX_DIST = 128
LRU_WIDTH = 512
LRU_BLOCKS = 8
CONV_WIDTH = 4
LRU_C = 8.0
MLSTM_HEADS = 4
MLSTM_DIM = 128
MLSTM_WIDTH = MLSTM_HEADS * MLSTM_DIM
MLSTM_CHUNK = 64
RMS_EPS = 1e-6
NEG = -1e30

LANES = 128
SUBLANES = 8
COL_BLOCK = 256
TQ = 128
INT_MIN = -2 ** 31
LOG2E = math.log2(math.e)
VMEM_LIMIT = 48 * 1024 * 1024

A_NSA_Q = 0
A_DSA_Q = 512
A_AZ = 1024
A_BZ = 1536
A_IDX_Q = 2048
A_KV = 2304
A_GATE = 3072
A_DSA_KV = 3200
A_IDX_K = 3328
A_IDX_W = 3456
A_TOTAL = 3584

R_CX = 0
R_CZ = 512
R_QK = 1024
R_V = 2048
R_O = 2560
R_Z = 3072
R_GATE = 3584
R_TOTAL = 3840


def _dot(a, b):
    return jnp.dot(a, b, preferred_element_type=F32)


def _dot_nt(a, b):
    return lax.dot_general(a, b, (((1,), (1,)), ((), ())), preferred_element_type=F32)


def _dot_tn(a, b):
    return lax.dot_general(a, b, (((0,), (0,)), ((), ())), preferred_element_type=F32)


def _mx(a):
    return a.astype(MXU_DTYPE)


def _params(n_grid):
    return pltpu.CompilerParams(dimension_semantics=("arbitrary",) * n_grid,
                                vmem_limit_bytes=VMEM_LIMIT)


def _norm_proj_kernel(x_ref, g_ref, w_ref, o_ref):
    x = x_ref[...]
    ms = jnp.mean(x * x, axis=-1, keepdims=True)
    y = x * lax.rsqrt(ms + RMS_EPS) * g_ref[...]
    o_ref[...] = _dot(_mx(y), w_ref[...])


def _norm_proj(x2d, g, w, tm=256):
    m, d = x2d.shape
    n = w.shape[1]
    return pl.pallas_call(
        _norm_proj_kernel,
        grid=(m // tm,),
        in_specs=[pl.BlockSpec((tm, d), lambda i: (i, 0)),
                  pl.BlockSpec((1, d), lambda i: (0, 0)),
                  pl.BlockSpec((d, n), lambda i: (0, 0))],
        out_specs=pl.BlockSpec((tm, n), lambda i: (i, 0)),
        out_shape=jax.ShapeDtypeStruct((m, n), F32),
        compiler_params=_params(1),
        name="norm_proj",
    )(x2d, g.reshape(1, d), w)


def _out_proj_kernel(x_ref, ya_ref, yb_ref, p_ref, wo_ref, pw_ref, gw_ref, fg_ref, o_ref, *, final):
    half = ya_ref.shape[-1]
    y = _dot(_mx(ya_ref[...]), wo_ref[:half, :]) + _dot(_mx(yb_ref[...]), wo_ref[half:, :])
    x1 = x_ref[...] + y
    gate = jax.nn.sigmoid(_dot(_mx(x1), gw_ref[...]))
    x2 = x1 + _dot(_mx(p_ref[...]), pw_ref[...]) * gate
    if final:
        ms = jnp.mean(x2 * x2, axis=-1, keepdims=True)
        x2 = x2 * lax.rsqrt(ms + RMS_EPS) * fg_ref[...]
    o_ref[...] = x2


def _out_proj(x2d, ya, yb, p2d, w_out, ple_w, gate_w, final_g, final, tm=512):
    m, d = x2d.shape
    half = ya.shape[1]
    pd = p2d.shape[1]
    row = lambda i: (i, 0)
    whole = lambda i: (0, 0)
    return pl.pallas_call(
        functools.partial(_out_proj_kernel, final=final),
        grid=(m // tm,),
        in_specs=[pl.BlockSpec((tm, d), row), pl.BlockSpec((tm, half), row),
                  pl.BlockSpec((tm, half), row), pl.BlockSpec((tm, pd), row),
                  pl.BlockSpec((2 * half, d), whole), pl.BlockSpec((pd, d), whole),
                  pl.BlockSpec((d, d), whole), pl.BlockSpec((1, d), whole)],
        out_specs=pl.BlockSpec((tm, d), row),
        out_shape=jax.ShapeDtypeStruct((m, d), F32),
        compiler_params=_params(1),
        name="out_proj",
    )(x2d, ya, yb, p2d, _mx(w_out), _mx(ple_w), _mx(gate_w), final_g.reshape(1, d))


def _bucket_np(n):
    n = np.asarray(n)
    max_exact = N_BUCKETS // 2
    nf = np.maximum(n, 1).astype(np.float32)
    large = max_exact + (np.log(nf / np.float32(max_exact)) / np.float32(math.log(T5_MAX_DIST / max_exact))
                         * np.float32(N_BUCKETS - max_exact)).astype(np.int32)
    large = np.minimum(large, N_BUCKETS - 1)
    return np.where(n < max_exact, n, large)


def _bias_index(dist):
    dist = np.asarray(dist)
    return np.where(dist >= 0, _bucket_np(np.maximum(dist, 0)), N_BUCKETS).astype(np.int32)


def _lookup(ext, idx):
    idx = jnp.asarray(idx)[..., None]
    out = jnp.zeros(idx.shape[:-1] + (ext.shape[1],), F32)
    for b in range(ext.shape[0]):
        out = jnp.where(idx == b, ext[b], out)
    return out


def _near_tiles(tbl):
    h = tbl.shape[1]
    far = tbl[N_BUCKETS - 1]
    ext = jnp.concatenate([tbl, jnp.full((1, h), NEG, F32)], axis=0)
    kj = np.arange(TQ)[:, None]
    qi = np.arange(TQ)[None, :]
    d0 = ((_lookup(ext, _bias_index(qi - kj)) - far) * LOG2E).transpose(0, 2, 1).reshape(TQ, h * TQ)
    d1 = ((_lookup(ext, _bias_index(TQ + qi - kj)) - far) * LOG2E).transpose(0, 2, 1).reshape(TQ, h * TQ)
    edge = jnp.asarray(np.tile(np.where(kj > qi, 0.0, NEG), (1, h)), F32)
    return jnp.stack([jnp.zeros_like(d0), d1, d0, jnp.full_like(d0, NEG), edge])


def _cmp_bias(tbl, nq, n16):
    h = tbl.shape[1]
    ext = jnp.concatenate([tbl, jnp.full((1, h), NEG, F32)], axis=0)
    per_tile = TQ // CMP_STRIDE
    off = per_tile * (nq - 1)
    width = n16 + off
    cc = np.arange(width)[:, None] - off
    qi = np.arange(TQ)[None, :]
    g = _lookup(ext, _bias_index(qi - CMP_STRIDE * cc - (CMP_BLOCK - 1))) * LOG2E
    g = g.transpose(0, 2, 1).reshape(width, h * TQ)
    return jnp.stack([g[off - per_tile * i: off - per_tile * i + n16] for i in range(nq)], axis=0)


def _compress_kernel(x_ref, pos_ref, w1_ref, w2_ref, o_ref):
    n16 = o_ref.shape[0]
    acc = None
    for g in range(NSA_GROUPS):
        pa = pb = None
        for l in range(CMP_STRIDE):
            x = x_ref[pl.ds(l, n16, stride=CMP_STRIDE), :]
            a = _dot(_mx(x + pos_ref[l]), w1_ref[g, l])
            c = _dot(_mx(x + pos_ref[CMP_STRIDE + l]), w1_ref[g, CMP_STRIDE + l])
            pa = a if pa is None else pa + a
            pb = c if pb is None else pb + c
        pre = pa + pltpu.roll(pb, n16 - 1, 0)
        t = _dot(_mx(jax.nn.silu(pre)), w2_ref[g])
        acc = t if acc is None else acc + t
    o_ref[...] = acc


def _compress(u, pos, w1, w2):
    b, s, _ = u.shape
    n16 = s // CMP_STRIDE
    return pl.pallas_call(
        _compress_kernel,
        grid=(2, b),
        in_specs=[pl.BlockSpec((None, s, LANES), lambda k, i: (i, 0, A_KV // LANES + k)),
                  pl.BlockSpec((None, CMP_BLOCK, 1, LANES), lambda k, i: (k, 0, 0, 0)),
                  pl.BlockSpec((None, NSA_GROUPS, CMP_BLOCK, LANES, CMP_HIDDEN), lambda k, i: (k, 0, 0, 0, 0)),
                  pl.BlockSpec((None, NSA_GROUPS, CMP_HIDDEN, LANES), lambda k, i: (k, 0, 0, 0))],
        out_specs=pl.BlockSpec((None, None, n16, LANES), lambda k, i: (k, i, 0, 0)),
        out_shape=jax.ShapeDtypeStruct((2, b, n16, LANES), F32),
        compiler_params=_params(2),
        name="compress",
    )(u, pos, w1, w2)


def _softmax_reset(m_s, acc_s):
    m_s[...] = jnp.full(m_s.shape, -jnp.inf, F32)
    acc_s[...] = jnp.zeros(acc_s.shape, F32)


def _softmax_step(score_fn, values, m_s, acc_s, col_block):
    r = m_s.shape[1]
    s_next = score_fn(0, col_block)
    for c0 in range(0, r, col_block):
        cols = slice(c0, c0 + col_block)
        s = s_next
        if c0 + col_block < r:
            s_next = score_fn(c0 + col_block, c0 + 2 * col_block)
        m_prev = m_s[:, cols]
        m_new = jnp.maximum(m_prev, jnp.max(s, axis=0, keepdims=True))
        alpha = jnp.exp2(m_prev - m_new)
        p = _mx(jnp.exp2(s - m_new))
        v = values[c0 * len(values) // r]
        acc_s[:, cols] = alpha * acc_s[:, cols] + _dot(v, p)
        m_s[:, cols] = m_new


def _store_value_tiles(v_ref, vt_s, ones_hi):
    row = lax.broadcasted_iota(jnp.int32, (LANES, TQ), 0)

    def body(j, carry):
        vt = v_ref[pl.ds(pl.multiple_of(j * TQ, TQ), TQ), :].T
        for k, hi in enumerate(ones_hi):
            vt_s[k, j] = _mx(jnp.where((row >= HEAD_DIM) == hi, 1.0, vt))
        return carry

    lax.fori_loop(0, vt_s.shape[1], body, 0)


def _value_tiles(vt_s, k, j0, ntile):
    tiles = [vt_s[k, j0 + t] for t in range(ntile)]
    return tiles[0] if ntile == 1 else jnp.concatenate(tiles, axis=1)


def _near_bias(d_ref, idx_of_tile, j0, ntile, c0, c1):
    tiles = [d_ref[idx_of_tile(j0 + t), :, c0:c1] for t in range(ntile)]
    return tiles[0] if ntile == 1 else jnp.concatenate(tiles, axis=0)


def _stack_heads(q_ref, hi):
    lane = lax.broadcasted_iota(jnp.int32, (TQ, LANES), 1)
    out = []
    for h, up in enumerate(hi):
        v = q_ref[:, (h // 2) * LANES:(h // 2 + 1) * LANES]
        if (h % 2 == 1) != up:
            v = pltpu.roll(v, HEAD_DIM, 1)
        out.append(jnp.where((lane >= HEAD_DIM) == up, v, 0.0))
    return jnp.concatenate(out, axis=0)


def _tiles_per_chunk(nq):
    for ch in (4, 2):
        if nq % ch == 0:
            return ch
    return 1


def _nsa_kernel(q_ref, kc_ref, vc_ref, ks_ref, vs_ref, kw_ref, vw_ref, g_ref, z_ref,
                bc_ref, d_ref, et_ref, ov_ref, o_ref,
                qa_s, m_s, acc_s, osel_s, vct_s, vst_s, vwt_s, *, n_pick, ch, n_win_tiles):
    i = pl.program_id(1)
    rows = NSA_HEADS * TQ
    nb = ov_ref.shape[0]
    q0 = i * TQ
    half = LANES // NSA_GROUPS

    @pl.when(i == 0)
    def _():
        vct_s[...] = _mx(vc_ref[...].T)
        _store_value_tiles(vs_ref, vst_s, (True, False))
        _store_value_tiles(vw_ref, vwt_s, (True, False))

    qs = _mx(_stack_heads(q_ref, [h >= NSA_HPG for h in range(NSA_HEADS)]))
    qa_s[:, :LANES] = qs

    s = _dot_nt(_mx(kc_ref[...]), qs) + bc_ref[...]
    p = jnp.exp2(s - jnp.max(s, axis=0, keepdims=True))
    t_lane = q0 + (lax.broadcasted_iota(jnp.int32, (1, rows), 1) & (TQ - 1))
    any_valid = jnp.where(t_lane >= CMP_BLOCK - 1, 1.0, 0.0)
    p = p * (any_valid / jnp.sum(p, axis=0, keepdims=True))
    oc_t = _dot(vct_s[...], _mx(p))

    blk = lax.broadcasted_iota(jnp.int32, (nb, TQ), 0)
    t = q0 + lax.broadcasted_iota(jnp.int32, (nb, TQ), 1)
    cur = lax.shift_right_logical(t, int(math.log2(SEL_BLOCK)))
    forced = jnp.where(blk == 0, 1.0, jnp.where(blk == cur, 1.0, jnp.where(blk == cur - 1, 1.0, 0.0)))
    admissible = blk * SEL_BLOCK <= t
    ov = ov_ref[...]
    for g in range(NSA_GROUPS):
        c0 = g * NSA_HPG * TQ
        ps = p[:, c0:c0 + TQ]
        for h in range(1, NSA_HPG):
            ps = ps + p[:, c0 + h * TQ:c0 + (h + 1) * TQ]
        hi = _mx(ps)
        lo = _mx(ps - hi.astype(F32))
        imp_t = _dot(ov, hi) + _dot(ov, lo)
        score = jnp.where(admissible, imp_t + FORCE_BONUS * forced, NEG)
        sel = jnp.zeros((nb, TQ), F32)
        for _ in range(n_pick):
            best = jnp.max(score, axis=0, keepdims=True)
            first = jnp.min(jnp.where(score == best, blk, nb), axis=0, keepdims=True)
            hit = blk == first
            sel = jnp.where(hit, 1.0, sel)
            score = jnp.where(hit, -jnp.inf, score)
        if nb < LANES:
            sel = jnp.concatenate([sel, jnp.ones((LANES - nb, TQ), F32)], axis=0)
        block_mask = _mx((sel.T - 1.0) * (-NEG))
        for h in range(NSA_HPG):
            r0 = (g * NSA_HPG + h) * TQ
            qa_s[r0:r0 + TQ, LANES:] = block_mask

    def values(vt_s, j0, ntile):
        return [_value_tiles(vt_s, 0, j0, ntile), _value_tiles(vt_s, 1, j0, ntile)]

    _softmax_reset(m_s, acc_s)

    def sel_chunk(c, near):
        tk = ch * TQ
        koff = pl.multiple_of(c * tk, tk)
        k_aug = jnp.concatenate([_mx(ks_ref[pl.ds(koff, tk), :]), et_ref[pl.ds(koff, tk), :]], axis=1)

        def scores(c0, c1):
            s = _dot_nt(k_aug, qa_s[c0:c1, :])
            if near:
                s = s + _near_bias(d_ref, lambda j: jnp.clip(j - (i - 2), 0, 3), c * ch, ch, c0, c1)
            return s

        _softmax_step(scores, values(vst_s, c * ch, ch), m_s, acc_s, COL_BLOCK)

    first_near = jnp.maximum(i - 1, 0) // ch

    def far_body(c, carry):
        sel_chunk(c, False)
        return carry

    def near_body(c, carry):
        sel_chunk(c, True)
        return carry

    lax.fori_loop(0, first_near, far_body, 0)
    lax.fori_loop(first_near, i // ch + 1, near_body, 0)
    osel_s[...] = acc_s[...]

    _softmax_reset(m_s, acc_s)
    n_win = WINDOW // TQ
    jw = jnp.maximum(i - (n_win_tiles - 1), 0)
    koff = pl.multiple_of(jw * TQ, TQ)
    tk = n_win_tiles * TQ

    def win_idx(j):
        r = j - (i - n_win)
        return jnp.where(r == 0, 4, jnp.clip(r - (n_win - 2), 0, 3))

    kw = _mx(kw_ref[pl.ds(koff, tk), :])
    _softmax_step(lambda c0, c1: (_dot_nt(kw, qa_s[c0:c1, :LANES])
                                  + _near_bias(d_ref, win_idx, jw, n_win_tiles, c0, c1)),
                  values(vwt_s, jw, n_win_tiles), m_s, acc_s, COL_BLOCK)
    ow_t = acc_s[...]
    os_t = osel_s[...]

    gate_t = jax.nn.sigmoid(g_ref[...]).T
    for m in range(NSA_HEADS // 2):
        parts = []
        for hh in (2 * m, 2 * m + 1):
            g = hh // NSA_HPG
            r = slice(g * half, (g + 1) * half)
            d = (1 - g) * half
            c = slice(hh * TQ, (hh + 1) * TQ)
            parts.append(gate_t[3 * hh:3 * hh + 1, :] * oc_t[r, c]
                         + gate_t[3 * hh + 1:3 * hh + 2, :] * (os_t[r, c] / os_t[d:d + 1, c])
                         + gate_t[3 * hh + 2:3 * hh + 3, :] * (ow_t[r, c] / ow_t[d:d + 1, c]))
        slab = jnp.concatenate(parts, axis=0).T
        o_ref[:, m * LANES:(m + 1) * LANES] = slab * jax.nn.silu(z_ref[:, m * LANES:(m + 1) * LANES])


def _nsa(u, cmp_kv, bias_c, d_tiles, e_t, ov, n_pick):
    b, s, _ = u.shape
    nq = s // TQ
    n16 = cmp_kv.shape[2]
    nb = ov.shape[0]
    rows = NSA_HEADS * TQ
    n_win_tiles = min(WINDOW // TQ + 1, nq)
    seq = lambda col: pl.BlockSpec((None, s, LANES), lambda bi, i, col=col: (bi, 0, col // LANES))
    return pl.pallas_call(
        functools.partial(_nsa_kernel, n_pick=n_pick, ch=_tiles_per_chunk(nq), n_win_tiles=n_win_tiles),
        grid=(b, nq),
        in_specs=[
            pl.BlockSpec((None, TQ, NSA_HEADS * HEAD_DIM), lambda bi, i: (bi, i, A_NSA_Q // (NSA_HEADS * HEAD_DIM))),
            pl.BlockSpec((None, None, n16, LANES), lambda bi, i: (0, bi, 0, 0)),
            pl.BlockSpec((None, None, n16, LANES), lambda bi, i: (1, bi, 0, 0)),
            seq(A_KV + 2 * LANES), seq(A_KV + 3 * LANES), seq(A_KV + 4 * LANES), seq(A_KV + 5 * LANES),
            pl.BlockSpec((None, TQ, LANES), lambda bi, i: (bi, i, A_GATE // LANES)),
            pl.BlockSpec((None, TQ, 512), lambda bi, i: (bi, i, A_AZ // 512)),
            pl.BlockSpec((None, n16, rows), lambda bi, i: (i, 0, 0)),
            pl.BlockSpec(d_tiles.shape, lambda bi, i: (0, 0, 0)),
            pl.BlockSpec((s, LANES), lambda bi, i: (0, 0)),
            pl.BlockSpec((nb, n16), lambda bi, i: (0, 0)),
        ],
        out_specs=pl.BlockSpec((None, TQ, 512), lambda bi, i: (bi, i, 0)),
        out_shape=jax.ShapeDtypeStruct((b, s, 512), F32),
        scratch_shapes=[pltpu.VMEM((rows, 2 * LANES), MXU_DTYPE),
                        pltpu.VMEM((1, rows), F32),
                        pltpu.VMEM((LANES, rows), F32),
                        pltpu.VMEM((LANES, rows), F32),
                        pltpu.VMEM((LANES, n16), MXU_DTYPE),
                        pltpu.VMEM((NSA_GROUPS, nq, LANES, TQ), MXU_DTYPE),
                        pltpu.VMEM((NSA_GROUPS, nq, LANES, TQ), MXU_DTYPE)],
        compiler_params=_params(2),
        name="nsa",
    )(u, cmp_kv, cmp_kv, u, u, u, u, u, u, bias_c, d_tiles, e_t, ov)


def _dsa_kernel(q_ref, kv_ref, qi_ref, ki_ref, wi_ref, z_ref, d_ref, tri_ref, o_ref,
                qs_s, qis_s, sc_s, hi_s, lo_s, eq_s, m_s, acc_s, seen_s, vt_s, *, k_top, ch):
    i = pl.program_id(1)

    @pl.when(i == 0)
    def _():
        _store_value_tiles(kv_ref, vt_s, (False,))

    qs_s[...] = _mx(_stack_heads(q_ref, [False] * DSA_HEADS))
    qis_s[...] = _mx(_stack_heads(qi_ref, [False] * IDX_HEADS))
    wi_t = wi_ref[...].T
    w_rows = [jnp.broadcast_to(wi_t[h:h + 1, :], (TQ, TQ)) for h in range(IDX_HEADS)]
    key_idx = lax.broadcasted_iota(jnp.int32, (TQ, TQ), 0)
    q_idx = lax.broadcasted_iota(jnp.int32, (TQ, TQ), 1)
    n_chunks = (i + ch) // ch
    pairs = ch // 2

    def score_chunk(c, carry):
        koff = pl.multiple_of(c * (ch * TQ), ch * TQ)
        r = jnp.maximum(_dot_nt(_mx(ki_ref[pl.ds(koff, ch * TQ), :]), qis_s[...]), 0.0)
        keys = []
        for t in range(ch):
            jt = c * ch + t
            rt = r[t * TQ:(t + 1) * TQ]
            sc = w_rows[0] * rt[:, 0:TQ]
            for h in range(1, IDX_HEADS):
                sc = sc + w_rows[h] * rt[:, h * TQ:(h + 1) * TQ]
            sc = jnp.where((jt == i) & (key_idx > q_idx), NEG, sc)
            bits = pltpu.bitcast(sc, jnp.int32)
            key = bits ^ (lax.shift_right_arithmetic(bits, 31) & jnp.int32(0x7FFFFFFF))
            key = jnp.where(sc == 0.0, 0, key)
            keys.append(jnp.where(jt > i, INT_MIN, key))
            sc_s[jt] = keys[-1]
        for pr in range(pairs):
            a, b = keys[2 * pr], keys[2 * pr + 1]
            hi_s[c * pairs + pr] = (a & jnp.int32(-65536)) | lax.shift_right_logical(b, 16)
            lo_s[c * pairs + pr] = (lax.shift_left((a & 0xFFFF) ^ 0x8000, 16)
                                    | ((b & 0xFFFF) ^ 0x8000))
        return carry

    lax.fori_loop(0, n_chunks, score_chunk, 0)

    def halves(words):
        return pltpu.bitcast(words, jnp.int16)

    def both_halves(v):
        w = lax.shift_left(v, 16) | (v & 0xFFFF)
        return halves(jnp.broadcast_to(w, (TQ, TQ)))

    def count16(words_s, pred, weight_s=None):
        one = jnp.ones((2 * TQ, TQ), jnp.int16)

        def body(c, acc):
            for pr in range(pairs):
                w = one if weight_s is None else halves(weight_s[c * pairs + pr])
                acc = acc + jnp.where(pred(halves(words_s[c * pairs + pr])), w, jnp.int16(0))
            return acc
        acc = lax.fori_loop(0, n_chunks, body, jnp.zeros((2 * TQ, TQ), jnp.int16))
        return jnp.sum(acc.astype(jnp.int32), axis=0, keepdims=True)

    def bisect16(words_s, target, weight_s=None):
        c0 = count16(words_s, lambda x: x >= jnp.int16(0), weight_s)
        v0 = jnp.where(c0 >= target, 0, -32768)

        def step(it, v):
            cand = v | lax.shift_left(jnp.int32(1), 14 - it)
            cand16 = both_halves(cand)
            c = count16(words_s, lambda x: x >= cand16, weight_s)
            return jnp.where(c >= target, cand, v)

        return lax.fori_loop(0, 15, step, v0)

    top = bisect16(hi_s, k_top)
    top16 = both_halves(top)
    above = count16(hi_s, lambda x: x > top16)

    def mark(c, carry):
        for pr in range(pairs):
            hit = jnp.where(halves(hi_s[c * pairs + pr]) == top16, jnp.int16(1), jnp.int16(0))
            eq_s[c * pairs + pr] = pltpu.bitcast(hit, jnp.int32)
        return carry

    lax.fori_loop(0, n_chunks, mark, 0)
    low = bisect16(lo_s, k_top - above, eq_s)
    thr = lax.shift_left(top, 16) | ((low ^ 0x8000) & 0xFFFF)
    thr = jnp.broadcast_to(thr, (TQ, TQ))

    def count_above(c, acc):
        for t in range(ch):
            acc = acc + jnp.where(sc_s[c * ch + t] > thr, 1.0, 0.0)
        return acc

    n_above = lax.fori_loop(0, n_chunks, count_above, jnp.zeros((TQ, TQ), F32))
    need = float(k_top) - jnp.sum(n_above, axis=0, keepdims=True)

    _softmax_reset(m_s, acc_s)
    seen_s[...] = jnp.zeros((TQ, TQ), F32)

    def att_chunk(c, near):
        tk = ch * TQ
        koff = pl.multiple_of(c * tk, tk)
        masks = []
        for t in range(ch):
            key = sc_s[c * ch + t]
            eq = key == thr
            prefix = _dot(tri_ref[...], _mx(jnp.where(eq, 1.0, 0.0)))
            seen = seen_s[...]
            take = jnp.where(eq, jnp.where(seen + prefix <= need, 1.0, 0.0), 0.0)
            seen_s[...] = seen + prefix[TQ - 1:TQ, :]
            masks.append(jnp.where(key > thr, 0.0, (take - 1.0) * (-NEG)))
        mb = masks[0] if ch == 1 else jnp.concatenate(masks, axis=0)
        mb = jnp.concatenate([mb] * DSA_HEADS, axis=1)
        kvm = _mx(kv_ref[pl.ds(koff, tk), :])

        def scores(c0, c1):
            s = _dot_nt(kvm, qs_s[c0:c1, :]) + mb[:, c0:c1]
            if near:
                s = s + _near_bias(d_ref, lambda j: jnp.clip(j - (i - 2), 0, 3), c * ch, ch, c0, c1)
            return s

        _softmax_step(scores, [_value_tiles(vt_s, 0, c * ch, ch)], m_s, acc_s, DSA_HEADS * TQ)

    first_near = jnp.maximum(i - 1, 0) // ch

    def far_body(c, carry):
        att_chunk(c, False)
        return carry

    def near_body(c, carry):
        att_chunk(c, True)
        return carry

    lax.fori_loop(0, first_near, far_body, 0)
    lax.fori_loop(first_near, n_chunks, near_body, 0)

    o_t = acc_s[...]
    o_t = o_t / o_t[0:1, :]
    for m in range(DSA_HEADS // 2):
        pair = [o_t[HEAD_DIM:, hh * TQ:(hh + 1) * TQ] for hh in (2 * m, 2 * m + 1)]
        slab = jnp.concatenate(pair, axis=0).T
        o_ref[:, m * LANES:(m + 1) * LANES] = slab * jax.nn.silu(z_ref[:, m * LANES:(m + 1) * LANES])


def _dsa(u, d_tiles, tri, k_top):
    b, s, _ = u.shape
    nq = s // TQ
    assert nq % 2 == 0, "the packed threshold search pairs key tiles"
    rows = DSA_HEADS * TQ
    irows = IDX_HEADS * TQ
    seq = lambda col: pl.BlockSpec((None, s, LANES), lambda bi, i, col=col: (bi, 0, col // LANES))
    return pl.pallas_call(
        functools.partial(_dsa_kernel, k_top=k_top, ch=_tiles_per_chunk(nq)),
        grid=(b, nq),
        in_specs=[
            pl.BlockSpec((None, TQ, DSA_HEADS * HEAD_DIM), lambda bi, i: (bi, i, A_DSA_Q // (DSA_HEADS * HEAD_DIM))),
            seq(A_DSA_KV),
            pl.BlockSpec((None, TQ, IDX_HEADS * IDX_DIM), lambda bi, i: (bi, i, A_IDX_Q // (IDX_HEADS * IDX_DIM))),
            seq(A_IDX_K),
            pl.BlockSpec((None, TQ, LANES), lambda bi, i: (bi, i, A_IDX_W // LANES)),
            pl.BlockSpec((None, TQ, 512), lambda bi, i: (bi, i, A_BZ // 512)),
            pl.BlockSpec(d_tiles.shape, lambda bi, i: (0, 0, 0)),
            pl.BlockSpec((TQ, TQ), lambda bi, i: (0, 0)),
        ],
        out_specs=pl.BlockSpec((None, TQ, 512), lambda bi, i: (bi, i, 0)),
        out_shape=jax.ShapeDtypeStruct((b, s, 512), F32),
        scratch_shapes=[pltpu.VMEM((rows, LANES), MXU_DTYPE),
                        pltpu.VMEM((irows, LANES), MXU_DTYPE),
                        pltpu.VMEM((nq, TQ, TQ), jnp.int32),
                        pltpu.VMEM((nq // 2, TQ, TQ), jnp.int32),
                        pltpu.VMEM((nq // 2, TQ, TQ), jnp.int32),
                        pltpu.VMEM((nq // 2, TQ, TQ), jnp.int32),
                        pltpu.VMEM((1, rows), F32),
                        pltpu.VMEM((LANES, rows), F32),
                        pltpu.VMEM((TQ, TQ), F32),
                        pltpu.VMEM((1, nq, LANES, TQ), MXU_DTYPE)],
        compiler_params=_params(2),
        name="dsa",
    )(u, u, u, u, u, u, d_tiles, tri)


def _causal_conv(x, xe_s, w_ref, b_ref):
    t = x.shape[0]
    xe_s[8:8 + t, :] = x
    y = b_ref[...] + w_ref[CONV_WIDTH - 1:CONV_WIDTH, :] * x
    for k in range(1, CONV_WIDTH):
        y = y + w_ref[CONV_WIDTH - 1 - k:CONV_WIDTH - k, :] * xe_s[8 - k:8 - k + t, :]
    xe_s[0:8, :] = x[t - 8:t, :]
    return y


def _rglru_kernel(x_ref, z_ref, cw_ref, cb_ref, wa_ref, ba_ref, wx_ref, bx_ref, lam_ref, o_ref,
                  xe_s, h_s):
    t, c = x_ref.shape

    @pl.when(pl.program_id(1) == 0)
    def _():
        xe_s[0:8, :] = jnp.zeros((8, c), F32)
        h_s[...] = jnp.zeros(h_s.shape, F32)

    xc = _causal_conv(x_ref[...], xe_s, cw_ref, cb_ref)
    xcm = _mx(xc)
    r = jax.nn.sigmoid(_dot(xcm, wa_ref[...]) + ba_ref[...])
    ig = jax.nn.sigmoid(_dot(xcm, wx_ref[...]) + bx_ref[...])
    nl = -lam_ref[...]
    softplus = jnp.maximum(nl, 0.0) + jnp.log1p(jnp.exp(-jnp.abs(nl)))
    log_a = (-LRU_C * r) * softplus
    a = jnp.exp(log_a)
    bb = jnp.sqrt(-jnp.tanh(log_a) * (a * a + 1.0)) * (ig * xc)
    row = lax.broadcasted_iota(jnp.int32, (t, c), 0) & (SUBLANES - 1)
    d = 1
    while d < SUBLANES:
        live = row >= d
        a_sh = jnp.where(live, pltpu.roll(a, d, 0), 1.0)
        b_sh = jnp.where(live, pltpu.roll(bb, d, 0), 0.0)
        bb = a * b_sh + bb
        a = a * a_sh
        d *= 2
    carry = h_s[0:1, :]
    groups = []
    for g in range(t // SUBLANES):
        rows = slice(g * SUBLANES, (g + 1) * SUBLANES)
        groups.append(a[rows] * carry + bb[rows])
        carry = groups[-1][SUBLANES - 1:SUBLANES, :]
    h = jnp.concatenate(groups, axis=0)
    h_s[0:1, :] = carry
    o_ref[...] = h * jax.nn.silu(z_ref[...])


def _rglru(u, conv_w, conv_b, wa_bd, ba, wx_bd, bx, lam, t=256):
    b, s, _ = u.shape
    c = LRU_WIDTH
    whole = lambda bi, i: (0, 0)
    return pl.pallas_call(
        _rglru_kernel,
        grid=(b, s // t),
        in_specs=[pl.BlockSpec((None, t, c), lambda bi, i: (bi, i, R_CX // c)),
                  pl.BlockSpec((None, t, c), lambda bi, i: (bi, i, R_CZ // c)),
                  pl.BlockSpec((CONV_WIDTH, c), whole), pl.BlockSpec((1, c), whole),
                  pl.BlockSpec((c, c), whole), pl.BlockSpec((1, c), whole),
                  pl.BlockSpec((c, c), whole), pl.BlockSpec((1, c), whole),
                  pl.BlockSpec((1, c), whole)],
        out_specs=pl.BlockSpec((None, t, c), lambda bi, i: (bi, i, 0)),
        out_shape=jax.ShapeDtypeStruct((b, s, c), F32),
        scratch_shapes=[pltpu.VMEM((t + 8, c), F32), pltpu.VMEM((8, c), F32)],
        compiler_params=_params(2),
        name="rglru",
    )(u, u, conv_w, conv_b.reshape(1, c), wa_bd, ba.reshape(1, c), wx_bd, bx.reshape(1, c),
      lam.reshape(1, c))


def _mlstm_kernel(qk_ref, v_ref, g_ref, og_ref, z_ref, cw_ref, cb_ref, gb_ref, o_ref,
                  xe_s, c_s, m_s):
    t = qk_ref.shape[0]
    L = MLSTM_CHUNK
    H = MLSTM_HEADS
    D = MLSTM_DIM

    @pl.when(pl.program_id(1) == 0)
    def _():
        xe_s[0:8, :] = jnp.zeros((8, xe_s.shape[1]), F32)
        c_s[...] = jnp.zeros(c_s.shape, F32)
        m_s[...] = jnp.zeros(m_s.shape, F32)

    qk = jax.nn.silu(_causal_conv(qk_ref[...], xe_s, cw_ref, cb_ref))

    lane = lax.broadcasted_iota(jnp.int32, (t, LANES), 1)
    row_in = lax.broadcasted_iota(jnp.int32, (t, LANES), 0) & (L - 1)
    gs = g_ref[...] + gb_ref[...]
    log_sig = jnp.minimum(gs, 0.0) - jnp.log1p(jnp.exp(-jnp.abs(gs)))
    pre = jnp.where(lane < H, gs, log_sig)
    cum = pre
    d = 1
    while d < L:
        cum = cum + jnp.where(row_in >= d, pltpu.roll(cum, d, 0), 0.0)
        d *= 2
    comb = jnp.where(lane < H, pre, cum)
    comb_t = comb.T

    causal_t = (lax.broadcasted_iota(jnp.int32, (L, L), 0) <= lax.broadcasted_iota(jnp.int32, (L, L), 1))
    pad_rows = jnp.zeros((7, L), F32)
    v_ts = [v_ref[:, h * D:(h + 1) * D].T for h in range(H)]
    states = [c_s[h] for h in range(H)]
    m_prevs = [m_s[h:h + 1, 0:1] for h in range(H)]
    outs = [[] for _ in range(H)]
    for c in range(t // L):
        r = slice(c * L, (c + 1) * L)
        for h in range(H):
            v_t, state, m_prev = v_ts[h], states[h], m_prevs[h]
            q_ = _mx(qk[r, h * D:(h + 1) * D])
            k_ = _mx(qk[r, (H + h) * D:(H + h + 1) * D] * (D ** -0.5))
            li_row = comb_t[h:h + 1, r]
            b_row = comb_t[H + h:H + h + 1, r]
            lb_col = comb[r, h:h + 1] - comb[r, H + h:H + h + 1]
            b_last = b_row[:, L - 1:L]

            dmat_t = jnp.where(causal_t, b_row + lb_col, -jnp.inf)
            decay_row = b_last - b_row + li_row
            m_new = jnp.maximum(b_last + m_prev, jnp.max(decay_row, axis=-1, keepdims=True))
            inter = b_row + m_prev
            m_t = jnp.maximum(inter, jnp.max(dmat_t, axis=0, keepdims=True))
            w_t = _dot_nt(k_, q_) * jnp.exp(dmat_t - m_t)
            prev = jnp.exp(inter - m_t)

            read = _dot_nt(_mx(state), q_)
            num_t = prev * read[:D] + _dot(_mx(v_t[:, r]), _mx(w_t))
            den = prev * read[D:D + 1] + jnp.sum(w_t, axis=0, keepdims=True)
            outs[h].append(num_t / jnp.maximum(jnp.abs(den), jnp.exp(-m_t)))

            wk = jnp.exp(decay_row - m_new)
            inc = jnp.concatenate([v_t[:, r] * wk, wk, pad_rows], axis=0)
            states[h] = jnp.exp(b_last + m_prev - m_new) * state + _dot(_mx(inc), k_)
            m_prevs[h] = m_new

    for h in range(H):
        c_s[h] = states[h]
        m_s[h:h + 1, :] = jnp.broadcast_to(m_prevs[h], (1, LANES))
        out = jnp.concatenate(outs[h], axis=1).T
        cols = slice(h * D, (h + 1) * D)
        o_ref[:, cols] = jax.nn.sigmoid(og_ref[:, cols]) * out * jax.nn.silu(z_ref[:, cols])


def _mlstm(u, conv_w, conv_b, gate_bias, t=128):
    b, s, _ = u.shape
    w = MLSTM_WIDTH
    whole = lambda bi, i: (0, 0)
    return pl.pallas_call(
        _mlstm_kernel,
        grid=(b, s // t),
        in_specs=[pl.BlockSpec((None, t, 2 * w), lambda bi, i: (bi, i, R_QK // (2 * w))),
                  pl.BlockSpec((None, t, w), lambda bi, i: (bi, i, R_V // w)),
                  pl.BlockSpec((None, t, LANES), lambda bi, i: (bi, i, R_GATE // LANES)),
                  pl.BlockSpec((None, t, w), lambda bi, i: (bi, i, R_O // w)),
                  pl.BlockSpec((None, t, w), lambda bi, i: (bi, i, R_Z // w)),
                  pl.BlockSpec((CONV_WIDTH, 2 * w), whole), pl.BlockSpec((1, 2 * w), whole),
                  pl.BlockSpec((1, LANES), whole)],
        out_specs=pl.BlockSpec((None, t, w), lambda bi, i: (bi, i, 0)),
        out_shape=jax.ShapeDtypeStruct((b, s, w), F32),
        scratch_shapes=[pltpu.VMEM((t + 8, 2 * w), F32),
                        pltpu.VMEM((MLSTM_HEADS, MLSTM_DIM + 8, MLSTM_DIM), F32),
                        pltpu.VMEM((8, LANES), F32)],
        compiler_params=_params(2),
        name="mlstm",
    )(u, u, u, u, u, conv_w, conv_b.reshape(1, 2 * w), gate_bias)


def _attn_weights(w_in):
    d = w_in.shape[0]
    widths = (512, 768, 24, 512, 512, 64, 64, 256, 64, 4, 512)
    offs = np.concatenate([[0], np.cumsum(widths)])
    a_q, a_kv, a_g, a_z, b_q, b_k, b_v, b_qi, b_ki, b_wi, b_z = [
        w_in[:, offs[k]:offs[k + 1]] for k in range(len(widths))]
    scale = HEAD_DIM ** -0.5 * LOG2E
    zeros = lambda n: jnp.zeros((d, n), w_in.dtype)
    cols = [
        a_q * scale, b_q * scale, a_z, b_z, b_qi, a_kv,
        a_g, zeros(LANES - 24),
        b_k, b_v,
        b_ki, zeros(LANES - IDX_DIM),
        b_wi * (IDX_DIM ** -0.5 * IDX_HEADS ** -0.5), zeros(LANES - IDX_HEADS),
    ]
    w = jnp.concatenate(cols, axis=1)
    assert w.shape[1] == A_TOTAL
    return _mx(w)


def _rec_weights(w_in):
    d = w_in.shape[0]
    widths = (512, 512, 512, 512, 512, 4, 4, 512, 512)
    offs = np.concatenate([[0], np.cumsum(widths)])
    c_x, c_z, d_q, d_k, d_v, d_i, d_f, d_o, d_z = [w_in[:, offs[k]:offs[k + 1]] for k in range(len(widths))]
    w = jnp.concatenate([c_x, c_z, d_q, d_k, d_v, d_o, d_z, d_i, d_f,
                         jnp.zeros((d, R_TOTAL - R_GATE - 2 * MLSTM_HEADS), w_in.dtype)], axis=1)
    assert w.shape[1] == R_TOTAL
    return _mx(w)


def _block_diag(w):
    g, n, _ = w.shape
    eye = jnp.eye(g, dtype=w.dtype)
    return (eye[:, None, :, None] * w[:, :, None, :]).reshape(g * n, g * n)


def _attention_layer(x2d, b, s, norm_g, w_in, cmp_pos_k, cmp_w1_k, cmp_w2_k, cmp_pos_v, cmp_w1_v, cmp_w2_v,
                     t5_table):
    nq = s // TQ
    n16 = s // CMP_STRIDE
    nb = s // SEL_BLOCK
    u = _norm_proj(x2d, norm_g, _attn_weights(w_in)).reshape(b, s, A_TOTAL)

    pos = jnp.stack([cmp_pos_k, cmp_pos_v])
    pos = jnp.concatenate([pos] * NSA_GROUPS, axis=-1).reshape(2, CMP_BLOCK, 1, LANES)
    w1 = jnp.stack([cmp_w1_k, cmp_w1_v]).reshape(2, CMP_BLOCK, HEAD_DIM, CMP_HIDDEN)
    z1 = jnp.zeros_like(w1)
    w1 = _mx(jnp.stack([jnp.concatenate([w1, z1], axis=2), jnp.concatenate([z1, w1], axis=2)], axis=1))
    w2 = jnp.stack([cmp_w2_k, cmp_w2_v])
    zpad = jnp.zeros_like(w2)
    w2 = _mx(jnp.stack([jnp.concatenate([w2, zpad], -1), jnp.concatenate([zpad, w2], -1)], axis=1))
    cmp_kv = _compress(u, pos, w1, w2)

    tbl = t5_table.astype(F32)
    tbl_a, tbl_b = tbl[:, :NSA_HEADS], tbl[:, NSA_HEADS:]
    kj = np.arange(TQ)[:, None]
    qi = np.arange(TQ)[None, :]
    assert nb <= LANES
    e_t = jnp.asarray((np.arange(s)[:, None] // SEL_BLOCK) == np.arange(LANES)[None, :], MXU_DTYPE)
    ci = np.arange(n16)[None, :]
    sj = np.arange(nb)[:, None]
    ov = jnp.asarray((ci * CMP_STRIDE < (sj + 1) * SEL_BLOCK) & (ci * CMP_STRIDE + CMP_BLOCK > sj * SEL_BLOCK)
                     & (ci < n16 - 1), MXU_DTYPE)
    tri = jnp.asarray(qi <= kj, MXU_DTYPE)

    ya = _nsa(u, cmp_kv, _cmp_bias(tbl_a, nq, n16), _near_tiles(tbl_a), e_t, ov, min(N_SEL, nb))
    yb = _dsa(u, _near_tiles(tbl_b), tri, min(DSA_TOPK_MAX, s // 4))
    return ya.reshape(b * s, -1), yb.reshape(b * s, -1)


def _recurrent_layer(x2d, b, s, norm_g, w_in, conv_c_w, conv_c_b, wa, ba, wx, bx, lam,
                     conv_d_w, conv_d_b, b_i, b_f):
    u = _norm_proj(x2d, norm_g, _rec_weights(w_in)).reshape(b, s, R_TOTAL)
    yc = _rglru(u, conv_c_w, conv_c_b, _mx(_block_diag(wa)), ba, _mx(_block_diag(wx)), bx, lam)
    gate_bias = jnp.concatenate([b_i, b_f, jnp.zeros((LANES - 2 * MLSTM_HEADS,), F32)]).reshape(1, LANES)
    yd = _mlstm(u, conv_d_w, conv_d_b, gate_bias)
    return yc.reshape(b * s, -1), yd.reshape(b * s, -1)


def kernel(x, p, norm_g, final_g, ple_w, ple_gate_w, t5_table, attn_w_in, attn_w_out, cmp_pos_k, cmp_w1_k, cmp_w2_k, cmp_pos_v, cmp_w1_v, cmp_w2_v, rec_w_in, rec_w_out, lru_conv_w, lru_conv_b, lru_wa, lru_ba, lru_wx, lru_bx, lru_lambda, mlstm_conv_w, mlstm_conv_b, mlstm_b_i, mlstm_b_f):
    b, s, d = x.shape
    depth = p.shape[0]
    x2d = x.reshape(b * s, d)
    for i in range(depth):
        j = i // 2
        if i % 2 == 0:
            ya, yb = _attention_layer(x2d, b, s, norm_g[i], attn_w_in[j], cmp_pos_k[j], cmp_w1_k[j], cmp_w2_k[j],
                                      cmp_pos_v[j], cmp_w1_v[j], cmp_w2_v[j], t5_table)
            w_out = attn_w_out[j]
        else:
            ya, yb = _recurrent_layer(x2d, b, s, norm_g[i], rec_w_in[j], lru_conv_w[j], lru_conv_b[j],
                                      lru_wa[j], lru_ba[j], lru_wx[j], lru_bx[j], lru_lambda[j],
                                      mlstm_conv_w[j], mlstm_conv_b[j], mlstm_b_i[j], mlstm_b_f[j])
            w_out = rec_w_out[j]
        x2d = _out_proj(x2d, ya, yb, p[i].reshape(b * s, -1), w_out, ple_w[i], ple_gate_w[i], final_g,
                        final=(i == depth - 1))
    return x2d.reshape(b, s, d)
```

```python
import functools
import math

import numpy as np
import jax
import jax.numpy as jnp
from jax import lax
from jax.experimental import pallas as pl
from jax.experimental.pallas import tpu as pltpu

F32 = jnp.float32
MXU_DTYPE = jnp.bfloat16

HEAD_DIM = 64
NSA_HEADS = 8
NSA_GROUPS = 2
NSA_HPG = NSA_HEADS // NSA_GROUPS
CMP_BLOCK = 32
CMP_STRIDE = 16
CMP_HIDDEN = 256
SEL_BLOCK = 64
N_SEL = 8
WINDOW = 512
FORCE_BONUS = 1e4
DSA_HEADS = 8
IDX_HEADS = 4
IDX_DIM = 64
DSA_TOPK_MAX = 256
N_BUCKETS = 32
T5_MAX_DIST = 128
LRU_WIDTH = 512
LRU_BLOCKS = 8
CONV_WIDTH = 4
LRU_C = 8.0
MLSTM_HEADS = 4
MLSTM_DIM = 128
MLSTM_WIDTH = MLSTM_HEADS * MLSTM_DIM
MLSTM_CHUNK = 64
RMS_EPS = 1e-6
NEG = -1e30

LANES = 128
SUBLANES = 8
COL_BLOCK = 256
TQ = 128
INT_MIN = -2 ** 31
LOG2E = math.log2(math.e)
VMEM_LIMIT = 48 * 1024 * 1024

A_NSA_Q = 0
A_DSA_Q = 512
A_AZ = 1024
A_BZ = 1536
A_IDX_Q = 2048
A_KV = 2304
A_GATE = 3072
A_DSA_KV = 3200
A_IDX_K = 3328
A_IDX_W = 3456
A_TOTAL = 3584

R_CX = 0
R_CZ = 512
R_QK = 1024
R_V = 2048
R_O = 2560
R_Z = 3072
R_GATE = 3584
R_TOTAL = 3840


def _dot(a, b):
    return jnp.dot(a, b, preferred_element_type=F32)


def _dot_nt(a, b):
    return lax.dot_general(a, b, (((1,), (1,)), ((), ())), preferred_element_type=F32)


def _dot_tn(a, b):
    return lax.dot_general(a, b, (((0,), (0,)), ((), ())), preferred_element_type=F32)


def _mx(a):
    return a.astype(MXU_DTYPE)


def _params(n_grid):
    return pltpu.CompilerParams(dimension_semantics=("arbitrary",) * n_grid,
                                vmem_limit_bytes=VMEM_LIMIT)


def _norm_proj_kernel(x_ref, g_ref, w_ref, o_ref):
    x = x_ref[...]
    ms = jnp.mean(x * x, axis=-1, keepdims=True)
    y = x * lax.rsqrt(ms + RMS_EPS) * g_ref[...]
    o_ref[...] = _dot(_mx(y), w_ref[...])


def _norm_proj(x2d, g, w, tm=256):
    m, d = x2d.shape
    n = w.shape[1]
    return pl.pallas_call(
        _norm_proj_kernel,
        grid=(m // tm,),
        in_specs=[pl.BlockSpec((tm, d), lambda i: (i, 0)),
                  pl.BlockSpec((1, d), lambda i: (0, 0)),
                  pl.BlockSpec((d, n), lambda i: (0, 0))],
        out_specs=pl.BlockSpec((tm, n), lambda i: (i, 0)),
        out_shape=jax.ShapeDtypeStruct((m, n), F32),
        compiler_params=_params(1),
        name="norm_proj",
    )(x2d, g.reshape(1, d), w)


def _out_proj_kernel(x_ref, ya_ref, yb_ref, p_ref, wo_ref, pw_ref, gw_ref, fg_ref, o_ref, *, final):
    half = ya_ref.shape[-1]
    y = _dot(_mx(ya_ref[...]), wo_ref[:half, :]) + _dot(_mx(yb_ref[...]), wo_ref[half:, :])
    x1 = x_ref[...] + y
    gate = jax.nn.sigmoid(_dot(_mx(x1), gw_ref[...]))
    x2 = x1 + _dot(_mx(p_ref[...]), pw_ref[...]) * gate
    if final:
        ms = jnp.mean(x2 * x2, axis=-1, keepdims=True)
        x2 = x2 * lax.rsqrt(ms + RMS_EPS) * fg_ref[...]
    o_ref[...] = x2


def _out_proj(x2d, ya, yb, p2d, w_out, ple_w, gate_w, final_g, final, tm=512):
    m, d = x2d.shape
    half = ya.shape[1]
    pd = p2d.shape[1]
    row = lambda i: (i, 0)
    whole = lambda i: (0, 0)
    return pl.pallas_call(
        functools.partial(_out_proj_kernel, final=final),
        grid=(m // tm,),
        in_specs=[pl.BlockSpec((tm, d), row), pl.BlockSpec((tm, half), row),
                  pl.BlockSpec((tm, half), row), pl.BlockSpec((tm, pd), row),
                  pl.BlockSpec((2 * half, d), whole), pl.BlockSpec((pd, d), whole),
                  pl.BlockSpec((d, d), whole), pl.BlockSpec((1, d), whole)],
        out_specs=pl.BlockSpec((tm, d), row),
        out_shape=jax.ShapeDtypeStruct((m, d), F32),
        compiler_params=_params(1),
        name="out_proj",
    )(x2d, ya, yb, p2d, _mx(w_out), _mx(ple_w), _mx(gate_w), final_g.reshape(1, d))


def _bucket_np(n):
    n = np.asarray(n)
    max_exact = N_BUCKETS // 2
    nf = np.maximum(n, 1).astype(np.float32)
    large = max_exact + (np.log(nf / np.float32(max_exact)) / np.float32(math.log(T5_MAX_DIST / max_exact))
                         * np.float32(N_BUCKETS - max_exact)).astype(np.int32)
    large = np.minimum(large, N_BUCKETS - 1)
    return np.where(n < max_exact, n, large)


def _bias_index(dist):
    dist = np.asarray(dist)
    return np.where(dist >= 0, _bucket_np(np.maximum(dist, 0)), N_BUCKETS).astype(np.int32)


def _lookup(ext, idx):
    idx = jnp.asarray(idx)[..., None]
    out = jnp.zeros(idx.shape[:-1] + (ext.shape[1],), F32)
    for b in range(ext.shape[0]):
        out = jnp.where(idx == b, ext[b], out)
    return out


def _near_tiles(tbl):
    h = tbl.shape[1]
    far = tbl[N_BUCKETS - 1]
    ext = jnp.concatenate([tbl, jnp.full((1, h), NEG, F32)], axis=0)
    kj = np.arange(TQ)[:, None]
    qi = np.arange(TQ)[None, :]
    d0 = ((_lookup(ext, _bias_index(qi - kj)) - far) * LOG2E).transpose(0, 2, 1).reshape(TQ, h * TQ)
    d1 = ((_lookup(ext, _bias_index(TQ + qi - kj)) - far) * LOG2E).transpose(0, 2, 1).reshape(TQ, h * TQ)
    edge = jnp.asarray(np.tile(np.where(kj > qi, 0.0, NEG), (1, h)), F32)
    return jnp.stack([jnp.zeros_like(d0), d1, d0, jnp.full_like(d0, NEG), edge])


def _cmp_bias(tbl, nq, n16):
    h = tbl.shape[1]
    ext = jnp.concatenate([tbl, jnp.full((1, h), NEG, F32)], axis=0)
    per_tile = TQ // CMP_STRIDE
    off = per_tile * (nq - 1)
    width = n16 + off
    cc = np.arange(width)[:, None] - off
    qi = np.arange(TQ)[None, :]
    g = _lookup(ext, _bias_index(qi - CMP_STRIDE * cc - (CMP_BLOCK - 1))) * LOG2E
    g = g.transpose(0, 2, 1).reshape(width, h * TQ)
    return jnp.stack([g[off - per_tile * i: off - per_tile * i + n16] for i in range(nq)], axis=0)


def _compress_kernel(x_ref, pos_ref, w1_ref, w2_ref, o_ref):
    n16 = o_ref.shape[0]
    acc = None
    for g in range(NSA_GROUPS):
        pa = pb = None
        for l in range(CMP_STRIDE):
            x = x_ref[pl.ds(l, n16, stride=CMP_STRIDE), :]
            a = _dot(_mx(x + pos_ref[l]), w1_ref[g, l])
            c = _dot(_mx(x + pos_ref[CMP_STRIDE + l]), w1_ref[g, CMP_STRIDE + l])
            pa = a if pa is None else pa + a
            pb = c if pb is None else pb + c
        pre = pa + pltpu.roll(pb, n16 - 1, 0)
        t = _dot(_mx(jax.nn.silu(pre)), w2_ref[g])
        acc = t if acc is None else acc + t
    o_ref[...] = acc


def _compress(u, pos, w1, w2):
    b, s, _ = u.shape
    n16 = s // CMP_STRIDE
    return pl.pallas_call(
        _compress_kernel,
        grid=(2, b),
        in_specs=[pl.BlockSpec((None, s, LANES), lambda k, i: (i, 0, A_KV // LANES + k)),
                  pl.BlockSpec((None, CMP_BLOCK, 1, LANES), lambda k, i: (k, 0, 0, 0)),
                  pl.BlockSpec((None, NSA_GROUPS, CMP_BLOCK, LANES, CMP_HIDDEN), lambda k, i: (k, 0, 0, 0, 0)),
                  pl.BlockSpec((None, NSA_GROUPS, CMP_HIDDEN, LANES), lambda k, i: (k, 0, 0, 0))],
        out_specs=pl.BlockSpec((None, None, n16, LANES), lambda k, i: (k, i, 0, 0)),
        out_shape=jax.ShapeDtypeStruct((2, b, n16, LANES), F32),
        compiler_params=_params(2),
        name="compress",
    )(u, pos, w1, w2)


def _softmax_reset(m_s, acc_s):
    m_s[...] = jnp.full(m_s.shape, -jnp.inf, F32)
    acc_s[...] = jnp.zeros(acc_s.shape, F32)


def _softmax_step(score_fn, values, m_s, acc_s, col_block):
    r = m_s.shape[1]
    s_next = score_fn(0, col_block)
    for c0 in range(0, r, col_block):
        cols = slice(c0, c0 + col_block)
        s = s_next
        if c0 + col_block < r:
            s_next = score_fn(c0 + col_block, c0 + 2 * col_block)
        m_prev = m_s[:, cols]
        m_new = jnp.maximum(m_prev, jnp.max(s, axis=0, keepdims=True))
        alpha = jnp.exp2(m_prev - m_new)
        p = _mx(jnp.exp2(s - m_new))
        v = values[c0 * len(values) // r]
        acc_s[:, cols] = alpha * acc_s[:, cols] + _dot(v, p)
        m_s[:, cols] = m_new


def _store_value_tiles(v_ref, vt_s, ones_hi):
    row = lax.broadcasted_iota(jnp.int32, (LANES, TQ), 0)

    def body(j, carry):
        vt = v_ref[pl.ds(pl.multiple_of(j * TQ, TQ), TQ), :].T
        for k, hi in enumerate(ones_hi):
            vt_s[k, j] = _mx(jnp.where((row >= HEAD_DIM) == hi, 1.0, vt))
        return carry

    lax.fori_loop(0, vt_s.shape[1], body, 0)


def _value_tiles(vt_s, k, j0, ntile):
    tiles = [vt_s[k, j0 + t] for t in range(ntile)]
    return tiles[0] if ntile == 1 else jnp.concatenate(tiles, axis=1)


def _near_bias(d_ref, idx_of_tile, j0, ntile, c0, c1):
    tiles = [d_ref[idx_of_tile(j0 + t), :, c0:c1] for t in range(ntile)]
    return tiles[0] if ntile == 1 else jnp.concatenate(tiles, axis=0)


def _stack_heads(q_ref, hi):
    lane = lax.broadcasted_iota(jnp.int32, (TQ, LANES), 1)
    out = []
    for h, up in enumerate(hi):
        v = q_ref[:, (h // 2) * LANES:(h // 2 + 1) * LANES]
        if (h % 2 == 1) != up:
            v = pltpu.roll(v, HEAD_DIM, 1)
        out.append(jnp.where((lane >= HEAD_DIM) == up, v, 0.0))
    return jnp.concatenate(out, axis=0)


def _tiles_per_chunk(nq):
    for ch in (4, 2):
        if nq % ch == 0:
            return ch
    return 1


def _nsa_kernel(q_ref, kc_ref, vc_ref, ks_ref, vs_ref, kw_ref, vw_ref, g_ref, z_ref,
                bc_ref, d_ref, et_ref, ov_ref, o_ref,
                qa_s, m_s, acc_s, osel_s, vct_s, vst_s, vwt_s, *, n_pick, ch, n_win_tiles):
    i = pl.program_id(1)
    rows = NSA_HEADS * TQ
    nb = ov_ref.shape[0]
    q0 = i * TQ
    half = LANES // NSA_GROUPS

    @pl.when(i == 0)
    def _():
        vct_s[...] = _mx(vc_ref[...].T)
        _store_value_tiles(vs_ref, vst_s, (True, False))
        _store_value_tiles(vw_ref, vwt_s, (True, False))

    qs = _mx(_stack_heads(q_ref, [h >= NSA_HPG for h in range(NSA_HEADS)]))
    qa_s[:, :LANES] = qs

    s = _dot_nt(_mx(kc_ref[...]), qs) + bc_ref[...]
    p = jnp.exp2(s - jnp.max(s, axis=0, keepdims=True))
    t_lane = q0 + (lax.broadcasted_iota(jnp.int32, (1, rows), 1) & (TQ - 1))
    any_valid = jnp.where(t_lane >= CMP_BLOCK - 1, 1.0, 0.0)
    p = p * (any_valid / jnp.sum(p, axis=0, keepdims=True))
    oc_t = _dot(vct_s[...], _mx(p))

    blk = lax.broadcasted_iota(jnp.int32, (nb, TQ), 0)
    t = q0 + lax.broadcasted_iota(jnp.int32, (nb, TQ), 1)
    cur = lax.shift_right_logical(t, int(math.log2(SEL_BLOCK)))
    forced = jnp.where(blk == 0, 1.0, jnp.where(blk == cur, 1.0, jnp.where(blk == cur - 1, 1.0, 0.0)))
    admissible = blk * SEL_BLOCK <= t
    ov = ov_ref[...]
    for g in range(NSA_GROUPS):
        c0 = g * NSA_HPG * TQ
        ps = p[:, c0:c0 + TQ]
        for h in range(1, NSA_HPG):
            ps = ps + p[:, c0 + h * TQ:c0 + (h + 1) * TQ]
        hi = _mx(ps)
        lo = _mx(ps - hi.astype(F32))
        imp_t = _dot(ov, hi) + _dot(ov, lo)
        score = jnp.where(admissible, imp_t + FORCE_BONUS * forced, NEG)
        sel = jnp.zeros((nb, TQ), F32)
        for _ in range(n_pick):
            best = jnp.max(score, axis=0, keepdims=True)
            first = jnp.min(jnp.where(score == best, blk, nb), axis=0, keepdims=True)
            hit = blk == first
            sel = jnp.where(hit, 1.0, sel)
            score = jnp.where(hit, -jnp.inf, score)
        if nb < LANES:
            sel = jnp.concatenate([sel, jnp.ones((LANES - nb, TQ), F32)], axis=0)
        block_mask = _mx((sel.T - 1.0) * (-NEG))
        for h in range(NSA_HPG):
            r0 = (g * NSA_HPG + h) * TQ
            qa_s[r0:r0 + TQ, LANES:] = block_mask

    def values(vt_s, j0, ntile):
        return [_value_tiles(vt_s, 0, j0, ntile), _value_tiles(vt_s, 1, j0, ntile)]

    _softmax_reset(m_s, acc_s)

    def sel_chunk(c, near):
        tk = ch * TQ
        koff = pl.multiple_of(c * tk, tk)
        k_aug = jnp.concatenate([_mx(ks_ref[pl.ds(koff, tk), :]), et_ref[pl.ds(koff, tk), :]], axis=1)

        def scores(c0, c1):
            s = _dot_nt(k_aug, qa_s[c0:c1, :])
            if near:
                s = s + _near_bias(d_ref, lambda j: jnp.clip(j - (i - 2), 0, 3), c * ch, ch, c0, c1)
            return s

        _softmax_step(scores, values(vst_s, c * ch, ch), m_s, acc_s, COL_BLOCK)

    first_near = jnp.maximum(i - 1, 0) // ch

    def far_body(c, carry):
        sel_chunk(c, False)
        return carry

    def near_body(c, carry):
        sel_chunk(c, True)
        return carry

    lax.fori_loop(0, first_near, far_body, 0)
    lax.fori_loop(first_near, i // ch + 1, near_body, 0)
    osel_s[...] = acc_s[...]

    _softmax_reset(m_s, acc_s)
    n_win = WINDOW // TQ
    jw = jnp.maximum(i - (n_win_tiles - 1), 0)
    koff = pl.multiple_of(jw * TQ, TQ)
    tk = n_win_tiles * TQ

    def win_idx(j):
        r = j - (i - n_win)
        return jnp.where(r == 0, 4, jnp.clip(r - (n_win - 2), 0, 3))

    kw = _mx(kw_ref[pl.ds(koff, tk), :])
    _softmax_step(lambda c0, c1: (_dot_nt(kw, qa_s[c0:c1, :LANES])
                                  + _near_bias(d_ref, win_idx, jw, n_win_tiles, c0, c1)),
                  values(vwt_s, jw, n_win_tiles), m_s, acc_s, COL_BLOCK)
    ow_t = acc_s[...]
    os_t = osel_s[...]

    gate_t = jax.nn.sigmoid(g_ref[...]).T
    for m in range(NSA_HEADS // 2):
        parts = []
        for hh in (2 * m, 2 * m + 1):
            g = hh // NSA_HPG
            r = slice(g * half, (g + 1) * half)
            d = (1 - g) * half
            c = slice(hh * TQ, (hh + 1) * TQ)
            parts.append(gate_t[3 * hh:3 * hh + 1, :] * oc_t[r, c]
                         + gate_t[3 * hh + 1:3 * hh + 2, :] * (os_t[r, c] / os_t[d:d + 1, c])
                         + gate_t[3 * hh + 2:3 * hh + 3, :] * (ow_t[r, c] / ow_t[d:d + 1, c]))
        slab = jnp.concatenate(parts, axis=0).T
        o_ref[:, m * LANES:(m + 1) * LANES] = slab * jax.nn.silu(z_ref[:, m * LANES:(m + 1) * LANES])


def _nsa(u, cmp_kv, bias_c, d_tiles, e_t, ov, n_pick):
    b, s, _ = u.shape
    nq = s // TQ
    n16 = cmp_kv.shape[2]
    nb = ov.shape[0]
    rows = NSA_HEADS * TQ
    n_win_tiles = min(WINDOW // TQ + 1, nq)
    seq = lambda col: pl.BlockSpec((None, s, LANES), lambda bi, i, col=col: (bi, 0, col // LANES))
    return pl.pallas_call(
        functools.partial(_nsa_kernel, n_pick=n_pick, ch=_tiles_per_chunk(nq), n_win_tiles=n_win_tiles),
        grid=(b, nq),
        in_specs=[
            pl.BlockSpec((None, TQ, NSA_HEADS * HEAD_DIM), lambda bi, i: (bi, i, A_NSA_Q // (NSA_HEADS * HEAD_DIM))),
            pl.BlockSpec((None, None, n16, LANES), lambda bi, i: (0, bi, 0, 0)),
            pl.BlockSpec((None, None, n16, LANES), lambda bi, i: (1, bi, 0, 0)),
            seq(A_KV + 2 * LANES), seq(A_KV + 3 * LANES), seq(A_KV + 4 * LANES), seq(A_KV + 5 * LANES),
            pl.BlockSpec((None, TQ, LANES), lambda bi, i: (bi, i, A_GATE // LANES)),
            pl.BlockSpec((None, TQ, 512), lambda bi, i: (bi, i, A_AZ // 512)),
            pl.BlockSpec((None, n16, rows), lambda bi, i: (i, 0, 0)),
            pl.BlockSpec(d_tiles.shape, lambda bi, i: (0, 0, 0)),
            pl.BlockSpec((s, LANES), lambda bi, i: (0, 0)),
            pl.BlockSpec((nb, n16), lambda bi, i: (0, 0)),
        ],
        out_specs=pl.BlockSpec((None, TQ, 512), lambda bi, i: (bi, i, 0)),
        out_shape=jax.ShapeDtypeStruct((b, s, 512), F32),
        scratch_shapes=[pltpu.VMEM((rows, 2 * LANES), MXU_DTYPE),
                        pltpu.VMEM((1, rows), F32),
                        pltpu.VMEM((LANES, rows), F32),
                        pltpu.VMEM((LANES, rows), F32),
                        pltpu.VMEM((LANES, n16), MXU_DTYPE),
                        pltpu.VMEM((NSA_GROUPS, nq, LANES, TQ), MXU_DTYPE),
                        pltpu.VMEM((NSA_GROUPS, nq, LANES, TQ), MXU_DTYPE)],
        compiler_params=_params(2),
        name="nsa",
    )(u, cmp_kv, cmp_kv, u, u, u, u, u, u, bias_c, d_tiles, e_t, ov)


def _dsa_kernel(q_ref, kv_ref, qi_ref, ki_ref, wi_ref, z_ref, d_ref, tri_ref, o_ref,
                qs_s, qis_s, sc_s, hi_s, lo_s, m_s, acc_s, seen_s, vt_s, *, k_top, ch):
    i = pl.program_id(1)

    @pl.when(i == 0)
    def _():
        _store_value_tiles(kv_ref, vt_s, (False,))

    qs_s[...] = _mx(_stack_heads(q_ref, [False] * DSA_HEADS))
    qis_s[...] = _mx(_stack_heads(qi_ref, [False] * IDX_HEADS))
    wi_t = wi_ref[...].T
    w_rows = [jnp.broadcast_to(wi_t[h:h + 1, :], (TQ, TQ)) for h in range(IDX_HEADS)]
    key_idx = lax.broadcasted_iota(jnp.int32, (TQ, TQ), 0)
    q_idx = lax.broadcasted_iota(jnp.int32, (TQ, TQ), 1)
    n_chunks = (i + ch) // ch
    pairs = ch // 2

    def score_chunk(c, carry):
        koff = pl.multiple_of(c * (ch * TQ), ch * TQ)
        r = jnp.maximum(_dot_nt(_mx(ki_ref[pl.ds(koff, ch * TQ), :]), qis_s[...]), 0.0)
        keys = []
        for t in range(ch):
            jt = c * ch + t
            rt = r[t * TQ:(t + 1) * TQ]
            sc = w_rows[0] * rt[:, 0:TQ]
            for h in range(1, IDX_HEADS):
                sc = sc + w_rows[h] * rt[:, h * TQ:(h + 1) * TQ]
            sc = jnp.where((jt == i) & (key_idx > q_idx), NEG, sc)
            bits = pltpu.bitcast(sc, jnp.int32)
            key = bits ^ (lax.shift_right_arithmetic(bits, 31) & jnp.int32(0x7FFFFFFF))
            key = jnp.where(sc == 0.0, 0, key)
            keys.append(jnp.where(jt > i, INT_MIN, key))
            sc_s[jt] = keys[-1]
        for pr in range(pairs):
            a, b = keys[2 * pr], keys[2 * pr + 1]
            hi_s[c * pairs + pr] = (a & jnp.int32(-65536)) | lax.shift_right_logical(b, 16)
            lo_s[c * pairs + pr] = (lax.shift_left((a & 0xFFFF) ^ 0x8000, 16)
                                    | ((b & 0xFFFF) ^ 0x8000))
        return carry

    lax.fori_loop(0, n_chunks, score_chunk, 0)

    def halves(words):
        return pltpu.bitcast(words, jnp.int16)

    def both_halves(v):
        w = lax.shift_left(v, 16) | (v & 0xFFFF)
        return halves(jnp.broadcast_to(w, (TQ, TQ)))

    def count16(words_s, pred):
        def body(c, acc):
            for pr in range(pairs):
                acc = acc + jnp.where(pred(halves(words_s[c * pairs + pr])), jnp.int16(1), jnp.int16(0))
            return acc
        acc = lax.fori_loop(0, n_chunks, body, jnp.zeros((2 * TQ, TQ), jnp.int16))
        acc = acc[:TQ] + acc[TQ:]
        return jnp.sum(acc.astype(jnp.int32), axis=0, keepdims=True)

    def bisect16(words_s, target):
        c0 = count16(words_s, lambda x: x >= jnp.int16(0))
        ok0 = c0 >= target

        def step(it, carry):
            v, n_gt = carry
            cand = v | lax.shift_left(jnp.int32(1), 14 - it)
            cand16 = both_halves(cand)
            c = count16(words_s, lambda x: x >= cand16)
            ok = c >= target
            return jnp.where(ok, cand, v), jnp.where(ok, n_gt, c)

        return lax.fori_loop(0, 15, step, (jnp.where(ok0, 0, -32768), jnp.where(ok0, 0, c0)))

    top, above = bisect16(hi_s, k_top)
    top16 = both_halves(top)

    def keep_ties(c, carry):
        for pr in range(pairs):
            p_ = c * pairs + pr
            tied = halves(hi_s[p_]) == top16
            lo_s[p_] = pltpu.bitcast(jnp.where(tied, halves(lo_s[p_]), jnp.int16(-32768)), jnp.int32)
        return carry

    lax.fori_loop(0, n_chunks, keep_ties, 0)
    low, above_low = bisect16(lo_s, k_top - above)
    thr = lax.shift_left(top, 16) | ((low ^ 0x8000) & 0xFFFF)
    thr = jnp.broadcast_to(thr, (TQ, TQ))
    need = (k_top - above - above_low).astype(F32)

    _softmax_reset(m_s, acc_s)
    seen_s[...] = jnp.zeros((TQ, TQ), F32)

    def att_chunk(c, near):
        tk = ch * TQ
        koff = pl.multiple_of(c * tk, tk)
        masks = []
        for t in range(ch):
            key = sc_s[c * ch + t]
            eq = key == thr
            prefix = _dot(tri_ref[...], _mx(jnp.where(eq, 1.0, 0.0)))
            seen = seen_s[...]
            take = jnp.where(eq, jnp.where(seen + prefix <= need, 1.0, 0.0), 0.0)
            seen_s[...] = seen + prefix[TQ - 1:TQ, :]
            masks.append(jnp.where(key > thr, 0.0, (take - 1.0) * (-NEG)))
        mb = masks[0] if ch == 1 else jnp.concatenate(masks, axis=0)
        mb = jnp.concatenate([mb] * DSA_HEADS, axis=1)
        kvm = _mx(kv_ref[pl.ds(koff, tk), :])

        def scores(c0, c1):
            s = _dot_nt(kvm, qs_s[c0:c1, :]) + mb[:, c0:c1]
            if near:
                s = s + _near_bias(d_ref, lambda j: jnp.clip(j - (i - 2), 0, 3), c * ch, ch, c0, c1)
            return s

        _softmax_step(scores, [_value_tiles(vt_s, 0, c * ch, ch)], m_s, acc_s, DSA_HEADS * TQ)

    first_near = jnp.maximum(i - 1, 0) // ch

    def far_body(c, carry):
        att_chunk(c, False)
        return carry

    def near_body(c, carry):
        att_chunk(c, True)
        return carry

    lax.fori_loop(0, first_near, far_body, 0)
    lax.fori_loop(first_near, n_chunks, near_body, 0)

    o_t = acc_s[...]
    o_t = o_t / o_t[0:1, :]
    for m in range(DSA_HEADS // 2):
        pair = [o_t[HEAD_DIM:, hh * TQ:(hh + 1) * TQ] for hh in (2 * m, 2 * m + 1)]
        slab = jnp.concatenate(pair, axis=0).T
        o_ref[:, m * LANES:(m + 1) * LANES] = slab * jax.nn.silu(z_ref[:, m * LANES:(m + 1) * LANES])


def _dsa(u, d_tiles, tri, k_top):
    b, s, _ = u.shape
    nq = s // TQ
    assert nq % 2 == 0, "the packed threshold search pairs key tiles"
    rows = DSA_HEADS * TQ
    irows = IDX_HEADS * TQ
    seq = lambda col: pl.BlockSpec((None, s, LANES), lambda bi, i, col=col: (bi, 0, col // LANES))
    return pl.pallas_call(
        functools.partial(_dsa_kernel, k_top=k_top, ch=_tiles_per_chunk(nq)),
        grid=(b, nq),
        in_specs=[
            pl.BlockSpec((None, TQ, DSA_HEADS * HEAD_DIM), lambda bi, i: (bi, i, A_DSA_Q // (DSA_HEADS * HEAD_DIM))),
            seq(A_DSA_KV),
            pl.BlockSpec((None, TQ, IDX_HEADS * IDX_DIM), lambda bi, i: (bi, i, A_IDX_Q // (IDX_HEADS * IDX_DIM))),
            seq(A_IDX_K),
            pl.BlockSpec((None, TQ, LANES), lambda bi, i: (bi, i, A_IDX_W // LANES)),
            pl.BlockSpec((None, TQ, 512), lambda bi, i: (bi, i, A_BZ // 512)),
            pl.BlockSpec(d_tiles.shape, lambda bi, i: (0, 0, 0)),
            pl.BlockSpec((TQ, TQ), lambda bi, i: (0, 0)),
        ],
        out_specs=pl.BlockSpec((None, TQ, 512), lambda bi, i: (bi, i, 0)),
        out_shape=jax.ShapeDtypeStruct((b, s, 512), F32),
        scratch_shapes=[pltpu.VMEM((rows, LANES), MXU_DTYPE),
                        pltpu.VMEM((irows, LANES), MXU_DTYPE),
                        pltpu.VMEM((nq, TQ, TQ), jnp.int32),
                        pltpu.VMEM((nq // 2, TQ, TQ), jnp.int32),
                        pltpu.VMEM((nq // 2, TQ, TQ), jnp.int32),
                        pltpu.VMEM((1, rows), F32),
                        pltpu.VMEM((LANES, rows), F32),
                        pltpu.VMEM((TQ, TQ), F32),
                        pltpu.VMEM((1, nq, LANES, TQ), MXU_DTYPE)],
        compiler_params=_params(2),
        name="dsa",
    )(u, u, u, u, u, u, d_tiles, tri)


def _causal_conv(x, xe_s, w_ref, b_ref):
    t = x.shape[0]
    xe_s[8:8 + t, :] = x
    y = b_ref[...] + w_ref[CONV_WIDTH - 1:CONV_WIDTH, :] * x
    for k in range(1, CONV_WIDTH):
        y = y + w_ref[CONV_WIDTH - 1 - k:CONV_WIDTH - k, :] * xe_s[8 - k:8 - k + t, :]
    xe_s[0:8, :] = x[t - 8:t, :]
    return y


def _rglru_kernel(x_ref, z_ref, cw_ref, cb_ref, wa_ref, ba_ref, wx_ref, bx_ref, lam_ref, o_ref,
                  xe_s, h_s):
    t, c = x_ref.shape

    @pl.when(pl.program_id(1) == 0)
    def _():
        xe_s[0:8, :] = jnp.zeros((8, c), F32)
        h_s[...] = jnp.zeros(h_s.shape, F32)

    xc = _causal_conv(x_ref[...], xe_s, cw_ref, cb_ref)
    xcm = _mx(xc)
    r = jax.nn.sigmoid(_dot(xcm, wa_ref[...]) + ba_ref[...])
    ig = jax.nn.sigmoid(_dot(xcm, wx_ref[...]) + bx_ref[...])
    nl = -lam_ref[...]
    softplus = jnp.maximum(nl, 0.0) + jnp.log1p(jnp.exp(-jnp.abs(nl)))
    log_a = (-LRU_C * r) * softplus
    a = jnp.exp(log_a)
    bb = jnp.sqrt(-jnp.tanh(log_a) * (a * a + 1.0)) * (ig * xc)
    row = lax.broadcasted_iota(jnp.int32, (t, c), 0) & (SUBLANES - 1)
    d = 1
    while d < SUBLANES:
        live = row >= d
        a_sh = jnp.where(live, pltpu.roll(a, d, 0), 1.0)
        b_sh = jnp.where(live, pltpu.roll(bb, d, 0), 0.0)
        bb = a * b_sh + bb
        a = a * a_sh
        d *= 2
    carry = h_s[0:1, :]
    groups = []
    for g in range(t // SUBLANES):
        rows = slice(g * SUBLANES, (g + 1) * SUBLANES)
        groups.append(a[rows] * carry + bb[rows])
        carry = groups[-1][SUBLANES - 1:SUBLANES, :]
    h = jnp.concatenate(groups, axis=0)
    h_s[0:1, :] = carry
    o_ref[...] = h * jax.nn.silu(z_ref[...])


def _rglru(u, conv_w, conv_b, wa_bd, ba, wx_bd, bx, lam, t=256):
    b, s, _ = u.shape
    c = LRU_WIDTH
    whole = lambda bi, i: (0, 0)
    return pl.pallas_call(
        _rglru_kernel,
        grid=(b, s // t),
        in_specs=[pl.BlockSpec((None, t, c), lambda bi, i: (bi, i, R_CX // c)),
                  pl.BlockSpec((None, t, c), lambda bi, i: (bi, i, R_CZ // c)),
                  pl.BlockSpec((CONV_WIDTH, c), whole), pl.BlockSpec((1, c), whole),
                  pl.BlockSpec((c, c), whole), pl.BlockSpec((1, c), whole),
                  pl.BlockSpec((c, c), whole), pl.BlockSpec((1, c), whole),
                  pl.BlockSpec((1, c), whole)],
        out_specs=pl.BlockSpec((None, t, c), lambda bi, i: (bi, i, 0)),
        out_shape=jax.ShapeDtypeStruct((b, s, c), F32),
        scratch_shapes=[pltpu.VMEM((t + 8, c), F32), pltpu.VMEM((8, c), F32)],
        compiler_params=_params(2),
        name="rglru",
    )(u, u, conv_w, conv_b.reshape(1, c), wa_bd, ba.reshape(1, c), wx_bd, bx.reshape(1, c),
      lam.reshape(1, c))


def _mlstm_kernel(qk_ref, v_ref, g_ref, og_ref, z_ref, cw_ref, cb_ref, gb_ref, o_ref,
                  xe_s, c_s, m_s):
    t = qk_ref.shape[0]
    L = MLSTM_CHUNK
    H = MLSTM_HEADS
    D = MLSTM_DIM

    @pl.when(pl.program_id(1) == 0)
    def _():
        xe_s[0:8, :] = jnp.zeros((8, xe_s.shape[1]), F32)
        c_s[...] = jnp.zeros(c_s.shape, F32)
        m_s[...] = jnp.zeros(m_s.shape, F32)

    qk = jax.nn.silu(_causal_conv(qk_ref[...], xe_s, cw_ref, cb_ref))

    lane = lax.broadcasted_iota(jnp.int32, (t, LANES), 1)
    row_in = lax.broadcasted_iota(jnp.int32, (t, LANES), 0) & (L - 1)
    gs = g_ref[...] + gb_ref[...]
    log_sig = jnp.minimum(gs, 0.0) - jnp.log1p(jnp.exp(-jnp.abs(gs)))
    pre = jnp.where(lane < H, gs, log_sig)
    cum = pre
    d = 1
    while d < L:
        cum = cum + jnp.where(row_in >= d, pltpu.roll(cum, d, 0), 0.0)
        d *= 2
    comb = jnp.where(lane < H, pre, cum)
    comb_t = comb.T

    causal_t = (lax.broadcasted_iota(jnp.int32, (L, L), 0) <= lax.broadcasted_iota(jnp.int32, (L, L), 1))
    pad_rows = jnp.zeros((7, L), F32)
    v_ts = [v_ref[:, h * D:(h + 1) * D].T for h in range(H)]
    states = [c_s[h] for h in range(H)]
    m_prevs = [m_s[h:h + 1, 0:1] for h in range(H)]
    outs = [[] for _ in range(H)]
    for c in range(t // L):
        r = slice(c * L, (c + 1) * L)
        for h in range(H):
            v_t, state, m_prev = v_ts[h], states[h], m_prevs[h]
            q_ = _mx(qk[r, h * D:(h + 1) * D])
            k_ = _mx(qk[r, (H + h) * D:(H + h + 1) * D] * (D ** -0.5))
            li_row = comb_t[h:h + 1, r]
            b_row = comb_t[H + h:H + h + 1, r]
            lb_col = comb[r, h:h + 1] - comb[r, H + h:H + h + 1]
            b_last = b_row[:, L - 1:L]

            dmat_t = jnp.where(causal_t, b_row + lb_col, -jnp.inf)
            decay_row = b_last - b_row + li_row
            m_new = jnp.maximum(b_last + m_prev, jnp.max(decay_row, axis=-1, keepdims=True))
            inter = b_row + m_prev
            m_t = jnp.maximum(inter, jnp.max(dmat_t, axis=0, keepdims=True))
            w_t = _dot_nt(k_, q_) * jnp.exp(dmat_t - m_t)
            prev = jnp.exp(inter - m_t)

            read = _dot_nt(_mx(state), q_)
            num_t = prev * read[:D] + _dot(_mx(v_t[:, r]), _mx(w_t))
            den = prev * read[D:D + 1] + jnp.sum(w_t, axis=0, keepdims=True)
            outs[h].append(num_t / jnp.maximum(jnp.abs(den), jnp.exp(-m_t)))

            wk = jnp.exp(decay_row - m_new)
            inc = jnp.concatenate([v_t[:, r] * wk, wk, pad_rows], axis=0)
            states[h] = jnp.exp(b_last + m_prev - m_new) * state + _dot(_mx(inc), k_)
            m_prevs[h] = m_new

    for h in range(H):
        c_s[h] = states[h]
        m_s[h:h + 1, :] = jnp.broadcast_to(m_prevs[h], (1, LANES))
        out = jnp.concatenate(outs[h], axis=1).T
        cols = slice(h * D, (h + 1) * D)
        o_ref[:, cols] = jax.nn.sigmoid(og_ref[:, cols]) * out * jax.nn.silu(z_ref[:, cols])


def _mlstm(u, conv_w, conv_b, gate_bias, t=128):
    b, s, _ = u.shape
    w = MLSTM_WIDTH
    whole = lambda bi, i: (0, 0)
    return pl.pallas_call(
        _mlstm_kernel,
        grid=(b, s // t),
        in_specs=[pl.BlockSpec((None, t, 2 * w), lambda bi, i: (bi, i, R_QK // (2 * w))),
                  pl.BlockSpec((None, t, w), lambda bi, i: (bi, i, R_V // w)),
                  pl.BlockSpec((None, t, LANES), lambda bi, i: (bi, i, R_GATE // LANES)),
                  pl.BlockSpec((None, t, w), lambda bi, i: (bi, i, R_O // w)),
                  pl.BlockSpec((None, t, w), lambda bi, i: (bi, i, R_Z // w)),
                  pl.BlockSpec((CONV_WIDTH, 2 * w), whole), pl.BlockSpec((1, 2 * w), whole),
                  pl.BlockSpec((1, LANES), whole)],
        out_specs=pl.BlockSpec((None, t, w), lambda bi, i: (bi, i, 0)),
        out_shape=jax.ShapeDtypeStruct((b, s, w), F32),
        scratch_shapes=[pltpu.VMEM((t + 8, 2 * w), F32),
                        pltpu.VMEM((MLSTM_HEADS, MLSTM_DIM + 8, MLSTM_DIM), F32),
                        pltpu.VMEM((8, LANES), F32)],
        compiler_params=_params(2),
        name="mlstm",
    )(u, u, u, u, u, conv_w, conv_b.reshape(1, 2 * w), gate_bias)


def _attn_weights(w_in):
    d = w_in.shape[0]
    widths = (512, 768, 24, 512, 512, 64, 64, 256, 64, 4, 512)
    offs = np.concatenate([[0], np.cumsum(widths)])
    a_q, a_kv, a_g, a_z, b_q, b_k, b_v, b_qi, b_ki, b_wi, b_z = [
        w_in[:, offs[k]:offs[k + 1]] for k in range(len(widths))]
    scale = HEAD_DIM ** -0.5 * LOG2E
    zeros = lambda n: jnp.zeros((d, n), w_in.dtype)
    cols = [
        a_q * scale, b_q * scale, a_z, b_z, b_qi, a_kv,
        a_g, zeros(LANES - 24),
        b_k, b_v,
        b_ki, zeros(LANES - IDX_DIM),
        b_wi * (IDX_DIM ** -0.5 * IDX_HEADS ** -0.5), zeros(LANES - IDX_HEADS),
    ]
    w = jnp.concatenate(cols, axis=1)
    assert w.shape[1] == A_TOTAL
    return _mx(w)


def _rec_weights(w_in):
    d = w_in.shape[0]
    widths = (512, 512, 512, 512, 512, 4, 4, 512, 512)
    offs = np.concatenate([[0], np.cumsum(widths)])
    c_x, c_z, d_q, d_k, d_v, d_i, d_f, d_o, d_z = [w_in[:, offs[k]:offs[k + 1]] for k in range(len(widths))]
    w = jnp.concatenate([c_x, c_z, d_q, d_k, d_v, d_o, d_z, d_i, d_f,
                         jnp.zeros((d, R_TOTAL - R_GATE - 2 * MLSTM_HEADS), w_in.dtype)], axis=1)
    assert w.shape[1] == R_TOTAL
    return _mx(w)


def _block_diag(w):
    g, n, _ = w.shape
    eye = jnp.eye(g, dtype=w.dtype)
    return (eye[:, None, :, None] * w[:, :, None, :]).reshape(g * n, g * n)


def _attention_layer(x2d, b, s, norm_g, w_in, cmp_pos_k, cmp_w1_k, cmp_w2_k, cmp_pos_v, cmp_w1_v, cmp_w2_v,
                     t5_table):
    nq = s // TQ
    n16 = s // CMP_STRIDE
    nb = s // SEL_BLOCK
    u = _norm_proj(x2d, norm_g, _attn_weights(w_in)).reshape(b, s, A_TOTAL)

    pos = jnp.stack([cmp_pos_k, cmp_pos_v])
    pos = jnp.concatenate([pos] * NSA_GROUPS, axis=-1).reshape(2, CMP_BLOCK, 1, LANES)
    w1 = jnp.stack([cmp_w1_k, cmp_w1_v]).reshape(2, CMP_BLOCK, HEAD_DIM, CMP_HIDDEN)
    z1 = jnp.zeros_like(w1)
    w1 = _mx(jnp.stack([jnp.concatenate([w1, z1], axis=2), jnp.concatenate([z1, w1], axis=2)], axis=1))
    w2 = jnp.stack([cmp_w2_k, cmp_w2_v])
    zpad = jnp.zeros_like(w2)
    w2 = _mx(jnp.stack([jnp.concatenate([w2, zpad], -1), jnp.concatenate([zpad, w2], -1)], axis=1))
    cmp_kv = _compress(u, pos, w1, w2)

    tbl = t5_table.astype(F32)
    tbl_a, tbl_b = tbl[:, :NSA_HEADS], tbl[:, NSA_HEADS:]
    kj = np.arange(TQ)[:, None]
    qi = np.arange(TQ)[None, :]
    assert nb <= LANES
    e_t = jnp.asarray((np.arange(s)[:, None] // SEL_BLOCK) == np.arange(LANES)[None, :], MXU_DTYPE)
    ci = np.arange(n16)[None, :]
    sj = np.arange(nb)[:, None]
    ov = jnp.asarray((ci * CMP_STRIDE < (sj + 1) * SEL_BLOCK) & (ci * CMP_STRIDE + CMP_BLOCK > sj * SEL_BLOCK)
                     & (ci < n16 - 1), MXU_DTYPE)
    tri = jnp.asarray(qi <= kj, MXU_DTYPE)

    ya = _nsa(u, cmp_kv, _cmp_bias(tbl_a, nq, n16), _near_tiles(tbl_a), e_t, ov, min(N_SEL, nb))
    yb = _dsa(u, _near_tiles(tbl_b), tri, min(DSA_TOPK_MAX, s // 4))
    return ya.reshape(b * s, -1), yb.reshape(b * s, -1)


def _recurrent_layer(x2d, b, s, norm_g, w_in, conv_c_w, conv_c_b, wa, ba, wx, bx, lam,
                     conv_d_w, conv_d_b, b_i, b_f):
    u = _norm_proj(x2d, norm_g, _rec_weights(w_in)).reshape(b, s, R_TOTAL)
    yc = _rglru(u, conv_c_w, conv_c_b, _mx(_block_diag(wa)), ba, _mx(_block_diag(wx)), bx, lam)
    gate_bias = jnp.concatenate([b_i, b_f, jnp.zeros((LANES - 2 * MLSTM_HEADS,), F32)]).reshape(1, LANES)
    yd = _mlstm(u, conv_d_w, conv_d_b, gate_bias)
    return yc.reshape(b * s, -1), yd.reshape(b * s, -1)


def kernel(x, p, norm_g, final_g, ple_w, ple_gate_w, t5_table, attn_w_in, attn_w_out, cmp_pos_k, cmp_w1_k, cmp_w2_k, cmp_pos_v, cmp_w1_v, cmp_w2_v, rec_w_in, rec_w_out, lru_conv_w, lru_conv_b, lru_wa, lru_ba, lru_wx, lru_bx, lru_lambda, mlstm_conv_w, mlstm_conv_b, mlstm_b_i, mlstm_b_f):
    b, s, d = x.shape
    depth = p.shape[0]
    x2d = x.reshape(b * s, d)
    for i in range(depth):
        j = i // 2
        if i % 2 == 0:
            ya, yb = _attention_layer(x2d, b, s, norm_g[i], attn_w_in[j], cmp_pos_k[j], cmp_w1_k[j], cmp_w2_k[j],
                                      cmp_pos_v[j], cmp_w1_v[j], cmp_w2_v[j], t5_table)
            w_out = attn_w_out[j]
        else:
            ya, yb = _recurrent_layer(x2d, b, s, norm_g[i], rec_w_in[j], lru_conv_w[j], lru_conv_b[j],
                                      lru_wa[j], lru_ba[j], lru_wx[j], lru_bx[j], lru_lambda[j],
                                      mlstm_conv_w[j], mlstm_conv_b[j], mlstm_b_i[j], mlstm_b_f[j])
            w_out = rec_w_out[j]
        x2d = _out_proj(x2d, ya, yb, p[i].reshape(b * s, -1), w_out, ple_w[i], ple_gate_w[i], final_g,
                        final=(i == depth - 1))
    return x2d.reshape(b, s, d)
```

```python
import functools
import math

import numpy as np
import jax
import jax.numpy as jnp
from jax import lax
from jax.experimental import pallas as pl
from jax.experimental.pallas import tpu as pltpu

F32 = jnp.float32
MXU_DTYPE = jnp.bfloat16

HEAD_DIM = 64
NSA_HEADS = 8
NSA_GROUPS = 2
NSA_HPG = NSA_HEADS // NSA_GROUPS
CMP_BLOCK = 32
CMP_STRIDE = 16
CMP_HIDDEN = 256
SEL_BLOCK = 64
N_SEL = 8
WINDOW = 512
FORCE_BONUS = 1e4
DSA_HEADS = 8
IDX_HEADS = 4
IDX_DIM = 64
DSA_TOPK_MAX = 256
N_BUCKETS = 32
T5_MAX_DIST = 128
LRU_WIDTH = 512
CONV_WIDTH = 4
LRU_C = 8.0
MLSTM_HEADS = 4
MLSTM_DIM = 128
MLSTM_WIDTH = MLSTM_HEADS * MLSTM_DIM
MLSTM_CHUNK = 64
RMS_EPS = 1e-6
NEG = -1e30

LANES = 128
SUBLANES = 8
COL_BLOCK = 256
TQ = 128
INT_MIN = -2 ** 31
LOG2E = math.log2(math.e)
VMEM_LIMIT = 48 * 1024 * 1024

A_NSA_Q = 0
A_DSA_Q = 512
A_AZ = 1024
A_BZ = 1536
A_IDX_Q = 2048
A_KV = 2304
A_GATE = 3072
A_DSA_KV = 3200
A_IDX_K = 3328
A_IDX_W = 3456
A_TOTAL = 3584

R_CX = 0
R_CZ = 512
R_QK = 1024
R_V = 2048
R_O = 2560
R_Z = 3072
R_GATE = 3584
R_TOTAL = 3840


def _dot(a, b):
    return jnp.dot(a, b, preferred_element_type=F32)


def _dot_nt(a, b):
    return lax.dot_general(a, b, (((1,), (1,)), ((), ())), preferred_element_type=F32)


def _mx(a):
    return a.astype(MXU_DTYPE)


def _params(n_grid):
    return pltpu.CompilerParams(dimension_semantics=("arbitrary",) * n_grid,
                                vmem_limit_bytes=VMEM_LIMIT)


def _norm_proj_kernel(x_ref, g_ref, w_ref, o_ref):
    x = x_ref[...]
    ms = jnp.mean(x * x, axis=-1, keepdims=True)
    y = x * lax.rsqrt(ms + RMS_EPS) * g_ref[...]
    o_ref[...] = _dot(_mx(y), w_ref[...])


def _norm_proj(x2d, g, w, tm=256):
    m, d = x2d.shape
    n = w.shape[1]
    return pl.pallas_call(
        _norm_proj_kernel,
        grid=(m // tm,),
        in_specs=[pl.BlockSpec((tm, d), lambda i: (i, 0)),
                  pl.BlockSpec((1, d), lambda i: (0, 0)),
                  pl.BlockSpec((d, n), lambda i: (0, 0))],
        out_specs=pl.BlockSpec((tm, n), lambda i: (i, 0)),
        out_shape=jax.ShapeDtypeStruct((m, n), F32),
        compiler_params=_params(1),
        name="norm_proj",
    )(x2d, g.reshape(1, d), w)


def _out_proj_kernel(x_ref, ya_ref, yb_ref, p_ref, wo_ref, pw_ref, gw_ref, fg_ref, o_ref, *, final):
    half = ya_ref.shape[-1]
    y = _dot(_mx(ya_ref[...]), wo_ref[:half, :]) + _dot(_mx(yb_ref[...]), wo_ref[half:, :])
    x1 = x_ref[...] + y
    gate = jax.nn.sigmoid(_dot(_mx(x1), gw_ref[...]))
    x2 = x1 + _dot(_mx(p_ref[...]), pw_ref[...]) * gate
    if final:
        ms = jnp.mean(x2 * x2, axis=-1, keepdims=True)
        x2 = x2 * lax.rsqrt(ms + RMS_EPS) * fg_ref[...]
    o_ref[...] = x2


def _out_proj(x2d, ya, yb, p2d, w_out, ple_w, gate_w, final_g, final, tm=512):
    m, d = x2d.shape
    half = ya.shape[1]
    pd = p2d.shape[1]
    row = lambda i: (i, 0)
    whole = lambda i: (0, 0)
    return pl.pallas_call(
        functools.partial(_out_proj_kernel, final=final),
        grid=(m // tm,),
        in_specs=[pl.BlockSpec((tm, d), row), pl.BlockSpec((tm, half), row),
                  pl.BlockSpec((tm, half), row), pl.BlockSpec((tm, pd), row),
                  pl.BlockSpec((2 * half, d), whole), pl.BlockSpec((pd, d), whole),
                  pl.BlockSpec((d, d), whole), pl.BlockSpec((1, d), whole)],
        out_specs=pl.BlockSpec((tm, d), row),
        out_shape=jax.ShapeDtypeStruct((m, d), F32),
        compiler_params=_params(1),
        name="out_proj",
    )(x2d, ya, yb, p2d, _mx(w_out), _mx(ple_w), _mx(gate_w), final_g.reshape(1, d))


def _bucket_np(n):
    n = np.asarray(n)
    max_exact = N_BUCKETS // 2
    nf = np.maximum(n, 1).astype(np.float32)
    large = max_exact + (np.log(nf / np.float32(max_exact)) / np.float32(math.log(T5_MAX_DIST / max_exact))
                         * np.float32(N_BUCKETS - max_exact)).astype(np.int32)
    large = np.minimum(large, N_BUCKETS - 1)
    return np.where(n < max_exact, n, large)


def _bias_index(dist):
    dist = np.asarray(dist)
    return np.where(dist >= 0, _bucket_np(np.maximum(dist, 0)), N_BUCKETS).astype(np.int32)


def _lookup(ext, idx):
    idx = jnp.asarray(idx)[..., None]
    out = jnp.zeros(idx.shape[:-1] + (ext.shape[1],), F32)
    for b in range(ext.shape[0]):
        out = jnp.where(idx == b, ext[b], out)
    return out


def _near_tiles(tbl):
    h = tbl.shape[1]
    far = tbl[N_BUCKETS - 1]
    ext = jnp.concatenate([tbl, jnp.full((1, h), NEG, F32)], axis=0)
    kj = np.arange(TQ)[:, None]
    qi = np.arange(TQ)[None, :]
    d0 = ((_lookup(ext, _bias_index(qi - kj)) - far) * LOG2E).transpose(0, 2, 1).reshape(TQ, h * TQ)
    d1 = ((_lookup(ext, _bias_index(TQ + qi - kj)) - far) * LOG2E).transpose(0, 2, 1).reshape(TQ, h * TQ)
    edge = jnp.asarray(np.tile(np.where(kj > qi, 0.0, NEG), (1, h)), F32)
    return jnp.stack([jnp.zeros_like(d0), d1, d0, jnp.full_like(d0, NEG), edge])


def _cmp_bias(tbl, nq, n16):
    h = tbl.shape[1]
    ext = jnp.concatenate([tbl, jnp.full((1, h), NEG, F32)], axis=0)
    off = (TQ // CMP_STRIDE) * (nq - 1)
    width = n16 + off
    cc = np.arange(width)[:, None] - off
    qi = np.arange(TQ)[None, :]
    g = _lookup(ext, _bias_index(qi - CMP_STRIDE * cc - (CMP_BLOCK - 1))) * LOG2E
    return g.transpose(0, 2, 1).reshape(width, h * TQ)


def _compress_kernel(x_ref, pos_ref, w1_ref, w2_ref, o_ref):
    n16 = o_ref.shape[0]
    acc = None
    for g in range(NSA_GROUPS):
        pa = pb = None
        for l in range(CMP_STRIDE):
            x = x_ref[pl.ds(l, n16, stride=CMP_STRIDE), :]
            a = _dot(_mx(x + pos_ref[l]), w1_ref[g, l])
            c = _dot(_mx(x + pos_ref[CMP_STRIDE + l]), w1_ref[g, CMP_STRIDE + l])
            pa = a if pa is None else pa + a
            pb = c if pb is None else pb + c
        pre = pa + pltpu.roll(pb, n16 - 1, 0)
        t = _dot(_mx(jax.nn.silu(pre)), w2_ref[g])
        acc = t if acc is None else acc + t
    o_ref[...] = acc


def _compress(u, pos, w1, w2):
    b, s, _ = u.shape
    n16 = s // CMP_STRIDE
    return pl.pallas_call(
        _compress_kernel,
        grid=(2, b),
        in_specs=[pl.BlockSpec((None, s, LANES), lambda k, i: (i, 0, A_KV // LANES + k)),
                  pl.BlockSpec((None, CMP_BLOCK, 1, LANES), lambda k, i: (k, 0, 0, 0)),
                  pl.BlockSpec((None, NSA_GROUPS, CMP_BLOCK, LANES, CMP_HIDDEN), lambda k, i: (k, 0, 0, 0, 0)),
                  pl.BlockSpec((None, NSA_GROUPS, CMP_HIDDEN, LANES), lambda k, i: (k, 0, 0, 0))],
        out_specs=pl.BlockSpec((None, None, n16, LANES), lambda k, i: (k, i, 0, 0)),
        out_shape=jax.ShapeDtypeStruct((2, b, n16, LANES), F32),
        compiler_params=_params(2),
        name="compress",
    )(u, pos, w1, w2)


def _softmax_reset(m_s, acc_s):
    m_s[...] = jnp.full(m_s.shape, -jnp.inf, F32)
    acc_s[...] = jnp.zeros(acc_s.shape, F32)


def _softmax_step(score_fn, values, m_s, acc_s, col_block):
    r = m_s.shape[1]
    s_next = score_fn(0, col_block)
    for c0 in range(0, r, col_block):
        cols = slice(c0, c0 + col_block)
        s = s_next
        if c0 + col_block < r:
            s_next = score_fn(c0 + col_block, c0 + 2 * col_block)
        m_prev = m_s[:, cols]
        m_new = jnp.maximum(m_prev, jnp.max(s, axis=0, keepdims=True))
        alpha = jnp.exp2(m_prev - m_new)
        p = _mx(jnp.exp2(s - m_new))
        v = values[c0 * len(values) // r]
        acc_s[:, cols] = alpha * acc_s[:, cols] + _dot(v, p)
        m_s[:, cols] = m_new


def _store_value_tiles(v_ref, vt_s, ones_hi):
    row = lax.broadcasted_iota(jnp.int32, (LANES, TQ), 0)

    def body(j, carry):
        vt = v_ref[pl.ds(pl.multiple_of(j * TQ, TQ), TQ), :].T
        for k, hi in enumerate(ones_hi):
            vt_s[k, j] = _mx(jnp.where((row >= HEAD_DIM) == hi, 1.0, vt))
        return carry

    lax.fori_loop(0, vt_s.shape[1], body, 0)


def _value_tiles(vt_s, k, j0, ntile):
    tiles = [vt_s[k, j0 + t] for t in range(ntile)]
    return tiles[0] if ntile == 1 else jnp.concatenate(tiles, axis=1)


def _near_bias(d_ref, idx_of_tile, j0, ntile, c0, c1):
    tiles = [d_ref[idx_of_tile(j0 + t), :, c0:c1] for t in range(ntile)]
    return tiles[0] if ntile == 1 else jnp.concatenate(tiles, axis=0)


def _stack_heads(q_ref, hi):
    lane = lax.broadcasted_iota(jnp.int32, (TQ, LANES), 1)
    out = []
    for h, up in enumerate(hi):
        v = q_ref[:, (h // 2) * LANES:(h // 2 + 1) * LANES]
        if (h % 2 == 1) != up:
            v = pltpu.roll(v, HEAD_DIM, 1)
        out.append(jnp.where((lane >= HEAD_DIM) == up, v, 0.0))
    return jnp.concatenate(out, axis=0)


def _tiles_per_chunk(nq):
    for ch in (4, 2):
        if nq % ch == 0:
            return ch
    return 1


def _nsa_kernel(q_ref, kc_ref, vc_ref, ks_ref, vs_ref, kw_ref, vw_ref, g_ref, z_ref,
                bc_ref, d_ref, et_ref, ov_ref, o_ref,
                qa_s, m_s, acc_s, osel_s, vct_s, vst_s, vwt_s, *, n_pick, ch, n_win_tiles):
    i = pl.program_id(1)
    rows = NSA_HEADS * TQ
    nb = ov_ref.shape[0]
    q0 = i * TQ
    half = LANES // NSA_GROUPS

    @pl.when(i == 0)
    def _():
        vct_s[...] = _mx(vc_ref[...].T)
        _store_value_tiles(vs_ref, vst_s, (True, False))
        _store_value_tiles(vw_ref, vwt_s, (True, False))

    qs = _mx(_stack_heads(q_ref, [h >= NSA_HPG for h in range(NSA_HEADS)]))
    qa_s[:, :LANES] = qs

    n16 = kc_ref.shape[0]
    bias_row = pl.multiple_of((TQ // CMP_STRIDE) * (pl.num_programs(1) - 1 - i), TQ // CMP_STRIDE)
    s = _dot_nt(_mx(kc_ref[...]), qs) + bc_ref[pl.ds(bias_row, n16), :]
    p = jnp.exp2(s - jnp.max(s, axis=0, keepdims=True))
    t_lane = q0 + (lax.broadcasted_iota(jnp.int32, (1, rows), 1) & (TQ - 1))
    any_valid = jnp.where(t_lane >= CMP_BLOCK - 1, 1.0, 0.0)
    p = p * (any_valid / jnp.sum(p, axis=0, keepdims=True))
    oc_t = _dot(vct_s[...], _mx(p))

    blk = lax.broadcasted_iota(jnp.int32, (nb, TQ), 0)
    t = q0 + lax.broadcasted_iota(jnp.int32, (nb, TQ), 1)
    cur = lax.shift_right_logical(t, int(math.log2(SEL_BLOCK)))
    forced = jnp.where(blk == 0, 1.0, jnp.where(blk == cur, 1.0, jnp.where(blk == cur - 1, 1.0, 0.0)))
    admissible = blk * SEL_BLOCK <= t
    ov = ov_ref[...]
    for g in range(NSA_GROUPS):
        c0 = g * NSA_HPG * TQ
        ps = p[:, c0:c0 + TQ]
        for h in range(1, NSA_HPG):
            ps = ps + p[:, c0 + h * TQ:c0 + (h + 1) * TQ]
        hi = _mx(ps)
        lo = _mx(ps - hi.astype(F32))
        imp_t = _dot(ov, hi) + _dot(ov, lo)
        score = jnp.where(admissible, imp_t + FORCE_BONUS * forced, NEG)
        sel = jnp.zeros((nb, TQ), F32)
        for _ in range(n_pick):
            best = jnp.max(score, axis=0, keepdims=True)
            first = jnp.min(jnp.where(score == best, blk, nb), axis=0, keepdims=True)
            hit = blk == first
            sel = jnp.where(hit, 1.0, sel)
            score = jnp.where(hit, -jnp.inf, score)
        if nb < LANES:
            sel = jnp.concatenate([sel, jnp.ones((LANES - nb, TQ), F32)], axis=0)
        block_mask = _mx((sel.T - 1.0) * (-NEG))
        for h in range(NSA_HPG):
            r0 = (g * NSA_HPG + h) * TQ
            qa_s[r0:r0 + TQ, LANES:] = block_mask

    def values(vt_s, j0, ntile):
        return [_value_tiles(vt_s, 0, j0, ntile), _value_tiles(vt_s, 1, j0, ntile)]

    _softmax_reset(m_s, acc_s)

    def sel_chunk(c, near):
        tk = ch * TQ
        koff = pl.multiple_of(c * tk, tk)
        k_aug = jnp.concatenate([_mx(ks_ref[pl.ds(koff, tk), :]), et_ref[pl.ds(koff, tk), :]], axis=1)

        def scores(c0, c1):
            s = _dot_nt(k_aug, qa_s[c0:c1, :])
            if near:
                s = s + _near_bias(d_ref, lambda j: jnp.clip(j - (i - 2), 0, 3), c * ch, ch, c0, c1)
            return s

        _softmax_step(scores, values(vst_s, c * ch, ch), m_s, acc_s, COL_BLOCK)

    first_near = jnp.maximum(i - 1, 0) // ch

    def far_body(c, carry):
        sel_chunk(c, False)
        return carry

    def near_body(c, carry):
        sel_chunk(c, True)
        return carry

    lax.fori_loop(0, first_near, far_body, 0)
    lax.fori_loop(first_near, i // ch + 1, near_body, 0)
    osel_s[...] = acc_s[...]

    _softmax_reset(m_s, acc_s)
    n_win = WINDOW // TQ
    jw = jnp.maximum(i - (n_win_tiles - 1), 0)
    koff = pl.multiple_of(jw * TQ, TQ)
    tk = n_win_tiles * TQ

    def win_idx(j):
        r = j - (i - n_win)
        return jnp.where(r == 0, 4, jnp.clip(r - (n_win - 2), 0, 3))

    kw = _mx(kw_ref[pl.ds(koff, tk), :])
    _softmax_step(lambda c0, c1: (_dot_nt(kw, qa_s[c0:c1, :LANES])
                                  + _near_bias(d_ref, win_idx, jw, n_win_tiles, c0, c1)),
                  values(vwt_s, jw, n_win_tiles), m_s, acc_s, COL_BLOCK)
    ow_t = acc_s[...]
    os_t = osel_s[...]

    gate_t = jax.nn.sigmoid(g_ref[...]).T
    for m in range(NSA_HEADS // 2):
        parts = []
        for hh in (2 * m, 2 * m + 1):
            g = hh // NSA_HPG
            r = slice(g * half, (g + 1) * half)
            d = (1 - g) * half
            c = slice(hh * TQ, (hh + 1) * TQ)
            parts.append(gate_t[3 * hh:3 * hh + 1, :] * oc_t[r, c]
                         + gate_t[3 * hh + 1:3 * hh + 2, :] * (os_t[r, c] / os_t[d:d + 1, c])
                         + gate_t[3 * hh + 2:3 * hh + 3, :] * (ow_t[r, c] / ow_t[d:d + 1, c]))
        slab = jnp.concatenate(parts, axis=0).T
        o_ref[:, m * LANES:(m + 1) * LANES] = slab * jax.nn.silu(z_ref[:, m * LANES:(m + 1) * LANES])


def _nsa(u, cmp_kv, bias_c, d_tiles, e_t, ov, n_pick):
    b, s, _ = u.shape
    nq = s // TQ
    n16 = cmp_kv.shape[2]
    nb = ov.shape[0]
    rows = NSA_HEADS * TQ
    n_win_tiles = min(WINDOW // TQ + 1, nq)
    seq = lambda col: pl.BlockSpec((None, s, LANES), lambda bi, i, col=col: (bi, 0, col // LANES))
    return pl.pallas_call(
        functools.partial(_nsa_kernel, n_pick=n_pick, ch=_tiles_per_chunk(nq), n_win_tiles=n_win_tiles),
        grid=(b, nq),
        in_specs=[
            pl.BlockSpec((None, TQ, NSA_HEADS * HEAD_DIM), lambda bi, i: (bi, i, A_NSA_Q // (NSA_HEADS * HEAD_DIM))),
            pl.BlockSpec((None, None, n16, LANES), lambda bi, i: (0, bi, 0, 0)),
            pl.BlockSpec((None, None, n16, LANES), lambda bi, i: (1, bi, 0, 0)),
            seq(A_KV + 2 * LANES), seq(A_KV + 3 * LANES), seq(A_KV + 4 * LANES), seq(A_KV + 5 * LANES),
            pl.BlockSpec((None, TQ, LANES), lambda bi, i: (bi, i, A_GATE // LANES)),
            pl.BlockSpec((None, TQ, 512), lambda bi, i: (bi, i, A_AZ // 512)),
            pl.BlockSpec(bias_c.shape, lambda bi, i: (0, 0)),
            pl.BlockSpec(d_tiles.shape, lambda bi, i: (0, 0, 0)),
            pl.BlockSpec((s, LANES), lambda bi, i: (0, 0)),
            pl.BlockSpec((nb, n16), lambda bi, i: (0, 0)),
        ],
        out_specs=pl.BlockSpec((None, TQ, 512), lambda bi, i: (bi, i, 0)),
        out_shape=jax.ShapeDtypeStruct((b, s, 512), F32),
        scratch_shapes=[pltpu.VMEM((rows, 2 * LANES), MXU_DTYPE),
                        pltpu.VMEM((1, rows), F32),
                        pltpu.VMEM((LANES, rows), F32),
                        pltpu.VMEM((LANES, rows), F32),
                        pltpu.VMEM((LANES, n16), MXU_DTYPE),
                        pltpu.VMEM((NSA_GROUPS, nq, LANES, TQ), MXU_DTYPE),
                        pltpu.VMEM((NSA_GROUPS, nq, LANES, TQ), MXU_DTYPE)],
        compiler_params=_params(2),
        name="nsa",
    )(u, cmp_kv, cmp_kv, u, u, u, u, u, u, bias_c, d_tiles, e_t, ov)


def _dsa_kernel(q_ref, kv_ref, qi_ref, ki_ref, wi_ref, z_ref, d_ref, tri_ref, o_ref,
                qs_s, qis_s, sc_s, hi_s, lo_s, m_s, acc_s, seen_s, vt_s, *, k_top, ch):
    i = pl.program_id(1)

    @pl.when(i == 0)
    def _():
        _store_value_tiles(kv_ref, vt_s, (False,))

    qs_s[...] = _mx(_stack_heads(q_ref, [False] * DSA_HEADS))
    qis_s[...] = _mx(_stack_heads(qi_ref, [False] * IDX_HEADS))
    wi_t = wi_ref[...].T
    w_rows = [jnp.broadcast_to(wi_t[h:h + 1, :], (TQ, TQ)) for h in range(IDX_HEADS)]
    key_idx = lax.broadcasted_iota(jnp.int32, (TQ, TQ), 0)
    q_idx = lax.broadcasted_iota(jnp.int32, (TQ, TQ), 1)
    n_chunks = (i + ch) // ch
    pairs = ch // 2

    def score_chunk(c, last):
        koff = pl.multiple_of(c * (ch * TQ), ch * TQ)
        r = jnp.maximum(_dot_nt(_mx(ki_ref[pl.ds(koff, ch * TQ), :]), qis_s[...]), 0.0)
        keys = []
        for t in range(ch):
            jt = c * ch + t
            rt = r[t * TQ:(t + 1) * TQ]
            sc = w_rows[0] * rt[:, 0:TQ]
            for h in range(1, IDX_HEADS):
                sc = sc + w_rows[h] * rt[:, h * TQ:(h + 1) * TQ]
            if last:
                sc = jnp.where((jt == i) & (key_idx > q_idx), NEG, sc)
            bits = pltpu.bitcast(sc, jnp.int32)
            key = bits ^ (lax.shift_right_arithmetic(bits, 31) & jnp.int32(0x7FFFFFFF))
            key = jnp.where(sc == 0.0, 0, key)
            if last:
                key = jnp.where(jt > i, INT_MIN, key)
            keys.append(key)
            sc_s[jt] = key
        for pr in range(pairs):
            a, b = keys[2 * pr], keys[2 * pr + 1]
            hi_s[c * pairs + pr] = (a & jnp.int32(-65536)) | lax.shift_right_logical(b, 16)
            lo_s[c * pairs + pr] = (lax.shift_left((a & 0xFFFF) ^ 0x8000, 16)
                                    | ((b & 0xFFFF) ^ 0x8000))

    def earlier_chunk(c, carry):
        score_chunk(c, False)
        return carry

    lax.fori_loop(0, n_chunks - 1, earlier_chunk, 0)
    score_chunk(n_chunks - 1, True)

    def halves(words):
        return pltpu.bitcast(words, jnp.int16)

    def both_halves(v):
        w = lax.shift_left(v, 16) | (v & 0xFFFF)
        return halves(jnp.broadcast_to(w, (TQ, TQ)))

    def count16(words_s, pred):
        def body(c, acc):
            for pr in range(pairs):
                acc = acc + jnp.where(pred(halves(words_s[c * pairs + pr])), jnp.int16(1), jnp.int16(0))
            return acc
        acc = lax.fori_loop(0, n_chunks, body, jnp.zeros((2 * TQ, TQ), jnp.int16))
        acc = acc[:TQ] + acc[TQ:]
        return jnp.sum(acc.astype(jnp.int32), axis=0, keepdims=True)

    def bisect16(words_s, target):
        c0 = count16(words_s, lambda x: x >= jnp.int16(0))
        ok0 = c0 >= target

        def step(it, carry):
            v, n_gt = carry
            cand = v | lax.shift_left(jnp.int32(1), 14 - it)
            cand16 = both_halves(cand)
            c = count16(words_s, lambda x: x >= cand16)
            ok = c >= target
            return jnp.where(ok, cand, v), jnp.where(ok, n_gt, c)

        return lax.fori_loop(0, 15, step, (jnp.where(ok0, 0, -32768), jnp.where(ok0, 0, c0)))

    top, above = bisect16(hi_s, k_top)
    top16 = both_halves(top)

    def keep_ties(c, carry):
        for pr in range(pairs):
            p_ = c * pairs + pr
            tied = halves(hi_s[p_]) == top16
            lo_s[p_] = pltpu.bitcast(jnp.where(tied, halves(lo_s[p_]), jnp.int16(-32768)), jnp.int32)
        return carry

    lax.fori_loop(0, n_chunks, keep_ties, 0)
    low, above_low = bisect16(lo_s, k_top - above)
    thr = lax.shift_left(top, 16) | ((low ^ 0x8000) & 0xFFFF)
    thr = jnp.broadcast_to(thr, (TQ, TQ))
    need = (k_top - above - above_low).astype(F32)

    _softmax_reset(m_s, acc_s)
    seen_s[...] = jnp.zeros((TQ, TQ), F32)

    def att_chunk(c, near):
        tk = ch * TQ
        koff = pl.multiple_of(c * tk, tk)
        masks = []
        for t in range(ch):
            key = sc_s[c * ch + t]
            eq = key == thr
            prefix = _dot(tri_ref[...], _mx(jnp.where(eq, 1.0, 0.0)))
            seen = seen_s[...]
            take = jnp.where(eq, jnp.where(seen + prefix <= need, 1.0, 0.0), 0.0)
            seen_s[...] = seen + prefix[TQ - 1:TQ, :]
            masks.append(jnp.where(key > thr, 0.0, (take - 1.0) * (-NEG)))
        mb = masks[0] if ch == 1 else jnp.concatenate(masks, axis=0)
        mb = jnp.concatenate([mb] * DSA_HEADS, axis=1)
        kvm = _mx(kv_ref[pl.ds(koff, tk), :])

        def scores(c0, c1):
            s = _dot_nt(kvm, qs_s[c0:c1, :]) + mb[:, c0:c1]
            if near:
                s = s + _near_bias(d_ref, lambda j: jnp.clip(j - (i - 2), 0, 3), c * ch, ch, c0, c1)
            return s

        _softmax_step(scores, [_value_tiles(vt_s, 0, c * ch, ch)], m_s, acc_s, DSA_HEADS * TQ)

    first_near = jnp.maximum(i - 1, 0) // ch

    def far_body(c, carry):
        att_chunk(c, False)
        return carry

    def near_body(c, carry):
        att_chunk(c, True)
        return carry

    lax.fori_loop(0, first_near, far_body, 0)
    lax.fori_loop(first_near, n_chunks, near_body, 0)

    o_t = acc_s[...]
    o_t = o_t / o_t[0:1, :]
    for m in range(DSA_HEADS // 2):
        pair = [o_t[HEAD_DIM:, hh * TQ:(hh + 1) * TQ] for hh in (2 * m, 2 * m + 1)]
        slab = jnp.concatenate(pair, axis=0).T
        o_ref[:, m * LANES:(m + 1) * LANES] = slab * jax.nn.silu(z_ref[:, m * LANES:(m + 1) * LANES])


def _dsa(u, d_tiles, tri, k_top):
    b, s, _ = u.shape
    nq = s // TQ
    assert nq % 2 == 0, "the packed threshold search pairs key tiles"
    rows = DSA_HEADS * TQ
    irows = IDX_HEADS * TQ
    seq = lambda col: pl.BlockSpec((None, s, LANES), lambda bi, i, col=col: (bi, 0, col // LANES))
    return pl.pallas_call(
        functools.partial(_dsa_kernel, k_top=k_top, ch=_tiles_per_chunk(nq)),
        grid=(b, nq),
        in_specs=[
            pl.BlockSpec((None, TQ, DSA_HEADS * HEAD_DIM), lambda bi, i: (bi, i, A_DSA_Q // (DSA_HEADS * HEAD_DIM))),
            seq(A_DSA_KV),
            pl.BlockSpec((None, TQ, IDX_HEADS * IDX_DIM), lambda bi, i: (bi, i, A_IDX_Q // (IDX_HEADS * IDX_DIM))),
            seq(A_IDX_K),
            pl.BlockSpec((None, TQ, LANES), lambda bi, i: (bi, i, A_IDX_W // LANES)),
            pl.BlockSpec((None, TQ, 512), lambda bi, i: (bi, i, A_BZ // 512)),
            pl.BlockSpec(d_tiles.shape, lambda bi, i: (0, 0, 0)),
            pl.BlockSpec((TQ, TQ), lambda bi, i: (0, 0)),
        ],
        out_specs=pl.BlockSpec((None, TQ, 512), lambda bi, i: (bi, i, 0)),
        out_shape=jax.ShapeDtypeStruct((b, s, 512), F32),
        scratch_shapes=[pltpu.VMEM((rows, LANES), MXU_DTYPE),
                        pltpu.VMEM((irows, LANES), MXU_DTYPE),
                        pltpu.VMEM((nq, TQ, TQ), jnp.int32),
                        pltpu.VMEM((nq // 2, TQ, TQ), jnp.int32),
                        pltpu.VMEM((nq // 2, TQ, TQ), jnp.int32),
                        pltpu.VMEM((1, rows), F32),
                        pltpu.VMEM((LANES, rows), F32),
                        pltpu.VMEM((TQ, TQ), F32),
                        pltpu.VMEM((1, nq, LANES, TQ), MXU_DTYPE)],
        compiler_params=_params(2),
        name="dsa",
    )(u, u, u, u, u, u, d_tiles, tri)


def _causal_conv(x, xe_s, w_ref, b_ref):
    t = x.shape[0]
    xe_s[8:8 + t, :] = x
    y = b_ref[...] + w_ref[CONV_WIDTH - 1:CONV_WIDTH, :] * x
    for k in range(1, CONV_WIDTH):
        y = y + w_ref[CONV_WIDTH - 1 - k:CONV_WIDTH - k, :] * xe_s[8 - k:8 - k + t, :]
    xe_s[0:8, :] = x[t - 8:t, :]
    return y


def _rglru_kernel(x_ref, z_ref, cw_ref, cb_ref, wa_ref, ba_ref, wx_ref, bx_ref, lam_ref, o_ref,
                  xe_s, h_s):
    t, c = x_ref.shape

    @pl.when(pl.program_id(1) == 0)
    def _():
        xe_s[0:8, :] = jnp.zeros((8, c), F32)
        h_s[...] = jnp.zeros(h_s.shape, F32)

    xc = _causal_conv(x_ref[...], xe_s, cw_ref, cb_ref)
    xcm = _mx(xc)
    r = jax.nn.sigmoid(_dot(xcm, wa_ref[...]) + ba_ref[...])
    ig = jax.nn.sigmoid(_dot(xcm, wx_ref[...]) + bx_ref[...])
    nl = -lam_ref[...]
    softplus = jnp.maximum(nl, 0.0) + jnp.log1p(jnp.exp(-jnp.abs(nl)))
    log_a = (-LRU_C * r) * softplus
    a = jnp.exp(log_a)
    bb = jnp.sqrt(-jnp.tanh(log_a) * (a * a + 1.0)) * (ig * xc)
    row = lax.broadcasted_iota(jnp.int32, (t, c), 0) & (SUBLANES - 1)
    d = 1
    while d < SUBLANES:
        live = row >= d
        a_sh = jnp.where(live, pltpu.roll(a, d, 0), 1.0)
        b_sh = jnp.where(live, pltpu.roll(bb, d, 0), 0.0)
        bb = a * b_sh + bb
        a = a * a_sh
        d *= 2
    carry = h_s[0:1, :]
    groups = []
    for g in range(t // SUBLANES):
        rows = slice(g * SUBLANES, (g + 1) * SUBLANES)
        groups.append(a[rows] * carry + bb[rows])
        carry = groups[-1][SUBLANES - 1:SUBLANES, :]
    h = jnp.concatenate(groups, axis=0)
    h_s[0:1, :] = carry
    o_ref[...] = h * jax.nn.silu(z_ref[...])


def _rglru(u, conv_w, conv_b, wa_bd, ba, wx_bd, bx, lam, t=256):
    b, s, _ = u.shape
    c = LRU_WIDTH
    whole = lambda bi, i: (0, 0)
    return pl.pallas_call(
        _rglru_kernel,
        grid=(b, s // t),
        in_specs=[pl.BlockSpec((None, t, c), lambda bi, i: (bi, i, R_CX // c)),
                  pl.BlockSpec((None, t, c), lambda bi, i: (bi, i, R_CZ // c)),
                  pl.BlockSpec((CONV_WIDTH, c), whole), pl.BlockSpec((1, c), whole),
                  pl.BlockSpec((c, c), whole), pl.BlockSpec((1, c), whole),
                  pl.BlockSpec((c, c), whole), pl.BlockSpec((1, c), whole),
                  pl.BlockSpec((1, c), whole)],
        out_specs=pl.BlockSpec((None, t, c), lambda bi, i: (bi, i, 0)),
        out_shape=jax.ShapeDtypeStruct((b, s, c), F32),
        scratch_shapes=[pltpu.VMEM((t + 8, c), F32), pltpu.VMEM((8, c), F32)],
        compiler_params=_params(2),
        name="rglru",
    )(u, u, conv_w, conv_b.reshape(1, c), wa_bd, ba.reshape(1, c), wx_bd, bx.reshape(1, c),
      lam.reshape(1, c))


def _mlstm_kernel(qk_ref, v_ref, g_ref, og_ref, z_ref, cw_ref, cb_ref, gb_ref, o_ref,
                  xe_s, c_s, m_s):
    t = qk_ref.shape[0]
    L = MLSTM_CHUNK
    H = MLSTM_HEADS
    D = MLSTM_DIM

    @pl.when(pl.program_id(1) == 0)
    def _():
        xe_s[0:8, :] = jnp.zeros((8, xe_s.shape[1]), F32)
        c_s[...] = jnp.zeros(c_s.shape, F32)
        m_s[...] = jnp.zeros(m_s.shape, F32)

    qk = jax.nn.silu(_causal_conv(qk_ref[...], xe_s, cw_ref, cb_ref))

    lane = lax.broadcasted_iota(jnp.int32, (t, LANES), 1)
    row_in = lax.broadcasted_iota(jnp.int32, (t, LANES), 0) & (L - 1)
    gs = g_ref[...] + gb_ref[...]
    log_sig = jnp.minimum(gs, 0.0) - jnp.log1p(jnp.exp(-jnp.abs(gs)))
    pre = jnp.where(lane < H, gs, log_sig)
    cum = pre
    d = 1
    while d < L:
        cum = cum + jnp.where(row_in >= d, pltpu.roll(cum, d, 0), 0.0)
        d *= 2
    comb = jnp.where(lane < H, pre, cum)
    comb_t = comb.T

    causal_t = (lax.broadcasted_iota(jnp.int32, (L, L), 0) <= lax.broadcasted_iota(jnp.int32, (L, L), 1))
    pad_rows = jnp.zeros((7, L), F32)
    v_ts = [v_ref[:, h * D:(h + 1) * D].T for h in range(H)]
    states = [c_s[h] for h in range(H)]
    m_prevs = [m_s[h:h + 1, 0:1] for h in range(H)]
    outs = [[] for _ in range(H)]
    for c in range(t // L):
        r = slice(c * L, (c + 1) * L)
        for h in range(H):
            v_t, state, m_prev = v_ts[h], states[h], m_prevs[h]
            q_ = _mx(qk[r, h * D:(h + 1) * D])
            k_ = _mx(qk[r, (H + h) * D:(H + h + 1) * D] * (D ** -0.5))
            li_row = comb_t[h:h + 1, r]
            b_row = comb_t[H + h:H + h + 1, r]
            lb_col = comb[r, h:h + 1] - comb[r, H + h:H + h + 1]
            b_last = b_row[:, L - 1:L]

            dmat_t = jnp.where(causal_t, b_row + lb_col, -jnp.inf)
            decay_row = b_last - b_row + li_row
            m_new = jnp.maximum(b_last + m_prev, jnp.max(decay_row, axis=-1, keepdims=True))
            inter = b_row + m_prev
            m_t = jnp.maximum(inter, jnp.max(dmat_t, axis=0, keepdims=True))
            w_t = _dot_nt(k_, q_) * jnp.exp(dmat_t - m_t)
            prev = jnp.exp(inter - m_t)

            read = _dot_nt(_mx(state), q_)
            num_t = prev * read[:D] + _dot(_mx(v_t[:, r]), _mx(w_t))
            den = prev * read[D:D + 1] + jnp.sum(w_t, axis=0, keepdims=True)
            outs[h].append(num_t / jnp.maximum(jnp.abs(den), jnp.exp(-m_t)))

            wk = jnp.exp(decay_row - m_new)
            inc = jnp.concatenate([v_t[:, r] * wk, wk, pad_rows], axis=0)
            states[h] = jnp.exp(b_last + m_prev - m_new) * state + _dot(_mx(inc), k_)
            m_prevs[h] = m_new

    for h in range(H):
        c_s[h] = states[h]
        m_s[h:h + 1, :] = jnp.broadcast_to(m_prevs[h], (1, LANES))
        out = jnp.concatenate(outs[h], axis=1).T
        cols = slice(h * D, (h + 1) * D)
        o_ref[:, cols] = jax.nn.sigmoid(og_ref[:, cols]) * out * jax.nn.silu(z_ref[:, cols])


def _mlstm(u, conv_w, conv_b, gate_bias, t=128):
    b, s, _ = u.shape
    w = MLSTM_WIDTH
    whole = lambda bi, i: (0, 0)
    return pl.pallas_call(
        _mlstm_kernel,
        grid=(b, s // t),
        in_specs=[pl.BlockSpec((None, t, 2 * w), lambda bi, i: (bi, i, R_QK // (2 * w))),
                  pl.BlockSpec((None, t, w), lambda bi, i: (bi, i, R_V // w)),
                  pl.BlockSpec((None, t, LANES), lambda bi, i: (bi, i, R_GATE // LANES)),
                  pl.BlockSpec((None, t, w), lambda bi, i: (bi, i, R_O // w)),
                  pl.BlockSpec((None, t, w), lambda bi, i: (bi, i, R_Z // w)),
                  pl.BlockSpec((CONV_WIDTH, 2 * w), whole), pl.BlockSpec((1, 2 * w), whole),
                  pl.BlockSpec((1, LANES), whole)],
        out_specs=pl.BlockSpec((None, t, w), lambda bi, i: (bi, i, 0)),
        out_shape=jax.ShapeDtypeStruct((b, s, w), F32),
        scratch_shapes=[pltpu.VMEM((t + 8, 2 * w), F32),
                        pltpu.VMEM((MLSTM_HEADS, MLSTM_DIM + 8, MLSTM_DIM), F32),
                        pltpu.VMEM((8, LANES), F32)],
        compiler_params=_params(2),
        name="mlstm",
    )(u, u, u, u, u, conv_w, conv_b.reshape(1, 2 * w), gate_bias)


def _attn_weights(w_in):
    d = w_in.shape[0]
    widths = (512, 768, 24, 512, 512, 64, 64, 256, 64, 4, 512)
    offs = np.concatenate([[0], np.cumsum(widths)])
    a_q, a_kv, a_g, a_z, b_q, b_k, b_v, b_qi, b_ki, b_wi, b_z = [
        w_in[:, offs[k]:offs[k + 1]] for k in range(len(widths))]
    scale = HEAD_DIM ** -0.5 * LOG2E
    zeros = lambda n: jnp.zeros((d, n), w_in.dtype)
    cols = [
        a_q * scale, b_q * scale, a_z, b_z, b_qi, a_kv,
        a_g, zeros(LANES - 24),
        b_k, b_v,
        b_ki, zeros(LANES - IDX_DIM),
        b_wi * (IDX_DIM ** -0.5 * IDX_HEADS ** -0.5), zeros(LANES - IDX_HEADS),
    ]
    w = jnp.concatenate(cols, axis=1)
    assert w.shape[1] == A_TOTAL
    return _mx(w)


def _rec_weights(w_in):
    d = w_in.shape[0]
    widths = (512, 512, 512, 512, 512, 4, 4, 512, 512)
    offs = np.concatenate([[0], np.cumsum(widths)])
    c_x, c_z, d_q, d_k, d_v, d_i, d_f, d_o, d_z = [w_in[:, offs[k]:offs[k + 1]] for k in range(len(widths))]
    w = jnp.concatenate([c_x, c_z, d_q, d_k, d_v, d_o, d_z, d_i, d_f,
                         jnp.zeros((d, R_TOTAL - R_GATE - 2 * MLSTM_HEADS), w_in.dtype)], axis=1)
    assert w.shape[1] == R_TOTAL
    return _mx(w)


def _block_diag(w):
    g, n, _ = w.shape
    eye = jnp.eye(g, dtype=w.dtype)
    return (eye[:, None, :, None] * w[:, :, None, :]).reshape(g * n, g * n)


def _attention_layer(x2d, b, s, norm_g, w_in, cmp_pos_k, cmp_w1_k, cmp_w2_k, cmp_pos_v, cmp_w1_v, cmp_w2_v,
                     t5_table):
    nq = s // TQ
    n16 = s // CMP_STRIDE
    nb = s // SEL_BLOCK
    u = _norm_proj(x2d, norm_g, _attn_weights(w_in)).reshape(b, s, A_TOTAL)

    pos = jnp.stack([cmp_pos_k, cmp_pos_v])
    pos = jnp.concatenate([pos] * NSA_GROUPS, axis=-1).reshape(2, CMP_BLOCK, 1, LANES)
    w1 = jnp.stack([cmp_w1_k, cmp_w1_v]).reshape(2, CMP_BLOCK, HEAD_DIM, CMP_HIDDEN)
    z1 = jnp.zeros_like(w1)
    w1 = _mx(jnp.stack([jnp.concatenate([w1, z1], axis=2), jnp.concatenate([z1, w1], axis=2)], axis=1))
    w2 = jnp.stack([cmp_w2_k, cmp_w2_v])
    zpad = jnp.zeros_like(w2)
    w2 = _mx(jnp.stack([jnp.concatenate([w2, zpad], -1), jnp.concatenate([zpad, w2], -1)], axis=1))
    cmp_kv = _compress(u, pos, w1, w2)

    tbl = t5_table.astype(F32)
    tbl_a, tbl_b = tbl[:, :NSA_HEADS], tbl[:, NSA_HEADS:]
    kj = np.arange(TQ)[:, None]
    qi = np.arange(TQ)[None, :]
    assert nb <= LANES
    e_t = jnp.asarray((np.arange(s)[:, None] // SEL_BLOCK) == np.arange(LANES)[None, :], MXU_DTYPE)
    ci = np.arange(n16)[None, :]
    sj = np.arange(nb)[:, None]
    ov = jnp.asarray((ci * CMP_STRIDE < (sj + 1) * SEL_BLOCK) & (ci * CMP_STRIDE + CMP_BLOCK > sj * SEL_BLOCK)
                     & (ci < n16 - 1), MXU_DTYPE)
    tri = jnp.asarray(qi <= kj, MXU_DTYPE)

    ya = _nsa(u, cmp_kv, _cmp_bias(tbl_a, nq, n16), _near_tiles(tbl_a), e_t, ov, min(N_SEL, nb))
    yb = _dsa(u, _near_tiles(tbl_b), tri, min(DSA_TOPK_MAX, s // 4))
    return ya.reshape(b * s, -1), yb.reshape(b * s, -1)


def _recurrent_layer(x2d, b, s, norm_g, w_in, conv_c_w, conv_c_b, wa, ba, wx, bx, lam,
                     conv_d_w, conv_d_b, b_i, b_f):
    u = _norm_proj(x2d, norm_g, _rec_weights(w_in)).reshape(b, s, R_TOTAL)
    yc = _rglru(u, conv_c_w, conv_c_b, _mx(_block_diag(wa)), ba, _mx(_block_diag(wx)), bx, lam)
    gate_bias = jnp.concatenate([b_i, b_f, jnp.zeros((LANES - 2 * MLSTM_HEADS,), F32)]).reshape(1, LANES)
    yd = _mlstm(u, conv_d_w, conv_d_b, gate_bias)
    return yc.reshape(b * s, -1), yd.reshape(b * s, -1)


def kernel(x, p, norm_g, final_g, ple_w, ple_gate_w, t5_table, attn_w_in, attn_w_out, cmp_pos_k, cmp_w1_k, cmp_w2_k, cmp_pos_v, cmp_w1_v, cmp_w2_v, rec_w_in, rec_w_out, lru_conv_w, lru_conv_b, lru_wa, lru_ba, lru_wx, lru_bx, lru_lambda, mlstm_conv_w, mlstm_conv_b, mlstm_b_i, mlstm_b_f):
    b, s, d = x.shape
    depth = p.shape[0]
    x2d = x.reshape(b * s, d)
    for i in range(depth):
        j = i // 2
        if i % 2 == 0:
            ya, yb = _attention_layer(x2d, b, s, norm_g[i], attn_w_in[j], cmp_pos_k[j], cmp_w1_k[j], cmp_w2_k[j],
                                      cmp_pos_v[j], cmp_w1_v[j], cmp_w2_v[j], t5_table)
            w_out = attn_w_out[j]
        else:
            ya, yb = _recurrent_layer(x2d, b, s, norm_g[i], rec_w_in[j], lru_conv_w[j], lru_conv_b[j],
                                      lru_wa[j], lru_ba[j], lru_wx[j], lru_bx[j], lru_lambda[j],
                                      mlstm_conv_w[j], mlstm_conv_b[j], mlstm_b_i[j], mlstm_b_f[j])
            w_out = rec_w_out[j]
        x2d = _out_proj(x2d, ya, yb, p[i].reshape(b * s, -1), w_out, ple_w[i], ple_gate_w[i], final_g,
                        final=(i == depth - 1))
    return x2d.reshape(b, s, d)
```

```python
import functools
import math

import numpy as np
import jax
import jax.numpy as jnp
from jax import lax
from jax.experimental import pallas as pl
from jax.experimental.pallas import tpu as pltpu

F32 = jnp.float32
MXU_DTYPE = jnp.bfloat16

HEAD_DIM = 64
NSA_HEADS = 8
NSA_GROUPS = 2
NSA_HPG = NSA_HEADS // NSA_GROUPS
CMP_BLOCK = 32
CMP_STRIDE = 16
CMP_HIDDEN = 256
SEL_BLOCK = 64
N_SEL = 8
WINDOW = 512
FORCE_BONUS = 1e4
DSA_HEADS = 8
IDX_HEADS = 4
IDX_DIM = 64
DSA_TOPK_MAX = 256
N_BUCKETS = 32
T5_MAX_DIST = 128
LRU_WIDTH = 512
CONV_WIDTH = 4
LRU_C = 8.0
MLSTM_HEADS = 4
MLSTM_DIM = 128
MLSTM_WIDTH = MLSTM_HEADS * MLSTM_DIM
MLSTM_CHUNK = 64
RMS_EPS = 1e-6
NEG = -1e30

LANES = 128
SUBLANES = 8
COL_BLOCK = 256
TQ = 128
INT_MIN = -2 ** 31
LOG2E = math.log2(math.e)
VMEM_LIMIT = 48 * 1024 * 1024

A_NSA_Q = 0
A_DSA_Q = 512
A_AZ = 1024
A_BZ = 1536
A_IDX_Q = 2048
A_KV = 2304
A_GATE = 3072
A_DSA_KV = 3200
A_IDX_K = 3328
A_IDX_W = 3456
A_TOTAL = 3584

R_CX = 0
R_CZ = 512
R_QK = 1024
R_V = 2048
R_O = 2560
R_Z = 3072
R_GATE = 3584
R_TOTAL = 3840


def _dot(a, b):
    return jnp.dot(a, b, preferred_element_type=F32)


def _dot_nt(a, b):
    return lax.dot_general(a, b, (((1,), (1,)), ((), ())), preferred_element_type=F32)


def _mx(a):
    return a.astype(MXU_DTYPE)


def _params(n_grid):
    return pltpu.CompilerParams(dimension_semantics=("arbitrary",) * n_grid,
                                vmem_limit_bytes=VMEM_LIMIT)


def _norm_proj_kernel(x_ref, g_ref, w_ref, o_ref):
    x = x_ref[...]
    ms = jnp.mean(x * x, axis=-1, keepdims=True)
    y = x * lax.rsqrt(ms + RMS_EPS) * g_ref[...]
    o_ref[...] = _dot(_mx(y), w_ref[...])


def _norm_proj(x2d, g, w, tm=256):
    m, d = x2d.shape
    n = w.shape[1]
    return pl.pallas_call(
        _norm_proj_kernel,
        grid=(m // tm,),
        in_specs=[pl.BlockSpec((tm, d), lambda i: (i, 0)),
                  pl.BlockSpec((1, d), lambda i: (0, 0)),
                  pl.BlockSpec((d, n), lambda i: (0, 0))],
        out_specs=pl.BlockSpec((tm, n), lambda i: (i, 0)),
        out_shape=jax.ShapeDtypeStruct((m, n), F32),
        compiler_params=_params(1),
        name="norm_proj",
    )(x2d, g.reshape(1, d), w)


def _out_proj_kernel(x_ref, ya_ref, yb_ref, p_ref, wo_ref, pw_ref, gw_ref, fg_ref, o_ref, *, final):
    half = ya_ref.shape[-1]
    y = _dot(_mx(ya_ref[...]), wo_ref[:half, :]) + _dot(_mx(yb_ref[...]), wo_ref[half:, :])
    x1 = x_ref[...] + y
    gate = jax.nn.sigmoid(_dot(_mx(x1), gw_ref[...]))
    x2 = x1 + _dot(_mx(p_ref[...]), pw_ref[...]) * gate
    if final:
        ms = jnp.mean(x2 * x2, axis=-1, keepdims=True)
        x2 = x2 * lax.rsqrt(ms + RMS_EPS) * fg_ref[...]
    o_ref[...] = x2


def _out_proj(x2d, ya, yb, p2d, w_out, ple_w, gate_w, final_g, final, tm=512):
    m, d = x2d.shape
    half = ya.shape[1]
    pd = p2d.shape[1]
    row = lambda i: (i, 0)
    whole = lambda i: (0, 0)
    return pl.pallas_call(
        functools.partial(_out_proj_kernel, final=final),
        grid=(m // tm,),
        in_specs=[pl.BlockSpec((tm, d), row), pl.BlockSpec((tm, half), row),
                  pl.BlockSpec((tm, half), row), pl.BlockSpec((tm, pd), row),
                  pl.BlockSpec((2 * half, d), whole), pl.BlockSpec((pd, d), whole),
                  pl.BlockSpec((d, d), whole), pl.BlockSpec((1, d), whole)],
        out_specs=pl.BlockSpec((tm, d), row),
        out_shape=jax.ShapeDtypeStruct((m, d), F32),
        compiler_params=_params(1),
        name="out_proj",
    )(x2d, ya, yb, p2d, _mx(w_out), _mx(ple_w), _mx(gate_w), final_g.reshape(1, d))


def _bucket_np(n):
    n = np.asarray(n)
    max_exact = N_BUCKETS // 2
    nf = np.maximum(n, 1).astype(np.float32)
    large = max_exact + (np.log(nf / np.float32(max_exact)) / np.float32(math.log(T5_MAX_DIST / max_exact))
                         * np.float32(N_BUCKETS - max_exact)).astype(np.int32)
    large = np.minimum(large, N_BUCKETS - 1)
    return np.where(n < max_exact, n, large)


def _bias_index(dist):
    dist = np.asarray(dist)
    return np.where(dist >= 0, _bucket_np(np.maximum(dist, 0)), N_BUCKETS).astype(np.int32)


def _lookup(ext, idx):
    a, b = idx.shape
    rows = jnp.arange(ext.shape[0], dtype=jnp.int32)[:, None]
    onehot = (rows == jnp.asarray(idx.reshape(1, -1), jnp.int32)).astype(F32)
    out = jnp.dot(ext.T, onehot, precision=lax.Precision.HIGHEST)
    return out.reshape(ext.shape[1], a, b).transpose(1, 0, 2).reshape(a, -1)


def _near_tiles(tbl):
    h = tbl.shape[1]
    far = tbl[N_BUCKETS - 1]
    ext = jnp.concatenate([tbl, jnp.full((1, h), NEG, F32)], axis=0)
    kj = np.arange(TQ)[:, None]
    qi = np.arange(TQ)[None, :]
    far = jnp.repeat(far, TQ)[None, :]
    d0 = (_lookup(ext, _bias_index(qi - kj)) - far) * LOG2E
    d1 = (_lookup(ext, _bias_index(TQ + qi - kj)) - far) * LOG2E
    edge = jnp.asarray(np.tile(np.where(kj > qi, 0.0, NEG), (1, h)), F32)
    return jnp.stack([jnp.zeros_like(d0), d1, d0, jnp.full_like(d0, NEG), edge])


def _cmp_bias(tbl, nq, n16):
    h = tbl.shape[1]
    ext = jnp.concatenate([tbl, jnp.full((1, h), NEG, F32)], axis=0)
    per_tile = TQ // CMP_STRIDE
    off = per_tile * (nq - 1)
    cc = np.arange(n16 + off)[:, None] - off
    qi = np.arange(TQ)[None, :]
    g = _lookup(ext, _bias_index(qi - CMP_STRIDE * cc - (CMP_BLOCK - 1))) * LOG2E
    windows = (off - per_tile * np.arange(nq))[:, None] + np.arange(n16)[None, :]
    return jnp.take(g, jnp.asarray(windows, jnp.int32), axis=0)


def _compress_kernel(x_ref, pos_ref, w1_ref, w2_ref, o_ref):
    n16 = o_ref.shape[0]
    acc = None
    for g in range(NSA_GROUPS):
        pa = pb = None
        for l in range(CMP_STRIDE):
            x = x_ref[pl.ds(l, n16, stride=CMP_STRIDE), :]
            a = _dot(_mx(x + pos_ref[l]), w1_ref[g, l])
            c = _dot(_mx(x + pos_ref[CMP_STRIDE + l]), w1_ref[g, CMP_STRIDE + l])
            pa = a if pa is None else pa + a
            pb = c if pb is None else pb + c
        pre = pa + pltpu.roll(pb, n16 - 1, 0)
        t = _dot(_mx(jax.nn.silu(pre)), w2_ref[g])
        acc = t if acc is None else acc + t
    o_ref[...] = acc


def _compress(u, pos, w1, w2):
    b, s, _ = u.shape
    n16 = s // CMP_STRIDE
    return pl.pallas_call(
        _compress_kernel,
        grid=(2, b),
        in_specs=[pl.BlockSpec((None, s, LANES), lambda k, i: (i, 0, A_KV // LANES + k)),
                  pl.BlockSpec((None, CMP_BLOCK, 1, LANES), lambda k, i: (k, 0, 0, 0)),
                  pl.BlockSpec((None, NSA_GROUPS, CMP_BLOCK, LANES, CMP_HIDDEN), lambda k, i: (k, 0, 0, 0, 0)),
                  pl.BlockSpec((None, NSA_GROUPS, CMP_HIDDEN, LANES), lambda k, i: (k, 0, 0, 0))],
        out_specs=pl.BlockSpec((None, None, n16, LANES), lambda k, i: (k, i, 0, 0)),
        out_shape=jax.ShapeDtypeStruct((2, b, n16, LANES), F32),
        compiler_params=_params(2),
        name="compress",
    )(u, pos, w1, w2)


def _softmax_reset(m_s, acc_s):
    m_s[...] = jnp.full(m_s.shape, -jnp.inf, F32)
    acc_s[...] = jnp.zeros(acc_s.shape, F32)


def _softmax_step(score_fn, values, m_s, acc_s, col_block):
    r = m_s.shape[1]
    s_next = score_fn(0, col_block)
    for c0 in range(0, r, col_block):
        cols = slice(c0, c0 + col_block)
        s = s_next
        if c0 + col_block < r:
            s_next = score_fn(c0 + col_block, c0 + 2 * col_block)
        m_prev = m_s[:, cols]
        m_new = jnp.maximum(m_prev, jnp.max(s, axis=0, keepdims=True))
        alpha = jnp.exp2(m_prev - m_new)
        p = _mx(jnp.exp2(s - m_new))
        v = values[c0 * len(values) // r]
        acc_s[:, cols] = alpha * acc_s[:, cols] + _dot(v, p)
        m_s[:, cols] = m_new


def _store_value_tiles(v_ref, vt_s, ones_hi):
    row = lax.broadcasted_iota(jnp.int32, (LANES, TQ), 0)

    def body(j, carry):
        vt = v_ref[pl.ds(pl.multiple_of(j * TQ, TQ), TQ), :].T
        for k, hi in enumerate(ones_hi):
            vt_s[k, j] = _mx(jnp.where((row >= HEAD_DIM) == hi, 1.0, vt))
        return carry

    lax.fori_loop(0, vt_s.shape[1], body, 0)


def _value_tiles(vt_s, k, j0, ntile):
    tiles = [vt_s[k, j0 + t] for t in range(ntile)]
    return tiles[0] if ntile == 1 else jnp.concatenate(tiles, axis=1)


def _near_bias(d_ref, idx_of_tile, j0, ntile, c0, c1):
    tiles = [d_ref[idx_of_tile(j0 + t), :, c0:c1] for t in range(ntile)]
    return tiles[0] if ntile == 1 else jnp.concatenate(tiles, axis=0)


def _stack_heads(q_ref, hi):
    lane = lax.broadcasted_iota(jnp.int32, (TQ, LANES), 1)
    out = []
    for h, up in enumerate(hi):
        v = q_ref[:, (h // 2) * LANES:(h // 2 + 1) * LANES]
        if (h % 2 == 1) != up:
            v = pltpu.roll(v, HEAD_DIM, 1)
        out.append(jnp.where((lane >= HEAD_DIM) == up, v, 0.0))
    return jnp.concatenate(out, axis=0)


def _tiles_per_chunk(nq):
    for ch in (4, 2):
        if nq % ch == 0:
            return ch
    return 1


def _nsa_kernel(q_ref, kc_ref, vc_ref, ks_ref, vs_ref, kw_ref, vw_ref, g_ref, z_ref,
                bc_ref, d_ref, et_ref, ov_ref, o_ref,
                qa_s, m_s, acc_s, osel_s, vct_s, vst_s, vwt_s, *, n_pick, ch, n_win_tiles):
    i = pl.program_id(1)
    rows = NSA_HEADS * TQ
    nb = ov_ref.shape[0]
    q0 = i * TQ
    half = LANES // NSA_GROUPS

    @pl.when(i == 0)
    def _():
        vct_s[...] = _mx(vc_ref[...].T)
        _store_value_tiles(vs_ref, vst_s, (True, False))
        _store_value_tiles(vw_ref, vwt_s, (True, False))

    qs = _mx(_stack_heads(q_ref, [h >= NSA_HPG for h in range(NSA_HEADS)]))
    qa_s[:, :LANES] = qs

    s = _dot_nt(_mx(kc_ref[...]), qs) + bc_ref[...]
    p = jnp.exp2(s - jnp.max(s, axis=0, keepdims=True))
    t_lane = q0 + (lax.broadcasted_iota(jnp.int32, (1, rows), 1) & (TQ - 1))
    any_valid = jnp.where(t_lane >= CMP_BLOCK - 1, 1.0, 0.0)
    p = p * (any_valid / jnp.sum(p, axis=0, keepdims=True))
    oc_t = _dot(vct_s[...], _mx(p))

    blk = lax.broadcasted_iota(jnp.int32, (nb, TQ), 0)
    t = q0 + lax.broadcasted_iota(jnp.int32, (nb, TQ), 1)
    cur = lax.shift_right_logical(t, int(math.log2(SEL_BLOCK)))
    forced = jnp.where(blk == 0, 1.0, jnp.where(blk == cur, 1.0, jnp.where(blk == cur - 1, 1.0, 0.0)))
    admissible = blk * SEL_BLOCK <= t
    ov = ov_ref[...]
    for g in range(NSA_GROUPS):
        c0 = g * NSA_HPG * TQ
        ps = p[:, c0:c0 + TQ]
        for h in range(1, NSA_HPG):
            ps = ps + p[:, c0 + h * TQ:c0 + (h + 1) * TQ]
        hi = _mx(ps)
        lo = _mx(ps - hi.astype(F32))
        imp_t = _dot(ov, hi) + _dot(ov, lo)
        score = jnp.where(admissible, imp_t + FORCE_BONUS * forced, NEG)
        sel = jnp.zeros((nb, TQ), F32)
        for _ in range(n_pick):
            best = jnp.max(score, axis=0, keepdims=True)
            first = jnp.min(jnp.where(score == best, blk, nb), axis=0, keepdims=True)
            hit = blk == first
            sel = jnp.where(hit, 1.0, sel)
            score = jnp.where(hit, -jnp.inf, score)
        if nb < LANES:
            sel = jnp.concatenate([sel, jnp.ones((LANES - nb, TQ), F32)], axis=0)
        block_mask = _mx((sel.T - 1.0) * (-NEG))
        for h in range(NSA_HPG):
            r0 = (g * NSA_HPG + h) * TQ
            qa_s[r0:r0 + TQ, LANES:] = block_mask

    def values(vt_s, j0, ntile):
        return [_value_tiles(vt_s, 0, j0, ntile), _value_tiles(vt_s, 1, j0, ntile)]

    _softmax_reset(m_s, acc_s)

    def sel_chunk(c, near):
        tk = ch * TQ
        koff = pl.multiple_of(c * tk, tk)
        k_aug = jnp.concatenate([_mx(ks_ref[pl.ds(koff, tk), :]), et_ref[pl.ds(koff, tk), :]], axis=1)

        def scores(c0, c1):
            s = _dot_nt(k_aug, qa_s[c0:c1, :])
            if near:
                s = s + _near_bias(d_ref, lambda j: jnp.clip(j - (i - 2), 0, 3), c * ch, ch, c0, c1)
            return s

        _softmax_step(scores, values(vst_s, c * ch, ch), m_s, acc_s, COL_BLOCK)

    first_near = jnp.maximum(i - 1, 0) // ch

    def far_body(c, carry):
        sel_chunk(c, False)
        return carry

    def near_body(c, carry):
        sel_chunk(c, True)
        return carry

    lax.fori_loop(0, first_near, far_body, 0)
    lax.fori_loop(first_near, i // ch + 1, near_body, 0)
    osel_s[...] = acc_s[...]

    _softmax_reset(m_s, acc_s)
    n_win = WINDOW // TQ
    jw = jnp.maximum(i - (n_win_tiles - 1), 0)
    koff = pl.multiple_of(jw * TQ, TQ)
    tk = n_win_tiles * TQ

    def win_idx(j):
        r = j - (i - n_win)
        return jnp.where(r == 0, 4, jnp.clip(r - (n_win - 2), 0, 3))

    kw = _mx(kw_ref[pl.ds(koff, tk), :])
    _softmax_step(lambda c0, c1: (_dot_nt(kw, qa_s[c0:c1, :LANES])
                                  + _near_bias(d_ref, win_idx, jw, n_win_tiles, c0, c1)),
                  values(vwt_s, jw, n_win_tiles), m_s, acc_s, COL_BLOCK)
    ow_t = acc_s[...]
    os_t = osel_s[...]

    gate_t = jax.nn.sigmoid(g_ref[...]).T
    for m in range(NSA_HEADS // 2):
        parts = []
        for hh in (2 * m, 2 * m + 1):
            g = hh // NSA_HPG
            r = slice(g * half, (g + 1) * half)
            d = (1 - g) * half
            c = slice(hh * TQ, (hh + 1) * TQ)
            parts.append(gate_t[3 * hh:3 * hh + 1, :] * oc_t[r, c]
                         + gate_t[3 * hh + 1:3 * hh + 2, :] * (os_t[r, c] / os_t[d:d + 1, c])
                         + gate_t[3 * hh + 2:3 * hh + 3, :] * (ow_t[r, c] / ow_t[d:d + 1, c]))
        slab = jnp.concatenate(parts, axis=0).T
        o_ref[:, m * LANES:(m + 1) * LANES] = slab * jax.nn.silu(z_ref[:, m * LANES:(m + 1) * LANES])


def _nsa(u, cmp_kv, bias_c, d_tiles, e_t, ov, n_pick):
    b, s, _ = u.shape
    nq = s // TQ
    n16 = cmp_kv.shape[2]
    nb = ov.shape[0]
    rows = NSA_HEADS * TQ
    n_win_tiles = min(WINDOW // TQ + 1, nq)
    seq = lambda col: pl.BlockSpec((None, s, LANES), lambda bi, i, col=col: (bi, 0, col // LANES))
    return pl.pallas_call(
        functools.partial(_nsa_kernel, n_pick=n_pick, ch=_tiles_per_chunk(nq), n_win_tiles=n_win_tiles),
        grid=(b, nq),
        in_specs=[
            pl.BlockSpec((None, TQ, NSA_HEADS * HEAD_DIM), lambda bi, i: (bi, i, A_NSA_Q // (NSA_HEADS * HEAD_DIM))),
            pl.BlockSpec((None, None, n16, LANES), lambda bi, i: (0, bi, 0, 0)),
            pl.BlockSpec((None, None, n16, LANES), lambda bi, i: (1, bi, 0, 0)),
            seq(A_KV + 2 * LANES), seq(A_KV + 3 * LANES), seq(A_KV + 4 * LANES), seq(A_KV + 5 * LANES),
            pl.BlockSpec((None, TQ, LANES), lambda bi, i: (bi, i, A_GATE // LANES)),
            pl.BlockSpec((None, TQ, 512), lambda bi, i: (bi, i, A_AZ // 512)),
            pl.BlockSpec((None, n16, rows), lambda bi, i: (i, 0, 0)),
            pl.BlockSpec(d_tiles.shape, lambda bi, i: (0, 0, 0)),
            pl.BlockSpec((s, LANES), lambda bi, i: (0, 0)),
            pl.BlockSpec((nb, n16), lambda bi, i: (0, 0)),
        ],
        out_specs=pl.BlockSpec((None, TQ, 512), lambda bi, i: (bi, i, 0)),
        out_shape=jax.ShapeDtypeStruct((b, s, 512), F32),
        scratch_shapes=[pltpu.VMEM((rows, 2 * LANES), MXU_DTYPE),
                        pltpu.VMEM((1, rows), F32),
                        pltpu.VMEM((LANES, rows), F32),
                        pltpu.VMEM((LANES, rows), F32),
                        pltpu.VMEM((LANES, n16), MXU_DTYPE),
                        pltpu.VMEM((NSA_GROUPS, nq, LANES, TQ), MXU_DTYPE),
                        pltpu.VMEM((NSA_GROUPS, nq, LANES, TQ), MXU_DTYPE)],
        compiler_params=_params(2),
        name="nsa",
    )(u, cmp_kv, cmp_kv, u, u, u, u, u, u, bias_c, d_tiles, e_t, ov)


def _dsa_kernel(q_ref, kv_ref, qi_ref, ki_ref, wi_ref, z_ref, d_ref, tri_ref, o_ref,
                qs_s, qis_s, sc_s, hi_s, lo_s, m_s, acc_s, seen_s, vt_s, *, k_top, ch):
    i = pl.program_id(1)

    @pl.when(i == 0)
    def _():
        _store_value_tiles(kv_ref, vt_s, (False,))

    qs_s[...] = _mx(_stack_heads(q_ref, [False] * DSA_HEADS))
    qis_s[...] = _mx(_stack_heads(qi_ref, [False] * IDX_HEADS))
    wi_t = wi_ref[...].T
    w_rows = [jnp.broadcast_to(wi_t[h:h + 1, :], (TQ, TQ)) for h in range(IDX_HEADS)]
    key_idx = lax.broadcasted_iota(jnp.int32, (TQ, TQ), 0)
    q_idx = lax.broadcasted_iota(jnp.int32, (TQ, TQ), 1)
    n_chunks = (i + ch) // ch
    pairs = ch // 2

    def score_chunk(c, last):
        koff = pl.multiple_of(c * (ch * TQ), ch * TQ)
        r = jnp.maximum(_dot_nt(_mx(ki_ref[pl.ds(koff, ch * TQ), :]), qis_s[...]), 0.0)
        keys = []
        for t in range(ch):
            jt = c * ch + t
            rt = r[t * TQ:(t + 1) * TQ]
            sc = w_rows[0] * rt[:, 0:TQ]
            for h in range(1, IDX_HEADS):
                sc = sc + w_rows[h] * rt[:, h * TQ:(h + 1) * TQ]
            if last:
                sc = jnp.where((jt == i) & (key_idx > q_idx), NEG, sc)
            bits = pltpu.bitcast(sc, jnp.int32)
            key = bits ^ (lax.shift_right_arithmetic(bits, 31) & jnp.int32(0x7FFFFFFF))
            key = jnp.where(sc == 0.0, 0, key)
            if last:
                key = jnp.where(jt > i, INT_MIN, key)
            keys.append(key)
            sc_s[jt] = key
        for pr in range(pairs):
            a, b = keys[2 * pr], keys[2 * pr + 1]
            hi_s[c * pairs + pr] = (a & jnp.int32(-65536)) | lax.shift_right_logical(b, 16)
            lo_s[c * pairs + pr] = (lax.shift_left((a & 0xFFFF) ^ 0x8000, 16)
                                    | ((b & 0xFFFF) ^ 0x8000))

    def earlier_chunk(c, carry):
        score_chunk(c, False)
        return carry

    lax.fori_loop(0, n_chunks - 1, earlier_chunk, 0)
    score_chunk(n_chunks - 1, True)

    def halves(words):
        return pltpu.bitcast(words, jnp.int16)

    def both_halves(v):
        w = lax.shift_left(v, 16) | (v & 0xFFFF)
        return halves(jnp.broadcast_to(w, (TQ, TQ)))

    def count16(words_s, pred):
        def body(c, acc):
            for pr in range(pairs):
                acc = acc + jnp.where(pred(halves(words_s[c * pairs + pr])), jnp.int16(1), jnp.int16(0))
            return acc
        acc = lax.fori_loop(0, n_chunks, body, jnp.zeros((2 * TQ, TQ), jnp.int16))
        acc = acc[:TQ] + acc[TQ:]
        return jnp.sum(acc.astype(jnp.int32), axis=0, keepdims=True)

    def bisect16(words_s, target):
        c0 = count16(words_s, lambda x: x >= jnp.int16(0))
        ok0 = c0 >= target

        def step(it, carry):
            v, n_gt = carry
            cand = v | lax.shift_left(jnp.int32(1), 14 - it)
            cand16 = both_halves(cand)
            c = count16(words_s, lambda x: x >= cand16)
            ok = c >= target
            return jnp.where(ok, cand, v), jnp.where(ok, n_gt, c)

        return lax.fori_loop(0, 15, step, (jnp.where(ok0, 0, -32768), jnp.where(ok0, 0, c0)))

    top, above = bisect16(hi_s, k_top)
    top16 = both_halves(top)

    def keep_ties(c, carry):
        for pr in range(pairs):
            p_ = c * pairs + pr
            tied = halves(hi_s[p_]) == top16
            lo_s[p_] = pltpu.bitcast(jnp.where(tied, halves(lo_s[p_]), jnp.int16(-32768)), jnp.int32)
        return carry

    lax.fori_loop(0, n_chunks, keep_ties, 0)
    low, above_low = bisect16(lo_s, k_top - above)
    thr = lax.shift_left(top, 16) | ((low ^ 0x8000) & 0xFFFF)
    thr = jnp.broadcast_to(thr, (TQ, TQ))
    need = (k_top - above - above_low).astype(F32)

    _softmax_reset(m_s, acc_s)
    seen_s[...] = jnp.zeros((TQ, TQ), F32)

    def att_chunk(c, near):
        tk = ch * TQ
        koff = pl.multiple_of(c * tk, tk)
        masks = []
        for t in range(ch):
            key = sc_s[c * ch + t]
            eq = key == thr
            prefix = _dot(tri_ref[...], _mx(jnp.where(eq, 1.0, 0.0)))
            seen = seen_s[...]
            take = jnp.where(eq, jnp.where(seen + prefix <= need, 1.0, 0.0), 0.0)
            seen_s[...] = seen + prefix[TQ - 1:TQ, :]
            masks.append(jnp.where(key > thr, 0.0, (take - 1.0) * (-NEG)))
        mb = masks[0] if ch == 1 else jnp.concatenate(masks, axis=0)
        mb = jnp.concatenate([mb] * DSA_HEADS, axis=1)
        kvm = _mx(kv_ref[pl.ds(koff, tk), :])

        def scores(c0, c1):
            s = _dot_nt(kvm, qs_s[c0:c1, :]) + mb[:, c0:c1]
            if near:
                s = s + _near_bias(d_ref, lambda j: jnp.clip(j - (i - 2), 0, 3), c * ch, ch, c0, c1)
            return s

        _softmax_step(scores, [_value_tiles(vt_s, 0, c * ch, ch)], m_s, acc_s, DSA_HEADS * TQ)

    first_near = jnp.maximum(i - 1, 0) // ch

    def far_body(c, carry):
        att_chunk(c, False)
        return carry

    def near_body(c, carry):
        att_chunk(c, True)
        return carry

    lax.fori_loop(0, first_near, far_body, 0)
    lax.fori_loop(first_near, n_chunks, near_body, 0)

    o_t = acc_s[...]
    o_t = o_t / o_t[0:1, :]
    for m in range(DSA_HEADS // 2):
        pair = [o_t[HEAD_DIM:, hh * TQ:(hh + 1) * TQ] for hh in (2 * m, 2 * m + 1)]
        slab = jnp.concatenate(pair, axis=0).T
        o_ref[:, m * LANES:(m + 1) * LANES] = slab * jax.nn.silu(z_ref[:, m * LANES:(m + 1) * LANES])


def _dsa(u, d_tiles, tri, k_top):
    b, s, _ = u.shape
    nq = s // TQ
    assert nq % 2 == 0, "the packed threshold search pairs key tiles"
    rows = DSA_HEADS * TQ
    irows = IDX_HEADS * TQ
    seq = lambda col: pl.BlockSpec((None, s, LANES), lambda bi, i, col=col: (bi, 0, col // LANES))
    return pl.pallas_call(
        functools.partial(_dsa_kernel, k_top=k_top, ch=_tiles_per_chunk(nq)),
        grid=(b, nq),
        in_specs=[
            pl.BlockSpec((None, TQ, DSA_HEADS * HEAD_DIM), lambda bi, i: (bi, i, A_DSA_Q // (DSA_HEADS * HEAD_DIM))),
            seq(A_DSA_KV),
            pl.BlockSpec((None, TQ, IDX_HEADS * IDX_DIM), lambda bi, i: (bi, i, A_IDX_Q // (IDX_HEADS * IDX_DIM))),
            seq(A_IDX_K),
            pl.BlockSpec((None, TQ, LANES), lambda bi, i: (bi, i, A_IDX_W // LANES)),
            pl.BlockSpec((None, TQ, 512), lambda bi, i: (bi, i, A_BZ // 512)),
            pl.BlockSpec(d_tiles.shape, lambda bi, i: (0, 0, 0)),
            pl.BlockSpec((TQ, TQ), lambda bi, i: (0, 0)),
        ],
        out_specs=pl.BlockSpec((None, TQ, 512), lambda bi, i: (bi, i, 0)),
        out_shape=jax.ShapeDtypeStruct((b, s, 512), F32),
        scratch_shapes=[pltpu.VMEM((rows, LANES), MXU_DTYPE),
                        pltpu.VMEM((irows, LANES), MXU_DTYPE),
                        pltpu.VMEM((nq, TQ, TQ), jnp.int32),
                        pltpu.VMEM((nq // 2, TQ, TQ), jnp.int32),
                        pltpu.VMEM((nq // 2, TQ, TQ), jnp.int32),
                        pltpu.VMEM((1, rows), F32),
                        pltpu.VMEM((LANES, rows), F32),
                        pltpu.VMEM((TQ, TQ), F32),
                        pltpu.VMEM((1, nq, LANES, TQ), MXU_DTYPE)],
        compiler_params=_params(2),
        name="dsa",
    )(u, u, u, u, u, u, d_tiles, tri)


def _causal_conv(x, xe_s, w_ref, b_ref):
    t = x.shape[0]
    xe_s[8:8 + t, :] = x
    y = b_ref[...] + w_ref[CONV_WIDTH - 1:CONV_WIDTH, :] * x
    for k in range(1, CONV_WIDTH):
        y = y + w_ref[CONV_WIDTH - 1 - k:CONV_WIDTH - k, :] * xe_s[8 - k:8 - k + t, :]
    xe_s[0:8, :] = x[t - 8:t, :]
    return y


def _rglru_kernel(x_ref, z_ref, cw_ref, cb_ref, wa_ref, ba_ref, wx_ref, bx_ref, lam_ref, o_ref,
                  xe_s, h_s):
    t, c = x_ref.shape

    @pl.when(pl.program_id(1) == 0)
    def _():
        xe_s[0:8, :] = jnp.zeros((8, c), F32)
        h_s[...] = jnp.zeros(h_s.shape, F32)

    xc = _causal_conv(x_ref[...], xe_s, cw_ref, cb_ref)
    xcm = _mx(xc)
    r = jax.nn.sigmoid(_dot(xcm, wa_ref[...]) + ba_ref[...])
    ig = jax.nn.sigmoid(_dot(xcm, wx_ref[...]) + bx_ref[...])
    nl = -lam_ref[...]
    softplus = jnp.maximum(nl, 0.0) + jnp.log1p(jnp.exp(-jnp.abs(nl)))
    log_a = (-LRU_C * r) * softplus
    a = jnp.exp(log_a)
    bb = jnp.sqrt(-jnp.tanh(log_a) * (a * a + 1.0)) * (ig * xc)
    row = lax.broadcasted_iota(jnp.int32, (t, c), 0) & (SUBLANES - 1)
    d = 1
    while d < SUBLANES:
        live = row >= d
        a_sh = jnp.where(live, pltpu.roll(a, d, 0), 1.0)
        b_sh = jnp.where(live, pltpu.roll(bb, d, 0), 0.0)
        bb = a * b_sh + bb
        a = a * a_sh
        d *= 2
    carry = h_s[0:1, :]
    groups = []
    for g in range(t // SUBLANES):
        rows = slice(g * SUBLANES, (g + 1) * SUBLANES)
        groups.append(a[rows] * carry + bb[rows])
        carry = groups[-1][SUBLANES - 1:SUBLANES, :]
    h = jnp.concatenate(groups, axis=0)
    h_s[0:1, :] = carry
    o_ref[...] = h * jax.nn.silu(z_ref[...])


def _rglru(u, conv_w, conv_b, wa_bd, ba, wx_bd, bx, lam, t=256):
    b, s, _ = u.shape
    c = LRU_WIDTH
    whole = lambda bi, i: (0, 0)
    return pl.pallas_call(
        _rglru_kernel,
        grid=(b, s // t),
        in_specs=[pl.BlockSpec((None, t, c), lambda bi, i: (bi, i, R_CX // c)),
                  pl.BlockSpec((None, t, c), lambda bi, i: (bi, i, R_CZ // c)),
                  pl.BlockSpec((CONV_WIDTH, c), whole), pl.BlockSpec((1, c), whole),
                  pl.BlockSpec((c, c), whole), pl.BlockSpec((1, c), whole),
                  pl.BlockSpec((c, c), whole), pl.BlockSpec((1, c), whole),
                  pl.BlockSpec((1, c), whole)],
        out_specs=pl.BlockSpec((None, t, c), lambda bi, i: (bi, i, 0)),
        out_shape=jax.ShapeDtypeStruct((b, s, c), F32),
        scratch_shapes=[pltpu.VMEM((t + 8, c), F32), pltpu.VMEM((8, c), F32)],
        compiler_params=_params(2),
        name="rglru",
    )(u, u, conv_w, conv_b.reshape(1, c), wa_bd, ba.reshape(1, c), wx_bd, bx.reshape(1, c),
      lam.reshape(1, c))


def _mlstm_kernel(qk_ref, v_ref, g_ref, og_ref, z_ref, cw_ref, cb_ref, gb_ref, o_ref,
                  xe_s, c_s, m_s):
    t = qk_ref.shape[0]
    L = MLSTM_CHUNK
    H = MLSTM_HEADS
    D = MLSTM_DIM

    @pl.when(pl.program_id(1) == 0)
    def _():
        xe_s[0:8, :] = jnp.zeros((8, xe_s.shape[1]), F32)
        c_s[...] = jnp.zeros(c_s.shape, F32)
        m_s[...] = jnp.zeros(m_s.shape, F32)

    qk = jax.nn.silu(_causal_conv(qk_ref[...], xe_s, cw_ref, cb_ref))

    lane = lax.broadcasted_iota(jnp.int32, (t, LANES), 1)
    row_in = lax.broadcasted_iota(jnp.int32, (t, LANES), 0) & (L - 1)
    gs = g_ref[...] + gb_ref[...]
    log_sig = jnp.minimum(gs, 0.0) - jnp.log1p(jnp.exp(-jnp.abs(gs)))
    pre = jnp.where(lane < H, gs, log_sig)
    cum = pre
    d = 1
    while d < L:
        cum = cum + jnp.where(row_in >= d, pltpu.roll(cum, d, 0), 0.0)
        d *= 2
    comb = jnp.where(lane < H, pre, cum)
    comb_t = comb.T

    causal_t = (lax.broadcasted_iota(jnp.int32, (L, L), 0) <= lax.broadcasted_iota(jnp.int32, (L, L), 1))
    pad_rows = jnp.zeros((7, L), F32)
    v_ts = [v_ref[:, h * D:(h + 1) * D].T for h in range(H)]
    states = [c_s[h] for h in range(H)]
    m_prevs = [m_s[h:h + 1, 0:1] for h in range(H)]
    outs = [[] for _ in range(H)]
    for c in range(t // L):
        r = slice(c * L, (c + 1) * L)
        for h in range(H):
            v_t, state, m_prev = v_ts[h], states[h], m_prevs[h]
            q_ = _mx(qk[r, h * D:(h + 1) * D])
            k_ = _mx(qk[r, (H + h) * D:(H + h + 1) * D] * (D ** -0.5))
            li_row = comb_t[h:h + 1, r]
            b_row = comb_t[H + h:H + h + 1, r]
            lb_col = comb[r, h:h + 1] - comb[r, H + h:H + h + 1]
            b_last = b_row[:, L - 1:L]

            dmat_t = jnp.where(causal_t, b_row + lb_col, -jnp.inf)
            decay_row = b_last - b_row + li_row
            m_new = jnp.maximum(b_last + m_prev, jnp.max(decay_row, axis=-1, keepdims=True))
            inter = b_row + m_prev
            m_t = jnp.maximum(inter, jnp.max(dmat_t, axis=0, keepdims=True))
            w_t = _dot_nt(k_, q_) * jnp.exp(dmat_t - m_t)
            prev = jnp.exp(inter - m_t)

            read = _dot_nt(_mx(state), q_)
            num_t = prev * read[:D] + _dot(_mx(v_t[:, r]), _mx(w_t))
            den = prev * read[D:D + 1] + jnp.sum(w_t, axis=0, keepdims=True)
            outs[h].append(num_t / jnp.maximum(jnp.abs(den), jnp.exp(-m_t)))

            wk = jnp.exp(decay_row - m_new)
            inc = jnp.concatenate([v_t[:, r] * wk, wk, pad_rows], axis=0)
            states[h] = jnp.exp(b_last + m_prev - m_new) * state + _dot(_mx(inc), k_)
            m_prevs[h] = m_new

    for h in range(H):
        c_s[h] = states[h]
        m_s[h:h + 1, :] = jnp.broadcast_to(m_prevs[h], (1, LANES))
        out = jnp.concatenate(outs[h], axis=1).T
        cols = slice(h * D, (h + 1) * D)
        o_ref[:, cols] = jax.nn.sigmoid(og_ref[:, cols]) * out * jax.nn.silu(z_ref[:, cols])


def _mlstm(u, conv_w, conv_b, gate_bias, t=128):
    b, s, _ = u.shape
    w = MLSTM_WIDTH
    whole = lambda bi, i: (0, 0)
    return pl.pallas_call(
        _mlstm_kernel,
        grid=(b, s // t),
        in_specs=[pl.BlockSpec((None, t, 2 * w), lambda bi, i: (bi, i, R_QK // (2 * w))),
                  pl.BlockSpec((None, t, w), lambda bi, i: (bi, i, R_V // w)),
                  pl.BlockSpec((None, t, LANES), lambda bi, i: (bi, i, R_GATE // LANES)),
                  pl.BlockSpec((None, t, w), lambda bi, i: (bi, i, R_O // w)),
                  pl.BlockSpec((None, t, w), lambda bi, i: (bi, i, R_Z // w)),
                  pl.BlockSpec((CONV_WIDTH, 2 * w), whole), pl.BlockSpec((1, 2 * w), whole),
                  pl.BlockSpec((1, LANES), whole)],
        out_specs=pl.BlockSpec((None, t, w), lambda bi, i: (bi, i, 0)),
        out_shape=jax.ShapeDtypeStruct((b, s, w), F32),
        scratch_shapes=[pltpu.VMEM((t + 8, 2 * w), F32),
                        pltpu.VMEM((MLSTM_HEADS, MLSTM_DIM + 8, MLSTM_DIM), F32),
                        pltpu.VMEM((8, LANES), F32)],
        compiler_params=_params(2),
        name="mlstm",
    )(u, u, u, u, u, conv_w, conv_b.reshape(1, 2 * w), gate_bias)


def _attn_weights(w_in):
    d = w_in.shape[0]
    widths = (512, 768, 24, 512, 512, 64, 64, 256, 64, 4, 512)
    offs = np.concatenate([[0], np.cumsum(widths)])
    a_q, a_kv, a_g, a_z, b_q, b_k, b_v, b_qi, b_ki, b_wi, b_z = [
        w_in[:, offs[k]:offs[k + 1]] for k in range(len(widths))]
    scale = HEAD_DIM ** -0.5 * LOG2E
    zeros = lambda n: jnp.zeros((d, n), w_in.dtype)
    cols = [
        a_q * scale, b_q * scale, a_z, b_z, b_qi, a_kv,
        a_g, zeros(LANES - 24),
        b_k, b_v,
        b_ki, zeros(LANES - IDX_DIM),
        b_wi * (IDX_DIM ** -0.5 * IDX_HEADS ** -0.5), zeros(LANES - IDX_HEADS),
    ]
    w = jnp.concatenate(cols, axis=1)
    assert w.shape[1] == A_TOTAL
    return _mx(w)


def _rec_weights(w_in):
    d = w_in.shape[0]
    widths = (512, 512, 512, 512, 512, 4, 4, 512, 512)
    offs = np.concatenate([[0], np.cumsum(widths)])
    c_x, c_z, d_q, d_k, d_v, d_i, d_f, d_o, d_z = [w_in[:, offs[k]:offs[k + 1]] for k in range(len(widths))]
    w = jnp.concatenate([c_x, c_z, d_q, d_k, d_v, d_o, d_z, d_i, d_f,
                         jnp.zeros((d, R_TOTAL - R_GATE - 2 * MLSTM_HEADS), w_in.dtype)], axis=1)
    assert w.shape[1] == R_TOTAL
    return _mx(w)


def _block_diag(w):
    g, n, _ = w.shape
    eye = jnp.eye(g, dtype=w.dtype)
    return (eye[:, None, :, None] * w[:, :, None, :]).reshape(g * n, g * n)


def _attention_layer(x2d, b, s, norm_g, w_in, cmp_pos_k, cmp_w1_k, cmp_w2_k, cmp_pos_v, cmp_w1_v, cmp_w2_v,
                     t5_table):
    nq = s // TQ
    n16 = s // CMP_STRIDE
    nb = s // SEL_BLOCK
    u = _norm_proj(x2d, norm_g, _attn_weights(w_in)).reshape(b, s, A_TOTAL)

    pos = jnp.stack([cmp_pos_k, cmp_pos_v])
    pos = jnp.concatenate([pos] * NSA_GROUPS, axis=-1).reshape(2, CMP_BLOCK, 1, LANES)
    w1 = jnp.stack([cmp_w1_k, cmp_w1_v]).reshape(2, CMP_BLOCK, HEAD_DIM, CMP_HIDDEN)
    z1 = jnp.zeros_like(w1)
    w1 = _mx(jnp.stack([jnp.concatenate([w1, z1], axis=2), jnp.concatenate([z1, w1], axis=2)], axis=1))
    w2 = jnp.stack([cmp_w2_k, cmp_w2_v])
    zpad = jnp.zeros_like(w2)
    w2 = _mx(jnp.stack([jnp.concatenate([w2, zpad], -1), jnp.concatenate([zpad, w2], -1)], axis=1))
    cmp_kv = _compress(u, pos, w1, w2)

    tbl = t5_table.astype(F32)
    tbl_a, tbl_b = tbl[:, :NSA_HEADS], tbl[:, NSA_HEADS:]
    kj = np.arange(TQ)[:, None]
    qi = np.arange(TQ)[None, :]
    assert nb <= LANES
    e_t = jnp.asarray((np.arange(s)[:, None] // SEL_BLOCK) == np.arange(LANES)[None, :], MXU_DTYPE)
    ci = np.arange(n16)[None, :]
    sj = np.arange(nb)[:, None]
    ov = jnp.asarray((ci * CMP_STRIDE < (sj + 1) * SEL_BLOCK) & (ci * CMP_STRIDE + CMP_BLOCK > sj * SEL_BLOCK)
                     & (ci < n16 - 1), MXU_DTYPE)
    tri = jnp.asarray(qi <= kj, MXU_DTYPE)

    ya = _nsa(u, cmp_kv, _cmp_bias(tbl_a, nq, n16), _near_tiles(tbl_a), e_t, ov, min(N_SEL, nb))
    yb = _dsa(u, _near_tiles(tbl_b), tri, min(DSA_TOPK_MAX, s // 4))
    return ya.reshape(b * s, -1), yb.reshape(b * s, -1)


def _recurrent_layer(x2d, b, s, norm_g, w_in, conv_c_w, conv_c_b, wa, ba, wx, bx, lam,
                     conv_d_w, conv_d_b, b_i, b_f):
    u = _norm_proj(x2d, norm_g, _rec_weights(w_in)).reshape(b, s, R_TOTAL)
    yc = _rglru(u, conv_c_w, conv_c_b, _mx(_block_diag(wa)), ba, _mx(_block_diag(wx)), bx, lam)
    gate_bias = jnp.concatenate([b_i, b_f, jnp.zeros((LANES - 2 * MLSTM_HEADS,), F32)]).reshape(1, LANES)
    yd = _mlstm(u, conv_d_w, conv_d_b, gate_bias)
    return yc.reshape(b * s, -1), yd.reshape(b * s, -1)


def kernel(x, p, norm_g, final_g, ple_w, ple_gate_w, t5_table, attn_w_in, attn_w_out, cmp_pos_k, cmp_w1_k, cmp_w2_k, cmp_pos_v, cmp_w1_v, cmp_w2_v, rec_w_in, rec_w_out, lru_conv_w, lru_conv_b, lru_wa, lru_ba, lru_wx, lru_bx, lru_lambda, mlstm_conv_w, mlstm_conv_b, mlstm_b_i, mlstm_b_f):
    b, s, d = x.shape
    depth = p.shape[0]
    x2d = x.reshape(b * s, d)
    for i in range(depth):
        j = i // 2
        if i % 2 == 0:
            ya, yb = _attention_layer(x2d, b, s, norm_g[i], attn_w_in[j], cmp_pos_k[j], cmp_w1_k[j], cmp_w2_k[j],
                                      cmp_pos_v[j], cmp_w1_v[j], cmp_w2_v[j], t5_table)
            w_out = attn_w_out[j]
        else:
            ya, yb = _recurrent_layer(x2d, b, s, norm_g[i], rec_w_in[j], lru_conv_w[j], lru_conv_b[j],
                                      lru_wa[j], lru_ba[j], lru_wx[j], lru_bx[j], lru_lambda[j],
                                      mlstm_conv_w[j], mlstm_conv_b[j], mlstm_b_i[j], mlstm_b_f[j])
            w_out = rec_w_out[j]
        x2d = _out_proj(x2d, ya, yb, p[i].reshape(b * s, -1), w_out, ple_w[i], ple_gate_w[i], final_g,
                        final=(i == depth - 1))
    return x2d.reshape(b, s, d)
```

```python
import functools
import math

import numpy as np
import jax
import jax.numpy as jnp
from jax import lax
from jax.experimental import pallas as pl
from jax.experimental.pallas import tpu as pltpu

F32 = jnp.float32
MXU_DTYPE = jnp.bfloat16

HEAD_DIM = 64
NSA_HEADS = 8
NSA_GROUPS = 2
NSA_HPG = NSA_HEADS // NSA_GROUPS
CMP_BLOCK = 32
CMP_STRIDE = 16
CMP_HIDDEN = 256
SEL_BLOCK = 64
N_SEL = 8
WINDOW = 512
FORCE_BONUS = 1e4
DSA_HEADS = 8
IDX_HEADS = 4
IDX_DIM = 64
DSA_TOPK_MAX = 256
N_BUCKETS = 32
T5_MAX_DIST = 128
LRU_WIDTH = 512
CONV_WIDTH = 4
LRU_C = 8.0
MLSTM_HEADS = 4
MLSTM_DIM = 128
MLSTM_WIDTH = MLSTM_HEADS * MLSTM_DIM
MLSTM_CHUNK = 64
RMS_EPS = 1e-6
NEG = -1e30

LANES = 128
SUBLANES = 8
COL_BLOCK = 256
TQ = 128
INT_MIN = -2 ** 31
LOG2E = math.log2(math.e)
VMEM_LIMIT = 48 * 1024 * 1024

A_NSA_Q = 0
A_DSA_Q = 512
A_AZ = 1024
A_BZ = 1536
A_IDX_Q = 2048
A_KV = 2304
A_GATE = 3072
A_DSA_KV = 3200
A_IDX_K = 3328
A_IDX_W = 3456
A_TOTAL = 3584

R_CX = 0
R_CZ = 512
R_QK = 1024
R_V = 2048
R_O = 2560
R_Z = 3072
R_GATE = 3584
R_TOTAL = 3840


def _dot(a, b):
    return jnp.dot(a, b, preferred_element_type=F32)


def _dot_nt(a, b):
    return lax.dot_general(a, b, (((1,), (1,)), ((), ())), preferred_element_type=F32)


def _mx(a):
    return a.astype(MXU_DTYPE)


def _params(n_grid):
    return pltpu.CompilerParams(dimension_semantics=("arbitrary",) * n_grid,
                                vmem_limit_bytes=VMEM_LIMIT)


def _norm_proj_kernel(x_ref, g_ref, w_ref, o_ref):
    x = x_ref[...]
    ms = jnp.mean(x * x, axis=-1, keepdims=True)
    y = x * lax.rsqrt(ms + RMS_EPS) * g_ref[...]
    o_ref[...] = _dot(_mx(y), w_ref[...])


def _norm_proj(x2d, g, w, tm=256):
    m, d = x2d.shape
    n = w.shape[1]
    return pl.pallas_call(
        _norm_proj_kernel,
        grid=(m // tm,),
        in_specs=[pl.BlockSpec((tm, d), lambda i: (i, 0)),
                  pl.BlockSpec((1, d), lambda i: (0, 0)),
                  pl.BlockSpec((d, n), lambda i: (0, 0))],
        out_specs=pl.BlockSpec((tm, n), lambda i: (i, 0)),
        out_shape=jax.ShapeDtypeStruct((m, n), F32),
        compiler_params=_params(1),
        name="norm_proj",
    )(x2d, g.reshape(1, d), w)


def _out_proj_kernel(x_ref, ya_ref, yb_ref, p_ref, wo_ref, pw_ref, gw_ref, fg_ref, o_ref, *, final):
    half = ya_ref.shape[-1]
    y = _dot(_mx(ya_ref[...]), wo_ref[:half, :]) + _dot(_mx(yb_ref[...]), wo_ref[half:, :])
    x1 = x_ref[...] + y
    gate = jax.nn.sigmoid(_dot(_mx(x1), gw_ref[...]))
    x2 = x1 + _dot(_mx(p_ref[...]), pw_ref[...]) * gate
    if final:
        ms = jnp.mean(x2 * x2, axis=-1, keepdims=True)
        x2 = x2 * lax.rsqrt(ms + RMS_EPS) * fg_ref[...]
    o_ref[...] = x2


def _out_proj(x2d, ya, yb, p_all, layer, w_out, ple_w, gate_w, final_g, final, tm=512):
    m, d = x2d.shape
    half = ya.shape[1]
    pd = p_all.shape[2]
    row = lambda i: (i, 0)
    whole = lambda i: (0, 0)
    return pl.pallas_call(
        functools.partial(_out_proj_kernel, final=final),
        grid=(m // tm,),
        in_specs=[pl.BlockSpec((tm, d), row), pl.BlockSpec((tm, half), row),
                  pl.BlockSpec((tm, half), row), pl.BlockSpec((None, tm, pd), lambda i: (layer, i, 0)),
                  pl.BlockSpec((2 * half, d), whole), pl.BlockSpec((pd, d), whole),
                  pl.BlockSpec((d, d), whole), pl.BlockSpec((1, d), whole)],
        out_specs=pl.BlockSpec((tm, d), row),
        out_shape=jax.ShapeDtypeStruct((m, d), F32),
        compiler_params=_params(1),
        name="out_proj",
    )(x2d, ya, yb, p_all, _mx(w_out), _mx(ple_w), _mx(gate_w), final_g.reshape(1, d))


def _bucket_np(n):
    n = np.asarray(n)
    max_exact = N_BUCKETS // 2
    nf = np.maximum(n, 1).astype(np.float32)
    large = max_exact + (np.log(nf / np.float32(max_exact)) / np.float32(math.log(T5_MAX_DIST / max_exact))
                         * np.float32(N_BUCKETS - max_exact)).astype(np.int32)
    large = np.minimum(large, N_BUCKETS - 1)
    return np.where(n < max_exact, n, large)


def _bias_index(dist):
    dist = np.asarray(dist)
    return np.where(dist >= 0, _bucket_np(np.maximum(dist, 0)), N_BUCKETS).astype(np.int32)


def _lookup(ext, idx):
    a, b = idx.shape
    rows = jnp.arange(ext.shape[0], dtype=jnp.int32)[:, None]
    onehot = (rows == jnp.asarray(idx.reshape(1, -1), jnp.int32)).astype(F32)
    out = jnp.dot(ext.T, onehot, precision=lax.Precision.HIGHEST)
    return out.reshape(ext.shape[1], a, b).transpose(1, 0, 2).reshape(a, -1)


def _near_tiles(tbl):
    h = tbl.shape[1]
    far = tbl[N_BUCKETS - 1]
    ext = jnp.concatenate([tbl, jnp.full((1, h), NEG, F32)], axis=0)
    kj = np.arange(TQ)[:, None]
    qi = np.arange(TQ)[None, :]
    far = jnp.repeat(far, TQ)[None, :]
    d0 = (_lookup(ext, _bias_index(qi - kj)) - far) * LOG2E
    d1 = (_lookup(ext, _bias_index(TQ + qi - kj)) - far) * LOG2E
    edge = jnp.asarray(np.tile(np.where(kj > qi, 0.0, NEG), (1, h)), F32)
    return jnp.stack([jnp.zeros_like(d0), d1, d0, jnp.full_like(d0, NEG), edge])


def _cmp_bias(tbl, nq, n16):
    h = tbl.shape[1]
    ext = jnp.concatenate([tbl, jnp.full((1, h), NEG, F32)], axis=0)
    off = (TQ // CMP_STRIDE) * (nq - 1)
    cc = np.arange(n16 + off)[:, None] - off
    qi = np.arange(TQ)[None, :]
    return _lookup(ext, _bias_index(qi - CMP_STRIDE * cc - (CMP_BLOCK - 1))) * LOG2E


def _compress_kernel(x_ref, pos_ref, w1_ref, w2_ref, o_ref):
    n16 = o_ref.shape[0]
    acc = None
    for g in range(NSA_GROUPS):
        pa = pb = None
        for l in range(CMP_STRIDE):
            x = x_ref[pl.ds(l, n16, stride=CMP_STRIDE), :]
            a = _dot(_mx(x + pos_ref[l]), w1_ref[g, l])
            c = _dot(_mx(x + pos_ref[CMP_STRIDE + l]), w1_ref[g, CMP_STRIDE + l])
            pa = a if pa is None else pa + a
            pb = c if pb is None else pb + c
        pre = pa + pltpu.roll(pb, n16 - 1, 0)
        t = _dot(_mx(jax.nn.silu(pre)), w2_ref[g])
        acc = t if acc is None else acc + t
    o_ref[...] = acc


def _compress(u, pos, w1, w2):
    b, s, _ = u.shape
    n16 = s // CMP_STRIDE
    return pl.pallas_call(
        _compress_kernel,
        grid=(2, b),
        in_specs=[pl.BlockSpec((None, s, LANES), lambda k, i: (i, 0, A_KV // LANES + k)),
                  pl.BlockSpec((None, CMP_BLOCK, 1, LANES), lambda k, i: (k, 0, 0, 0)),
                  pl.BlockSpec((None, NSA_GROUPS, CMP_BLOCK, LANES, CMP_HIDDEN), lambda k, i: (k, 0, 0, 0, 0)),
                  pl.BlockSpec((None, NSA_GROUPS, CMP_HIDDEN, LANES), lambda k, i: (k, 0, 0, 0))],
        out_specs=pl.BlockSpec((None, None, n16, LANES), lambda k, i: (k, i, 0, 0)),
        out_shape=jax.ShapeDtypeStruct((2, b, n16, LANES), F32),
        compiler_params=_params(2),
        name="compress",
    )(u, pos, w1, w2)


def _softmax_reset(m_s, acc_s):
    m_s[...] = jnp.full(m_s.shape, -jnp.inf, F32)
    acc_s[...] = jnp.zeros(acc_s.shape, F32)


def _softmax_step(score_fn, values, m_s, acc_s, col_block):
    r = m_s.shape[1]
    s_next = score_fn(0, col_block)
    for c0 in range(0, r, col_block):
        cols = slice(c0, c0 + col_block)
        s = s_next
        if c0 + col_block < r:
            s_next = score_fn(c0 + col_block, c0 + 2 * col_block)
        m_prev = m_s[:, cols]
        m_new = jnp.maximum(m_prev, jnp.max(s, axis=0, keepdims=True))
        alpha = jnp.exp2(m_prev - m_new)
        p = _mx(jnp.exp2(s - m_new))
        v = values[c0 * len(values) // r]
        acc_s[:, cols] = alpha * acc_s[:, cols] + _dot(v, p)
        m_s[:, cols] = m_new


def _store_value_tiles(v_ref, vt_s, ones_hi):
    row = lax.broadcasted_iota(jnp.int32, (LANES, TQ), 0)

    def body(j, carry):
        vt = v_ref[pl.ds(pl.multiple_of(j * TQ, TQ), TQ), :].T
        for k, hi in enumerate(ones_hi):
            vt_s[k, j] = _mx(jnp.where((row >= HEAD_DIM) == hi, 1.0, vt))
        return carry

    lax.fori_loop(0, vt_s.shape[1], body, 0)


def _value_tiles(vt_s, k, j0, ntile):
    tiles = [vt_s[k, j0 + t] for t in range(ntile)]
    return tiles[0] if ntile == 1 else jnp.concatenate(tiles, axis=1)


def _near_bias(d_ref, idx_of_tile, j0, ntile, c0, c1):
    tiles = [d_ref[idx_of_tile(j0 + t), :, c0:c1] for t in range(ntile)]
    return tiles[0] if ntile == 1 else jnp.concatenate(tiles, axis=0)


def _stack_heads(q_ref, hi):
    lane = lax.broadcasted_iota(jnp.int32, (TQ, LANES), 1)
    out = []
    for h, up in enumerate(hi):
        v = q_ref[:, (h // 2) * LANES:(h // 2 + 1) * LANES]
        if (h % 2 == 1) != up:
            v = pltpu.roll(v, HEAD_DIM, 1)
        out.append(jnp.where((lane >= HEAD_DIM) == up, v, 0.0))
    return jnp.concatenate(out, axis=0)


def _tiles_per_chunk(nq):
    for ch in (4, 2):
        if nq % ch == 0:
            return ch
    return 1


def _nsa_kernel(q_ref, kc_ref, vc_ref, ks_ref, vs_ref, kw_ref, vw_ref, g_ref, z_ref,
                bc_ref, d_ref, et_ref, ov_ref, o_ref,
                qa_s, m_s, acc_s, osel_s, vct_s, vst_s, vwt_s, *, n_pick, ch, n_win_tiles):
    i = pl.program_id(1)
    rows = NSA_HEADS * TQ
    nb = ov_ref.shape[0]
    q0 = i * TQ
    half = LANES // NSA_GROUPS

    @pl.when(i == 0)
    def _():
        vct_s[...] = _mx(vc_ref[...].T)
        _store_value_tiles(vs_ref, vst_s, (True, False))
        _store_value_tiles(vw_ref, vwt_s, (True, False))

    qs = _mx(_stack_heads(q_ref, [h >= NSA_HPG for h in range(NSA_HEADS)]))
    qa_s[:, :LANES] = qs

    s = _dot_nt(_mx(kc_ref[...]), qs) + bc_ref[...]
    p = jnp.exp2(s - jnp.max(s, axis=0, keepdims=True))
    t_lane = q0 + (lax.broadcasted_iota(jnp.int32, (1, rows), 1) & (TQ - 1))
    any_valid = jnp.where(t_lane >= CMP_BLOCK - 1, 1.0, 0.0)
    p = p * (any_valid / jnp.sum(p, axis=0, keepdims=True))
    oc_t = _dot(vct_s[...], _mx(p))

    blk = lax.broadcasted_iota(jnp.int32, (nb, TQ), 0)
    t = q0 + lax.broadcasted_iota(jnp.int32, (nb, TQ), 1)
    cur = lax.shift_right_logical(t, int(math.log2(SEL_BLOCK)))
    forced = jnp.where(blk == 0, 1.0, jnp.where(blk == cur, 1.0, jnp.where(blk == cur - 1, 1.0, 0.0)))
    admissible = blk * SEL_BLOCK <= t
    ov = ov_ref[...]
    for g in range(NSA_GROUPS):
        c0 = g * NSA_HPG * TQ
        ps = p[:, c0:c0 + TQ]
        for h in range(1, NSA_HPG):
            ps = ps + p[:, c0 + h * TQ:c0 + (h + 1) * TQ]
        hi = _mx(ps)
        lo = _mx(ps - hi.astype(F32))
        imp_t = _dot(ov, hi) + _dot(ov, lo)
        score = jnp.where(admissible, imp_t + FORCE_BONUS * forced, NEG)
        sel = jnp.zeros((nb, TQ), F32)
        for _ in range(n_pick):
            best = jnp.max(score, axis=0, keepdims=True)
            first = jnp.min(jnp.where(score == best, blk, nb), axis=0, keepdims=True)
            hit = blk == first
            sel = jnp.where(hit, 1.0, sel)
            score = jnp.where(hit, -jnp.inf, score)
        if nb < LANES:
            sel = jnp.concatenate([sel, jnp.ones((LANES - nb, TQ), F32)], axis=0)
        block_mask = _mx((sel.T - 1.0) * (-NEG))
        for h in range(NSA_HPG):
            r0 = (g * NSA_HPG + h) * TQ
            qa_s[r0:r0 + TQ, LANES:] = block_mask

    def values(vt_s, j0, ntile):
        return [_value_tiles(vt_s, 0, j0, ntile), _value_tiles(vt_s, 1, j0, ntile)]

    _softmax_reset(m_s, acc_s)

    def sel_chunk(c, near):
        tk = ch * TQ
        koff = pl.multiple_of(c * tk, tk)
        k_aug = jnp.concatenate([_mx(ks_ref[pl.ds(koff, tk), :]), et_ref[pl.ds(koff, tk), :]], axis=1)

        def scores(c0, c1):
            s = _dot_nt(k_aug, qa_s[c0:c1, :])
            if near:
                s = s + _near_bias(d_ref, lambda j: jnp.clip(j - (i - 2), 0, 3), c * ch, ch, c0, c1)
            return s

        _softmax_step(scores, values(vst_s, c * ch, ch), m_s, acc_s, COL_BLOCK)

    first_near = jnp.maximum(i - 1, 0) // ch

    def far_body(c, carry):
        sel_chunk(c, False)
        return carry

    def near_body(c, carry):
        sel_chunk(c, True)
        return carry

    lax.fori_loop(0, first_near, far_body, 0)
    lax.fori_loop(first_near, i // ch + 1, near_body, 0)
    osel_s[...] = acc_s[...]

    _softmax_reset(m_s, acc_s)
    n_win = WINDOW // TQ
    jw = jnp.maximum(i - (n_win_tiles - 1), 0)
    koff = pl.multiple_of(jw * TQ, TQ)
    tk = n_win_tiles * TQ

    def win_idx(j):
        r = j - (i - n_win)
        return jnp.where(r == 0, 4, jnp.clip(r - (n_win - 2), 0, 3))

    kw = _mx(kw_ref[pl.ds(koff, tk), :])
    _softmax_step(lambda c0, c1: (_dot_nt(kw, qa_s[c0:c1, :LANES])
                                  + _near_bias(d_ref, win_idx, jw, n_win_tiles, c0, c1)),
                  values(vwt_s, jw, n_win_tiles), m_s, acc_s, COL_BLOCK)
    ow_t = acc_s[...]
    os_t = osel_s[...]

    gate_t = jax.nn.sigmoid(g_ref[...]).T
    for m in range(NSA_HEADS // 2):
        parts = []
        for hh in (2 * m, 2 * m + 1):
            g = hh // NSA_HPG
            r = slice(g * half, (g + 1) * half)
            d = (1 - g) * half
            c = slice(hh * TQ, (hh + 1) * TQ)
            parts.append(gate_t[3 * hh:3 * hh + 1, :] * oc_t[r, c]
                         + gate_t[3 * hh + 1:3 * hh + 2, :] * (os_t[r, c] / os_t[d:d + 1, c])
                         + gate_t[3 * hh + 2:3 * hh + 3, :] * (ow_t[r, c] / ow_t[d:d + 1, c]))
        slab = jnp.concatenate(parts, axis=0).T
        o_ref[:, m * LANES:(m + 1) * LANES] = slab * jax.nn.silu(z_ref[:, m * LANES:(m + 1) * LANES])


def _nsa(u, cmp_kv, bias_c, d_tiles, e_t, ov, n_pick):
    b, s, _ = u.shape
    nq = s // TQ
    n16 = cmp_kv.shape[2]
    nb = ov.shape[0]
    rows = NSA_HEADS * TQ
    n_win_tiles = min(WINDOW // TQ + 1, nq)
    seq = lambda col: pl.BlockSpec((None, s, LANES), lambda bi, i, col=col: (bi, 0, col // LANES))
    return pl.pallas_call(
        functools.partial(_nsa_kernel, n_pick=n_pick, ch=_tiles_per_chunk(nq), n_win_tiles=n_win_tiles),
        grid=(b, nq),
        in_specs=[
            pl.BlockSpec((None, TQ, NSA_HEADS * HEAD_DIM), lambda bi, i: (bi, i, A_NSA_Q // (NSA_HEADS * HEAD_DIM))),
            pl.BlockSpec((None, None, n16, LANES), lambda bi, i: (0, bi, 0, 0)),
            pl.BlockSpec((None, None, n16, LANES), lambda bi, i: (1, bi, 0, 0)),
            seq(A_KV + 2 * LANES), seq(A_KV + 3 * LANES), seq(A_KV + 4 * LANES), seq(A_KV + 5 * LANES),
            pl.BlockSpec((None, TQ, LANES), lambda bi, i: (bi, i, A_GATE // LANES)),
            pl.BlockSpec((None, TQ, 512), lambda bi, i: (bi, i, A_AZ // 512)),
            pl.BlockSpec((pl.Element(n16), pl.Element(rows)),
                         lambda bi, i: ((TQ // CMP_STRIDE) * (nq - 1 - i), 0)),
            pl.BlockSpec(d_tiles.shape, lambda bi, i: (0, 0, 0)),
            pl.BlockSpec((s, LANES), lambda bi, i: (0, 0)),
            pl.BlockSpec((nb, n16), lambda bi, i: (0, 0)),
        ],
        out_specs=pl.BlockSpec((None, TQ, 512), lambda bi, i: (bi, i, 0)),
        out_shape=jax.ShapeDtypeStruct((b, s, 512), F32),
        scratch_shapes=[pltpu.VMEM((rows, 2 * LANES), MXU_DTYPE),
                        pltpu.VMEM((1, rows), F32),
                        pltpu.VMEM((LANES, rows), F32),
                        pltpu.VMEM((LANES, rows), F32),
                        pltpu.VMEM((LANES, n16), MXU_DTYPE),
                        pltpu.VMEM((NSA_GROUPS, nq, LANES, TQ), MXU_DTYPE),
                        pltpu.VMEM((NSA_GROUPS, nq, LANES, TQ), MXU_DTYPE)],
        compiler_params=_params(2),
        name="nsa",
    )(u, cmp_kv, cmp_kv, u, u, u, u, u, u, bias_c, d_tiles, e_t, ov)


def _dsa_kernel(q_ref, kv_ref, qi_ref, ki_ref, wi_ref, z_ref, d_ref, tri_ref, o_ref,
                qs_s, qis_s, sc_s, hi_s, lo_s, m_s, acc_s, seen_s, vt_s, *, k_top, ch):
    i = pl.program_id(1)

    @pl.when(i == 0)
    def _():
        _store_value_tiles(kv_ref, vt_s, (False,))

    qs_s[...] = _mx(_stack_heads(q_ref, [False] * DSA_HEADS))
    qis_s[...] = _mx(_stack_heads(qi_ref, [False] * IDX_HEADS))
    wi_t = wi_ref[...].T
    w_rows = [jnp.broadcast_to(wi_t[h:h + 1, :], (TQ, TQ)) for h in range(IDX_HEADS)]
    key_idx = lax.broadcasted_iota(jnp.int32, (TQ, TQ), 0)
    q_idx = lax.broadcasted_iota(jnp.int32, (TQ, TQ), 1)
    n_chunks = (i + ch) // ch
    pairs = ch // 2

    def score_chunk(c, last):
        koff = pl.multiple_of(c * (ch * TQ), ch * TQ)
        r = jnp.maximum(_dot_nt(_mx(ki_ref[pl.ds(koff, ch * TQ), :]), qis_s[...]), 0.0)
        keys = []
        for t in range(ch):
            jt = c * ch + t
            rt = r[t * TQ:(t + 1) * TQ]
            sc = w_rows[0] * rt[:, 0:TQ]
            for h in range(1, IDX_HEADS):
                sc = sc + w_rows[h] * rt[:, h * TQ:(h + 1) * TQ]
            if last:
                sc = jnp.where((jt == i) & (key_idx > q_idx), NEG, sc)
            bits = pltpu.bitcast(sc, jnp.int32)
            key = bits ^ (lax.shift_right_arithmetic(bits, 31) & jnp.int32(0x7FFFFFFF))
            key = jnp.where(sc == 0.0, 0, key)
            if last:
                key = jnp.where(jt > i, INT_MIN, key)
            keys.append(key)
            sc_s[jt] = key
        for pr in range(pairs):
            a, b = keys[2 * pr], keys[2 * pr + 1]
            hi_s[c * pairs + pr] = (a & jnp.int32(-65536)) | lax.shift_right_logical(b, 16)
            lo_s[c * pairs + pr] = (lax.shift_left((a & 0xFFFF) ^ 0x8000, 16)
                                    | ((b & 0xFFFF) ^ 0x8000))

    def earlier_chunk(c, carry):
        score_chunk(c, False)
        return carry

    lax.fori_loop(0, n_chunks - 1, earlier_chunk, 0)
    score_chunk(n_chunks - 1, True)

    def halves(words):
        return pltpu.bitcast(words, jnp.int16)

    def both_halves(v):
        w = lax.shift_left(v, 16) | (v & 0xFFFF)
        return halves(jnp.broadcast_to(w, (TQ, TQ)))

    def count16(words_s, pred):
        def body(c, acc):
            for pr in range(pairs):
                acc = acc + jnp.where(pred(halves(words_s[c * pairs + pr])), jnp.int16(1), jnp.int16(0))
            return acc
        acc = lax.fori_loop(0, n_chunks, body, jnp.zeros((2 * TQ, TQ), jnp.int16))
        acc = acc[:TQ] + acc[TQ:]
        return jnp.sum(acc.astype(jnp.int32), axis=0, keepdims=True)

    def bisect16(words_s, target):
        c0 = count16(words_s, lambda x: x >= jnp.int16(0))
        ok0 = c0 >= target

        def step(it, carry):
            v, n_gt = carry
            cand = v | lax.shift_left(jnp.int32(1), 14 - it)
            cand16 = both_halves(cand)
            c = count16(words_s, lambda x: x >= cand16)
            ok = c >= target
            return jnp.where(ok, cand, v), jnp.where(ok, n_gt, c)

        return lax.fori_loop(0, 15, step, (jnp.where(ok0, 0, -32768), jnp.where(ok0, 0, c0)))

    top, above = bisect16(hi_s, k_top)
    top16 = both_halves(top)

    def keep_ties(c, carry):
        for pr in range(pairs):
            p_ = c * pairs + pr
            tied = halves(hi_s[p_]) == top16
            lo_s[p_] = pltpu.bitcast(jnp.where(tied, halves(lo_s[p_]), jnp.int16(-32768)), jnp.int32)
        return carry

    lax.fori_loop(0, n_chunks, keep_ties, 0)
    low, above_low = bisect16(lo_s, k_top - above)
    thr = lax.shift_left(top, 16) | ((low ^ 0x8000) & 0xFFFF)
    thr = jnp.broadcast_to(thr, (TQ, TQ))
    need = (k_top - above - above_low).astype(F32)

    _softmax_reset(m_s, acc_s)
    seen_s[...] = jnp.zeros((TQ, TQ), F32)

    def att_chunk(c, near):
        tk = ch * TQ
        koff = pl.multiple_of(c * tk, tk)
        masks = []
        for t in range(ch):
            key = sc_s[c * ch + t]
            eq = key == thr
            prefix = _dot(tri_ref[...], _mx(jnp.where(eq, 1.0, 0.0)))
            seen = seen_s[...]
            take = jnp.where(eq, jnp.where(seen + prefix <= need, 1.0, 0.0), 0.0)
            seen_s[...] = seen + prefix[TQ - 1:TQ, :]
            masks.append(jnp.where(key > thr, 0.0, (take - 1.0) * (-NEG)))
        mb = masks[0] if ch == 1 else jnp.concatenate(masks, axis=0)
        mb = jnp.concatenate([mb] * DSA_HEADS, axis=1)
        kvm = _mx(kv_ref[pl.ds(koff, tk), :])

        def scores(c0, c1):
            s = _dot_nt(kvm, qs_s[c0:c1, :]) + mb[:, c0:c1]
            if near:
                s = s + _near_bias(d_ref, lambda j: jnp.clip(j - (i - 2), 0, 3), c * ch, ch, c0, c1)
            return s

        _softmax_step(scores, [_value_tiles(vt_s, 0, c * ch, ch)], m_s, acc_s, DSA_HEADS * TQ)

    first_near = jnp.maximum(i - 1, 0) // ch

    def far_body(c, carry):
        att_chunk(c, False)
        return carry

    def near_body(c, carry):
        att_chunk(c, True)
        return carry

    lax.fori_loop(0, first_near, far_body, 0)
    lax.fori_loop(first_near, n_chunks, near_body, 0)

    o_t = acc_s[...]
    o_t = o_t / o_t[0:1, :]
    for m in range(DSA_HEADS // 2):
        pair = [o_t[HEAD_DIM:, hh * TQ:(hh + 1) * TQ] for hh in (2 * m, 2 * m + 1)]
        slab = jnp.concatenate(pair, axis=0).T
        o_ref[:, m * LANES:(m + 1) * LANES] = slab * jax.nn.silu(z_ref[:, m * LANES:(m + 1) * LANES])


def _dsa(u, d_tiles, tri, k_top):
    b, s, _ = u.shape
    nq = s // TQ
    assert nq % 2 == 0, "the packed threshold search pairs key tiles"
    rows = DSA_HEADS * TQ
    irows = IDX_HEADS * TQ
    seq = lambda col: pl.BlockSpec((None, s, LANES), lambda bi, i, col=col: (bi, 0, col // LANES))
    return pl.pallas_call(
        functools.partial(_dsa_kernel, k_top=k_top, ch=_tiles_per_chunk(nq)),
        grid=(b, nq),
        in_specs=[
            pl.BlockSpec((None, TQ, DSA_HEADS * HEAD_DIM), lambda bi, i: (bi, i, A_DSA_Q // (DSA_HEADS * HEAD_DIM))),
            seq(A_DSA_KV),
            pl.BlockSpec((None, TQ, IDX_HEADS * IDX_DIM), lambda bi, i: (bi, i, A_IDX_Q // (IDX_HEADS * IDX_DIM))),
            seq(A_IDX_K),
            pl.BlockSpec((None, TQ, LANES), lambda bi, i: (bi, i, A_IDX_W // LANES)),
            pl.BlockSpec((None, TQ, 512), lambda bi, i: (bi, i, A_BZ // 512)),
            pl.BlockSpec(d_tiles.shape, lambda bi, i: (0, 0, 0)),
            pl.BlockSpec((TQ, TQ), lambda bi, i: (0, 0)),
        ],
        out_specs=pl.BlockSpec((None, TQ, 512), lambda bi, i: (bi, i, 0)),
        out_shape=jax.ShapeDtypeStruct((b, s, 512), F32),
        scratch_shapes=[pltpu.VMEM((rows, LANES), MXU_DTYPE),
                        pltpu.VMEM((irows, LANES), MXU_DTYPE),
                        pltpu.VMEM((nq, TQ, TQ), jnp.int32),
                        pltpu.VMEM((nq // 2, TQ, TQ), jnp.int32),
                        pltpu.VMEM((nq // 2, TQ, TQ), jnp.int32),
                        pltpu.VMEM((1, rows), F32),
                        pltpu.VMEM((LANES, rows), F32),
                        pltpu.VMEM((TQ, TQ), F32),
                        pltpu.VMEM((1, nq, LANES, TQ), MXU_DTYPE)],
        compiler_params=_params(2),
        name="dsa",
    )(u, u, u, u, u, u, d_tiles, tri)


def _causal_conv(x, xe_s, w_ref, b_ref):
    t = x.shape[0]
    xe_s[8:8 + t, :] = x
    y = b_ref[...] + w_ref[CONV_WIDTH - 1:CONV_WIDTH, :] * x
    for k in range(1, CONV_WIDTH):
        y = y + w_ref[CONV_WIDTH - 1 - k:CONV_WIDTH - k, :] * xe_s[8 - k:8 - k + t, :]
    xe_s[0:8, :] = x[t - 8:t, :]
    return y


def _rglru_kernel(x_ref, z_ref, cw_ref, cb_ref, wa_ref, ba_ref, wx_ref, bx_ref, lam_ref, o_ref,
                  xe_s, h_s):
    t, c = x_ref.shape

    @pl.when(pl.program_id(1) == 0)
    def _():
        xe_s[0:8, :] = jnp.zeros((8, c), F32)
        h_s[...] = jnp.zeros(h_s.shape, F32)

    xc = _causal_conv(x_ref[...], xe_s, cw_ref, cb_ref)
    xcm = _mx(xc)
    r = jax.nn.sigmoid(_dot(xcm, wa_ref[...]) + ba_ref[...])
    ig = jax.nn.sigmoid(_dot(xcm, wx_ref[...]) + bx_ref[...])
    nl = -lam_ref[...]
    softplus = jnp.maximum(nl, 0.0) + jnp.log1p(jnp.exp(-jnp.abs(nl)))
    log_a = (-LRU_C * r) * softplus
    a = jnp.exp(log_a)
    bb = jnp.sqrt(-jnp.tanh(log_a) * (a * a + 1.0)) * (ig * xc)
    row = lax.broadcasted_iota(jnp.int32, (t, c), 0) & (SUBLANES - 1)
    d = 1
    while d < SUBLANES:
        live = row >= d
        a_sh = jnp.where(live, pltpu.roll(a, d, 0), 1.0)
        b_sh = jnp.where(live, pltpu.roll(bb, d, 0), 0.0)
        bb = a * b_sh + bb
        a = a * a_sh
        d *= 2
    carry = h_s[0:1, :]
    groups = []
    for g in range(t // SUBLANES):
        rows = slice(g * SUBLANES, (g + 1) * SUBLANES)
        groups.append(a[rows] * carry + bb[rows])
        carry = groups[-1][SUBLANES - 1:SUBLANES, :]
    h = jnp.concatenate(groups, axis=0)
    h_s[0:1, :] = carry
    o_ref[...] = h * jax.nn.silu(z_ref[...])


def _rglru(u, conv_w, conv_b, wa_bd, ba, wx_bd, bx, lam, t=256):
    b, s, _ = u.shape
    c = LRU_WIDTH
    whole = lambda bi, i: (0, 0)
    return pl.pallas_call(
        _rglru_kernel,
        grid=(b, s // t),
        in_specs=[pl.BlockSpec((None, t, c), lambda bi, i: (bi, i, R_CX // c)),
                  pl.BlockSpec((None, t, c), lambda bi, i: (bi, i, R_CZ // c)),
                  pl.BlockSpec((CONV_WIDTH, c), whole), pl.BlockSpec((1, c), whole),
                  pl.BlockSpec((c, c), whole), pl.BlockSpec((1, c), whole),
                  pl.BlockSpec((c, c), whole), pl.BlockSpec((1, c), whole),
                  pl.BlockSpec((1, c), whole)],
        out_specs=pl.BlockSpec((None, t, c), lambda bi, i: (bi, i, 0)),
        out_shape=jax.ShapeDtypeStruct((b, s, c), F32),
        scratch_shapes=[pltpu.VMEM((t + 8, c), F32), pltpu.VMEM((8, c), F32)],
        compiler_params=_params(2),
        name="rglru",
    )(u, u, conv_w, conv_b.reshape(1, c), wa_bd, ba.reshape(1, c), wx_bd, bx.reshape(1, c),
      lam.reshape(1, c))


def _mlstm_kernel(qk_ref, v_ref, g_ref, og_ref, z_ref, cw_ref, cb_ref, gb_ref, o_ref,
                  xe_s, c_s, m_s):
    t = qk_ref.shape[0]
    L = MLSTM_CHUNK
    H = MLSTM_HEADS
    D = MLSTM_DIM

    @pl.when(pl.program_id(1) == 0)
    def _():
        xe_s[0:8, :] = jnp.zeros((8, xe_s.shape[1]), F32)
        c_s[...] = jnp.zeros(c_s.shape, F32)
        m_s[...] = jnp.zeros(m_s.shape, F32)

    qk = jax.nn.silu(_causal_conv(qk_ref[...], xe_s, cw_ref, cb_ref))

    lane = lax.broadcasted_iota(jnp.int32, (t, LANES), 1)
    row_in = lax.broadcasted_iota(jnp.int32, (t, LANES), 0) & (L - 1)
    gs = g_ref[...] + gb_ref[...]
    log_sig = jnp.minimum(gs, 0.0) - jnp.log1p(jnp.exp(-jnp.abs(gs)))
    pre = jnp.where(lane < H, gs, log_sig)
    cum = pre
    d = 1
    while d < L:
        cum = cum + jnp.where(row_in >= d, pltpu.roll(cum, d, 0), 0.0)
        d *= 2
    comb = jnp.where(lane < H, pre, cum)
    comb_t = comb.T

    causal_t = (lax.broadcasted_iota(jnp.int32, (L, L), 0) <= lax.broadcasted_iota(jnp.int32, (L, L), 1))
    pad_rows = jnp.zeros((7, L), F32)
    v_ts = [v_ref[:, h * D:(h + 1) * D].T for h in range(H)]
    states = [c_s[h] for h in range(H)]
    m_prevs = [m_s[h:h + 1, 0:1] for h in range(H)]
    outs = [[] for _ in range(H)]
    for c in range(t // L):
        r = slice(c * L, (c + 1) * L)
        for h in range(H):
            v_t, state, m_prev = v_ts[h], states[h], m_prevs[h]
            q_ = _mx(qk[r, h * D:(h + 1) * D])
            k_ = _mx(qk[r, (H + h) * D:(H + h + 1) * D] * (D ** -0.5))
            li_row = comb_t[h:h + 1, r]
            b_row = comb_t[H + h:H + h + 1, r]
            lb_col = comb[r, h:h + 1] - comb[r, H + h:H + h + 1]
            b_last = b_row[:, L - 1:L]

            dmat_t = jnp.where(causal_t, b_row + lb_col, -jnp.inf)
            decay_row = b_last - b_row + li_row
            m_new = jnp.maximum(b_last + m_prev, jnp.max(decay_row, axis=-1, keepdims=True))
            inter = b_row + m_prev
            m_t = jnp.maximum(inter, jnp.max(dmat_t, axis=0, keepdims=True))
            w_t = _dot_nt(k_, q_) * jnp.exp(dmat_t - m_t)
            prev = jnp.exp(inter - m_t)

            read = _dot_nt(_mx(state), q_)
            num_t = prev * read[:D] + _dot(_mx(v_t[:, r]), _mx(w_t))
            den = prev * read[D:D + 1] + jnp.sum(w_t, axis=0, keepdims=True)
            outs[h].append(num_t / jnp.maximum(jnp.abs(den), jnp.exp(-m_t)))

            wk = jnp.exp(decay_row - m_new)
            inc = jnp.concatenate([v_t[:, r] * wk, wk, pad_rows], axis=0)
            states[h] = jnp.exp(b_last + m_prev - m_new) * state + _dot(_mx(inc), k_)
            m_prevs[h] = m_new

    for h in range(H):
        c_s[h] = states[h]
        m_s[h:h + 1, :] = jnp.broadcast_to(m_prevs[h], (1, LANES))
        out = jnp.concatenate(outs[h], axis=1).T
        cols = slice(h * D, (h + 1) * D)
        o_ref[:, cols] = jax.nn.sigmoid(og_ref[:, cols]) * out * jax.nn.silu(z_ref[:, cols])


def _mlstm(u, conv_w, conv_b, gate_bias, t=128):
    b, s, _ = u.shape
    w = MLSTM_WIDTH
    whole = lambda bi, i: (0, 0)
    return pl.pallas_call(
        _mlstm_kernel,
        grid=(b, s // t),
        in_specs=[pl.BlockSpec((None, t, 2 * w), lambda bi, i: (bi, i, R_QK // (2 * w))),
                  pl.BlockSpec((None, t, w), lambda bi, i: (bi, i, R_V // w)),
                  pl.BlockSpec((None, t, LANES), lambda bi, i: (bi, i, R_GATE // LANES)),
                  pl.BlockSpec((None, t, w), lambda bi, i: (bi, i, R_O // w)),
                  pl.BlockSpec((None, t, w), lambda bi, i: (bi, i, R_Z // w)),
                  pl.BlockSpec((CONV_WIDTH, 2 * w), whole), pl.BlockSpec((1, 2 * w), whole),
                  pl.BlockSpec((1, LANES), whole)],
        out_specs=pl.BlockSpec((None, t, w), lambda bi, i: (bi, i, 0)),
        out_shape=jax.ShapeDtypeStruct((b, s, w), F32),
        scratch_shapes=[pltpu.VMEM((t + 8, 2 * w), F32),
                        pltpu.VMEM((MLSTM_HEADS, MLSTM_DIM + 8, MLSTM_DIM), F32),
                        pltpu.VMEM((8, LANES), F32)],
        compiler_params=_params(2),
        name="mlstm",
    )(u, u, u, u, u, conv_w, conv_b.reshape(1, 2 * w), gate_bias)


def _attn_weights(w_in):
    d = w_in.shape[0]
    widths = (512, 768, 24, 512, 512, 64, 64, 256, 64, 4, 512)
    offs = np.concatenate([[0], np.cumsum(widths)])
    a_q, a_kv, a_g, a_z, b_q, b_k, b_v, b_qi, b_ki, b_wi, b_z = [
        w_in[:, offs[k]:offs[k + 1]] for k in range(len(widths))]
    scale = HEAD_DIM ** -0.5 * LOG2E
    zeros = lambda n: jnp.zeros((d, n), w_in.dtype)
    cols = [
        a_q * scale, b_q * scale, a_z, b_z, b_qi, a_kv,
        a_g, zeros(LANES - 24),
        b_k, b_v,
        b_ki, zeros(LANES - IDX_DIM),
        b_wi * (IDX_DIM ** -0.5 * IDX_HEADS ** -0.5), zeros(LANES - IDX_HEADS),
    ]
    w = jnp.concatenate(cols, axis=1)
    assert w.shape[1] == A_TOTAL
    return _mx(w)


def _rec_weights(w_in):
    d = w_in.shape[0]
    widths = (512, 512, 512, 512, 512, 4, 4, 512, 512)
    offs = np.concatenate([[0], np.cumsum(widths)])
    c_x, c_z, d_q, d_k, d_v, d_i, d_f, d_o, d_z = [w_in[:, offs[k]:offs[k + 1]] for k in range(len(widths))]
    w = jnp.concatenate([c_x, c_z, d_q, d_k, d_v, d_o, d_z, d_i, d_f,
                         jnp.zeros((d, R_TOTAL - R_GATE - 2 * MLSTM_HEADS), w_in.dtype)], axis=1)
    assert w.shape[1] == R_TOTAL
    return _mx(w)


def _block_diag(w):
    g, n, _ = w.shape
    eye = jnp.eye(g, dtype=w.dtype)
    return (eye[:, None, :, None] * w[:, :, None, :]).reshape(g * n, g * n)


def _attention_layer(x2d, b, s, norm_g, w_in, cmp_pos_k, cmp_w1_k, cmp_w2_k, cmp_pos_v, cmp_w1_v, cmp_w2_v,
                     t5_table):
    nq = s // TQ
    n16 = s // CMP_STRIDE
    nb = s // SEL_BLOCK
    u = _norm_proj(x2d, norm_g, _attn_weights(w_in)).reshape(b, s, A_TOTAL)

    pos = jnp.stack([cmp_pos_k, cmp_pos_v])
    pos = jnp.concatenate([pos] * NSA_GROUPS, axis=-1).reshape(2, CMP_BLOCK, 1, LANES)
    w1 = jnp.stack([cmp_w1_k, cmp_w1_v]).reshape(2, CMP_BLOCK, HEAD_DIM, CMP_HIDDEN)
    z1 = jnp.zeros_like(w1)
    w1 = _mx(jnp.stack([jnp.concatenate([w1, z1], axis=2), jnp.concatenate([z1, w1], axis=2)], axis=1))
    w2 = jnp.stack([cmp_w2_k, cmp_w2_v])
    zpad = jnp.zeros_like(w2)
    w2 = _mx(jnp.stack([jnp.concatenate([w2, zpad], -1), jnp.concatenate([zpad, w2], -1)], axis=1))
    cmp_kv = _compress(u, pos, w1, w2)

    tbl = t5_table.astype(F32)
    tbl_a, tbl_b = tbl[:, :NSA_HEADS], tbl[:, NSA_HEADS:]
    kj = np.arange(TQ)[:, None]
    qi = np.arange(TQ)[None, :]
    assert nb <= LANES
    e_t = jnp.asarray((np.arange(s)[:, None] // SEL_BLOCK) == np.arange(LANES)[None, :], MXU_DTYPE)
    ci = np.arange(n16)[None, :]
    sj = np.arange(nb)[:, None]
    ov = jnp.asarray((ci * CMP_STRIDE < (sj + 1) * SEL_BLOCK) & (ci * CMP_STRIDE + CMP_BLOCK > sj * SEL_BLOCK)
                     & (ci < n16 - 1), MXU_DTYPE)
    tri = jnp.asarray(qi <= kj, MXU_DTYPE)

    ya = _nsa(u, cmp_kv, _cmp_bias(tbl_a, nq, n16), _near_tiles(tbl_a), e_t, ov, min(N_SEL, nb))
    yb = _dsa(u, _near_tiles(tbl_b), tri, min(DSA_TOPK_MAX, s // 4))
    return ya.reshape(b * s, -1), yb.reshape(b * s, -1)


def _recurrent_layer(x2d, b, s, norm_g, w_in, conv_c_w, conv_c_b, wa, ba, wx, bx, lam,
                     conv_d_w, conv_d_b, b_i, b_f):
    u = _norm_proj(x2d, norm_g, _rec_weights(w_in)).reshape(b, s, R_TOTAL)
    yc = _rglru(u, conv_c_w, conv_c_b, _mx(_block_diag(wa)), ba, _mx(_block_diag(wx)), bx, lam)
    gate_bias = jnp.concatenate([b_i, b_f, jnp.zeros((LANES - 2 * MLSTM_HEADS,), F32)]).reshape(1, LANES)
    yd = _mlstm(u, conv_d_w, conv_d_b, gate_bias)
    return yc.reshape(b * s, -1), yd.reshape(b * s, -1)


def kernel(x, p, norm_g, final_g, ple_w, ple_gate_w, t5_table, attn_w_in, attn_w_out, cmp_pos_k, cmp_w1_k, cmp_w2_k, cmp_pos_v, cmp_w1_v, cmp_w2_v, rec_w_in, rec_w_out, lru_conv_w, lru_conv_b, lru_wa, lru_ba, lru_wx, lru_bx, lru_lambda, mlstm_conv_w, mlstm_conv_b, mlstm_b_i, mlstm_b_f):
    b, s, d = x.shape
    depth = p.shape[0]
    x2d = x.reshape(b * s, d)
    for i in range(depth):
        j = i // 2
        if i % 2 == 0:
            ya, yb = _attention_layer(x2d, b, s, norm_g[i], attn_w_in[j], cmp_pos_k[j], cmp_w1_k[j], cmp_w2_k[j],
                                      cmp_pos_v[j], cmp_w1_v[j], cmp_w2_v[j], t5_table)
            w_out = attn_w_out[j]
        else:
            ya, yb = _recurrent_layer(x2d, b, s, norm_g[i], rec_w_in[j], lru_conv_w[j], lru_conv_b[j],
                                      lru_wa[j], lru_ba[j], lru_wx[j], lru_bx[j], lru_lambda[j],
                                      mlstm_conv_w[j], mlstm_conv_b[j], mlstm_b_i[j], mlstm_b_f[j])
            w_out = rec_w_out[j]
        x2d = _out_proj(x2d, ya, yb, p.reshape(depth, b * s, -1), i, w_out, ple_w[i], ple_gate_w[i], final_g,
                        final=(i == depth - 1))
    return x2d.reshape(b, s, d)
```

```python
import functools
import math

import numpy as np
import jax
import jax.numpy as jnp
from jax import lax
from jax.experimental import pallas as pl
from jax.experimental.pallas import tpu as pltpu

F32 = jnp.float32
MXU_DTYPE = jnp.bfloat16

HEAD_DIM = 64
NSA_HEADS = 8
NSA_GROUPS = 2
NSA_HPG = NSA_HEADS // NSA_GROUPS
CMP_BLOCK = 32
CMP_STRIDE = 16
CMP_HIDDEN = 256
SEL_BLOCK = 64
N_SEL = 8
WINDOW = 512
FORCE_BONUS = 1e4
DSA_HEADS = 8
IDX_HEADS = 4
IDX_DIM = 64
DSA_TOPK_MAX = 256
N_BUCKETS = 32
T5_MAX_DIST = 128
LRU_WIDTH = 512
CONV_WIDTH = 4
LRU_C = 8.0
MLSTM_HEADS = 4
MLSTM_DIM = 128
MLSTM_WIDTH = MLSTM_HEADS * MLSTM_DIM
MLSTM_CHUNK = 64
RMS_EPS = 1e-6
NEG = -1e30

LANES = 128
SUBLANES = 8
COL_BLOCK = 256
TQ = 128
INT_MIN = -2 ** 31
LOG2E = math.log2(math.e)
VMEM_LIMIT = 48 * 1024 * 1024

A_NSA_Q = 0
A_DSA_Q = 512
A_AZ = 1024
A_BZ = 1536
A_IDX_Q = 2048
A_KV = 2304
A_GATE = 3072
A_DSA_KV = 3200
A_IDX_K = 3328
A_IDX_W = 3456
A_TOTAL = 3584

R_CX = 0
R_CZ = 512
R_QK = 1024
R_V = 2048
R_O = 2560
R_Z = 3072
R_GATE = 3584
R_TOTAL = 3840


def _dot(a, b):
    return jnp.dot(a, b, preferred_element_type=F32)


def _dot_nt(a, b):
    return lax.dot_general(a, b, (((1,), (1,)), ((), ())), preferred_element_type=F32)


def _mx(a):
    return a.astype(MXU_DTYPE)


def _params(n_grid):
    return pltpu.CompilerParams(dimension_semantics=("arbitrary",) * n_grid,
                                vmem_limit_bytes=VMEM_LIMIT)


def _norm_proj_kernel(x_ref, g_ref, w_ref, o_ref):
    x = x_ref[...]
    ms = jnp.mean(x * x, axis=-1, keepdims=True)
    y = x * lax.rsqrt(ms + RMS_EPS) * g_ref[...]
    o_ref[...] = _dot(_mx(y), w_ref[...])


def _norm_proj(x2d, g, w, tm=256):
    m, d = x2d.shape
    n = w.shape[1]
    return pl.pallas_call(
        _norm_proj_kernel,
        grid=(m // tm,),
        in_specs=[pl.BlockSpec((tm, d), lambda i: (i, 0)),
                  pl.BlockSpec((1, d), lambda i: (0, 0)),
                  pl.BlockSpec((d, n), lambda i: (0, 0))],
        out_specs=pl.BlockSpec((tm, n), lambda i: (i, 0)),
        out_shape=jax.ShapeDtypeStruct((m, n), F32),
        compiler_params=_params(1),
        name="norm_proj",
    )(x2d, g.reshape(1, d), w)


def _out_proj_kernel(x_ref, ya_ref, yb_ref, p_ref, wo_ref, pw_ref, gw_ref, fg_ref, o_ref, *, final):
    half = ya_ref.shape[-1]
    y = _dot(_mx(ya_ref[...]), wo_ref[:half, :]) + _dot(_mx(yb_ref[...]), wo_ref[half:, :])
    x1 = x_ref[...] + y
    gate = jax.nn.sigmoid(_dot(_mx(x1), gw_ref[...]))
    x2 = x1 + _dot(_mx(p_ref[...]), pw_ref[...]) * gate
    if final:
        ms = jnp.mean(x2 * x2, axis=-1, keepdims=True)
        x2 = x2 * lax.rsqrt(ms + RMS_EPS) * fg_ref[...]
    o_ref[...] = x2


def _out_proj(x2d, ya, yb, p_all, layer, w_out, ple_w, gate_w, final_g, final, tm=512):
    m, d = x2d.shape
    half = ya.shape[1]
    pd = p_all.shape[2]
    row = lambda i: (i, 0)
    whole = lambda i: (0, 0)
    return pl.pallas_call(
        functools.partial(_out_proj_kernel, final=final),
        grid=(m // tm,),
        in_specs=[pl.BlockSpec((tm, d), row), pl.BlockSpec((tm, half), row),
                  pl.BlockSpec((tm, half), row), pl.BlockSpec((None, tm, pd), lambda i: (layer, i, 0)),
                  pl.BlockSpec((2 * half, d), whole), pl.BlockSpec((pd, d), whole),
                  pl.BlockSpec((d, d), whole), pl.BlockSpec((1, d), whole)],
        out_specs=pl.BlockSpec((tm, d), row),
        out_shape=jax.ShapeDtypeStruct((m, d), F32),
        compiler_params=_params(1),
        name="out_proj",
    )(x2d, ya, yb, p_all, _mx(w_out), _mx(ple_w), _mx(gate_w), final_g.reshape(1, d))


def _bucket_np(n):
    n = np.asarray(n)
    max_exact = N_BUCKETS // 2
    nf = np.maximum(n, 1).astype(np.float32)
    large = max_exact + (np.log(nf / np.float32(max_exact)) / np.float32(math.log(T5_MAX_DIST / max_exact))
                         * np.float32(N_BUCKETS - max_exact)).astype(np.int32)
    large = np.minimum(large, N_BUCKETS - 1)
    return np.where(n < max_exact, n, large)


def _bias_index(dist):
    dist = np.asarray(dist)
    return np.where(dist >= 0, _bucket_np(np.maximum(dist, 0)), N_BUCKETS).astype(np.int32)


def _lookup(ext, idx):
    a, b = idx.shape
    rows = jnp.arange(ext.shape[0], dtype=jnp.int32)[:, None]
    onehot = (rows == jnp.asarray(idx.reshape(1, -1), jnp.int32)).astype(F32)
    out = jnp.dot(ext.T, onehot, precision=lax.Precision.HIGHEST)
    return out.reshape(ext.shape[1], a, b).transpose(1, 0, 2).reshape(a, -1)


def _near_tiles(tbl):
    h = tbl.shape[1]
    far = tbl[N_BUCKETS - 1]
    ext = jnp.concatenate([tbl, jnp.full((1, h), NEG, F32)], axis=0)
    kj = np.arange(TQ)[:, None]
    qi = np.arange(TQ)[None, :]
    far = jnp.repeat(far, TQ)[None, :]
    d0 = (_lookup(ext, _bias_index(qi - kj)) - far) * LOG2E
    d1 = (_lookup(ext, _bias_index(TQ + qi - kj)) - far) * LOG2E
    edge = jnp.asarray(np.tile(np.where(kj > qi, 0.0, NEG), (1, h)), F32)
    return jnp.stack([jnp.zeros_like(d0), d1, d0, jnp.full_like(d0, NEG), edge])


def _cmp_bias(tbl, nq, n16):
    h = tbl.shape[1]
    ext = jnp.concatenate([tbl, jnp.full((1, h), NEG, F32)], axis=0)
    off = (TQ // CMP_STRIDE) * (nq - 1)
    cc = np.arange(n16 + off)[:, None] - off
    qi = np.arange(TQ)[None, :]
    return _lookup(ext, _bias_index(qi - CMP_STRIDE * cc - (CMP_BLOCK - 1))) * LOG2E


def _compress_kernel(x_ref, pos_ref, w1_ref, w2_ref, o_ref):
    n16 = o_ref.shape[0]
    acc = None
    for g in range(NSA_GROUPS):
        pa = pb = None
        for l in range(CMP_STRIDE):
            x = x_ref[pl.ds(l, n16, stride=CMP_STRIDE), :]
            a = _dot(_mx(x + pos_ref[l]), w1_ref[g, l])
            c = _dot(_mx(x + pos_ref[CMP_STRIDE + l]), w1_ref[g, CMP_STRIDE + l])
            pa = a if pa is None else pa + a
            pb = c if pb is None else pb + c
        pre = pa + pltpu.roll(pb, n16 - 1, 0)
        t = _dot(_mx(jax.nn.silu(pre)), w2_ref[g])
        acc = t if acc is None else acc + t
    o_ref[...] = acc


def _compress(u, pos, w1, w2):
    b, s, _ = u.shape
    n16 = s // CMP_STRIDE
    return pl.pallas_call(
        _compress_kernel,
        grid=(2, b),
        in_specs=[pl.BlockSpec((None, s, LANES), lambda k, i: (i, 0, A_KV // LANES + k)),
                  pl.BlockSpec((None, CMP_BLOCK, 1, LANES), lambda k, i: (k, 0, 0, 0)),
                  pl.BlockSpec((None, NSA_GROUPS, CMP_BLOCK, LANES, CMP_HIDDEN), lambda k, i: (k, 0, 0, 0, 0)),
                  pl.BlockSpec((None, NSA_GROUPS, CMP_HIDDEN, LANES), lambda k, i: (k, 0, 0, 0))],
        out_specs=pl.BlockSpec((None, None, n16, LANES), lambda k, i: (k, i, 0, 0)),
        out_shape=jax.ShapeDtypeStruct((2, b, n16, LANES), F32),
        compiler_params=_params(2),
        name="compress",
    )(u, pos, w1, w2)


def _softmax_reset(m_s, acc_s):
    m_s[...] = jnp.full(m_s.shape, -jnp.inf, F32)
    acc_s[...] = jnp.zeros(acc_s.shape, F32)


def _softmax_step(score_fn, values, m_s, acc_s, col_block):
    r = m_s.shape[1]
    s_next = score_fn(0, col_block)
    for c0 in range(0, r, col_block):
        cols = slice(c0, c0 + col_block)
        s = s_next
        if c0 + col_block < r:
            s_next = score_fn(c0 + col_block, c0 + 2 * col_block)
        m_prev = m_s[:, cols]
        m_new = jnp.maximum(m_prev, jnp.max(s, axis=0, keepdims=True))
        alpha = jnp.exp2(m_prev - m_new)
        p = _mx(jnp.exp2(s - m_new))
        v = values[c0 * len(values) // r]
        acc_s[:, cols] = alpha * acc_s[:, cols] + _dot(v, p)
        m_s[:, cols] = m_new


def _store_value_tiles(v_ref, vt_s, ones_hi):
    row = lax.broadcasted_iota(jnp.int32, (LANES, TQ), 0)

    def body(j, carry):
        vt = v_ref[pl.ds(pl.multiple_of(j * TQ, TQ), TQ), :].T
        for k, hi in enumerate(ones_hi):
            vt_s[k, j] = _mx(jnp.where((row >= HEAD_DIM) == hi, 1.0, vt))
        return carry

    lax.fori_loop(0, vt_s.shape[1], body, 0)


def _value_tiles(vt_s, k, j0, ntile):
    tiles = [vt_s[k, j0 + t] for t in range(ntile)]
    return tiles[0] if ntile == 1 else jnp.concatenate(tiles, axis=1)


def _near_bias(d_ref, idx_of_tile, j0, ntile, c0, c1):
    tiles = [d_ref[idx_of_tile(j0 + t), :, c0:c1] for t in range(ntile)]
    return tiles[0] if ntile == 1 else jnp.concatenate(tiles, axis=0)


def _stack_heads(q_ref, hi):
    lane = lax.broadcasted_iota(jnp.int32, (TQ, LANES), 1)
    out = []
    for h, up in enumerate(hi):
        v = q_ref[:, (h // 2) * LANES:(h // 2 + 1) * LANES]
        if (h % 2 == 1) != up:
            v = pltpu.roll(v, HEAD_DIM, 1)
        out.append(jnp.where((lane >= HEAD_DIM) == up, v, 0.0))
    return jnp.concatenate(out, axis=0)


def _last_chunk(chunk_fn, i, last, ch):
    if ch < 2:
        chunk_fn(last, True)
        return
    short = (i - last * ch) < ch // 2

    @pl.when(short)
    def _():
        chunk_fn(last, True, ch // 2)

    @pl.when(jnp.logical_not(short))
    def _():
        chunk_fn(last, True)


def _tiles_per_chunk(nq):
    for ch in (4, 2):
        if nq % ch == 0:
            return ch
    return 1


def _nsa_kernel(q_ref, kc_ref, vc_ref, ks_ref, vs_ref, kw_ref, vw_ref, g_ref, z_ref,
                bc_ref, d_ref, et_ref, ov_ref, o_ref,
                qa_s, m_s, acc_s, osel_s, vct_s, vst_s, vwt_s, *, n_pick, ch, n_win_tiles):
    i = pl.program_id(1)
    rows = NSA_HEADS * TQ
    nb = ov_ref.shape[0]
    q0 = i * TQ
    half = LANES // NSA_GROUPS

    @pl.when(i == 0)
    def _():
        vct_s[...] = _mx(vc_ref[...].T)
        _store_value_tiles(vs_ref, vst_s, (True, False))
        _store_value_tiles(vw_ref, vwt_s, (True, False))

    qs = _mx(_stack_heads(q_ref, [h >= NSA_HPG for h in range(NSA_HEADS)]))
    qa_s[:, :LANES] = qs

    s = _dot_nt(_mx(kc_ref[...]), qs) + bc_ref[...]
    p = jnp.exp2(s - jnp.max(s, axis=0, keepdims=True))
    t_lane = q0 + (lax.broadcasted_iota(jnp.int32, (1, rows), 1) & (TQ - 1))
    any_valid = jnp.where(t_lane >= CMP_BLOCK - 1, 1.0, 0.0)
    p = p * (any_valid / jnp.sum(p, axis=0, keepdims=True))
    oc_t = _dot(vct_s[...], _mx(p))

    blk = lax.broadcasted_iota(jnp.int32, (nb, TQ), 0)
    t = q0 + lax.broadcasted_iota(jnp.int32, (nb, TQ), 1)
    cur = lax.shift_right_logical(t, int(math.log2(SEL_BLOCK)))
    forced = jnp.where(blk == 0, 1.0, jnp.where(blk == cur, 1.0, jnp.where(blk == cur - 1, 1.0, 0.0)))
    admissible = blk * SEL_BLOCK <= t
    ov = ov_ref[...]
    for g in range(NSA_GROUPS):
        c0 = g * NSA_HPG * TQ
        ps = p[:, c0:c0 + TQ]
        for h in range(1, NSA_HPG):
            ps = ps + p[:, c0 + h * TQ:c0 + (h + 1) * TQ]
        hi = _mx(ps)
        lo = _mx(ps - hi.astype(F32))
        imp_t = _dot(ov, hi) + _dot(ov, lo)
        score = jnp.where(admissible, imp_t + FORCE_BONUS * forced, NEG)
        sel = jnp.zeros((nb, TQ), F32)
        for _ in range(n_pick):
            best = jnp.max(score, axis=0, keepdims=True)
            first = jnp.min(jnp.where(score == best, blk, nb), axis=0, keepdims=True)
            hit = blk == first
            sel = jnp.where(hit, 1.0, sel)
            score = jnp.where(hit, -jnp.inf, score)
        if nb < LANES:
            sel = jnp.concatenate([sel, jnp.ones((LANES - nb, TQ), F32)], axis=0)
        block_mask = _mx((sel.T - 1.0) * (-NEG))
        for h in range(NSA_HPG):
            r0 = (g * NSA_HPG + h) * TQ
            qa_s[r0:r0 + TQ, LANES:] = block_mask

    def values(vt_s, j0, ntile):
        return [_value_tiles(vt_s, 0, j0, ntile), _value_tiles(vt_s, 1, j0, ntile)]

    _softmax_reset(m_s, acc_s)

    def sel_chunk(c, near, ntile=ch):
        tk = ntile * TQ
        koff = pl.multiple_of(c * (ch * TQ), ch * TQ)
        k_aug = jnp.concatenate([_mx(ks_ref[pl.ds(koff, tk), :]), et_ref[pl.ds(koff, tk), :]], axis=1)

        def scores(c0, c1):
            s = _dot_nt(k_aug, qa_s[c0:c1, :])
            if near:
                s = s + _near_bias(d_ref, lambda j: jnp.clip(j - (i - 2), 0, 3), c * ch, ntile, c0, c1)
            return s

        _softmax_step(scores, values(vst_s, c * ch, ntile), m_s, acc_s, COL_BLOCK)

    first_near = jnp.maximum(i - 1, 0) // ch
    last = i // ch

    def far_body(c, carry):
        sel_chunk(c, False)
        return carry

    def near_body(c, carry):
        sel_chunk(c, True)
        return carry

    lax.fori_loop(0, first_near, far_body, 0)
    lax.fori_loop(first_near, last, near_body, 0)
    _last_chunk(sel_chunk, i, last, ch)
    osel_s[...] = acc_s[...]

    _softmax_reset(m_s, acc_s)
    n_win = WINDOW // TQ
    jw = jnp.maximum(i - (n_win_tiles - 1), 0)
    koff = pl.multiple_of(jw * TQ, TQ)
    tk = n_win_tiles * TQ

    def win_idx(j):
        r = j - (i - n_win)
        return jnp.where(r == 0, 4, jnp.clip(r - (n_win - 2), 0, 3))

    kw = _mx(kw_ref[pl.ds(koff, tk), :])
    _softmax_step(lambda c0, c1: (_dot_nt(kw, qa_s[c0:c1, :LANES])
                                  + _near_bias(d_ref, win_idx, jw, n_win_tiles, c0, c1)),
                  values(vwt_s, jw, n_win_tiles), m_s, acc_s, COL_BLOCK)
    ow_t = acc_s[...]
    os_t = osel_s[...]

    gate_t = jax.nn.sigmoid(g_ref[...]).T
    for m in range(NSA_HEADS // 2):
        parts = []
        for hh in (2 * m, 2 * m + 1):
            g = hh // NSA_HPG
            r = slice(g * half, (g + 1) * half)
            d = (1 - g) * half
            c = slice(hh * TQ, (hh + 1) * TQ)
            parts.append(gate_t[3 * hh:3 * hh + 1, :] * oc_t[r, c]
                         + gate_t[3 * hh + 1:3 * hh + 2, :] * (os_t[r, c] / os_t[d:d + 1, c])
                         + gate_t[3 * hh + 2:3 * hh + 3, :] * (ow_t[r, c] / ow_t[d:d + 1, c]))
        slab = jnp.concatenate(parts, axis=0).T
        o_ref[:, m * LANES:(m + 1) * LANES] = slab * jax.nn.silu(z_ref[:, m * LANES:(m + 1) * LANES])


def _nsa(u, cmp_kv, bias_c, d_tiles, e_t, ov, n_pick):
    b, s, _ = u.shape
    nq = s // TQ
    n16 = cmp_kv.shape[2]
    nb = ov.shape[0]
    rows = NSA_HEADS * TQ
    n_win_tiles = min(WINDOW // TQ + 1, nq)
    seq = lambda col: pl.BlockSpec((None, s, LANES), lambda bi, i, col=col: (bi, 0, col // LANES))
    return pl.pallas_call(
        functools.partial(_nsa_kernel, n_pick=n_pick, ch=_tiles_per_chunk(nq), n_win_tiles=n_win_tiles),
        grid=(b, nq),
        in_specs=[
            pl.BlockSpec((None, TQ, NSA_HEADS * HEAD_DIM), lambda bi, i: (bi, i, A_NSA_Q // (NSA_HEADS * HEAD_DIM))),
            pl.BlockSpec((None, None, n16, LANES), lambda bi, i: (0, bi, 0, 0)),
            pl.BlockSpec((None, None, n16, LANES), lambda bi, i: (1, bi, 0, 0)),
            seq(A_KV + 2 * LANES), seq(A_KV + 3 * LANES), seq(A_KV + 4 * LANES), seq(A_KV + 5 * LANES),
            pl.BlockSpec((None, TQ, LANES), lambda bi, i: (bi, i, A_GATE // LANES)),
            pl.BlockSpec((None, TQ, 512), lambda bi, i: (bi, i, A_AZ // 512)),
            pl.BlockSpec((pl.Element(n16), pl.Element(rows)),
                         lambda bi, i: ((TQ // CMP_STRIDE) * (nq - 1 - i), 0)),
            pl.BlockSpec(d_tiles.shape, lambda bi, i: (0, 0, 0)),
            pl.BlockSpec((s, LANES), lambda bi, i: (0, 0)),
            pl.BlockSpec((nb, n16), lambda bi, i: (0, 0)),
        ],
        out_specs=pl.BlockSpec((None, TQ, 512), lambda bi, i: (bi, i, 0)),
        out_shape=jax.ShapeDtypeStruct((b, s, 512), F32),
        scratch_shapes=[pltpu.VMEM((rows, 2 * LANES), MXU_DTYPE),
                        pltpu.VMEM((1, rows), F32),
                        pltpu.VMEM((LANES, rows), F32),
                        pltpu.VMEM((LANES, rows), F32),
                        pltpu.VMEM((LANES, n16), MXU_DTYPE),
                        pltpu.VMEM((NSA_GROUPS, nq, LANES, TQ), MXU_DTYPE),
                        pltpu.VMEM((NSA_GROUPS, nq, LANES, TQ), MXU_DTYPE)],
        compiler_params=_params(2),
        name="nsa",
    )(u, cmp_kv, cmp_kv, u, u, u, u, u, u, bias_c, d_tiles, e_t, ov)


def _dsa_kernel(q_ref, kv_ref, qi_ref, ki_ref, wi_ref, z_ref, d_ref, tri_ref, o_ref,
                qs_s, qis_s, sc_s, hi_s, lo_s, m_s, acc_s, seen_s, vt_s, *, k_top, ch):
    i = pl.program_id(1)

    @pl.when(i == 0)
    def _():
        _store_value_tiles(kv_ref, vt_s, (False,))

    qs_s[...] = _mx(_stack_heads(q_ref, [False] * DSA_HEADS))
    qis_s[...] = _mx(_stack_heads(qi_ref, [False] * IDX_HEADS))
    wi_t = wi_ref[...].T
    w_rows = [jnp.broadcast_to(wi_t[h:h + 1, :], (TQ, TQ)) for h in range(IDX_HEADS)]
    key_idx = lax.broadcasted_iota(jnp.int32, (TQ, TQ), 0)
    q_idx = lax.broadcasted_iota(jnp.int32, (TQ, TQ), 1)
    n_chunks = (i + ch) // ch
    pairs = ch // 2

    def score_chunk(c, last):
        koff = pl.multiple_of(c * (ch * TQ), ch * TQ)
        r = jnp.maximum(_dot_nt(_mx(ki_ref[pl.ds(koff, ch * TQ), :]), qis_s[...]), 0.0)
        keys = []
        for t in range(ch):
            jt = c * ch + t
            rt = r[t * TQ:(t + 1) * TQ]
            sc = w_rows[0] * rt[:, 0:TQ]
            for h in range(1, IDX_HEADS):
                sc = sc + w_rows[h] * rt[:, h * TQ:(h + 1) * TQ]
            if last:
                sc = jnp.where((jt == i) & (key_idx > q_idx), NEG, sc)
            bits = pltpu.bitcast(sc, jnp.int32)
            key = bits ^ (lax.shift_right_arithmetic(bits, 31) & jnp.int32(0x7FFFFFFF))
            key = jnp.where(sc == 0.0, 0, key)
            if last:
                key = jnp.where(jt > i, INT_MIN, key)
            keys.append(key)
            sc_s[jt] = key
        for pr in range(pairs):
            a, b = keys[2 * pr], keys[2 * pr + 1]
            hi_s[c * pairs + pr] = (a & jnp.int32(-65536)) | lax.shift_right_logical(b, 16)
            lo_s[c * pairs + pr] = (lax.shift_left((a & 0xFFFF) ^ 0x8000, 16)
                                    | ((b & 0xFFFF) ^ 0x8000))

    def earlier_chunk(c, carry):
        score_chunk(c, False)
        return carry

    lax.fori_loop(0, n_chunks - 1, earlier_chunk, 0)
    score_chunk(n_chunks - 1, True)

    def halves(words):
        return pltpu.bitcast(words, jnp.int16)

    def both_halves(v):
        w = lax.shift_left(v, 16) | (v & 0xFFFF)
        return halves(jnp.broadcast_to(w, (TQ, TQ)))

    def count16(words_s, pred):
        def body(c, acc):
            for pr in range(pairs):
                acc = acc + jnp.where(pred(halves(words_s[c * pairs + pr])), jnp.int16(1), jnp.int16(0))
            return acc
        acc = lax.fori_loop(0, n_chunks, body, jnp.zeros((2 * TQ, TQ), jnp.int16))
        acc = acc[:TQ] + acc[TQ:]
        return jnp.sum(acc.astype(jnp.int32), axis=0, keepdims=True)

    def bisect16(words_s, target):
        c0 = count16(words_s, lambda x: x >= jnp.int16(0))
        ok0 = c0 >= target

        def step(it, carry):
            v, n_gt = carry
            cand = v | lax.shift_left(jnp.int32(1), 14 - it)
            cand16 = both_halves(cand)
            c = count16(words_s, lambda x: x >= cand16)
            ok = c >= target
            return jnp.where(ok, cand, v), jnp.where(ok, n_gt, c)

        return lax.fori_loop(0, 15, step, (jnp.where(ok0, 0, -32768), jnp.where(ok0, 0, c0)))

    top, above = bisect16(hi_s, k_top)
    top16 = both_halves(top)

    def keep_ties(c, carry):
        for pr in range(pairs):
            p_ = c * pairs + pr
            tied = halves(hi_s[p_]) == top16
            lo_s[p_] = pltpu.bitcast(jnp.where(tied, halves(lo_s[p_]), jnp.int16(-32768)), jnp.int32)
        return carry

    lax.fori_loop(0, n_chunks, keep_ties, 0)
    low, above_low = bisect16(lo_s, k_top - above)
    thr = lax.shift_left(top, 16) | ((low ^ 0x8000) & 0xFFFF)
    thr = jnp.broadcast_to(thr, (TQ, TQ))
    need = (k_top - above - above_low).astype(F32)

    _softmax_reset(m_s, acc_s)
    seen_s[...] = jnp.zeros((TQ, TQ), F32)

    def att_chunk(c, near, ntile=ch):
        tk = ntile * TQ
        koff = pl.multiple_of(c * (ch * TQ), ch * TQ)
        masks = []
        for t in range(ntile):
            key = sc_s[c * ch + t]
            eq = key == thr
            prefix = _dot(tri_ref[...], _mx(jnp.where(eq, 1.0, 0.0)))
            seen = seen_s[...]
            take = jnp.where(eq, jnp.where(seen + prefix <= need, 1.0, 0.0), 0.0)
            seen_s[...] = seen + prefix[TQ - 1:TQ, :]
            masks.append(jnp.where(key > thr, 0.0, (take - 1.0) * (-NEG)))
        mb = masks[0] if ntile == 1 else jnp.concatenate(masks, axis=0)
        mb = jnp.concatenate([mb] * DSA_HEADS, axis=1)
        kvm = _mx(kv_ref[pl.ds(koff, tk), :])

        def scores(c0, c1):
            s = _dot_nt(kvm, qs_s[c0:c1, :]) + mb[:, c0:c1]
            if near:
                s = s + _near_bias(d_ref, lambda j: jnp.clip(j - (i - 2), 0, 3), c * ch, ntile, c0, c1)
            return s

        _softmax_step(scores, [_value_tiles(vt_s, 0, c * ch, ntile)], m_s, acc_s, DSA_HEADS * TQ)

    first_near = jnp.maximum(i - 1, 0) // ch

    def far_body(c, carry):
        att_chunk(c, False)
        return carry

    def near_body(c, carry):
        att_chunk(c, True)
        return carry

    lax.fori_loop(0, first_near, far_body, 0)
    lax.fori_loop(first_near, n_chunks - 1, near_body, 0)
    _last_chunk(att_chunk, i, n_chunks - 1, ch)

    o_t = acc_s[...]
    o_t = o_t / o_t[0:1, :]
    for m in range(DSA_HEADS // 2):
        pair = [o_t[HEAD_DIM:, hh * TQ:(hh + 1) * TQ] for hh in (2 * m, 2 * m + 1)]
        slab = jnp.concatenate(pair, axis=0).T
        o_ref[:, m * LANES:(m + 1) * LANES] = slab * jax.nn.silu(z_ref[:, m * LANES:(m + 1) * LANES])


def _dsa(u, d_tiles, tri, k_top):
    b, s, _ = u.shape
    nq = s // TQ
    assert nq % 2 == 0, "the packed threshold search pairs key tiles"
    rows = DSA_HEADS * TQ
    irows = IDX_HEADS * TQ
    seq = lambda col: pl.BlockSpec((None, s, LANES), lambda bi, i, col=col: (bi, 0, col // LANES))
    return pl.pallas_call(
        functools.partial(_dsa_kernel, k_top=k_top, ch=_tiles_per_chunk(nq)),
        grid=(b, nq),
        in_specs=[
            pl.BlockSpec((None, TQ, DSA_HEADS * HEAD_DIM), lambda bi, i: (bi, i, A_DSA_Q // (DSA_HEADS * HEAD_DIM))),
            seq(A_DSA_KV),
            pl.BlockSpec((None, TQ, IDX_HEADS * IDX_DIM), lambda bi, i: (bi, i, A_IDX_Q // (IDX_HEADS * IDX_DIM))),
            seq(A_IDX_K),
            pl.BlockSpec((None, TQ, LANES), lambda bi, i: (bi, i, A_IDX_W // LANES)),
            pl.BlockSpec((None, TQ, 512), lambda bi, i: (bi, i, A_BZ // 512)),
            pl.BlockSpec(d_tiles.shape, lambda bi, i: (0, 0, 0)),
            pl.BlockSpec((TQ, TQ), lambda bi, i: (0, 0)),
        ],
        out_specs=pl.BlockSpec((None, TQ, 512), lambda bi, i: (bi, i, 0)),
        out_shape=jax.ShapeDtypeStruct((b, s, 512), F32),
        scratch_shapes=[pltpu.VMEM((rows, LANES), MXU_DTYPE),
                        pltpu.VMEM((irows, LANES), MXU_DTYPE),
                        pltpu.VMEM((nq, TQ, TQ), jnp.int32),
                        pltpu.VMEM((nq // 2, TQ, TQ), jnp.int32),
                        pltpu.VMEM((nq // 2, TQ, TQ), jnp.int32),
                        pltpu.VMEM((1, rows), F32),
                        pltpu.VMEM((LANES, rows), F32),
                        pltpu.VMEM((TQ, TQ), F32),
                        pltpu.VMEM((1, nq, LANES, TQ), MXU_DTYPE)],
        compiler_params=_params(2),
        name="dsa",
    )(u, u, u, u, u, u, d_tiles, tri)


def _causal_conv(x, xe_s, w_ref, b_ref):
    t = x.shape[0]
    xe_s[8:8 + t, :] = x
    y = b_ref[...] + w_ref[CONV_WIDTH - 1:CONV_WIDTH, :] * x
    for k in range(1, CONV_WIDTH):
        y = y + w_ref[CONV_WIDTH - 1 - k:CONV_WIDTH - k, :] * xe_s[8 - k:8 - k + t, :]
    xe_s[0:8, :] = x[t - 8:t, :]
    return y


def _rglru_kernel(x_ref, z_ref, cw_ref, cb_ref, wa_ref, ba_ref, wx_ref, bx_ref, lam_ref, o_ref,
                  xe_s, h_s):
    t, c = x_ref.shape

    @pl.when(pl.program_id(1) == 0)
    def _():
        xe_s[0:8, :] = jnp.zeros((8, c), F32)
        h_s[...] = jnp.zeros(h_s.shape, F32)

    xc = _causal_conv(x_ref[...], xe_s, cw_ref, cb_ref)
    xcm = _mx(xc)
    r = jax.nn.sigmoid(_dot(xcm, wa_ref[...]) + ba_ref[...])
    ig = jax.nn.sigmoid(_dot(xcm, wx_ref[...]) + bx_ref[...])
    nl = -lam_ref[...]
    softplus = jnp.maximum(nl, 0.0) + jnp.log1p(jnp.exp(-jnp.abs(nl)))
    log_a = (-LRU_C * r) * softplus
    a = jnp.exp(log_a)
    bb = jnp.sqrt(-jnp.tanh(log_a) * (a * a + 1.0)) * (ig * xc)
    row = lax.broadcasted_iota(jnp.int32, (t, c), 0) & (SUBLANES - 1)
    d = 1
    while d < SUBLANES:
        live = row >= d
        a_sh = jnp.where(live, pltpu.roll(a, d, 0), 1.0)
        b_sh = jnp.where(live, pltpu.roll(bb, d, 0), 0.0)
        bb = a * b_sh + bb
        a = a * a_sh
        d *= 2
    carry = h_s[0:1, :]
    groups = []
    for g in range(t // SUBLANES):
        rows = slice(g * SUBLANES, (g + 1) * SUBLANES)
        groups.append(a[rows] * carry + bb[rows])
        carry = groups[-1][SUBLANES - 1:SUBLANES, :]
    h = jnp.concatenate(groups, axis=0)
    h_s[0:1, :] = carry
    o_ref[...] = h * jax.nn.silu(z_ref[...])


def _rglru(u, conv_w, conv_b, wa_bd, ba, wx_bd, bx, lam, t=256):
    b, s, _ = u.shape
    c = LRU_WIDTH
    whole = lambda bi, i: (0, 0)
    return pl.pallas_call(
        _rglru_kernel,
        grid=(b, s // t),
        in_specs=[pl.BlockSpec((None, t, c), lambda bi, i: (bi, i, R_CX // c)),
                  pl.BlockSpec((None, t, c), lambda bi, i: (bi, i, R_CZ // c)),
                  pl.BlockSpec((CONV_WIDTH, c), whole), pl.BlockSpec((1, c), whole),
                  pl.BlockSpec((c, c), whole), pl.BlockSpec((1, c), whole),
                  pl.BlockSpec((c, c), whole), pl.BlockSpec((1, c), whole),
                  pl.BlockSpec((1, c), whole)],
        out_specs=pl.BlockSpec((None, t, c), lambda bi, i: (bi, i, 0)),
        out_shape=jax.ShapeDtypeStruct((b, s, c), F32),
        scratch_shapes=[pltpu.VMEM((t + 8, c), F32), pltpu.VMEM((8, c), F32)],
        compiler_params=_params(2),
        name="rglru",
    )(u, u, conv_w, conv_b.reshape(1, c), wa_bd, ba.reshape(1, c), wx_bd, bx.reshape(1, c),
      lam.reshape(1, c))


def _mlstm_kernel(qk_ref, v_ref, g_ref, og_ref, z_ref, cw_ref, cb_ref, gb_ref, o_ref,
                  xe_s, c_s, m_s):
    t = qk_ref.shape[0]
    L = MLSTM_CHUNK
    H = MLSTM_HEADS
    D = MLSTM_DIM

    @pl.when(pl.program_id(1) == 0)
    def _():
        xe_s[0:8, :] = jnp.zeros((8, xe_s.shape[1]), F32)
        c_s[...] = jnp.zeros(c_s.shape, F32)
        m_s[...] = jnp.zeros(m_s.shape, F32)

    qk = jax.nn.silu(_causal_conv(qk_ref[...], xe_s, cw_ref, cb_ref))

    lane = lax.broadcasted_iota(jnp.int32, (t, LANES), 1)
    row_in = lax.broadcasted_iota(jnp.int32, (t, LANES), 0) & (L - 1)
    gs = g_ref[...] + gb_ref[...]
    log_sig = jnp.minimum(gs, 0.0) - jnp.log1p(jnp.exp(-jnp.abs(gs)))
    pre = jnp.where(lane < H, gs, log_sig)
    cum = pre
    d = 1
    while d < L:
        cum = cum + jnp.where(row_in >= d, pltpu.roll(cum, d, 0), 0.0)
        d *= 2
    comb = jnp.where(lane < H, pre, cum)
    comb_t = comb.T

    causal_t = (lax.broadcasted_iota(jnp.int32, (L, L), 0) <= lax.broadcasted_iota(jnp.int32, (L, L), 1))
    pad_rows = jnp.zeros((7, L), F32)
    v_ts = [v_ref[:, h * D:(h + 1) * D].T for h in range(H)]
    states = [c_s[h] for h in range(H)]
    m_prevs = [m_s[h:h + 1, 0:1] for h in range(H)]
    outs = [[] for _ in range(H)]
    for c in range(t // L):
        r = slice(c * L, (c + 1) * L)
        for h in range(H):
            v_t, state, m_prev = v_ts[h], states[h], m_prevs[h]
            q_ = _mx(qk[r, h * D:(h + 1) * D])
            k_ = _mx(qk[r, (H + h) * D:(H + h + 1) * D] * (D ** -0.5))
            li_row = comb_t[h:h + 1, r]
            b_row = comb_t[H + h:H + h + 1, r]
            lb_col = comb[r, h:h + 1] - comb[r, H + h:H + h + 1]
            b_last = b_row[:, L - 1:L]

            dmat_t = jnp.where(causal_t, b_row + lb_col, -jnp.inf)
            decay_row = b_last - b_row + li_row
            m_new = jnp.maximum(b_last + m_prev, jnp.max(decay_row, axis=-1, keepdims=True))
            inter = b_row + m_prev
            m_t = jnp.maximum(inter, jnp.max(dmat_t, axis=0, keepdims=True))
            w_t = _dot_nt(k_, q_) * jnp.exp(dmat_t - m_t)
            prev = jnp.exp(inter - m_t)

            read = _dot_nt(_mx(state), q_)
            num_t = prev * read[:D] + _dot(_mx(v_t[:, r]), _mx(w_t))
            den = prev * read[D:D + 1] + jnp.sum(w_t, axis=0, keepdims=True)
            outs[h].append(num_t / jnp.maximum(jnp.abs(den), jnp.exp(-m_t)))

            wk = jnp.exp(decay_row - m_new)
            inc = jnp.concatenate([v_t[:, r] * wk, wk, pad_rows], axis=0)
            states[h] = jnp.exp(b_last + m_prev - m_new) * state + _dot(_mx(inc), k_)
            m_prevs[h] = m_new

    for h in range(H):
        c_s[h] = states[h]
        m_s[h:h + 1, :] = jnp.broadcast_to(m_prevs[h], (1, LANES))
        out = jnp.concatenate(outs[h], axis=1).T
        cols = slice(h * D, (h + 1) * D)
        o_ref[:, cols] = jax.nn.sigmoid(og_ref[:, cols]) * out * jax.nn.silu(z_ref[:, cols])


def _mlstm(u, conv_w, conv_b, gate_bias, t=128):
    b, s, _ = u.shape
    w = MLSTM_WIDTH
    whole = lambda bi, i: (0, 0)
    return pl.pallas_call(
        _mlstm_kernel,
        grid=(b, s // t),
        in_specs=[pl.BlockSpec((None, t, 2 * w), lambda bi, i: (bi, i, R_QK // (2 * w))),
                  pl.BlockSpec((None, t, w), lambda bi, i: (bi, i, R_V // w)),
                  pl.BlockSpec((None, t, LANES), lambda bi, i: (bi, i, R_GATE // LANES)),
                  pl.BlockSpec((None, t, w), lambda bi, i: (bi, i, R_O // w)),
                  pl.BlockSpec((None, t, w), lambda bi, i: (bi, i, R_Z // w)),
                  pl.BlockSpec((CONV_WIDTH, 2 * w), whole), pl.BlockSpec((1, 2 * w), whole),
                  pl.BlockSpec((1, LANES), whole)],
        out_specs=pl.BlockSpec((None, t, w), lambda bi, i: (bi, i, 0)),
        out_shape=jax.ShapeDtypeStruct((b, s, w), F32),
        scratch_shapes=[pltpu.VMEM((t + 8, 2 * w), F32),
                        pltpu.VMEM((MLSTM_HEADS, MLSTM_DIM + 8, MLSTM_DIM), F32),
                        pltpu.VMEM((8, LANES), F32)],
        compiler_params=_params(2),
        name="mlstm",
    )(u, u, u, u, u, conv_w, conv_b.reshape(1, 2 * w), gate_bias)


def _attn_weights(w_in):
    d = w_in.shape[0]
    widths = (512, 768, 24, 512, 512, 64, 64, 256, 64, 4, 512)
    offs = np.concatenate([[0], np.cumsum(widths)])
    a_q, a_kv, a_g, a_z, b_q, b_k, b_v, b_qi, b_ki, b_wi, b_z = [
        w_in[:, offs[k]:offs[k + 1]] for k in range(len(widths))]
    scale = HEAD_DIM ** -0.5 * LOG2E
    zeros = lambda n: jnp.zeros((d, n), w_in.dtype)
    cols = [
        a_q * scale, b_q * scale, a_z, b_z, b_qi, a_kv,
        a_g, zeros(LANES - 24),
        b_k, b_v,
        b_ki, zeros(LANES - IDX_DIM),
        b_wi * (IDX_DIM ** -0.5 * IDX_HEADS ** -0.5), zeros(LANES - IDX_HEADS),
    ]
    w = jnp.concatenate(cols, axis=1)
    assert w.shape[1] == A_TOTAL
    return _mx(w)


def _rec_weights(w_in):
    d = w_in.shape[0]
    widths = (512, 512, 512, 512, 512, 4, 4, 512, 512)
    offs = np.concatenate([[0], np.cumsum(widths)])
    c_x, c_z, d_q, d_k, d_v, d_i, d_f, d_o, d_z = [w_in[:, offs[k]:offs[k + 1]] for k in range(len(widths))]
    w = jnp.concatenate([c_x, c_z, d_q, d_k, d_v, d_o, d_z, d_i, d_f,
                         jnp.zeros((d, R_TOTAL - R_GATE - 2 * MLSTM_HEADS), w_in.dtype)], axis=1)
    assert w.shape[1] == R_TOTAL
    return _mx(w)


def _block_diag(w):
    g, n, _ = w.shape
    eye = jnp.eye(g, dtype=w.dtype)
    return (eye[:, None, :, None] * w[:, :, None, :]).reshape(g * n, g * n)


def _attention_layer(x2d, b, s, norm_g, w_in, cmp_pos_k, cmp_w1_k, cmp_w2_k, cmp_pos_v, cmp_w1_v, cmp_w2_v,
                     t5_table):
    nq = s // TQ
    n16 = s // CMP_STRIDE
    nb = s // SEL_BLOCK
    u = _norm_proj(x2d, norm_g, _attn_weights(w_in)).reshape(b, s, A_TOTAL)

    pos = jnp.stack([cmp_pos_k, cmp_pos_v])
    pos = jnp.concatenate([pos] * NSA_GROUPS, axis=-1).reshape(2, CMP_BLOCK, 1, LANES)
    w1 = jnp.stack([cmp_w1_k, cmp_w1_v]).reshape(2, CMP_BLOCK, HEAD_DIM, CMP_HIDDEN)
    z1 = jnp.zeros_like(w1)
    w1 = _mx(jnp.stack([jnp.concatenate([w1, z1], axis=2), jnp.concatenate([z1, w1], axis=2)], axis=1))
    w2 = jnp.stack([cmp_w2_k, cmp_w2_v])
    zpad = jnp.zeros_like(w2)
    w2 = _mx(jnp.stack([jnp.concatenate([w2, zpad], -1), jnp.concatenate([zpad, w2], -1)], axis=1))
    cmp_kv = _compress(u, pos, w1, w2)

    tbl = t5_table.astype(F32)
    tbl_a, tbl_b = tbl[:, :NSA_HEADS], tbl[:, NSA_HEADS:]
    kj = np.arange(TQ)[:, None]
    qi = np.arange(TQ)[None, :]
    assert nb <= LANES
    e_t = jnp.asarray((np.arange(s)[:, None] // SEL_BLOCK) == np.arange(LANES)[None, :], MXU_DTYPE)
    ci = np.arange(n16)[None, :]
    sj = np.arange(nb)[:, None]
    ov = jnp.asarray((ci * CMP_STRIDE < (sj + 1) * SEL_BLOCK) & (ci * CMP_STRIDE + CMP_BLOCK > sj * SEL_BLOCK)
                     & (ci < n16 - 1), MXU_DTYPE)
    tri = jnp.asarray(qi <= kj, MXU_DTYPE)

    ya = _nsa(u, cmp_kv, _cmp_bias(tbl_a, nq, n16), _near_tiles(tbl_a), e_t, ov, min(N_SEL, nb))
    yb = _dsa(u, _near_tiles(tbl_b), tri, min(DSA_TOPK_MAX, s // 4))
    return ya.reshape(b * s, -1), yb.reshape(b * s, -1)


def _recurrent_layer(x2d, b, s, norm_g, w_in, conv_c_w, conv_c_b, wa, ba, wx, bx, lam,
                     conv_d_w, conv_d_b, b_i, b_f):
    u = _norm_proj(x2d, norm_g, _rec_weights(w_in)).reshape(b, s, R_TOTAL)
    yc = _rglru(u, conv_c_w, conv_c_b, _mx(_block_diag(wa)), ba, _mx(_block_diag(wx)), bx, lam)
    gate_bias = jnp.concatenate([b_i, b_f, jnp.zeros((LANES - 2 * MLSTM_HEADS,), F32)]).reshape(1, LANES)
    yd = _mlstm(u, conv_d_w, conv_d_b, gate_bias)
    return yc.reshape(b * s, -1), yd.reshape(b * s, -1)


def kernel(x, p, norm_g, final_g, ple_w, ple_gate_w, t5_table, attn_w_in, attn_w_out, cmp_pos_k, cmp_w1_k, cmp_w2_k, cmp_pos_v, cmp_w1_v, cmp_w2_v, rec_w_in, rec_w_out, lru_conv_w, lru_conv_b, lru_wa, lru_ba, lru_wx, lru_bx, lru_lambda, mlstm_conv_w, mlstm_conv_b, mlstm_b_i, mlstm_b_f):
    b, s, d = x.shape
    depth = p.shape[0]
    x2d = x.reshape(b * s, d)
    for i in range(depth):
        j = i // 2
        if i % 2 == 0:
            ya, yb = _attention_layer(x2d, b, s, norm_g[i], attn_w_in[j], cmp_pos_k[j], cmp_w1_k[j], cmp_w2_k[j],
                                      cmp_pos_v[j], cmp_w1_v[j], cmp_w2_v[j], t5_table)
            w_out = attn_w_out[j]
        else:
            ya, yb = _recurrent_layer(x2d, b, s, norm_g[i], rec_w_in[j], lru_conv_w[j], lru_conv_b[j],
                                      lru_wa[j], lru_ba[j], lru_wx[j], lru_bx[j], lru_lambda[j],
                                      mlstm_conv_w[j], mlstm_conv_b[j], mlstm_b_i[j], mlstm_b_f[j])
            w_out = rec_w_out[j]
        x2d = _out_proj(x2d, ya, yb, p.reshape(depth, b * s, -1), i, w_out, ple_w[i], ple_gate_w[i], final_g,
                        final=(i == depth - 1))
    return x2d.reshape(b, s, d)
```

```python
import functools
import math

import numpy as np
import jax
import jax.numpy as jnp
from jax import lax
from jax.experimental import pallas as pl
from jax.experimental.pallas import tpu as pltpu

F32 = jnp.float32
MXU_DTYPE = jnp.bfloat16

HEAD_DIM = 64
NSA_HEADS = 8
NSA_GROUPS = 2
NSA_HPG = NSA_HEADS // NSA_GROUPS
CMP_BLOCK = 32
CMP_STRIDE = 16
CMP_HIDDEN = 256
SEL_BLOCK = 64
N_SEL = 8
WINDOW = 512
FORCE_BONUS = 1e4
DSA_HEADS = 8
IDX_HEADS = 4
IDX_DIM = 64
DSA_TOPK_MAX = 256
N_BUCKETS = 32
T5_MAX_DIST = 128
LRU_WIDTH = 512
CONV_WIDTH = 4
LRU_C = 8.0
MLSTM_HEADS = 4
MLSTM_DIM = 128
MLSTM_WIDTH = MLSTM_HEADS * MLSTM_DIM
MLSTM_CHUNK = 64
RMS_EPS = 1e-6
NEG = -1e30

LANES = 128
SUBLANES = 8
COL_BLOCK = 512
TQ = 128
INT_MIN = -2 ** 31
LOG2E = math.log2(math.e)
VMEM_LIMIT = 48 * 1024 * 1024

A_NSA_Q = 0
A_DSA_Q = 512
A_AZ = 1024
A_BZ = 1536
A_IDX_Q = 2048
A_KV = 2304
A_GATE = 3072
A_DSA_KV = 3200
A_IDX_K = 3328
A_IDX_W = 3456
A_TOTAL = 3584

R_CX = 0
R_CZ = 512
R_QK = 1024
R_V = 2048
R_O = 2560
R_Z = 3072
R_GATE = 3584
R_TOTAL = 3840


def _dot(a, b):
    return jnp.dot(a, b, preferred_element_type=F32)


def _dot_nt(a, b):
    return lax.dot_general(a, b, (((1,), (1,)), ((), ())), preferred_element_type=F32)


def _mx(a):
    return a.astype(MXU_DTYPE)


def _params(n_grid):
    return pltpu.CompilerParams(dimension_semantics=("arbitrary",) * n_grid,
                                vmem_limit_bytes=VMEM_LIMIT)


def _norm_proj_kernel(x_ref, g_ref, w_ref, o_ref):
    x = x_ref[...]
    ms = jnp.mean(x * x, axis=-1, keepdims=True)
    y = x * lax.rsqrt(ms + RMS_EPS) * g_ref[...]
    o_ref[...] = _dot(_mx(y), w_ref[...])


def _norm_proj(x2d, g, w, tm=256):
    m, d = x2d.shape
    n = w.shape[1]
    return pl.pallas_call(
        _norm_proj_kernel,
        grid=(m // tm,),
        in_specs=[pl.BlockSpec((tm, d), lambda i: (i, 0)),
                  pl.BlockSpec((1, d), lambda i: (0, 0)),
                  pl.BlockSpec((d, n), lambda i: (0, 0))],
        out_specs=pl.BlockSpec((tm, n), lambda i: (i, 0)),
        out_shape=jax.ShapeDtypeStruct((m, n), F32),
        compiler_params=_params(1),
        name="norm_proj",
    )(x2d, g.reshape(1, d), w)


def _out_proj_kernel(x_ref, ya_ref, yb_ref, p_ref, wo_ref, pw_ref, gw_ref, fg_ref, o_ref, *, final):
    half = ya_ref.shape[-1]
    y = _dot(_mx(ya_ref[...]), wo_ref[:half, :]) + _dot(_mx(yb_ref[...]), wo_ref[half:, :])
    x1 = x_ref[...] + y
    gate = jax.nn.sigmoid(_dot(_mx(x1), gw_ref[...]))
    x2 = x1 + _dot(_mx(p_ref[...]), pw_ref[...]) * gate
    if final:
        ms = jnp.mean(x2 * x2, axis=-1, keepdims=True)
        x2 = x2 * lax.rsqrt(ms + RMS_EPS) * fg_ref[...]
    o_ref[...] = x2


def _out_proj(x2d, ya, yb, p_all, layer, w_out, ple_w, gate_w, final_g, final, tm=512):
    m, d = x2d.shape
    half = ya.shape[1]
    pd = p_all.shape[2]
    row = lambda i: (i, 0)
    whole = lambda i: (0, 0)
    return pl.pallas_call(
        functools.partial(_out_proj_kernel, final=final),
        grid=(m // tm,),
        in_specs=[pl.BlockSpec((tm, d), row), pl.BlockSpec((tm, half), row),
                  pl.BlockSpec((tm, half), row), pl.BlockSpec((None, tm, pd), lambda i: (layer, i, 0)),
                  pl.BlockSpec((2 * half, d), whole), pl.BlockSpec((pd, d), whole),
                  pl.BlockSpec((d, d), whole), pl.BlockSpec((1, d), whole)],
        out_specs=pl.BlockSpec((tm, d), row),
        out_shape=jax.ShapeDtypeStruct((m, d), F32),
        compiler_params=_params(1),
        name="out_proj",
    )(x2d, ya, yb, p_all, _mx(w_out), _mx(ple_w), _mx(gate_w), final_g.reshape(1, d))


def _bucket_np(n):
    n = np.asarray(n)
    max_exact = N_BUCKETS // 2
    nf = np.maximum(n, 1).astype(np.float32)
    large = max_exact + (np.log(nf / np.float32(max_exact)) / np.float32(math.log(T5_MAX_DIST / max_exact))
                         * np.float32(N_BUCKETS - max_exact)).astype(np.int32)
    large = np.minimum(large, N_BUCKETS - 1)
    return np.where(n < max_exact, n, large)


def _bias_index(dist):
    dist = np.asarray(dist)
    return np.where(dist >= 0, _bucket_np(np.maximum(dist, 0)), N_BUCKETS).astype(np.int32)


def _lookup(ext, idx):
    a, b = idx.shape
    rows = jnp.arange(ext.shape[0], dtype=jnp.int32)[:, None]
    onehot = (rows == jnp.asarray(idx.reshape(1, -1), jnp.int32)).astype(F32)
    out = jnp.dot(ext.T, onehot, precision=lax.Precision.HIGHEST)
    return out.reshape(ext.shape[1], a, b).transpose(1, 0, 2).reshape(a, -1)


def _near_tiles(tbl):
    h = tbl.shape[1]
    far = tbl[N_BUCKETS - 1]
    ext = jnp.concatenate([tbl, jnp.full((1, h), NEG, F32)], axis=0)
    kj = np.arange(TQ)[:, None]
    qi = np.arange(TQ)[None, :]
    far = jnp.repeat(far, TQ)[None, :]
    d0 = (_lookup(ext, _bias_index(qi - kj)) - far) * LOG2E
    d1 = (_lookup(ext, _bias_index(TQ + qi - kj)) - far) * LOG2E
    edge = jnp.asarray(np.tile(np.where(kj > qi, 0.0, NEG), (1, h)), F32)
    return jnp.stack([jnp.zeros_like(d0), d1, d0, jnp.full_like(d0, NEG), edge])


def _cmp_bias(tbl, nq, n16):
    h = tbl.shape[1]
    ext = jnp.concatenate([tbl, jnp.full((1, h), NEG, F32)], axis=0)
    off = (TQ // CMP_STRIDE) * (nq - 1)
    cc = np.arange(n16 + off)[:, None] - off
    qi = np.arange(TQ)[None, :]
    return _lookup(ext, _bias_index(qi - CMP_STRIDE * cc - (CMP_BLOCK - 1))) * LOG2E


def _compress_kernel(x_ref, pos_ref, w1_ref, w2_ref, o_ref):
    n16 = o_ref.shape[0]
    acc = None
    for g in range(NSA_GROUPS):
        pa = pb = None
        for l in range(CMP_STRIDE):
            x = x_ref[pl.ds(l, n16, stride=CMP_STRIDE), :]
            a = _dot(_mx(x + pos_ref[l]), w1_ref[g, l])
            c = _dot(_mx(x + pos_ref[CMP_STRIDE + l]), w1_ref[g, CMP_STRIDE + l])
            pa = a if pa is None else pa + a
            pb = c if pb is None else pb + c
        pre = pa + pltpu.roll(pb, n16 - 1, 0)
        t = _dot(_mx(jax.nn.silu(pre)), w2_ref[g])
        acc = t if acc is None else acc + t
    o_ref[...] = acc


def _compress(u, pos, w1, w2):
    b, s, _ = u.shape
    n16 = s // CMP_STRIDE
    return pl.pallas_call(
        _compress_kernel,
        grid=(2, b),
        in_specs=[pl.BlockSpec((None, s, LANES), lambda k, i: (i, 0, A_KV // LANES + k)),
                  pl.BlockSpec((None, CMP_BLOCK, 1, LANES), lambda k, i: (k, 0, 0, 0)),
                  pl.BlockSpec((None, NSA_GROUPS, CMP_BLOCK, LANES, CMP_HIDDEN), lambda k, i: (k, 0, 0, 0, 0)),
                  pl.BlockSpec((None, NSA_GROUPS, CMP_HIDDEN, LANES), lambda k, i: (k, 0, 0, 0))],
        out_specs=pl.BlockSpec((None, None, n16, LANES), lambda k, i: (k, i, 0, 0)),
        out_shape=jax.ShapeDtypeStruct((2, b, n16, LANES), F32),
        compiler_params=_params(2),
        name="compress",
    )(u, pos, w1, w2)


def _softmax_reset(m_s, acc_s):
    m_s[...] = jnp.full(m_s.shape, -jnp.inf, F32)
    acc_s[...] = jnp.zeros(acc_s.shape, F32)


def _softmax_step(score_fn, values, m_s, acc_s, col_block):
    r = m_s.shape[1]
    s_next = score_fn(0, col_block)
    for c0 in range(0, r, col_block):
        cols = slice(c0, c0 + col_block)
        s = s_next
        if c0 + col_block < r:
            s_next = score_fn(c0 + col_block, c0 + 2 * col_block)
        m_prev = m_s[:, cols]
        m_new = jnp.maximum(m_prev, jnp.max(s, axis=0, keepdims=True))
        alpha = jnp.exp2(m_prev - m_new)
        p = _mx(jnp.exp2(s - m_new))
        v = values[c0 * len(values) // r]
        acc_s[:, cols] = alpha * acc_s[:, cols] + _dot(v, p)
        m_s[:, cols] = m_new


def _store_value_tiles(v_ref, vt_s, ones_hi):
    row = lax.broadcasted_iota(jnp.int32, (LANES, TQ), 0)

    def body(j, carry):
        vt = v_ref[pl.ds(pl.multiple_of(j * TQ, TQ), TQ), :].T
        for k, hi in enumerate(ones_hi):
            vt_s[k, j] = _mx(jnp.where((row >= HEAD_DIM) == hi, 1.0, vt))
        return carry

    lax.fori_loop(0, vt_s.shape[1], body, 0)


def _value_tiles(vt_s, k, j0, ntile):
    tiles = [vt_s[k, j0 + t] for t in range(ntile)]
    return tiles[0] if ntile == 1 else jnp.concatenate(tiles, axis=1)


def _near_bias(d_ref, idx_of_tile, j0, ntile, c0, c1):
    tiles = [d_ref[idx_of_tile(j0 + t), :, c0:c1] for t in range(ntile)]
    return tiles[0] if ntile == 1 else jnp.concatenate(tiles, axis=0)


def _stack_heads(q_ref, hi):
    lane = lax.broadcasted_iota(jnp.int32, (TQ, LANES), 1)
    out = []
    for h, up in enumerate(hi):
        v = q_ref[:, (h // 2) * LANES:(h // 2 + 1) * LANES]
        if (h % 2 == 1) != up:
            v = pltpu.roll(v, HEAD_DIM, 1)
        out.append(jnp.where((lane >= HEAD_DIM) == up, v, 0.0))
    return jnp.concatenate(out, axis=0)


def _last_chunk(chunk_fn, i, last, ch):
    if ch < 2:
        chunk_fn(last, True)
        return
    short = (i - last * ch) < ch // 2

    @pl.when(short)
    def _():
        chunk_fn(last, True, ch // 2)

    @pl.when(jnp.logical_not(short))
    def _():
        chunk_fn(last, True)


def _tiles_per_chunk(nq):
    for ch in (4, 2):
        if nq % ch == 0:
            return ch
    return 1


def _nsa_kernel(q_ref, kc_ref, vc_ref, ks_ref, vs_ref, kw_ref, vw_ref, g_ref, z_ref,
                bc_ref, d_ref, et_ref, ov_ref, o_ref,
                qa_s, m_s, acc_s, osel_s, vct_s, vst_s, vwt_s, *, n_pick, ch, n_win_tiles):
    i = pl.program_id(1)
    rows = NSA_HEADS * TQ
    nb = ov_ref.shape[0]
    q0 = i * TQ
    half = LANES // NSA_GROUPS

    @pl.when(i == 0)
    def _():
        vct_s[...] = _mx(vc_ref[...].T)
        _store_value_tiles(vs_ref, vst_s, (True, False))
        _store_value_tiles(vw_ref, vwt_s, (True, False))

    qs = _mx(_stack_heads(q_ref, [h >= NSA_HPG for h in range(NSA_HEADS)]))
    qa_s[:, :LANES] = qs

    s = _dot_nt(_mx(kc_ref[...]), qs) + bc_ref[...]
    p = jnp.exp2(s - jnp.max(s, axis=0, keepdims=True))
    t_lane = q0 + (lax.broadcasted_iota(jnp.int32, (1, rows), 1) & (TQ - 1))
    any_valid = jnp.where(t_lane >= CMP_BLOCK - 1, 1.0, 0.0)
    p = p * (any_valid / jnp.sum(p, axis=0, keepdims=True))
    oc_t = _dot(vct_s[...], _mx(p))

    blk = lax.broadcasted_iota(jnp.int32, (nb, TQ), 0)
    t = q0 + lax.broadcasted_iota(jnp.int32, (nb, TQ), 1)
    cur = lax.shift_right_logical(t, int(math.log2(SEL_BLOCK)))
    forced = jnp.where(blk == 0, 1.0, jnp.where(blk == cur, 1.0, jnp.where(blk == cur - 1, 1.0, 0.0)))
    admissible = blk * SEL_BLOCK <= t
    ov = ov_ref[...]
    for g in range(NSA_GROUPS):
        c0 = g * NSA_HPG * TQ
        ps = p[:, c0:c0 + TQ]
        for h in range(1, NSA_HPG):
            ps = ps + p[:, c0 + h * TQ:c0 + (h + 1) * TQ]
        hi = _mx(ps)
        lo = _mx(ps - hi.astype(F32))
        imp_t = _dot(ov, hi) + _dot(ov, lo)
        score = jnp.where(admissible, imp_t + FORCE_BONUS * forced, NEG)
        sel = jnp.zeros((nb, TQ), F32)
        for _ in range(n_pick):
            best = jnp.max(score, axis=0, keepdims=True)
            first = jnp.min(jnp.where(score == best, blk, nb), axis=0, keepdims=True)
            hit = blk == first
            sel = jnp.where(hit, 1.0, sel)
            score = jnp.where(hit, -jnp.inf, score)
        if nb < LANES:
            sel = jnp.concatenate([sel, jnp.ones((LANES - nb, TQ), F32)], axis=0)
        block_mask = _mx((sel.T - 1.0) * (-NEG))
        for h in range(NSA_HPG):
            r0 = (g * NSA_HPG + h) * TQ
            qa_s[r0:r0 + TQ, LANES:] = block_mask

    def values(vt_s, j0, ntile):
        return [_value_tiles(vt_s, 0, j0, ntile), _value_tiles(vt_s, 1, j0, ntile)]

    _softmax_reset(m_s, acc_s)

    def sel_chunk(c, near, ntile=ch):
        tk = ntile * TQ
        koff = pl.multiple_of(c * (ch * TQ), ch * TQ)
        k_aug = jnp.concatenate([_mx(ks_ref[pl.ds(koff, tk), :]), et_ref[pl.ds(koff, tk), :]], axis=1)

        def scores(c0, c1):
            s = _dot_nt(k_aug, qa_s[c0:c1, :])
            if near:
                s = s + _near_bias(d_ref, lambda j: jnp.clip(j - (i - 2), 0, 3), c * ch, ntile, c0, c1)
            return s

        _softmax_step(scores, values(vst_s, c * ch, ntile), m_s, acc_s, COL_BLOCK)

    first_near = jnp.maximum(i - 1, 0) // ch
    last = i // ch

    def far_body(c, carry):
        sel_chunk(c, False)
        return carry

    def near_body(c, carry):
        sel_chunk(c, True)
        return carry

    lax.fori_loop(0, first_near, far_body, 0)
    lax.fori_loop(first_near, last, near_body, 0)
    _last_chunk(sel_chunk, i, last, ch)
    osel_s[...] = acc_s[...]

    _softmax_reset(m_s, acc_s)
    n_win = WINDOW // TQ
    jw = jnp.maximum(i - (n_win_tiles - 1), 0)
    koff = pl.multiple_of(jw * TQ, TQ)
    tk = n_win_tiles * TQ

    def win_idx(j):
        r = j - (i - n_win)
        return jnp.where(r == 0, 4, jnp.clip(r - (n_win - 2), 0, 3))

    kw = _mx(kw_ref[pl.ds(koff, tk), :])
    _softmax_step(lambda c0, c1: (_dot_nt(kw, qa_s[c0:c1, :LANES])
                                  + _near_bias(d_ref, win_idx, jw, n_win_tiles, c0, c1)),
                  values(vwt_s, jw, n_win_tiles), m_s, acc_s, COL_BLOCK)
    ow_t = acc_s[...]
    os_t = osel_s[...]

    gate_t = jax.nn.sigmoid(g_ref[...]).T
    for m in range(NSA_HEADS // 2):
        parts = []
        for hh in (2 * m, 2 * m + 1):
            g = hh // NSA_HPG
            r = slice(g * half, (g + 1) * half)
            d = (1 - g) * half
            c = slice(hh * TQ, (hh + 1) * TQ)
            parts.append(gate_t[3 * hh:3 * hh + 1, :] * oc_t[r, c]
                         + gate_t[3 * hh + 1:3 * hh + 2, :] * (os_t[r, c] / os_t[d:d + 1, c])
                         + gate_t[3 * hh + 2:3 * hh + 3, :] * (ow_t[r, c] / ow_t[d:d + 1, c]))
        slab = jnp.concatenate(parts, axis=0).T
        o_ref[:, m * LANES:(m + 1) * LANES] = slab * jax.nn.silu(z_ref[:, m * LANES:(m + 1) * LANES])


def _nsa(u, cmp_kv, bias_c, d_tiles, e_t, ov, n_pick):
    b, s, _ = u.shape
    nq = s // TQ
    n16 = cmp_kv.shape[2]
    nb = ov.shape[0]
    rows = NSA_HEADS * TQ
    n_win_tiles = min(WINDOW // TQ + 1, nq)
    seq = lambda col: pl.BlockSpec((None, s, LANES), lambda bi, i, col=col: (bi, 0, col // LANES))
    return pl.pallas_call(
        functools.partial(_nsa_kernel, n_pick=n_pick, ch=_tiles_per_chunk(nq), n_win_tiles=n_win_tiles),
        grid=(b, nq),
        in_specs=[
            pl.BlockSpec((None, TQ, NSA_HEADS * HEAD_DIM), lambda bi, i: (bi, i, A_NSA_Q // (NSA_HEADS * HEAD_DIM))),
            pl.BlockSpec((None, None, n16, LANES), lambda bi, i: (0, bi, 0, 0)),
            pl.BlockSpec((None, None, n16, LANES), lambda bi, i: (1, bi, 0, 0)),
            seq(A_KV + 2 * LANES), seq(A_KV + 3 * LANES), seq(A_KV + 4 * LANES), seq(A_KV + 5 * LANES),
            pl.BlockSpec((None, TQ, LANES), lambda bi, i: (bi, i, A_GATE // LANES)),
            pl.BlockSpec((None, TQ, 512), lambda bi, i: (bi, i, A_AZ // 512)),
            pl.BlockSpec((pl.Element(n16), pl.Element(rows)),
                         lambda bi, i: ((TQ // CMP_STRIDE) * (nq - 1 - i), 0)),
            pl.BlockSpec(d_tiles.shape, lambda bi, i: (0, 0, 0)),
            pl.BlockSpec((s, LANES), lambda bi, i: (0, 0)),
            pl.BlockSpec((nb, n16), lambda bi, i: (0, 0)),
        ],
        out_specs=pl.BlockSpec((None, TQ, 512), lambda bi, i: (bi, i, 0)),
        out_shape=jax.ShapeDtypeStruct((b, s, 512), F32),
        scratch_shapes=[pltpu.VMEM((rows, 2 * LANES), MXU_DTYPE),
                        pltpu.VMEM((1, rows), F32),
                        pltpu.VMEM((LANES, rows), F32),
                        pltpu.VMEM((LANES, rows), F32),
                        pltpu.VMEM((LANES, n16), MXU_DTYPE),
                        pltpu.VMEM((NSA_GROUPS, nq, LANES, TQ), MXU_DTYPE),
                        pltpu.VMEM((NSA_GROUPS, nq, LANES, TQ), MXU_DTYPE)],
        compiler_params=_params(2),
        name="nsa",
    )(u, cmp_kv, cmp_kv, u, u, u, u, u, u, bias_c, d_tiles, e_t, ov)


def _dsa_kernel(q_ref, kv_ref, qi_ref, ki_ref, wi_ref, z_ref, d_ref, tri_ref, o_ref,
                qs_s, qis_s, sc_s, hi_s, lo_s, m_s, acc_s, seen_s, vt_s, *, k_top, ch):
    i = pl.program_id(1)

    @pl.when(i == 0)
    def _():
        _store_value_tiles(kv_ref, vt_s, (False,))

    qs_s[...] = _mx(_stack_heads(q_ref, [False] * DSA_HEADS))
    qis_s[...] = _mx(_stack_heads(qi_ref, [False] * IDX_HEADS))
    wi_t = wi_ref[...].T
    w_rows = [jnp.broadcast_to(wi_t[h:h + 1, :], (TQ, TQ)) for h in range(IDX_HEADS)]
    key_idx = lax.broadcasted_iota(jnp.int32, (TQ, TQ), 0)
    q_idx = lax.broadcasted_iota(jnp.int32, (TQ, TQ), 1)
    n_chunks = (i + ch) // ch
    pairs = ch // 2

    def score_chunk(c, last):
        koff = pl.multiple_of(c * (ch * TQ), ch * TQ)
        r = jnp.maximum(_dot_nt(_mx(ki_ref[pl.ds(koff, ch * TQ), :]), qis_s[...]), 0.0)
        keys = []
        for t in range(ch):
            jt = c * ch + t
            rt = r[t * TQ:(t + 1) * TQ]
            sc = w_rows[0] * rt[:, 0:TQ]
            for h in range(1, IDX_HEADS):
                sc = sc + w_rows[h] * rt[:, h * TQ:(h + 1) * TQ]
            if last:
                sc = jnp.where((jt == i) & (key_idx > q_idx), NEG, sc)
            bits = pltpu.bitcast(sc, jnp.int32)
            key = bits ^ (lax.shift_right_arithmetic(bits, 31) & jnp.int32(0x7FFFFFFF))
            key = jnp.where(sc == 0.0, 0, key)
            if last:
                key = jnp.where(jt > i, INT_MIN, key)
            keys.append(key)
            sc_s[jt] = key
        for pr in range(pairs):
            a, b = keys[2 * pr], keys[2 * pr + 1]
            hi_s[c * pairs + pr] = (a & jnp.int32(-65536)) | lax.shift_right_logical(b, 16)
            lo_s[c * pairs + pr] = (lax.shift_left((a & 0xFFFF) ^ 0x8000, 16)
                                    | ((b & 0xFFFF) ^ 0x8000))

    def earlier_chunk(c, carry):
        score_chunk(c, False)
        return carry

    lax.fori_loop(0, n_chunks - 1, earlier_chunk, 0)
    score_chunk(n_chunks - 1, True)

    def halves(words):
        return pltpu.bitcast(words, jnp.int16)

    def both_halves(v):
        w = lax.shift_left(v, 16) | (v & 0xFFFF)
        return halves(jnp.broadcast_to(w, (TQ, TQ)))

    def count16(words_s, pred):
        def body(c, acc):
            for pr in range(pairs):
                acc = acc + jnp.where(pred(halves(words_s[c * pairs + pr])), jnp.int16(1), jnp.int16(0))
            return acc
        acc = lax.fori_loop(0, n_chunks, body, jnp.zeros((2 * TQ, TQ), jnp.int16))
        rows = 2 * TQ
        while rows > 2 * SUBLANES:
            rows //= 2
            acc = acc[:rows] + acc[rows:]
        return jnp.sum(acc.astype(jnp.int32), axis=0, keepdims=True)

    def bisect16(words_s, target):
        c0 = count16(words_s, lambda x: x >= jnp.int16(0))
        ok0 = c0 >= target

        def step(it, carry):
            v, n_gt = carry
            cand = v | lax.shift_left(jnp.int32(1), 14 - it)
            cand16 = both_halves(cand)
            c = count16(words_s, lambda x: x >= cand16)
            ok = c >= target
            return jnp.where(ok, cand, v), jnp.where(ok, n_gt, c)

        return lax.fori_loop(0, 15, step, (jnp.where(ok0, 0, -32768), jnp.where(ok0, 0, c0)))

    top, above = bisect16(hi_s, k_top)
    top16 = both_halves(top)

    def keep_ties(c, carry):
        for pr in range(pairs):
            p_ = c * pairs + pr
            tied = halves(hi_s[p_]) == top16
            lo_s[p_] = pltpu.bitcast(jnp.where(tied, halves(lo_s[p_]), jnp.int16(-32768)), jnp.int32)
        return carry

    lax.fori_loop(0, n_chunks, keep_ties, 0)
    low, above_low = bisect16(lo_s, k_top - above)
    thr = lax.shift_left(top, 16) | ((low ^ 0x8000) & 0xFFFF)
    thr = jnp.broadcast_to(thr, (TQ, TQ))
    need = (k_top - above - above_low).astype(F32)

    _softmax_reset(m_s, acc_s)
    seen_s[...] = jnp.zeros((TQ, TQ), F32)

    def att_chunk(c, near, ntile=ch):
        tk = ntile * TQ
        koff = pl.multiple_of(c * (ch * TQ), ch * TQ)
        masks = []
        for t in range(ntile):
            key = sc_s[c * ch + t]
            eq = key == thr
            prefix = _dot(tri_ref[...], _mx(jnp.where(eq, 1.0, 0.0)))
            seen = seen_s[...]
            tie = jnp.where(seen + prefix <= need, 0.0, NEG)
            seen_s[...] = seen + prefix[TQ - 1:TQ, :]
            masks.append(jnp.where(key > thr, 0.0, jnp.where(eq, tie, NEG)))
        mb = masks[0] if ntile == 1 else jnp.concatenate(masks, axis=0)
        mb = jnp.concatenate([mb] * DSA_HEADS, axis=1)
        kvm = _mx(kv_ref[pl.ds(koff, tk), :])

        def scores(c0, c1):
            s = _dot_nt(kvm, qs_s[c0:c1, :]) + mb[:, c0:c1]
            if near:
                s = s + _near_bias(d_ref, lambda j: jnp.clip(j - (i - 2), 0, 3), c * ch, ntile, c0, c1)
            return s

        _softmax_step(scores, [_value_tiles(vt_s, 0, c * ch, ntile)], m_s, acc_s, DSA_HEADS * TQ)

    first_near = jnp.maximum(i - 1, 0) // ch

    def far_body(c, carry):
        att_chunk(c, False)
        return carry

    def near_body(c, carry):
        att_chunk(c, True)
        return carry

    lax.fori_loop(0, first_near, far_body, 0)
    lax.fori_loop(first_near, n_chunks - 1, near_body, 0)
    _last_chunk(att_chunk, i, n_chunks - 1, ch)

    o_t = acc_s[...]
    o_t = o_t / o_t[0:1, :]
    for m in range(DSA_HEADS // 2):
        pair = [o_t[HEAD_DIM:, hh * TQ:(hh + 1) * TQ] for hh in (2 * m, 2 * m + 1)]
        slab = jnp.concatenate(pair, axis=0).T
        o_ref[:, m * LANES:(m + 1) * LANES] = slab * jax.nn.silu(z_ref[:, m * LANES:(m + 1) * LANES])


def _dsa(u, d_tiles, tri, k_top):
    b, s, _ = u.shape
    nq = s // TQ
    assert nq % 2 == 0, "the packed threshold search pairs key tiles"
    rows = DSA_HEADS * TQ
    irows = IDX_HEADS * TQ
    seq = lambda col: pl.BlockSpec((None, s, LANES), lambda bi, i, col=col: (bi, 0, col // LANES))
    return pl.pallas_call(
        functools.partial(_dsa_kernel, k_top=k_top, ch=_tiles_per_chunk(nq)),
        grid=(b, nq),
        in_specs=[
            pl.BlockSpec((None, TQ, DSA_HEADS * HEAD_DIM), lambda bi, i: (bi, i, A_DSA_Q // (DSA_HEADS * HEAD_DIM))),
            seq(A_DSA_KV),
            pl.BlockSpec((None, TQ, IDX_HEADS * IDX_DIM), lambda bi, i: (bi, i, A_IDX_Q // (IDX_HEADS * IDX_DIM))),
            seq(A_IDX_K),
            pl.BlockSpec((None, TQ, LANES), lambda bi, i: (bi, i, A_IDX_W // LANES)),
            pl.BlockSpec((None, TQ, 512), lambda bi, i: (bi, i, A_BZ // 512)),
            pl.BlockSpec(d_tiles.shape, lambda bi, i: (0, 0, 0)),
            pl.BlockSpec((TQ, TQ), lambda bi, i: (0, 0)),
        ],
        out_specs=pl.BlockSpec((None, TQ, 512), lambda bi, i: (bi, i, 0)),
        out_shape=jax.ShapeDtypeStruct((b, s, 512), F32),
        scratch_shapes=[pltpu.VMEM((rows, LANES), MXU_DTYPE),
                        pltpu.VMEM((irows, LANES), MXU_DTYPE),
                        pltpu.VMEM((nq, TQ, TQ), jnp.int32),
                        pltpu.VMEM((nq // 2, TQ, TQ), jnp.int32),
                        pltpu.VMEM((nq // 2, TQ, TQ), jnp.int32),
                        pltpu.VMEM((1, rows), F32),
                        pltpu.VMEM((LANES, rows), F32),
                        pltpu.VMEM((TQ, TQ), F32),
                        pltpu.VMEM((1, nq, LANES, TQ), MXU_DTYPE)],
        compiler_params=_params(2),
        name="dsa",
    )(u, u, u, u, u, u, d_tiles, tri)


def _causal_conv(x, xe_s, w_ref, b_ref):
    t = x.shape[0]
    xe_s[8:8 + t, :] = x
    y = b_ref[...] + w_ref[CONV_WIDTH - 1:CONV_WIDTH, :] * x
    for k in range(1, CONV_WIDTH):
        y = y + w_ref[CONV_WIDTH - 1 - k:CONV_WIDTH - k, :] * xe_s[8 - k:8 - k + t, :]
    xe_s[0:8, :] = x[t - 8:t, :]
    return y


def _rglru_kernel(x_ref, z_ref, cw_ref, cb_ref, wa_ref, ba_ref, wx_ref, bx_ref, lam_ref, o_ref,
                  xe_s, h_s):
    t, c = x_ref.shape

    @pl.when(pl.program_id(1) == 0)
    def _():
        xe_s[0:8, :] = jnp.zeros((8, c), F32)
        h_s[...] = jnp.zeros(h_s.shape, F32)

    xc = _causal_conv(x_ref[...], xe_s, cw_ref, cb_ref)
    xcm = _mx(xc)
    r = jax.nn.sigmoid(_dot(xcm, wa_ref[...]) + ba_ref[...])
    ig = jax.nn.sigmoid(_dot(xcm, wx_ref[...]) + bx_ref[...])
    nl = -lam_ref[...]
    softplus = jnp.maximum(nl, 0.0) + jnp.log1p(jnp.exp(-jnp.abs(nl)))
    log_a = (-LRU_C * r) * softplus
    a = jnp.exp(log_a)
    bb = jnp.sqrt(-jnp.tanh(log_a) * (a * a + 1.0)) * (ig * xc)
    row = lax.broadcasted_iota(jnp.int32, (t, c), 0) & (SUBLANES - 1)
    d = 1
    while d < SUBLANES:
        live = row >= d
        a_sh = jnp.where(live, pltpu.roll(a, d, 0), 1.0)
        b_sh = jnp.where(live, pltpu.roll(bb, d, 0), 0.0)
        bb = a * b_sh + bb
        a = a * a_sh
        d *= 2
    carry = h_s[0:1, :]
    groups = []
    for g in range(t // SUBLANES):
        rows = slice(g * SUBLANES, (g + 1) * SUBLANES)
        groups.append(a[rows] * carry + bb[rows])
        carry = groups[-1][SUBLANES - 1:SUBLANES, :]
    h = jnp.concatenate(groups, axis=0)
    h_s[0:1, :] = carry
    o_ref[...] = h * jax.nn.silu(z_ref[...])


def _rglru(u, conv_w, conv_b, wa_bd, ba, wx_bd, bx, lam, t=256):
    b, s, _ = u.shape
    c = LRU_WIDTH
    whole = lambda bi, i: (0, 0)
    return pl.pallas_call(
        _rglru_kernel,
        grid=(b, s // t),
        in_specs=[pl.BlockSpec((None, t, c), lambda bi, i: (bi, i, R_CX // c)),
                  pl.BlockSpec((None, t, c), lambda bi, i: (bi, i, R_CZ // c)),
                  pl.BlockSpec((CONV_WIDTH, c), whole), pl.BlockSpec((1, c), whole),
                  pl.BlockSpec((c, c), whole), pl.BlockSpec((1, c), whole),
                  pl.BlockSpec((c, c), whole), pl.BlockSpec((1, c), whole),
                  pl.BlockSpec((1, c), whole)],
        out_specs=pl.BlockSpec((None, t, c), lambda bi, i: (bi, i, 0)),
        out_shape=jax.ShapeDtypeStruct((b, s, c), F32),
        scratch_shapes=[pltpu.VMEM((t + 8, c), F32), pltpu.VMEM((8, c), F32)],
        compiler_params=_params(2),
        name="rglru",
    )(u, u, conv_w, conv_b.reshape(1, c), wa_bd, ba.reshape(1, c), wx_bd, bx.reshape(1, c),
      lam.reshape(1, c))


def _mlstm_kernel(qk_ref, v_ref, g_ref, og_ref, z_ref, cw_ref, cb_ref, gb_ref, o_ref,
                  xe_s, c_s, m_s):
    t = qk_ref.shape[0]
    L = MLSTM_CHUNK
    H = MLSTM_HEADS
    D = MLSTM_DIM

    @pl.when(pl.program_id(1) == 0)
    def _():
        xe_s[0:8, :] = jnp.zeros((8, xe_s.shape[1]), F32)
        c_s[...] = jnp.zeros(c_s.shape, F32)
        m_s[...] = jnp.zeros(m_s.shape, F32)

    qk = jax.nn.silu(_causal_conv(qk_ref[...], xe_s, cw_ref, cb_ref))

    lane = lax.broadcasted_iota(jnp.int32, (t, LANES), 1)
    row_in = lax.broadcasted_iota(jnp.int32, (t, LANES), 0) & (L - 1)
    gs = g_ref[...] + gb_ref[...]
    log_sig = jnp.minimum(gs, 0.0) - jnp.log1p(jnp.exp(-jnp.abs(gs)))
    pre = jnp.where(lane < H, gs, log_sig)
    cum = pre
    d = 1
    while d < L:
        cum = cum + jnp.where(row_in >= d, pltpu.roll(cum, d, 0), 0.0)
        d *= 2
    comb = jnp.where(lane < H, pre, cum)
    comb_t = comb.T

    causal_t = (lax.broadcasted_iota(jnp.int32, (L, L), 0) <= lax.broadcasted_iota(jnp.int32, (L, L), 1))
    pad_rows = jnp.zeros((7, L), F32)
    v_ts = [v_ref[:, h * D:(h + 1) * D].T for h in range(H)]
    states = [c_s[h] for h in range(H)]
    m_prevs = [m_s[h:h + 1, 0:1] for h in range(H)]
    outs = [[] for _ in range(H)]
    for c in range(t // L):
        r = slice(c * L, (c + 1) * L)
        for h in range(H):
            v_t, state, m_prev = v_ts[h], states[h], m_prevs[h]
            q_ = _mx(qk[r, h * D:(h + 1) * D])
            k_ = _mx(qk[r, (H + h) * D:(H + h + 1) * D] * (D ** -0.5))
            li_row = comb_t[h:h + 1, r]
            b_row = comb_t[H + h:H + h + 1, r]
            lb_col = comb[r, h:h + 1] - comb[r, H + h:H + h + 1]
            b_last = b_row[:, L - 1:L]

            dmat_t = jnp.where(causal_t, b_row + lb_col, -jnp.inf)
            decay_row = b_last - b_row + li_row
            m_new = jnp.maximum(b_last + m_prev, jnp.max(decay_row, axis=-1, keepdims=True))
            inter = b_row + m_prev
            m_t = jnp.maximum(inter, jnp.max(dmat_t, axis=0, keepdims=True))
            w_t = _dot_nt(k_, q_) * jnp.exp(dmat_t - m_t)
            prev = jnp.exp(inter - m_t)

            read = _dot_nt(_mx(state), q_)
            num_t = prev * read[:D] + _dot(_mx(v_t[:, r]), _mx(w_t))
            den = prev * read[D:D + 1] + jnp.sum(w_t, axis=0, keepdims=True)
            outs[h].append(num_t / jnp.maximum(jnp.abs(den), jnp.exp(-m_t)))

            wk = jnp.exp(decay_row - m_new)
            inc = jnp.concatenate([v_t[:, r] * wk, wk, pad_rows], axis=0)
            states[h] = jnp.exp(b_last + m_prev - m_new) * state + _dot(_mx(inc), k_)
            m_prevs[h] = m_new

    for h in range(H):
        c_s[h] = states[h]
        m_s[h:h + 1, :] = jnp.broadcast_to(m_prevs[h], (1, LANES))
        out = jnp.concatenate(outs[h], axis=1).T
        cols = slice(h * D, (h + 1) * D)
        o_ref[:, cols] = jax.nn.sigmoid(og_ref[:, cols]) * out * jax.nn.silu(z_ref[:, cols])


def _mlstm(u, conv_w, conv_b, gate_bias, t=128):
    b, s, _ = u.shape
    w = MLSTM_WIDTH
    whole = lambda bi, i: (0, 0)
    return pl.pallas_call(
        _mlstm_kernel,
        grid=(b, s // t),
        in_specs=[pl.BlockSpec((None, t, 2 * w), lambda bi, i: (bi, i, R_QK // (2 * w))),
                  pl.BlockSpec((None, t, w), lambda bi, i: (bi, i, R_V // w)),
                  pl.BlockSpec((None, t, LANES), lambda bi, i: (bi, i, R_GATE // LANES)),
                  pl.BlockSpec((None, t, w), lambda bi, i: (bi, i, R_O // w)),
                  pl.BlockSpec((None, t, w), lambda bi, i: (bi, i, R_Z // w)),
                  pl.BlockSpec((CONV_WIDTH, 2 * w), whole), pl.BlockSpec((1, 2 * w), whole),
                  pl.BlockSpec((1, LANES), whole)],
        out_specs=pl.BlockSpec((None, t, w), lambda bi, i: (bi, i, 0)),
        out_shape=jax.ShapeDtypeStruct((b, s, w), F32),
        scratch_shapes=[pltpu.VMEM((t + 8, 2 * w), F32),
                        pltpu.VMEM((MLSTM_HEADS, MLSTM_DIM + 8, MLSTM_DIM), F32),
                        pltpu.VMEM((8, LANES), F32)],
        compiler_params=_params(2),
        name="mlstm",
    )(u, u, u, u, u, conv_w, conv_b.reshape(1, 2 * w), gate_bias)


def _attn_weights(w_in):
    d = w_in.shape[0]
    widths = (512, 768, 24, 512, 512, 64, 64, 256, 64, 4, 512)
    offs = np.concatenate([[0], np.cumsum(widths)])
    a_q, a_kv, a_g, a_z, b_q, b_k, b_v, b_qi, b_ki, b_wi, b_z = [
        w_in[:, offs[k]:offs[k + 1]] for k in range(len(widths))]
    scale = HEAD_DIM ** -0.5 * LOG2E
    zeros = lambda n: jnp.zeros((d, n), w_in.dtype)
    cols = [
        a_q * scale, b_q * scale, a_z, b_z, b_qi, a_kv,
        a_g, zeros(LANES - 24),
        b_k, b_v,
        b_ki, zeros(LANES - IDX_DIM),
        b_wi * (IDX_DIM ** -0.5 * IDX_HEADS ** -0.5), zeros(LANES - IDX_HEADS),
    ]
    w = jnp.concatenate(cols, axis=1)
    assert w.shape[1] == A_TOTAL
    return _mx(w)


def _rec_weights(w_in):
    d = w_in.shape[0]
    widths = (512, 512, 512, 512, 512, 4, 4, 512, 512)
    offs = np.concatenate([[0], np.cumsum(widths)])
    c_x, c_z, d_q, d_k, d_v, d_i, d_f, d_o, d_z = [w_in[:, offs[k]:offs[k + 1]] for k in range(len(widths))]
    w = jnp.concatenate([c_x, c_z, d_q, d_k, d_v, d_o, d_z, d_i, d_f,
                         jnp.zeros((d, R_TOTAL - R_GATE - 2 * MLSTM_HEADS), w_in.dtype)], axis=1)
    assert w.shape[1] == R_TOTAL
    return _mx(w)


def _block_diag(w):
    g, n, _ = w.shape
    eye = jnp.eye(g, dtype=w.dtype)
    return (eye[:, None, :, None] * w[:, :, None, :]).reshape(g * n, g * n)


def _attention_layer(x2d, b, s, norm_g, w_in, cmp_pos_k, cmp_w1_k, cmp_w2_k, cmp_pos_v, cmp_w1_v, cmp_w2_v,
                     t5_table):
    nq = s // TQ
    n16 = s // CMP_STRIDE
    nb = s // SEL_BLOCK
    u = _norm_proj(x2d, norm_g, _attn_weights(w_in)).reshape(b, s, A_TOTAL)

    pos = jnp.stack([cmp_pos_k, cmp_pos_v])
    pos = jnp.concatenate([pos] * NSA_GROUPS, axis=-1).reshape(2, CMP_BLOCK, 1, LANES)
    w1 = jnp.stack([cmp_w1_k, cmp_w1_v]).reshape(2, CMP_BLOCK, HEAD_DIM, CMP_HIDDEN)
    z1 = jnp.zeros_like(w1)
    w1 = _mx(jnp.stack([jnp.concatenate([w1, z1], axis=2), jnp.concatenate([z1, w1], axis=2)], axis=1))
    w2 = jnp.stack([cmp_w2_k, cmp_w2_v])
    zpad = jnp.zeros_like(w2)
    w2 = _mx(jnp.stack([jnp.concatenate([w2, zpad], -1), jnp.concatenate([zpad, w2], -1)], axis=1))
    cmp_kv = _compress(u, pos, w1, w2)

    tbl = t5_table.astype(F32)
    tbl_a, tbl_b = tbl[:, :NSA_HEADS], tbl[:, NSA_HEADS:]
    kj = np.arange(TQ)[:, None]
    qi = np.arange(TQ)[None, :]
    assert nb <= LANES
    e_t = jnp.asarray((np.arange(s)[:, None] // SEL_BLOCK) == np.arange(LANES)[None, :], MXU_DTYPE)
    ci = np.arange(n16)[None, :]
    sj = np.arange(nb)[:, None]
    ov = jnp.asarray((ci * CMP_STRIDE < (sj + 1) * SEL_BLOCK) & (ci * CMP_STRIDE + CMP_BLOCK > sj * SEL_BLOCK)
                     & (ci < n16 - 1), MXU_DTYPE)
    tri = jnp.asarray(qi <= kj, MXU_DTYPE)

    ya = _nsa(u, cmp_kv, _cmp_bias(tbl_a, nq, n16), _near_tiles(tbl_a), e_t, ov, min(N_SEL, nb))
    yb = _dsa(u, _near_tiles(tbl_b), tri, min(DSA_TOPK_MAX, s // 4))
    return ya.reshape(b * s, -1), yb.reshape(b * s, -1)


def _recurrent_layer(x2d, b, s, norm_g, w_in, conv_c_w, conv_c_b, wa, ba, wx, bx, lam,
                     conv_d_w, conv_d_b, b_i, b_f):
    u = _norm_proj(x2d, norm_g, _rec_weights(w_in)).reshape(b, s, R_TOTAL)
    yc = _rglru(u, conv_c_w, conv_c_b, _mx(_block_diag(wa)), ba, _mx(_block_diag(wx)), bx, lam)
    gate_bias = jnp.concatenate([b_i, b_f, jnp.zeros((LANES - 2 * MLSTM_HEADS,), F32)]).reshape(1, LANES)
    yd = _mlstm(u, conv_d_w, conv_d_b, gate_bias)
    return yc.reshape(b * s, -1), yd.reshape(b * s, -1)


def kernel(x, p, norm_g, final_g, ple_w, ple_gate_w, t5_table, attn_w_in, attn_w_out, cmp_pos_k, cmp_w1_k, cmp_w2_k, cmp_pos_v, cmp_w1_v, cmp_w2_v, rec_w_in, rec_w_out, lru_conv_w, lru_conv_b, lru_wa, lru_ba, lru_wx, lru_bx, lru_lambda, mlstm_conv_w, mlstm_conv_b, mlstm_b_i, mlstm_b_f):
    b, s, d = x.shape
    depth = p.shape[0]
    x2d = x.reshape(b * s, d)
    for i in range(depth):
        j = i // 2
        if i % 2 == 0:
            ya, yb = _attention_layer(x2d, b, s, norm_g[i], attn_w_in[j], cmp_pos_k[j], cmp_w1_k[j], cmp_w2_k[j],
                                      cmp_pos_v[j], cmp_w1_v[j], cmp_w2_v[j], t5_table)
            w_out = attn_w_out[j]
        else:
            ya, yb = _recurrent_layer(x2d, b, s, norm_g[i], rec_w_in[j], lru_conv_w[j], lru_conv_b[j],
                                      lru_wa[j], lru_ba[j], lru_wx[j], lru_bx[j], lru_lambda[j],
                                      mlstm_conv_w[j], mlstm_conv_b[j], mlstm_b_i[j], mlstm_b_f[j])
            w_out = rec_w_out[j]
        x2d = _out_proj(x2d, ya, yb, p.reshape(depth, b * s, -1), i, w_out, ple_w[i], ple_gate_w[i], final_g,
                        final=(i == depth - 1))
    return x2d.reshape(b, s, d)
```

```python
import functools
import math

import numpy as np
import jax
import jax.numpy as jnp
from jax import lax
from jax.experimental import pallas as pl
from jax.experimental.pallas import tpu as pltpu

F32 = jnp.float32
MXU_DTYPE = jnp.bfloat16

HEAD_DIM = 64
NSA_HEADS = 8
NSA_GROUPS = 2
NSA_HPG = NSA_HEADS // NSA_GROUPS
CMP_BLOCK = 32
CMP_STRIDE = 16
CMP_HIDDEN = 256
SEL_BLOCK = 64
N_SEL = 8
WINDOW = 512
FORCE_BONUS = 1e4
DSA_HEADS = 8
IDX_HEADS = 4
IDX_DIM = 64
DSA_TOPK_MAX = 256
N_BUCKETS = 32
T5_MAX_DIST = 128
LRU_WIDTH = 512
CONV_WIDTH = 4
LRU_C = 8.0
MLSTM_HEADS = 4
MLSTM_DIM = 128
MLSTM_WIDTH = MLSTM_HEADS * MLSTM_DIM
MLSTM_CHUNK = 64
RMS_EPS = 1e-6
NEG = -1e30

LANES = 128
SUBLANES = 8
COL_BLOCK = 512
TQ = 128
INT_MIN = -2 ** 31
LOG2E = math.log2(math.e)
VMEM_LIMIT = 48 * 1024 * 1024

A_NSA_Q = 0
A_DSA_Q = 512
A_AZ = 1024
A_BZ = 1536
A_IDX_Q = 2048
A_KV = 2304
A_GATE = 3072
A_DSA_KV = 3200
A_IDX_K = 3328
A_IDX_W = 3456
A_TOTAL = 3584

R_CX = 0
R_CZ = 512
R_QK = 1024
R_V = 2048
R_O = 2560
R_Z = 3072
R_GATE = 3584
R_TOTAL = 3840


def _dot(a, b):
    return jnp.dot(a, b, preferred_element_type=F32)


def _dot_nt(a, b):
    return lax.dot_general(a, b, (((1,), (1,)), ((), ())), preferred_element_type=F32)


def _mx(a):
    return a.astype(MXU_DTYPE)


def _params(n_grid):
    return pltpu.CompilerParams(dimension_semantics=("arbitrary",) * n_grid,
                                vmem_limit_bytes=VMEM_LIMIT)


def _norm_proj_kernel(x_ref, g_ref, w_ref, o_ref):
    x = x_ref[...]
    ms = jnp.mean(x * x, axis=-1, keepdims=True)
    y = x * lax.rsqrt(ms + RMS_EPS) * g_ref[...]
    o_ref[...] = _dot(_mx(y), w_ref[...])


def _norm_proj(x2d, g, w, tm=512):
    m, d = x2d.shape
    n = w.shape[1]
    return pl.pallas_call(
        _norm_proj_kernel,
        grid=(m // tm,),
        in_specs=[pl.BlockSpec((tm, d), lambda i: (i, 0)),
                  pl.BlockSpec((1, d), lambda i: (0, 0)),
                  pl.BlockSpec((d, n), lambda i: (0, 0))],
        out_specs=pl.BlockSpec((tm, n), lambda i: (i, 0)),
        out_shape=jax.ShapeDtypeStruct((m, n), F32),
        compiler_params=_params(1),
        name="norm_proj",
    )(x2d, g.reshape(1, d), w)


def _out_proj_kernel(x_ref, ya_ref, yb_ref, p_ref, wo_ref, pw_ref, gw_ref, fg_ref, o_ref, *, final):
    half = ya_ref.shape[-1]
    y = _dot(_mx(ya_ref[...]), wo_ref[:half, :]) + _dot(_mx(yb_ref[...]), wo_ref[half:, :])
    x1 = x_ref[...] + y
    gate = jax.nn.sigmoid(_dot(_mx(x1), gw_ref[...]))
    x2 = x1 + _dot(_mx(p_ref[...]), pw_ref[...]) * gate
    if final:
        ms = jnp.mean(x2 * x2, axis=-1, keepdims=True)
        x2 = x2 * lax.rsqrt(ms + RMS_EPS) * fg_ref[...]
    o_ref[...] = x2


def _out_proj(x2d, ya, yb, p_all, layer, w_out, ple_w, gate_w, final_g, final, tm=1024):
    m, d = x2d.shape
    half = ya.shape[1]
    pd = p_all.shape[2]
    row = lambda i: (i, 0)
    whole = lambda i: (0, 0)
    return pl.pallas_call(
        functools.partial(_out_proj_kernel, final=final),
        grid=(m // tm,),
        in_specs=[pl.BlockSpec((tm, d), row), pl.BlockSpec((tm, half), row),
                  pl.BlockSpec((tm, half), row), pl.BlockSpec((None, tm, pd), lambda i: (layer, i, 0)),
                  pl.BlockSpec((2 * half, d), whole), pl.BlockSpec((pd, d), whole),
                  pl.BlockSpec((d, d), whole), pl.BlockSpec((1, d), whole)],
        out_specs=pl.BlockSpec((tm, d), row),
        out_shape=jax.ShapeDtypeStruct((m, d), F32),
        compiler_params=_params(1),
        name="out_proj",
    )(x2d, ya, yb, p_all, _mx(w_out), _mx(ple_w), _mx(gate_w), final_g.reshape(1, d))


def _bucket_np(n):
    n = np.asarray(n)
    max_exact = N_BUCKETS // 2
    nf = np.maximum(n, 1).astype(np.float32)
    large = max_exact + (np.log(nf / np.float32(max_exact)) / np.float32(math.log(T5_MAX_DIST / max_exact))
                         * np.float32(N_BUCKETS - max_exact)).astype(np.int32)
    large = np.minimum(large, N_BUCKETS - 1)
    return np.where(n < max_exact, n, large)


def _bias_index(dist):
    dist = np.asarray(dist)
    return np.where(dist >= 0, _bucket_np(np.maximum(dist, 0)), N_BUCKETS).astype(np.int32)


def _lookup(ext, idx):
    a, b = idx.shape
    rows = jnp.arange(ext.shape[0], dtype=jnp.int32)[:, None]
    onehot = (rows == jnp.asarray(idx.reshape(1, -1), jnp.int32)).astype(F32)
    out = jnp.dot(ext.T, onehot, precision=lax.Precision.HIGHEST)
    return out.reshape(ext.shape[1], a, b).transpose(1, 0, 2).reshape(a, -1)


def _near_tiles(tbl):
    h = tbl.shape[1]
    far = tbl[N_BUCKETS - 1]
    ext = jnp.concatenate([tbl, jnp.full((1, h), NEG, F32)], axis=0)
    kj = np.arange(TQ)[:, None]
    qi = np.arange(TQ)[None, :]
    far = jnp.repeat(far, TQ)[None, :]
    d0 = (_lookup(ext, _bias_index(qi - kj)) - far) * LOG2E
    d1 = (_lookup(ext, _bias_index(TQ + qi - kj)) - far) * LOG2E
    edge = jnp.asarray(np.tile(np.where(kj > qi, 0.0, NEG), (1, h)), F32)
    return jnp.stack([jnp.zeros_like(d0), d1, d0, jnp.full_like(d0, NEG), edge])


def _cmp_bias(tbl, nq, n16):
    h = tbl.shape[1]
    ext = jnp.concatenate([tbl, jnp.full((1, h), NEG, F32)], axis=0)
    off = (TQ // CMP_STRIDE) * (nq - 1)
    cc = np.arange(n16 + off)[:, None] - off
    qi = np.arange(TQ)[None, :]
    return _lookup(ext, _bias_index(qi - CMP_STRIDE * cc - (CMP_BLOCK - 1))) * LOG2E


def _compress_kernel(x_ref, pos_ref, w1_ref, w2_ref, o_ref):
    n16 = o_ref.shape[0]
    acc = None
    for g in range(NSA_GROUPS):
        pa = pb = None
        for l in range(CMP_STRIDE):
            x = x_ref[pl.ds(l, n16, stride=CMP_STRIDE), :]
            a = _dot(_mx(x + pos_ref[l]), w1_ref[g, l])
            c = _dot(_mx(x + pos_ref[CMP_STRIDE + l]), w1_ref[g, CMP_STRIDE + l])
            pa = a if pa is None else pa + a
            pb = c if pb is None else pb + c
        pre = pa + pltpu.roll(pb, n16 - 1, 0)
        t = _dot(_mx(jax.nn.silu(pre)), w2_ref[g])
        acc = t if acc is None else acc + t
    o_ref[...] = acc


def _compress(u, pos, w1, w2):
    b, s, _ = u.shape
    n16 = s // CMP_STRIDE
    return pl.pallas_call(
        _compress_kernel,
        grid=(2, b),
        in_specs=[pl.BlockSpec((None, s, LANES), lambda k, i: (i, 0, A_KV // LANES + k)),
                  pl.BlockSpec((None, CMP_BLOCK, 1, LANES), lambda k, i: (k, 0, 0, 0)),
                  pl.BlockSpec((None, NSA_GROUPS, CMP_BLOCK, LANES, CMP_HIDDEN), lambda k, i: (k, 0, 0, 0, 0)),
                  pl.BlockSpec((None, NSA_GROUPS, CMP_HIDDEN, LANES), lambda k, i: (k, 0, 0, 0))],
        out_specs=pl.BlockSpec((None, None, n16, LANES), lambda k, i: (k, i, 0, 0)),
        out_shape=jax.ShapeDtypeStruct((2, b, n16, LANES), F32),
        compiler_params=_params(2),
        name="compress",
    )(u, pos, w1, w2)


def _softmax_reset(m_s, acc_s):
    m_s[...] = jnp.full(m_s.shape, -jnp.inf, F32)
    acc_s[...] = jnp.zeros(acc_s.shape, F32)


def _softmax_step(score_fn, values, m_s, acc_s, col_block):
    r = m_s.shape[1]
    assert (r // len(values)) % col_block == 0, "a column block must not straddle two value operands"
    s_next = score_fn(0, col_block)
    for c0 in range(0, r, col_block):
        cols = slice(c0, c0 + col_block)
        s = s_next
        if c0 + col_block < r:
            s_next = score_fn(c0 + col_block, c0 + 2 * col_block)
        m_prev = m_s[:, cols]
        m_new = jnp.maximum(m_prev, jnp.max(s, axis=0, keepdims=True))
        alpha = jnp.exp2(m_prev - m_new)
        p = _mx(jnp.exp2(s - m_new))
        v = values[c0 * len(values) // r]
        acc_s[:, cols] = alpha * acc_s[:, cols] + _dot(v, p)
        m_s[:, cols] = m_new


def _store_value_tiles(v_ref, vt_s, ones_hi):
    row = lax.broadcasted_iota(jnp.int32, (LANES, TQ), 0)

    def body(j, carry):
        vt = v_ref[pl.ds(pl.multiple_of(j * TQ, TQ), TQ), :].T
        for k, hi in enumerate(ones_hi):
            vt_s[k, j] = _mx(jnp.where((row >= HEAD_DIM) == hi, 1.0, vt))
        return carry

    lax.fori_loop(0, vt_s.shape[1], body, 0)


def _value_tiles(vt_s, k, j0, ntile):
    tiles = [vt_s[k, j0 + t] for t in range(ntile)]
    return tiles[0] if ntile == 1 else jnp.concatenate(tiles, axis=1)


def _near_bias(d_ref, idx_of_tile, j0, ntile, c0, c1):
    tiles = [d_ref[idx_of_tile(j0 + t), :, c0:c1] for t in range(ntile)]
    return tiles[0] if ntile == 1 else jnp.concatenate(tiles, axis=0)


def _stack_heads(q_ref, hi):
    lane = lax.broadcasted_iota(jnp.int32, (TQ, LANES), 1)
    out = []
    for h, up in enumerate(hi):
        v = q_ref[:, (h // 2) * LANES:(h // 2 + 1) * LANES]
        if (h % 2 == 1) != up:
            v = pltpu.roll(v, HEAD_DIM, 1)
        out.append(jnp.where((lane >= HEAD_DIM) == up, v, 0.0))
    return jnp.concatenate(out, axis=0)


def _last_chunk(chunk_fn, i, last, ch):
    if ch < 2:
        chunk_fn(last, True)
        return
    short = (i - last * ch) < ch // 2

    @pl.when(short)
    def _():
        chunk_fn(last, True, ch // 2)

    @pl.when(jnp.logical_not(short))
    def _():
        chunk_fn(last, True)


def _tiles_per_chunk(nq):
    for ch in (4, 2):
        if nq % ch == 0:
            return ch
    return 1


def _nsa_kernel(q_ref, kc_ref, vc_ref, ks_ref, vs_ref, kw_ref, vw_ref, g_ref, z_ref,
                bc_ref, d_ref, et_ref, ov_ref, o_ref,
                qa_s, m_s, acc_s, osel_s, vct_s, vst_s, vwt_s, *, n_pick, ch, n_win_tiles):
    i = pl.program_id(1)
    rows = NSA_HEADS * TQ
    nb = ov_ref.shape[0]
    q0 = i * TQ
    half = LANES // NSA_GROUPS

    @pl.when(i == 0)
    def _():
        vct_s[...] = _mx(vc_ref[...].T)
        _store_value_tiles(vs_ref, vst_s, (True, False))
        _store_value_tiles(vw_ref, vwt_s, (True, False))

    qs = _mx(_stack_heads(q_ref, [h >= NSA_HPG for h in range(NSA_HEADS)]))
    qa_s[:, :LANES] = qs

    s = _dot_nt(_mx(kc_ref[...]), qs) + bc_ref[...]
    p = jnp.exp2(s - jnp.max(s, axis=0, keepdims=True))
    t_lane = q0 + (lax.broadcasted_iota(jnp.int32, (1, rows), 1) & (TQ - 1))
    any_valid = jnp.where(t_lane >= CMP_BLOCK - 1, 1.0, 0.0)
    p = p * (any_valid / jnp.sum(p, axis=0, keepdims=True))
    oc_t = _dot(vct_s[...], _mx(p))

    blk = lax.broadcasted_iota(jnp.int32, (nb, TQ), 0)
    t = q0 + lax.broadcasted_iota(jnp.int32, (nb, TQ), 1)
    cur = lax.shift_right_logical(t, int(math.log2(SEL_BLOCK)))
    forced = jnp.where(blk == 0, 1.0, jnp.where(blk == cur, 1.0, jnp.where(blk == cur - 1, 1.0, 0.0)))
    admissible = blk * SEL_BLOCK <= t
    ov = ov_ref[...]
    for g in range(NSA_GROUPS):
        c0 = g * NSA_HPG * TQ
        ps = p[:, c0:c0 + TQ]
        for h in range(1, NSA_HPG):
            ps = ps + p[:, c0 + h * TQ:c0 + (h + 1) * TQ]
        hi = _mx(ps)
        lo = _mx(ps - hi.astype(F32))
        imp_t = _dot(ov, hi) + _dot(ov, lo)
        score = jnp.where(admissible, imp_t + FORCE_BONUS * forced, NEG)
        sel = jnp.zeros((nb, TQ), F32)
        for _ in range(n_pick):
            best = jnp.max(score, axis=0, keepdims=True)
            first = jnp.min(jnp.where(score == best, blk, nb), axis=0, keepdims=True)
            hit = blk == first
            sel = jnp.where(hit, 1.0, sel)
            score = jnp.where(hit, -jnp.inf, score)
        if nb < LANES:
            sel = jnp.concatenate([sel, jnp.ones((LANES - nb, TQ), F32)], axis=0)
        block_mask = _mx((sel.T - 1.0) * (-NEG))
        for h in range(NSA_HPG):
            r0 = (g * NSA_HPG + h) * TQ
            qa_s[r0:r0 + TQ, LANES:] = block_mask

    def values(vt_s, j0, ntile):
        return [_value_tiles(vt_s, 0, j0, ntile), _value_tiles(vt_s, 1, j0, ntile)]

    _softmax_reset(m_s, acc_s)

    def sel_chunk(c, near, ntile=ch):
        tk = ntile * TQ
        koff = pl.multiple_of(c * (ch * TQ), ch * TQ)
        k_aug = jnp.concatenate([_mx(ks_ref[pl.ds(koff, tk), :]), et_ref[pl.ds(koff, tk), :]], axis=1)

        def scores(c0, c1):
            s = _dot_nt(k_aug, qa_s[c0:c1, :])
            if near:
                s = s + _near_bias(d_ref, lambda j: jnp.clip(j - (i - 2), 0, 3), c * ch, ntile, c0, c1)
            return s

        _softmax_step(scores, values(vst_s, c * ch, ntile), m_s, acc_s, COL_BLOCK)

    first_near = jnp.maximum(i - 1, 0) // ch
    last = i // ch

    def far_body(c, carry):
        sel_chunk(c, False)
        return carry

    def near_body(c, carry):
        sel_chunk(c, True)
        return carry

    lax.fori_loop(0, first_near, far_body, 0)
    lax.fori_loop(first_near, last, near_body, 0)
    _last_chunk(sel_chunk, i, last, ch)
    osel_s[...] = acc_s[...]

    _softmax_reset(m_s, acc_s)
    n_win = WINDOW // TQ
    jw = jnp.maximum(i - (n_win_tiles - 1), 0)
    koff = pl.multiple_of(jw * TQ, TQ)
    tk = n_win_tiles * TQ

    def win_idx(j):
        r = j - (i - n_win)
        return jnp.where(r == 0, 4, jnp.clip(r - (n_win - 2), 0, 3))

    kw = _mx(kw_ref[pl.ds(koff, tk), :])
    _softmax_step(lambda c0, c1: (_dot_nt(kw, qa_s[c0:c1, :LANES])
                                  + _near_bias(d_ref, win_idx, jw, n_win_tiles, c0, c1)),
                  values(vwt_s, jw, n_win_tiles), m_s, acc_s, COL_BLOCK)
    ow_t = acc_s[...]
    os_t = osel_s[...]

    gate_t = jax.nn.sigmoid(g_ref[...]).T
    for m in range(NSA_HEADS // 2):
        parts = []
        for hh in (2 * m, 2 * m + 1):
            g = hh // NSA_HPG
            r = slice(g * half, (g + 1) * half)
            d = (1 - g) * half
            c = slice(hh * TQ, (hh + 1) * TQ)
            parts.append(gate_t[3 * hh:3 * hh + 1, :] * oc_t[r, c]
                         + gate_t[3 * hh + 1:3 * hh + 2, :] * (os_t[r, c] / os_t[d:d + 1, c])
                         + gate_t[3 * hh + 2:3 * hh + 3, :] * (ow_t[r, c] / ow_t[d:d + 1, c]))
        slab = jnp.concatenate(parts, axis=0).T
        o_ref[:, m * LANES:(m + 1) * LANES] = slab * jax.nn.silu(z_ref[:, m * LANES:(m + 1) * LANES])


def _nsa(u, cmp_kv, bias_c, d_tiles, e_t, ov, n_pick):
    b, s, _ = u.shape
    nq = s // TQ
    n16 = cmp_kv.shape[2]
    nb = ov.shape[0]
    rows = NSA_HEADS * TQ
    n_win_tiles = min(WINDOW // TQ + 1, nq)
    seq = lambda col: pl.BlockSpec((None, s, LANES), lambda bi, i, col=col: (bi, 0, col // LANES))
    return pl.pallas_call(
        functools.partial(_nsa_kernel, n_pick=n_pick, ch=_tiles_per_chunk(nq), n_win_tiles=n_win_tiles),
        grid=(b, nq),
        in_specs=[
            pl.BlockSpec((None, TQ, NSA_HEADS * HEAD_DIM), lambda bi, i: (bi, i, A_NSA_Q // (NSA_HEADS * HEAD_DIM))),
            pl.BlockSpec((None, None, n16, LANES), lambda bi, i: (0, bi, 0, 0)),
            pl.BlockSpec((None, None, n16, LANES), lambda bi, i: (1, bi, 0, 0)),
            seq(A_KV + 2 * LANES), seq(A_KV + 3 * LANES), seq(A_KV + 4 * LANES), seq(A_KV + 5 * LANES),
            pl.BlockSpec((None, TQ, LANES), lambda bi, i: (bi, i, A_GATE // LANES)),
            pl.BlockSpec((None, TQ, 512), lambda bi, i: (bi, i, A_AZ // 512)),
            pl.BlockSpec((pl.Element(n16), pl.Element(rows)),
                         lambda bi, i: ((TQ // CMP_STRIDE) * (nq - 1 - i), 0)),
            pl.BlockSpec(d_tiles.shape, lambda bi, i: (0, 0, 0)),
            pl.BlockSpec((s, LANES), lambda bi, i: (0, 0)),
            pl.BlockSpec((nb, n16), lambda bi, i: (0, 0)),
        ],
        out_specs=pl.BlockSpec((None, TQ, 512), lambda bi, i: (bi, i, 0)),
        out_shape=jax.ShapeDtypeStruct((b, s, 512), F32),
        scratch_shapes=[pltpu.VMEM((rows, 2 * LANES), MXU_DTYPE),
                        pltpu.VMEM((1, rows), F32),
                        pltpu.VMEM((LANES, rows), F32),
                        pltpu.VMEM((LANES, rows), F32),
                        pltpu.VMEM((LANES, n16), MXU_DTYPE),
                        pltpu.VMEM((NSA_GROUPS, nq, LANES, TQ), MXU_DTYPE),
                        pltpu.VMEM((NSA_GROUPS, nq, LANES, TQ), MXU_DTYPE)],
        compiler_params=_params(2),
        name="nsa",
    )(u, cmp_kv, cmp_kv, u, u, u, u, u, u, bias_c, d_tiles, e_t, ov)


def _dsa_kernel(q_ref, kv_ref, qi_ref, ki_ref, wi_ref, z_ref, d_ref, tri_ref, o_ref,
                qs_s, qis_s, sc_s, hi_s, lo_s, m_s, acc_s, seen_s, vt_s, *, k_top, ch):
    i = pl.program_id(1)

    @pl.when(i == 0)
    def _():
        _store_value_tiles(kv_ref, vt_s, (False,))

    qs_s[...] = _mx(_stack_heads(q_ref, [False] * DSA_HEADS))
    qis_s[...] = _mx(_stack_heads(qi_ref, [False] * IDX_HEADS))
    wi_t = wi_ref[...].T
    w_rows = [jnp.broadcast_to(wi_t[h:h + 1, :], (TQ, TQ)) for h in range(IDX_HEADS)]
    key_idx = lax.broadcasted_iota(jnp.int32, (TQ, TQ), 0)
    q_idx = lax.broadcasted_iota(jnp.int32, (TQ, TQ), 1)
    n_chunks = (i + ch) // ch
    pairs = ch // 2

    def score_chunk(c, last):
        koff = pl.multiple_of(c * (ch * TQ), ch * TQ)
        r = jnp.maximum(_dot_nt(_mx(ki_ref[pl.ds(koff, ch * TQ), :]), qis_s[...]), 0.0)
        keys = []
        for t in range(ch):
            jt = c * ch + t
            rt = r[t * TQ:(t + 1) * TQ]
            sc = w_rows[0] * rt[:, 0:TQ]
            for h in range(1, IDX_HEADS):
                sc = sc + w_rows[h] * rt[:, h * TQ:(h + 1) * TQ]
            if last:
                sc = jnp.where((jt == i) & (key_idx > q_idx), NEG, sc)
            bits = pltpu.bitcast(sc, jnp.int32)
            key = bits ^ (lax.shift_right_arithmetic(bits, 31) & jnp.int32(0x7FFFFFFF))
            key = jnp.where(sc == 0.0, 0, key)
            if last:
                key = jnp.where(jt > i, INT_MIN, key)
            keys.append(key)
            sc_s[jt] = key
        for pr in range(pairs):
            a, b = keys[2 * pr], keys[2 * pr + 1]
            hi_s[c * pairs + pr] = (a & jnp.int32(-65536)) | lax.shift_right_logical(b, 16)
            lo_s[c * pairs + pr] = (lax.shift_left((a & 0xFFFF) ^ 0x8000, 16)
                                    | ((b & 0xFFFF) ^ 0x8000))

    def earlier_chunk(c, carry):
        score_chunk(c, False)
        return carry

    lax.fori_loop(0, n_chunks - 1, earlier_chunk, 0)
    score_chunk(n_chunks - 1, True)

    def halves(words):
        return pltpu.bitcast(words, jnp.int16)

    def both_halves(v):
        w = lax.shift_left(v, 16) | (v & 0xFFFF)
        return halves(jnp.broadcast_to(w, (TQ, TQ)))

    def count16(words_s, pred):
        def body(c, acc):
            for pr in range(pairs):
                acc = acc + jnp.where(pred(halves(words_s[c * pairs + pr])), jnp.int16(1), jnp.int16(0))
            return acc
        acc = lax.fori_loop(0, n_chunks, body, jnp.zeros((2 * TQ, TQ), jnp.int16))
        rows = 2 * TQ
        while rows > 2 * SUBLANES:
            rows //= 2
            acc = acc[:rows] + acc[rows:]
        return jnp.sum(acc.astype(jnp.int32), axis=0, keepdims=True)

    def bisect16(words_s, target):
        c0 = count16(words_s, lambda x: x >= jnp.int16(0))
        ok0 = c0 >= target

        def step(it, carry):
            v, n_gt = carry
            cand = v | lax.shift_left(jnp.int32(1), 14 - it)
            cand16 = both_halves(cand)
            c = count16(words_s, lambda x: x >= cand16)
            ok = c >= target
            return jnp.where(ok, cand, v), jnp.where(ok, n_gt, c)

        return lax.fori_loop(0, 15, step, (jnp.where(ok0, 0, -32768), jnp.where(ok0, 0, c0)))

    top, above = bisect16(hi_s, k_top)
    top16 = both_halves(top)

    def keep_ties(c, carry):
        for pr in range(pairs):
            p_ = c * pairs + pr
            tied = halves(hi_s[p_]) == top16
            lo_s[p_] = pltpu.bitcast(jnp.where(tied, halves(lo_s[p_]), jnp.int16(-32768)), jnp.int32)
        return carry

    lax.fori_loop(0, n_chunks, keep_ties, 0)
    low, above_low = bisect16(lo_s, k_top - above)
    thr = lax.shift_left(top, 16) | ((low ^ 0x8000) & 0xFFFF)
    thr = jnp.broadcast_to(thr, (TQ, TQ))
    need = (k_top - above - above_low).astype(F32)

    _softmax_reset(m_s, acc_s)
    seen_s[...] = jnp.zeros((TQ, TQ), F32)

    def att_chunk(c, near, ntile=ch):
        tk = ntile * TQ
        koff = pl.multiple_of(c * (ch * TQ), ch * TQ)
        masks = []
        for t in range(ntile):
            key = sc_s[c * ch + t]
            eq = key == thr
            prefix = _dot(tri_ref[...], _mx(jnp.where(eq, 1.0, 0.0)))
            seen = seen_s[...]
            tie = jnp.where(seen + prefix <= need, 0.0, NEG)
            seen_s[...] = seen + prefix[TQ - 1:TQ, :]
            masks.append(jnp.where(key > thr, 0.0, jnp.where(eq, tie, NEG)))
        mb = masks[0] if ntile == 1 else jnp.concatenate(masks, axis=0)
        mb = jnp.concatenate([mb] * DSA_HEADS, axis=1)
        kvm = _mx(kv_ref[pl.ds(koff, tk), :])

        def scores(c0, c1):
            s = _dot_nt(kvm, qs_s[c0:c1, :]) + mb[:, c0:c1]
            if near:
                s = s + _near_bias(d_ref, lambda j: jnp.clip(j - (i - 2), 0, 3), c * ch, ntile, c0, c1)
            return s

        _softmax_step(scores, [_value_tiles(vt_s, 0, c * ch, ntile)], m_s, acc_s, DSA_HEADS * TQ)

    first_near = jnp.maximum(i - 1, 0) // ch

    def far_body(c, carry):
        att_chunk(c, False)
        return carry

    def near_body(c, carry):
        att_chunk(c, True)
        return carry

    lax.fori_loop(0, first_near, far_body, 0)
    lax.fori_loop(first_near, n_chunks - 1, near_body, 0)
    _last_chunk(att_chunk, i, n_chunks - 1, ch)

    o_t = acc_s[...]
    o_t = o_t / o_t[0:1, :]
    for m in range(DSA_HEADS // 2):
        pair = [o_t[HEAD_DIM:, hh * TQ:(hh + 1) * TQ] for hh in (2 * m, 2 * m + 1)]
        slab = jnp.concatenate(pair, axis=0).T
        o_ref[:, m * LANES:(m + 1) * LANES] = slab * jax.nn.silu(z_ref[:, m * LANES:(m + 1) * LANES])


def _dsa(u, d_tiles, tri, k_top):
    b, s, _ = u.shape
    nq = s // TQ
    assert nq % 2 == 0, "the packed threshold search pairs key tiles"
    rows = DSA_HEADS * TQ
    irows = IDX_HEADS * TQ
    seq = lambda col: pl.BlockSpec((None, s, LANES), lambda bi, i, col=col: (bi, 0, col // LANES))
    return pl.pallas_call(
        functools.partial(_dsa_kernel, k_top=k_top, ch=_tiles_per_chunk(nq)),
        grid=(b, nq),
        in_specs=[
            pl.BlockSpec((None, TQ, DSA_HEADS * HEAD_DIM), lambda bi, i: (bi, i, A_DSA_Q // (DSA_HEADS * HEAD_DIM))),
            seq(A_DSA_KV),
            pl.BlockSpec((None, TQ, IDX_HEADS * IDX_DIM), lambda bi, i: (bi, i, A_IDX_Q // (IDX_HEADS * IDX_DIM))),
            seq(A_IDX_K),
            pl.BlockSpec((None, TQ, LANES), lambda bi, i: (bi, i, A_IDX_W // LANES)),
            pl.BlockSpec((None, TQ, 512), lambda bi, i: (bi, i, A_BZ // 512)),
            pl.BlockSpec(d_tiles.shape, lambda bi, i: (0, 0, 0)),
            pl.BlockSpec((TQ, TQ), lambda bi, i: (0, 0)),
        ],
        out_specs=pl.BlockSpec((None, TQ, 512), lambda bi, i: (bi, i, 0)),
        out_shape=jax.ShapeDtypeStruct((b, s, 512), F32),
        scratch_shapes=[pltpu.VMEM((rows, LANES), MXU_DTYPE),
                        pltpu.VMEM((irows, LANES), MXU_DTYPE),
                        pltpu.VMEM((nq, TQ, TQ), jnp.int32),
                        pltpu.VMEM((nq // 2, TQ, TQ), jnp.int32),
                        pltpu.VMEM((nq // 2, TQ, TQ), jnp.int32),
                        pltpu.VMEM((1, rows), F32),
                        pltpu.VMEM((LANES, rows), F32),
                        pltpu.VMEM((TQ, TQ), F32),
                        pltpu.VMEM((1, nq, LANES, TQ), MXU_DTYPE)],
        compiler_params=_params(2),
        name="dsa",
    )(u, u, u, u, u, u, d_tiles, tri)


def _causal_conv(x, xe_s, w_ref, b_ref):
    t = x.shape[0]
    xe_s[8:8 + t, :] = x
    y = b_ref[...] + w_ref[CONV_WIDTH - 1:CONV_WIDTH, :] * x
    for k in range(1, CONV_WIDTH):
        y = y + w_ref[CONV_WIDTH - 1 - k:CONV_WIDTH - k, :] * xe_s[8 - k:8 - k + t, :]
    xe_s[0:8, :] = x[t - 8:t, :]
    return y


def _rglru_kernel(x_ref, z_ref, cw_ref, cb_ref, wa_ref, ba_ref, wx_ref, bx_ref, lam_ref, o_ref,
                  xe_s, h_s):
    t, c = x_ref.shape

    @pl.when(pl.program_id(1) == 0)
    def _():
        xe_s[0:8, :] = jnp.zeros((8, c), F32)
        h_s[...] = jnp.zeros(h_s.shape, F32)

    xc = _causal_conv(x_ref[...], xe_s, cw_ref, cb_ref)
    xcm = _mx(xc)
    r = jax.nn.sigmoid(_dot(xcm, wa_ref[...]) + ba_ref[...])
    ig = jax.nn.sigmoid(_dot(xcm, wx_ref[...]) + bx_ref[...])
    nl = -lam_ref[...]
    softplus = jnp.maximum(nl, 0.0) + jnp.log1p(jnp.exp(-jnp.abs(nl)))
    log_a = (-LRU_C * r) * softplus
    a = jnp.exp(log_a)
    bb = jnp.sqrt(-jnp.tanh(log_a) * (a * a + 1.0)) * (ig * xc)
    row = lax.broadcasted_iota(jnp.int32, (t, c), 0) & (SUBLANES - 1)
    d = 1
    while d < SUBLANES:
        live = row >= d
        a_sh = jnp.where(live, pltpu.roll(a, d, 0), 1.0)
        b_sh = jnp.where(live, pltpu.roll(bb, d, 0), 0.0)
        bb = a * b_sh + bb
        a = a * a_sh
        d *= 2
    carry = h_s[0:1, :]
    groups = []
    for g in range(t // SUBLANES):
        rows = slice(g * SUBLANES, (g + 1) * SUBLANES)
        groups.append(a[rows] * carry + bb[rows])
        carry = groups[-1][SUBLANES - 1:SUBLANES, :]
    h = jnp.concatenate(groups, axis=0)
    h_s[0:1, :] = carry
    o_ref[...] = h * jax.nn.silu(z_ref[...])


def _rglru(u, conv_w, conv_b, wa_bd, ba, wx_bd, bx, lam, t=256):
    b, s, _ = u.shape
    c = LRU_WIDTH
    whole = lambda bi, i: (0, 0)
    return pl.pallas_call(
        _rglru_kernel,
        grid=(b, s // t),
        in_specs=[pl.BlockSpec((None, t, c), lambda bi, i: (bi, i, R_CX // c)),
                  pl.BlockSpec((None, t, c), lambda bi, i: (bi, i, R_CZ // c)),
                  pl.BlockSpec((CONV_WIDTH, c), whole), pl.BlockSpec((1, c), whole),
                  pl.BlockSpec((c, c), whole), pl.BlockSpec((1, c), whole),
                  pl.BlockSpec((c, c), whole), pl.BlockSpec((1, c), whole),
                  pl.BlockSpec((1, c), whole)],
        out_specs=pl.BlockSpec((None, t, c), lambda bi, i: (bi, i, 0)),
        out_shape=jax.ShapeDtypeStruct((b, s, c), F32),
        scratch_shapes=[pltpu.VMEM((t + 8, c), F32), pltpu.VMEM((8, c), F32)],
        compiler_params=_params(2),
        name="rglru",
    )(u, u, conv_w, conv_b.reshape(1, c), wa_bd, ba.reshape(1, c), wx_bd, bx.reshape(1, c),
      lam.reshape(1, c))


def _mlstm_kernel(qk_ref, v_ref, g_ref, og_ref, z_ref, cw_ref, cb_ref, gb_ref, o_ref,
                  xe_s, c_s, m_s):
    t = qk_ref.shape[0]
    L = MLSTM_CHUNK
    H = MLSTM_HEADS
    D = MLSTM_DIM

    @pl.when(pl.program_id(1) == 0)
    def _():
        xe_s[0:8, :] = jnp.zeros((8, xe_s.shape[1]), F32)
        c_s[...] = jnp.zeros(c_s.shape, F32)
        m_s[...] = jnp.zeros(m_s.shape, F32)

    qk = jax.nn.silu(_causal_conv(qk_ref[...], xe_s, cw_ref, cb_ref))

    lane = lax.broadcasted_iota(jnp.int32, (t, LANES), 1)
    row_in = lax.broadcasted_iota(jnp.int32, (t, LANES), 0) & (L - 1)
    gs = g_ref[...] + gb_ref[...]
    log_sig = jnp.minimum(gs, 0.0) - jnp.log1p(jnp.exp(-jnp.abs(gs)))
    pre = jnp.where(lane < H, gs, log_sig)
    cum = pre
    d = 1
    while d < L:
        cum = cum + jnp.where(row_in >= d, pltpu.roll(cum, d, 0), 0.0)
        d *= 2
    comb = jnp.where(lane < H, pre, cum)
    comb_t = comb.T

    causal_t = (lax.broadcasted_iota(jnp.int32, (L, L), 0) <= lax.broadcasted_iota(jnp.int32, (L, L), 1))
    pad_rows = jnp.zeros((7, L), F32)
    v_ts = [v_ref[:, h * D:(h + 1) * D].T for h in range(H)]
    states = [c_s[h] for h in range(H)]
    m_prevs = [m_s[h:h + 1, 0:1] for h in range(H)]
    outs = [[] for _ in range(H)]
    for c in range(t // L):
        r = slice(c * L, (c + 1) * L)
        for h in range(H):
            v_t, state, m_prev = v_ts[h], states[h], m_prevs[h]
            q_ = _mx(qk[r, h * D:(h + 1) * D])
            k_ = _mx(qk[r, (H + h) * D:(H + h + 1) * D] * (D ** -0.5))
            li_row = comb_t[h:h + 1, r]
            b_row = comb_t[H + h:H + h + 1, r]
            lb_col = comb[r, h:h + 1] - comb[r, H + h:H + h + 1]
            b_last = b_row[:, L - 1:L]

            dmat_t = jnp.where(causal_t, b_row + lb_col, -jnp.inf)
            decay_row = b_last - b_row + li_row
            m_new = jnp.maximum(b_last + m_prev, jnp.max(decay_row, axis=-1, keepdims=True))
            inter = b_row + m_prev
            m_t = jnp.maximum(inter, jnp.max(dmat_t, axis=0, keepdims=True))
            w_t = _dot_nt(k_, q_) * jnp.exp(dmat_t - m_t)
            prev = jnp.exp(inter - m_t)

            read = _dot_nt(_mx(state), q_)
            num_t = prev * read[:D] + _dot(_mx(v_t[:, r]), _mx(w_t))
            den = prev * read[D:D + 1] + jnp.sum(w_t, axis=0, keepdims=True)
            outs[h].append(num_t / jnp.maximum(jnp.abs(den), jnp.exp(-m_t)))

            wk = jnp.exp(decay_row - m_new)
            inc = jnp.concatenate([v_t[:, r] * wk, wk, pad_rows], axis=0)
            states[h] = jnp.exp(b_last + m_prev - m_new) * state + _dot(_mx(inc), k_)
            m_prevs[h] = m_new

    for h in range(H):
        c_s[h] = states[h]
        m_s[h:h + 1, :] = jnp.broadcast_to(m_prevs[h], (1, LANES))
        out = jnp.concatenate(outs[h], axis=1).T
        cols = slice(h * D, (h + 1) * D)
        o_ref[:, cols] = jax.nn.sigmoid(og_ref[:, cols]) * out * jax.nn.silu(z_ref[:, cols])


def _mlstm(u, conv_w, conv_b, gate_bias, t=128):
    b, s, _ = u.shape
    w = MLSTM_WIDTH
    whole = lambda bi, i: (0, 0)
    return pl.pallas_call(
        _mlstm_kernel,
        grid=(b, s // t),
        in_specs=[pl.BlockSpec((None, t, 2 * w), lambda bi, i: (bi, i, R_QK // (2 * w))),
                  pl.BlockSpec((None, t, w), lambda bi, i: (bi, i, R_V // w)),
                  pl.BlockSpec((None, t, LANES), lambda bi, i: (bi, i, R_GATE // LANES)),
                  pl.BlockSpec((None, t, w), lambda bi, i: (bi, i, R_O // w)),
                  pl.BlockSpec((None, t, w), lambda bi, i: (bi, i, R_Z // w)),
                  pl.BlockSpec((CONV_WIDTH, 2 * w), whole), pl.BlockSpec((1, 2 * w), whole),
                  pl.BlockSpec((1, LANES), whole)],
        out_specs=pl.BlockSpec((None, t, w), lambda bi, i: (bi, i, 0)),
        out_shape=jax.ShapeDtypeStruct((b, s, w), F32),
        scratch_shapes=[pltpu.VMEM((t + 8, 2 * w), F32),
                        pltpu.VMEM((MLSTM_HEADS, MLSTM_DIM + 8, MLSTM_DIM), F32),
                        pltpu.VMEM((8, LANES), F32)],
        compiler_params=_params(2),
        name="mlstm",
    )(u, u, u, u, u, conv_w, conv_b.reshape(1, 2 * w), gate_bias)


def _attn_weights(w_in):
    d = w_in.shape[0]
    widths = (512, 768, 24, 512, 512, 64, 64, 256, 64, 4, 512)
    offs = np.concatenate([[0], np.cumsum(widths)])
    a_q, a_kv, a_g, a_z, b_q, b_k, b_v, b_qi, b_ki, b_wi, b_z = [
        w_in[:, offs[k]:offs[k + 1]] for k in range(len(widths))]
    scale = HEAD_DIM ** -0.5 * LOG2E
    zeros = lambda n: jnp.zeros((d, n), w_in.dtype)
    cols = [
        a_q * scale, b_q * scale, a_z, b_z, b_qi, a_kv,
        a_g, zeros(LANES - 24),
        b_k, b_v,
        b_ki, zeros(LANES - IDX_DIM),
        b_wi * (IDX_DIM ** -0.5 * IDX_HEADS ** -0.5), zeros(LANES - IDX_HEADS),
    ]
    w = jnp.concatenate(cols, axis=1)
    assert w.shape[1] == A_TOTAL
    return _mx(w)


def _rec_weights(w_in):
    d = w_in.shape[0]
    widths = (512, 512, 512, 512, 512, 4, 4, 512, 512)
    offs = np.concatenate([[0], np.cumsum(widths)])
    c_x, c_z, d_q, d_k, d_v, d_i, d_f, d_o, d_z = [w_in[:, offs[k]:offs[k + 1]] for k in range(len(widths))]
    w = jnp.concatenate([c_x, c_z, d_q, d_k, d_v, d_o, d_z, d_i, d_f,
                         jnp.zeros((d, R_TOTAL - R_GATE - 2 * MLSTM_HEADS), w_in.dtype)], axis=1)
    assert w.shape[1] == R_TOTAL
    return _mx(w)


def _block_diag(w):
    g, n, _ = w.shape
    eye = jnp.eye(g, dtype=w.dtype)
    return (eye[:, None, :, None] * w[:, :, None, :]).reshape(g * n, g * n)


def _attention_layer(x2d, b, s, norm_g, w_in, cmp_pos_k, cmp_w1_k, cmp_w2_k, cmp_pos_v, cmp_w1_v, cmp_w2_v,
                     t5_table):
    nq = s // TQ
    n16 = s // CMP_STRIDE
    nb = s // SEL_BLOCK
    u = _norm_proj(x2d, norm_g, _attn_weights(w_in)).reshape(b, s, A_TOTAL)

    pos = jnp.stack([cmp_pos_k, cmp_pos_v])
    pos = jnp.concatenate([pos] * NSA_GROUPS, axis=-1).reshape(2, CMP_BLOCK, 1, LANES)
    w1 = jnp.stack([cmp_w1_k, cmp_w1_v]).reshape(2, CMP_BLOCK, HEAD_DIM, CMP_HIDDEN)
    z1 = jnp.zeros_like(w1)
    w1 = _mx(jnp.stack([jnp.concatenate([w1, z1], axis=2), jnp.concatenate([z1, w1], axis=2)], axis=1))
    w2 = jnp.stack([cmp_w2_k, cmp_w2_v])
    zpad = jnp.zeros_like(w2)
    w2 = _mx(jnp.stack([jnp.concatenate([w2, zpad], -1), jnp.concatenate([zpad, w2], -1)], axis=1))
    cmp_kv = _compress(u, pos, w1, w2)

    tbl = t5_table.astype(F32)
    tbl_a, tbl_b = tbl[:, :NSA_HEADS], tbl[:, NSA_HEADS:]
    kj = np.arange(TQ)[:, None]
    qi = np.arange(TQ)[None, :]
    assert nb <= LANES
    e_t = jnp.asarray((np.arange(s)[:, None] // SEL_BLOCK) == np.arange(LANES)[None, :], MXU_DTYPE)
    ci = np.arange(n16)[None, :]
    sj = np.arange(nb)[:, None]
    ov = jnp.asarray((ci * CMP_STRIDE < (sj + 1) * SEL_BLOCK) & (ci * CMP_STRIDE + CMP_BLOCK > sj * SEL_BLOCK)
                     & (ci < n16 - 1), MXU_DTYPE)
    tri = jnp.asarray(qi <= kj, MXU_DTYPE)

    ya = _nsa(u, cmp_kv, _cmp_bias(tbl_a, nq, n16), _near_tiles(tbl_a), e_t, ov, min(N_SEL, nb))
    yb = _dsa(u, _near_tiles(tbl_b), tri, min(DSA_TOPK_MAX, s // 4))
    return ya.reshape(b * s, -1), yb.reshape(b * s, -1)


def _recurrent_layer(x2d, b, s, norm_g, w_in, conv_c_w, conv_c_b, wa, ba, wx, bx, lam,
                     conv_d_w, conv_d_b, b_i, b_f):
    u = _norm_proj(x2d, norm_g, _rec_weights(w_in)).reshape(b, s, R_TOTAL)
    yc = _rglru(u, conv_c_w, conv_c_b, _mx(_block_diag(wa)), ba, _mx(_block_diag(wx)), bx, lam)
    gate_bias = jnp.concatenate([b_i, b_f, jnp.zeros((LANES - 2 * MLSTM_HEADS,), F32)]).reshape(1, LANES)
    yd = _mlstm(u, conv_d_w, conv_d_b, gate_bias)
    return yc.reshape(b * s, -1), yd.reshape(b * s, -1)


def kernel(x, p, norm_g, final_g, ple_w, ple_gate_w, t5_table, attn_w_in, attn_w_out, cmp_pos_k, cmp_w1_k, cmp_w2_k, cmp_pos_v, cmp_w1_v, cmp_w2_v, rec_w_in, rec_w_out, lru_conv_w, lru_conv_b, lru_wa, lru_ba, lru_wx, lru_bx, lru_lambda, mlstm_conv_w, mlstm_conv_b, mlstm_b_i, mlstm_b_f):
    b, s, d = x.shape
    depth = p.shape[0]
    x2d = x.reshape(b * s, d)
    for i in range(depth):
        j = i // 2
        if i % 2 == 0:
            ya, yb = _attention_layer(x2d, b, s, norm_g[i], attn_w_in[j], cmp_pos_k[j], cmp_w1_k[j], cmp_w2_k[j],
                                      cmp_pos_v[j], cmp_w1_v[j], cmp_w2_v[j], t5_table)
            w_out = attn_w_out[j]
        else:
            ya, yb = _recurrent_layer(x2d, b, s, norm_g[i], rec_w_in[j], lru_conv_w[j], lru_conv_b[j],
                                      lru_wa[j], lru_ba[j], lru_wx[j], lru_bx[j], lru_lambda[j],
                                      mlstm_conv_w[j], mlstm_conv_b[j], mlstm_b_i[j], mlstm_b_f[j])
            w_out = rec_w_out[j]
        x2d = _out_proj(x2d, ya, yb, p.reshape(depth, b * s, -1), i, w_out, ple_w[i], ple_gate_w[i], final_g,
                        final=(i == depth - 1))
    return x2d.reshape(b, s, d)
```

```python
import functools
import math

import numpy as np
import jax
import jax.numpy as jnp
from jax import lax
from jax.experimental import pallas as pl
from jax.experimental.pallas import tpu as pltpu

F32 = jnp.float32
MXU_DTYPE = jnp.bfloat16

HEAD_DIM = 64
NSA_HEADS = 8
NSA_GROUPS = 2
NSA_HPG = NSA_HEADS // NSA_GROUPS
CMP_BLOCK = 32
CMP_STRIDE = 16
CMP_HIDDEN = 256
SEL_BLOCK = 64
N_SEL = 8
WINDOW = 512
FORCE_BONUS = 1e4
DSA_HEADS = 8
IDX_HEADS = 4
IDX_DIM = 64
DSA_TOPK_MAX = 256
N_BUCKETS = 32
T5_MAX_DIST = 128
LRU_WIDTH = 512
CONV_WIDTH = 4
LRU_C = 8.0
MLSTM_HEADS = 4
MLSTM_DIM = 128
MLSTM_WIDTH = MLSTM_HEADS * MLSTM_DIM
MLSTM_CHUNK = 64
RMS_EPS = 1e-6
NEG = -1e30

LANES = 128
SUBLANES = 8
COL_BLOCK = 512
TQ = 128
INT_MIN = -2 ** 31
LOG2E = math.log2(math.e)
VMEM_LIMIT = 48 * 1024 * 1024

A_NSA_Q = 0
A_DSA_Q = 512
A_AZ = 1024
A_BZ = 1536
A_IDX_Q = 2048
A_KV = 2304
A_GATE = 3072
A_DSA_KV = 3200
A_IDX_K = 3328
A_IDX_W = 3456
A_TOTAL = 3584

R_CX = 0
R_CZ = 512
R_QK = 1024
R_V = 2048
R_O = 2560
R_Z = 3072
R_GATE = 3584
R_TOTAL = 3840


def _dot(a, b):
    return jnp.dot(a, b, preferred_element_type=F32)


def _dot_nt(a, b):
    return lax.dot_general(a, b, (((1,), (1,)), ((), ())), preferred_element_type=F32)


def _mx(a):
    return a.astype(MXU_DTYPE)


def _params(n_grid):
    return pltpu.CompilerParams(dimension_semantics=("arbitrary",) * n_grid,
                                vmem_limit_bytes=VMEM_LIMIT)


def _norm_proj_kernel(x_ref, g_ref, w_ref, o_ref):
    x = x_ref[...]
    ms = jnp.mean(x * x, axis=-1, keepdims=True)
    y = x * lax.rsqrt(ms + RMS_EPS) * g_ref[...]
    o_ref[...] = _dot(_mx(y), w_ref[...])


def _norm_proj(x2d, g, w, tm=512):
    m, d = x2d.shape
    n = w.shape[1]
    return pl.pallas_call(
        _norm_proj_kernel,
        grid=(m // tm,),
        in_specs=[pl.BlockSpec((tm, d), lambda i: (i, 0)),
                  pl.BlockSpec((1, d), lambda i: (0, 0)),
                  pl.BlockSpec((d, n), lambda i: (0, 0))],
        out_specs=pl.BlockSpec((tm, n), lambda i: (i, 0)),
        out_shape=jax.ShapeDtypeStruct((m, n), F32),
        compiler_params=_params(1),
        name="norm_proj",
    )(x2d, g.reshape(1, d), w)


def _out_proj_kernel(x_ref, ya_ref, yb_ref, p_ref, wo_ref, pw_ref, gw_ref, fg_ref, o_ref, *, final):
    half = ya_ref.shape[-1]
    y = _dot(_mx(ya_ref[...]), wo_ref[:half, :]) + _dot(_mx(yb_ref[...]), wo_ref[half:, :])
    x1 = x_ref[...] + y
    gate = jax.nn.sigmoid(_dot(_mx(x1), gw_ref[...]))
    x2 = x1 + _dot(_mx(p_ref[...]), pw_ref[...]) * gate
    if final:
        ms = jnp.mean(x2 * x2, axis=-1, keepdims=True)
        x2 = x2 * lax.rsqrt(ms + RMS_EPS) * fg_ref[...]
    o_ref[...] = x2


def _out_proj(x2d, ya, yb, p_all, layer, w_out, ple_w, gate_w, final_g, final, tm=1024):
    m, d = x2d.shape
    half = ya.shape[1]
    pd = p_all.shape[2]
    row = lambda i: (i, 0)
    whole = lambda i: (0, 0)
    return pl.pallas_call(
        functools.partial(_out_proj_kernel, final=final),
        grid=(m // tm,),
        in_specs=[pl.BlockSpec((tm, d), row), pl.BlockSpec((tm, half), row),
                  pl.BlockSpec((tm, half), row), pl.BlockSpec((None, tm, pd), lambda i: (layer, i, 0)),
                  pl.BlockSpec((2 * half, d), whole), pl.BlockSpec((pd, d), whole),
                  pl.BlockSpec((d, d), whole), pl.BlockSpec((1, d), whole)],
        out_specs=pl.BlockSpec((tm, d), row),
        out_shape=jax.ShapeDtypeStruct((m, d), F32),
        compiler_params=_params(1),
        name="out_proj",
    )(x2d, ya, yb, p_all, _mx(w_out), _mx(ple_w), _mx(gate_w), final_g.reshape(1, d))


def _bucket_np(n):
    n = np.asarray(n)
    max_exact = N_BUCKETS // 2
    nf = np.maximum(n, 1).astype(np.float32)
    large = max_exact + (np.log(nf / np.float32(max_exact)) / np.float32(math.log(T5_MAX_DIST / max_exact))
                         * np.float32(N_BUCKETS - max_exact)).astype(np.int32)
    large = np.minimum(large, N_BUCKETS - 1)
    return np.where(n < max_exact, n, large)


def _bias_index(dist):
    dist = np.asarray(dist)
    return np.where(dist >= 0, _bucket_np(np.maximum(dist, 0)), N_BUCKETS).astype(np.int32)


def _lookup(ext, idx):
    a, b = idx.shape
    rows = jnp.arange(ext.shape[0], dtype=jnp.int32)[:, None]
    onehot = (rows == jnp.asarray(idx.reshape(1, -1), jnp.int32)).astype(F32)
    out = jnp.dot(ext.T, onehot, precision=lax.Precision.HIGHEST)
    return out.reshape(ext.shape[1], a, b).transpose(1, 0, 2).reshape(a, -1)


def _near_tiles(tbl):
    h = tbl.shape[1]
    far = tbl[N_BUCKETS - 1]
    ext = jnp.concatenate([tbl, jnp.full((1, h), NEG, F32)], axis=0)
    kj = np.arange(TQ)[:, None]
    qi = np.arange(TQ)[None, :]
    far = jnp.repeat(far, TQ)[None, :]
    d0 = (_lookup(ext, _bias_index(qi - kj)) - far) * LOG2E
    d1 = (_lookup(ext, _bias_index(TQ + qi - kj)) - far) * LOG2E
    edge = jnp.asarray(np.tile(np.where(kj > qi, 0.0, NEG), (1, h)), F32)
    return jnp.stack([jnp.zeros_like(d0), d1, d0, jnp.full_like(d0, NEG), edge])


def _cmp_bias(tbl, nq, n16):
    h = tbl.shape[1]
    ext = jnp.concatenate([tbl, jnp.full((1, h), NEG, F32)], axis=0)
    off = (TQ // CMP_STRIDE) * (nq - 1)
    cc = np.arange(n16 + off)[:, None] - off
    qi = np.arange(TQ)[None, :]
    return _lookup(ext, _bias_index(qi - CMP_STRIDE * cc - (CMP_BLOCK - 1))) * LOG2E


def _compress_kernel(x_ref, pos_ref, w1_ref, w2_ref, o_ref):
    n16 = o_ref.shape[0]
    acc = None
    for g in range(NSA_GROUPS):
        pa = pb = None
        for l in range(CMP_STRIDE):
            x = x_ref[pl.ds(l, n16, stride=CMP_STRIDE), :]
            a = _dot(_mx(x + pos_ref[l]), w1_ref[g, l])
            c = _dot(_mx(x + pos_ref[CMP_STRIDE + l]), w1_ref[g, CMP_STRIDE + l])
            pa = a if pa is None else pa + a
            pb = c if pb is None else pb + c
        pre = pa + pltpu.roll(pb, n16 - 1, 0)
        t = _dot(_mx(jax.nn.silu(pre)), w2_ref[g])
        acc = t if acc is None else acc + t
    o_ref[...] = acc


def _compress(u, pos, w1, w2):
    b, s, _ = u.shape
    n16 = s // CMP_STRIDE
    return pl.pallas_call(
        _compress_kernel,
        grid=(2, b),
        in_specs=[pl.BlockSpec((None, s, LANES), lambda k, i: (i, 0, A_KV // LANES + k)),
                  pl.BlockSpec((None, CMP_BLOCK, 1, LANES), lambda k, i: (k, 0, 0, 0)),
                  pl.BlockSpec((None, NSA_GROUPS, CMP_BLOCK, LANES, CMP_HIDDEN), lambda k, i: (k, 0, 0, 0, 0)),
                  pl.BlockSpec((None, NSA_GROUPS, CMP_HIDDEN, LANES), lambda k, i: (k, 0, 0, 0))],
        out_specs=pl.BlockSpec((None, None, n16, LANES), lambda k, i: (k, i, 0, 0)),
        out_shape=jax.ShapeDtypeStruct((2, b, n16, LANES), F32),
        compiler_params=_params(2),
        name="compress",
    )(u, pos, w1, w2)


def _softmax_reset(m_s, acc_s):
    m_s[...] = jnp.full(m_s.shape, -jnp.inf, F32)
    acc_s[...] = jnp.zeros(acc_s.shape, F32)


def _softmax_step(score_fn, values, m_s, acc_s, col_block):
    r = m_s.shape[1]
    assert (r // len(values)) % col_block == 0, "a column block must not straddle two value operands"
    s_next = score_fn(0, col_block)
    for c0 in range(0, r, col_block):
        cols = slice(c0, c0 + col_block)
        s = s_next
        if c0 + col_block < r:
            s_next = score_fn(c0 + col_block, c0 + 2 * col_block)
        m_prev = m_s[:, cols]
        m_new = jnp.maximum(m_prev, jnp.max(s, axis=0, keepdims=True))
        alpha = jnp.exp2(m_prev - m_new)
        p = _mx(jnp.exp2(s - m_new))
        v = values[c0 * len(values) // r]
        acc_s[:, cols] = alpha * acc_s[:, cols] + _dot(v, p)
        m_s[:, cols] = m_new


def _store_value_tiles(v_ref, vt_s, ones_hi):
    row = lax.broadcasted_iota(jnp.int32, (LANES, TQ), 0)

    def body(j, carry):
        vt = v_ref[pl.ds(pl.multiple_of(j * TQ, TQ), TQ), :].T
        for k, hi in enumerate(ones_hi):
            vt_s[k, j] = _mx(jnp.where((row >= HEAD_DIM) == hi, 1.0, vt))
        return carry

    lax.fori_loop(0, vt_s.shape[1], body, 0)


def _value_tiles(vt_s, k, j0, ntile):
    tiles = [vt_s[k, j0 + t] for t in range(ntile)]
    return tiles[0] if ntile == 1 else jnp.concatenate(tiles, axis=1)


def _near_bias(d_ref, idx_of_tile, j0, ntile, c0, c1):
    tiles = [d_ref[idx_of_tile(j0 + t), :, c0:c1] for t in range(ntile)]
    return tiles[0] if ntile == 1 else jnp.concatenate(tiles, axis=0)


def _stack_heads(q_ref, hi):
    lane = lax.broadcasted_iota(jnp.int32, (TQ, LANES), 1)
    out = []
    for h, up in enumerate(hi):
        v = q_ref[:, (h // 2) * LANES:(h // 2 + 1) * LANES]
        if (h % 2 == 1) != up:
            v = pltpu.roll(v, HEAD_DIM, 1)
        out.append(jnp.where((lane >= HEAD_DIM) == up, v, 0.0))
    return jnp.concatenate(out, axis=0)


def _last_chunk(chunk_fn, i, last, ch):
    if ch < 2:
        chunk_fn(last, True)
        return
    short = (i - last * ch) < ch // 2

    @pl.when(short)
    def _():
        chunk_fn(last, True, ch // 2)

    @pl.when(jnp.logical_not(short))
    def _():
        chunk_fn(last, True)


def _tiles_per_chunk(nq):
    for ch in (4, 2):
        if nq % ch == 0:
            return ch
    return 1


def _nsa_kernel(q_ref, kc_ref, vc_ref, ks_ref, vs_ref, kw_ref, vw_ref, g_ref, z_ref,
                bc_ref, d_ref, et_ref, ov_ref, o_ref,
                qa_s, m_s, acc_s, osel_s, vct_s, vst_s, vwt_s, *, n_pick, ch, n_win_tiles):
    i = pl.program_id(1)
    rows = NSA_HEADS * TQ
    nb = ov_ref.shape[0]
    q0 = i * TQ
    half = LANES // NSA_GROUPS

    @pl.when(i == 0)
    def _():
        vct_s[...] = _mx(vc_ref[...].T)
        _store_value_tiles(vs_ref, vst_s, (True, False))
        _store_value_tiles(vw_ref, vwt_s, (True, False))

    qs = _mx(_stack_heads(q_ref, [h >= NSA_HPG for h in range(NSA_HEADS)]))
    qa_s[:, :LANES] = qs

    s = _dot_nt(_mx(kc_ref[...]), qs) + bc_ref[...]
    p = jnp.exp2(s - jnp.max(s, axis=0, keepdims=True))
    t_lane = q0 + (lax.broadcasted_iota(jnp.int32, (1, rows), 1) & (TQ - 1))
    any_valid = jnp.where(t_lane >= CMP_BLOCK - 1, 1.0, 0.0)
    p = p * (any_valid / jnp.sum(p, axis=0, keepdims=True))
    oc_t = _dot(vct_s[...], _mx(p))

    blk = lax.broadcasted_iota(jnp.int32, (nb, TQ), 0)
    t = q0 + lax.broadcasted_iota(jnp.int32, (nb, TQ), 1)
    cur = lax.shift_right_logical(t, int(math.log2(SEL_BLOCK)))
    forced = jnp.where(blk == 0, 1.0, jnp.where(blk == cur, 1.0, jnp.where(blk == cur - 1, 1.0, 0.0)))
    admissible = blk * SEL_BLOCK <= t
    ov = ov_ref[...]
    for g in range(NSA_GROUPS):
        c0 = g * NSA_HPG * TQ
        ps = p[:, c0:c0 + TQ]
        for h in range(1, NSA_HPG):
            ps = ps + p[:, c0 + h * TQ:c0 + (h + 1) * TQ]
        hi = _mx(ps)
        lo = _mx(ps - hi.astype(F32))
        imp_t = _dot(ov, hi) + _dot(ov, lo)
        score = jnp.where(admissible, imp_t + FORCE_BONUS * forced, NEG)
        sel = jnp.zeros((nb, TQ), F32)
        for _ in range(n_pick):
            best = jnp.max(score, axis=0, keepdims=True)
            first = jnp.min(jnp.where(score == best, blk, nb), axis=0, keepdims=True)
            hit = blk == first
            sel = jnp.where(hit, 1.0, sel)
            score = jnp.where(hit, -jnp.inf, score)
        if nb < LANES:
            sel = jnp.concatenate([sel, jnp.ones((LANES - nb, TQ), F32)], axis=0)
        block_mask = _mx((sel.T - 1.0) * (-NEG))
        for h in range(NSA_HPG):
            r0 = (g * NSA_HPG + h) * TQ
            qa_s[r0:r0 + TQ, LANES:] = block_mask

    def values(vt_s, j0, ntile):
        return [_value_tiles(vt_s, 0, j0, ntile), _value_tiles(vt_s, 1, j0, ntile)]

    _softmax_reset(m_s, acc_s)

    def sel_chunk(c, near, ntile=ch):
        tk = ntile * TQ
        koff = pl.multiple_of(c * (ch * TQ), ch * TQ)
        k_aug = jnp.concatenate([_mx(ks_ref[pl.ds(koff, tk), :]), et_ref[pl.ds(koff, tk), :]], axis=1)

        def scores(c0, c1):
            s = _dot_nt(k_aug, qa_s[c0:c1, :])
            if near:
                s = s + _near_bias(d_ref, lambda j: jnp.clip(j - (i - 2), 0, 3), c * ch, ntile, c0, c1)
            return s

        _softmax_step(scores, values(vst_s, c * ch, ntile), m_s, acc_s, COL_BLOCK)

    first_near = jnp.maximum(i - 1, 0) // ch
    last = i // ch

    def far_body(c, carry):
        sel_chunk(c, False)
        return carry

    def near_body(c, carry):
        sel_chunk(c, True)
        return carry

    lax.fori_loop(0, first_near, far_body, 0)
    lax.fori_loop(first_near, last, near_body, 0)
    _last_chunk(sel_chunk, i, last, ch)
    osel_s[...] = acc_s[...]

    _softmax_reset(m_s, acc_s)
    n_win = WINDOW // TQ
    jw = jnp.maximum(i - (n_win_tiles - 1), 0)
    koff = pl.multiple_of(jw * TQ, TQ)
    tk = n_win_tiles * TQ

    def win_idx(j):
        r = j - (i - n_win)
        return jnp.where(r == 0, 4, jnp.clip(r - (n_win - 2), 0, 3))

    kw = _mx(kw_ref[pl.ds(koff, tk), :])
    _softmax_step(lambda c0, c1: (_dot_nt(kw, qa_s[c0:c1, :LANES])
                                  + _near_bias(d_ref, win_idx, jw, n_win_tiles, c0, c1)),
                  values(vwt_s, jw, n_win_tiles), m_s, acc_s, COL_BLOCK)
    ow_t = acc_s[...]
    os_t = osel_s[...]

    gate_t = jax.nn.sigmoid(g_ref[...]).T
    for m in range(NSA_HEADS // 2):
        parts = []
        for hh in (2 * m, 2 * m + 1):
            g = hh // NSA_HPG
            r = slice(g * half, (g + 1) * half)
            d = (1 - g) * half
            c = slice(hh * TQ, (hh + 1) * TQ)
            parts.append(gate_t[3 * hh:3 * hh + 1, :] * oc_t[r, c]
                         + gate_t[3 * hh + 1:3 * hh + 2, :] * (os_t[r, c] / os_t[d:d + 1, c])
                         + gate_t[3 * hh + 2:3 * hh + 3, :] * (ow_t[r, c] / ow_t[d:d + 1, c]))
        slab = jnp.concatenate(parts, axis=0).T
        o_ref[:, m * LANES:(m + 1) * LANES] = slab * jax.nn.silu(z_ref[:, m * LANES:(m + 1) * LANES])


def _nsa(u, cmp_kv, bias_c, d_tiles, e_t, ov, n_pick):
    b, s, _ = u.shape
    nq = s // TQ
    n16 = cmp_kv.shape[2]
    nb = ov.shape[0]
    rows = NSA_HEADS * TQ
    n_win_tiles = min(WINDOW // TQ + 1, nq)
    seq = lambda col: pl.BlockSpec((None, s, LANES), lambda bi, i, col=col: (bi, 0, col // LANES))
    return pl.pallas_call(
        functools.partial(_nsa_kernel, n_pick=n_pick, ch=_tiles_per_chunk(nq), n_win_tiles=n_win_tiles),
        grid=(b, nq),
        in_specs=[
            pl.BlockSpec((None, TQ, NSA_HEADS * HEAD_DIM), lambda bi, i: (bi, i, A_NSA_Q // (NSA_HEADS * HEAD_DIM))),
            pl.BlockSpec((None, None, n16, LANES), lambda bi, i: (0, bi, 0, 0)),
            pl.BlockSpec((None, None, n16, LANES), lambda bi, i: (1, bi, 0, 0)),
            seq(A_KV + 2 * LANES), seq(A_KV + 3 * LANES), seq(A_KV + 4 * LANES), seq(A_KV + 5 * LANES),
            pl.BlockSpec((None, TQ, LANES), lambda bi, i: (bi, i, A_GATE // LANES)),
            pl.BlockSpec((None, TQ, 512), lambda bi, i: (bi, i, A_AZ // 512)),
            pl.BlockSpec((pl.Element(n16), pl.Element(rows)),
                         lambda bi, i: ((TQ // CMP_STRIDE) * (nq - 1 - i), 0)),
            pl.BlockSpec(d_tiles.shape, lambda bi, i: (0, 0, 0)),
            pl.BlockSpec((s, LANES), lambda bi, i: (0, 0)),
            pl.BlockSpec((nb, n16), lambda bi, i: (0, 0)),
        ],
        out_specs=pl.BlockSpec((None, TQ, 512), lambda bi, i: (bi, i, 0)),
        out_shape=jax.ShapeDtypeStruct((b, s, 512), F32),
        scratch_shapes=[pltpu.VMEM((rows, 2 * LANES), MXU_DTYPE),
                        pltpu.VMEM((1, rows), F32),
                        pltpu.VMEM((LANES, rows), F32),
                        pltpu.VMEM((LANES, rows), F32),
                        pltpu.VMEM((LANES, n16), MXU_DTYPE),
                        pltpu.VMEM((NSA_GROUPS, nq, LANES, TQ), MXU_DTYPE),
                        pltpu.VMEM((NSA_GROUPS, nq, LANES, TQ), MXU_DTYPE)],
        compiler_params=_params(2),
        name="nsa",
    )(u, cmp_kv, cmp_kv, u, u, u, u, u, u, bias_c, d_tiles, e_t, ov)


def _dsa_kernel(q_ref, kv_ref, qi_ref, ki_ref, wi_ref, z_ref, d_ref, tri_ref, o_ref,
                qs_s, qis_s, sc_s, hi_s, lo_s, m_s, acc_s, seen_s, vt_s, *, k_top, ch):
    i = pl.program_id(1)

    @pl.when(i == 0)
    def _():
        _store_value_tiles(kv_ref, vt_s, (False,))

    qs_s[...] = _mx(_stack_heads(q_ref, [False] * DSA_HEADS))
    qis_s[...] = _mx(_stack_heads(qi_ref, [False] * IDX_HEADS))
    wi_t = wi_ref[...].T
    w_rows = [jnp.broadcast_to(wi_t[h:h + 1, :], (TQ, TQ)) for h in range(IDX_HEADS)]
    key_idx = lax.broadcasted_iota(jnp.int32, (TQ, TQ), 0)
    q_idx = lax.broadcasted_iota(jnp.int32, (TQ, TQ), 1)
    n_chunks = (i + ch) // ch
    pairs = ch // 2

    def score_chunk(c, last):
        koff = pl.multiple_of(c * (ch * TQ), ch * TQ)
        r = jnp.maximum(_dot_nt(_mx(ki_ref[pl.ds(koff, ch * TQ), :]), qis_s[...]), 0.0)
        keys = []
        for t in range(ch):
            jt = c * ch + t
            rt = r[t * TQ:(t + 1) * TQ]
            sc = w_rows[0] * rt[:, 0:TQ]
            for h in range(1, IDX_HEADS):
                sc = sc + w_rows[h] * rt[:, h * TQ:(h + 1) * TQ]
            if last:
                sc = jnp.where((jt == i) & (key_idx > q_idx), NEG, sc)
            bits = pltpu.bitcast(sc, jnp.int32)
            key = bits ^ (lax.shift_right_arithmetic(bits, 31) & jnp.int32(0x7FFFFFFF))
            key = jnp.where(sc == 0.0, 0, key)
            if last:
                key = jnp.where(jt > i, INT_MIN, key)
            keys.append(key)
            sc_s[jt] = key
        for pr in range(pairs):
            a, b = keys[2 * pr], keys[2 * pr + 1]
            hi_s[c * pairs + pr] = (a & jnp.int32(-65536)) | lax.shift_right_logical(b, 16)
            lo_s[c * pairs + pr] = (lax.shift_left((a & 0xFFFF) ^ 0x8000, 16)
                                    | ((b & 0xFFFF) ^ 0x8000))

    def earlier_chunk(c, carry):
        score_chunk(c, False)
        return carry

    lax.fori_loop(0, n_chunks - 1, earlier_chunk, 0)
    score_chunk(n_chunks - 1, True)

    def halves(words):
        return pltpu.bitcast(words, jnp.int16)

    def both_halves(v):
        w = lax.shift_left(v, 16) | (v & 0xFFFF)
        return halves(jnp.broadcast_to(w, (TQ, TQ)))

    def count16(words_s, pred):
        def body(c, acc):
            for pr in range(pairs):
                acc = acc + jnp.where(pred(halves(words_s[c * pairs + pr])), jnp.int16(1), jnp.int16(0))
            return acc
        acc = lax.fori_loop(0, n_chunks, body, jnp.zeros((2 * TQ, TQ), jnp.int16))
        rows = 2 * TQ
        while rows > 2 * SUBLANES:
            rows //= 2
            acc = acc[:rows] + acc[rows:]
        return jnp.sum(acc.astype(jnp.int32), axis=0, keepdims=True)

    def bisect16(words_s, target):
        c0 = count16(words_s, lambda x: x >= jnp.int16(0))
        ok0 = c0 >= target

        def step(it, carry):
            v, n_gt = carry
            cand = v | lax.shift_left(jnp.int32(1), 14 - it)
            cand16 = both_halves(cand)
            c = count16(words_s, lambda x: x >= cand16)
            ok = c >= target
            return jnp.where(ok, cand, v), jnp.where(ok, n_gt, c)

        return lax.fori_loop(0, 15, step, (jnp.where(ok0, 0, -32768), jnp.where(ok0, 0, c0)))

    top, above = bisect16(hi_s, k_top)
    top16 = both_halves(top)

    def keep_ties(c, carry):
        for pr in range(pairs):
            p_ = c * pairs + pr
            tied = halves(hi_s[p_]) == top16
            lo_s[p_] = pltpu.bitcast(jnp.where(tied, halves(lo_s[p_]), jnp.int16(-32768)), jnp.int32)
        return carry

    lax.fori_loop(0, n_chunks, keep_ties, 0)
    low, above_low = bisect16(lo_s, k_top - above)
    thr = lax.shift_left(top, 16) | ((low ^ 0x8000) & 0xFFFF)
    thr = jnp.broadcast_to(thr, (TQ, TQ))
    need = (k_top - above - above_low).astype(F32)

    _softmax_reset(m_s, acc_s)
    seen_s[...] = jnp.zeros((TQ, TQ), F32)

    def att_chunk(c, near, ntile=ch):
        tk = ntile * TQ
        koff = pl.multiple_of(c * (ch * TQ), ch * TQ)
        masks = []
        for t in range(ntile):
            key = sc_s[c * ch + t]
            eq = key == thr
            prefix = _dot(tri_ref[...], _mx(jnp.where(eq, 1.0, 0.0)))
            seen = seen_s[...]
            tie = jnp.where(seen + prefix <= need, 0.0, NEG)
            seen_s[...] = seen + prefix[TQ - 1:TQ, :]
            masks.append(jnp.where(key > thr, 0.0, jnp.where(eq, tie, NEG)))
        mb = masks[0] if ntile == 1 else jnp.concatenate(masks, axis=0)
        mb = jnp.concatenate([mb] * DSA_HEADS, axis=1)
        kvm = _mx(kv_ref[pl.ds(koff, tk), :])

        def scores(c0, c1):
            s = _dot_nt(kvm, qs_s[c0:c1, :]) + mb[:, c0:c1]
            if near:
                s = s + _near_bias(d_ref, lambda j: jnp.clip(j - (i - 2), 0, 3), c * ch, ntile, c0, c1)
            return s

        _softmax_step(scores, [_value_tiles(vt_s, 0, c * ch, ntile)], m_s, acc_s, DSA_HEADS * TQ)

    first_near = jnp.maximum(i - 1, 0) // ch

    def far_body(c, carry):
        att_chunk(c, False)
        return carry

    def near_body(c, carry):
        att_chunk(c, True)
        return carry

    lax.fori_loop(0, first_near, far_body, 0)
    lax.fori_loop(first_near, n_chunks - 1, near_body, 0)
    _last_chunk(att_chunk, i, n_chunks - 1, ch)

    o_t = acc_s[...]
    o_t = o_t / o_t[0:1, :]
    for m in range(DSA_HEADS // 2):
        pair = [o_t[HEAD_DIM:, hh * TQ:(hh + 1) * TQ] for hh in (2 * m, 2 * m + 1)]
        slab = jnp.concatenate(pair, axis=0).T
        o_ref[:, m * LANES:(m + 1) * LANES] = slab * jax.nn.silu(z_ref[:, m * LANES:(m + 1) * LANES])


def _dsa(u, d_tiles, tri, k_top):
    b, s, _ = u.shape
    nq = s // TQ
    assert nq % 2 == 0, "the packed threshold search pairs key tiles"
    rows = DSA_HEADS * TQ
    irows = IDX_HEADS * TQ
    seq = lambda col: pl.BlockSpec((None, s, LANES), lambda bi, i, col=col: (bi, 0, col // LANES))
    return pl.pallas_call(
        functools.partial(_dsa_kernel, k_top=k_top, ch=_tiles_per_chunk(nq)),
        grid=(b, nq),
        in_specs=[
            pl.BlockSpec((None, TQ, DSA_HEADS * HEAD_DIM), lambda bi, i: (bi, i, A_DSA_Q // (DSA_HEADS * HEAD_DIM))),
            seq(A_DSA_KV),
            pl.BlockSpec((None, TQ, IDX_HEADS * IDX_DIM), lambda bi, i: (bi, i, A_IDX_Q // (IDX_HEADS * IDX_DIM))),
            seq(A_IDX_K),
            pl.BlockSpec((None, TQ, LANES), lambda bi, i: (bi, i, A_IDX_W // LANES)),
            pl.BlockSpec((None, TQ, 512), lambda bi, i: (bi, i, A_BZ // 512)),
            pl.BlockSpec(d_tiles.shape, lambda bi, i: (0, 0, 0)),
            pl.BlockSpec((TQ, TQ), lambda bi, i: (0, 0)),
        ],
        out_specs=pl.BlockSpec((None, TQ, 512), lambda bi, i: (bi, i, 0)),
        out_shape=jax.ShapeDtypeStruct((b, s, 512), F32),
        scratch_shapes=[pltpu.VMEM((rows, LANES), MXU_DTYPE),
                        pltpu.VMEM((irows, LANES), MXU_DTYPE),
                        pltpu.VMEM((nq, TQ, TQ), jnp.int32),
                        pltpu.VMEM((nq // 2, TQ, TQ), jnp.int32),
                        pltpu.VMEM((nq // 2, TQ, TQ), jnp.int32),
                        pltpu.VMEM((1, rows), F32),
                        pltpu.VMEM((LANES, rows), F32),
                        pltpu.VMEM((TQ, TQ), F32),
                        pltpu.VMEM((1, nq, LANES, TQ), MXU_DTYPE)],
        compiler_params=_params(2),
        name="dsa",
    )(u, u, u, u, u, u, d_tiles, tri)


def _causal_conv(x, xe_s, w_ref, b_ref):
    t = x.shape[0]
    xe_s[8:8 + t, :] = x
    y = b_ref[...] + w_ref[CONV_WIDTH - 1:CONV_WIDTH, :] * x
    for k in range(1, CONV_WIDTH):
        y = y + w_ref[CONV_WIDTH - 1 - k:CONV_WIDTH - k, :] * xe_s[8 - k:8 - k + t, :]
    xe_s[0:8, :] = x[t - 8:t, :]
    return y


def _rglru_kernel(x_ref, z_ref, cw_ref, cb_ref, wa_ref, ba_ref, wx_ref, bx_ref, lam_ref, o_ref,
                  xe_s, h_s):
    t, c = x_ref.shape

    @pl.when(pl.program_id(1) == 0)
    def _():
        xe_s[0:8, :] = jnp.zeros((8, c), F32)
        h_s[...] = jnp.zeros(h_s.shape, F32)

    xc = _causal_conv(x_ref[...], xe_s, cw_ref, cb_ref)
    xcm = _mx(xc)
    r = jax.nn.sigmoid(_dot(xcm, wa_ref[...]) + ba_ref[...])
    ig = jax.nn.sigmoid(_dot(xcm, wx_ref[...]) + bx_ref[...])
    nl = -lam_ref[...]
    softplus = jnp.maximum(nl, 0.0) + jnp.log1p(jnp.exp(-jnp.abs(nl)))
    log_a = (-LRU_C * r) * softplus
    a = jnp.exp(log_a)
    bb = jnp.sqrt(-jnp.tanh(log_a) * (a * a + 1.0)) * (ig * xc)
    row = lax.broadcasted_iota(jnp.int32, (t, c), 0) & (SUBLANES - 1)
    d = 1
    while d < SUBLANES:
        live = row >= d
        a_sh = jnp.where(live, pltpu.roll(a, d, 0), 1.0)
        b_sh = jnp.where(live, pltpu.roll(bb, d, 0), 0.0)
        bb = a * b_sh + bb
        a = a * a_sh
        d *= 2
    carry = h_s[0:1, :]
    groups = []
    for g in range(t // SUBLANES):
        rows = slice(g * SUBLANES, (g + 1) * SUBLANES)
        groups.append(a[rows] * carry + bb[rows])
        carry = groups[-1][SUBLANES - 1:SUBLANES, :]
    h = jnp.concatenate(groups, axis=0)
    h_s[0:1, :] = carry
    o_ref[...] = h * jax.nn.silu(z_ref[...])


def _rglru(u, conv_w, conv_b, wa_bd, ba, wx_bd, bx, lam, t=1024):
    b, s, _ = u.shape
    c = LRU_WIDTH
    whole = lambda bi, i: (0, 0)
    return pl.pallas_call(
        _rglru_kernel,
        grid=(b, s // t),
        in_specs=[pl.BlockSpec((None, t, c), lambda bi, i: (bi, i, R_CX // c)),
                  pl.BlockSpec((None, t, c), lambda bi, i: (bi, i, R_CZ // c)),
                  pl.BlockSpec((CONV_WIDTH, c), whole), pl.BlockSpec((1, c), whole),
                  pl.BlockSpec((c, c), whole), pl.BlockSpec((1, c), whole),
                  pl.BlockSpec((c, c), whole), pl.BlockSpec((1, c), whole),
                  pl.BlockSpec((1, c), whole)],
        out_specs=pl.BlockSpec((None, t, c), lambda bi, i: (bi, i, 0)),
        out_shape=jax.ShapeDtypeStruct((b, s, c), F32),
        scratch_shapes=[pltpu.VMEM((t + 8, c), F32), pltpu.VMEM((8, c), F32)],
        compiler_params=_params(2),
        name="rglru",
    )(u, u, conv_w, conv_b.reshape(1, c), wa_bd, ba.reshape(1, c), wx_bd, bx.reshape(1, c),
      lam.reshape(1, c))


def _mlstm_kernel(qk_ref, v_ref, g_ref, og_ref, z_ref, cw_ref, cb_ref, gb_ref, o_ref,
                  xe_s, c_s, m_s):
    t = qk_ref.shape[0]
    L = MLSTM_CHUNK
    H = MLSTM_HEADS
    D = MLSTM_DIM

    @pl.when(pl.program_id(1) == 0)
    def _():
        xe_s[0:8, :] = jnp.zeros((8, xe_s.shape[1]), F32)
        c_s[...] = jnp.zeros(c_s.shape, F32)
        m_s[...] = jnp.zeros(m_s.shape, F32)

    qk = jax.nn.silu(_causal_conv(qk_ref[...], xe_s, cw_ref, cb_ref))

    lane = lax.broadcasted_iota(jnp.int32, (t, LANES), 1)
    row_in = lax.broadcasted_iota(jnp.int32, (t, LANES), 0) & (L - 1)
    gs = g_ref[...] + gb_ref[...]
    log_sig = jnp.minimum(gs, 0.0) - jnp.log1p(jnp.exp(-jnp.abs(gs)))
    pre = jnp.where(lane < H, gs, log_sig)
    cum = pre
    d = 1
    while d < L:
        cum = cum + jnp.where(row_in >= d, pltpu.roll(cum, d, 0), 0.0)
        d *= 2
    comb = jnp.where(lane < H, pre, cum)
    comb_t = comb.T

    causal_t = (lax.broadcasted_iota(jnp.int32, (L, L), 0) <= lax.broadcasted_iota(jnp.int32, (L, L), 1))
    pad_rows = jnp.zeros((7, L), F32)
    v_ts = [v_ref[:, h * D:(h + 1) * D].T for h in range(H)]
    states = [c_s[h] for h in range(H)]
    m_prevs = [m_s[h:h + 1, 0:1] for h in range(H)]
    outs = [[] for _ in range(H)]
    for c in range(t // L):
        r = slice(c * L, (c + 1) * L)
        for h in range(H):
            v_t, state, m_prev = v_ts[h], states[h], m_prevs[h]
            q_ = _mx(qk[r, h * D:(h + 1) * D])
            k_ = _mx(qk[r, (H + h) * D:(H + h + 1) * D] * (D ** -0.5))
            li_row = comb_t[h:h + 1, r]
            b_row = comb_t[H + h:H + h + 1, r]
            lb_col = comb[r, h:h + 1] - comb[r, H + h:H + h + 1]
            b_last = b_row[:, L - 1:L]

            dmat_t = jnp.where(causal_t, b_row + lb_col, -jnp.inf)
            decay_row = b_last - b_row + li_row
            m_new = jnp.maximum(b_last + m_prev, jnp.max(decay_row, axis=-1, keepdims=True))
            inter = b_row + m_prev
            m_t = jnp.maximum(inter, jnp.max(dmat_t, axis=0, keepdims=True))
            w_t = _dot_nt(k_, q_) * jnp.exp(dmat_t - m_t)
            prev = jnp.exp(inter - m_t)

            read = _dot_nt(_mx(state), q_)
            num_t = prev * read[:D] + _dot(_mx(v_t[:, r]), _mx(w_t))
            den = prev * read[D:D + 1] + jnp.sum(w_t, axis=0, keepdims=True)
            outs[h].append(num_t / jnp.maximum(jnp.abs(den), jnp.exp(-m_t)))

            wk = jnp.exp(decay_row - m_new)
            inc = jnp.concatenate([v_t[:, r] * wk, wk, pad_rows], axis=0)
            states[h] = jnp.exp(b_last + m_prev - m_new) * state + _dot(_mx(inc), k_)
            m_prevs[h] = m_new

    for h in range(H):
        c_s[h] = states[h]
        m_s[h:h + 1, :] = jnp.broadcast_to(m_prevs[h], (1, LANES))
        out = jnp.concatenate(outs[h], axis=1).T
        cols = slice(h * D, (h + 1) * D)
        o_ref[:, cols] = jax.nn.sigmoid(og_ref[:, cols]) * out * jax.nn.silu(z_ref[:, cols])


def _mlstm(u, conv_w, conv_b, gate_bias, t=128):
    b, s, _ = u.shape
    w = MLSTM_WIDTH
    whole = lambda bi, i: (0, 0)
    return pl.pallas_call(
        _mlstm_kernel,
        grid=(b, s // t),
        in_specs=[pl.BlockSpec((None, t, 2 * w), lambda bi, i: (bi, i, R_QK // (2 * w))),
                  pl.BlockSpec((None, t, w), lambda bi, i: (bi, i, R_V // w)),
                  pl.BlockSpec((None, t, LANES), lambda bi, i: (bi, i, R_GATE // LANES)),
                  pl.BlockSpec((None, t, w), lambda bi, i: (bi, i, R_O // w)),
                  pl.BlockSpec((None, t, w), lambda bi, i: (bi, i, R_Z // w)),
                  pl.BlockSpec((CONV_WIDTH, 2 * w), whole), pl.BlockSpec((1, 2 * w), whole),
                  pl.BlockSpec((1, LANES), whole)],
        out_specs=pl.BlockSpec((None, t, w), lambda bi, i: (bi, i, 0)),
        out_shape=jax.ShapeDtypeStruct((b, s, w), F32),
        scratch_shapes=[pltpu.VMEM((t + 8, 2 * w), F32),
                        pltpu.VMEM((MLSTM_HEADS, MLSTM_DIM + 8, MLSTM_DIM), F32),
                        pltpu.VMEM((8, LANES), F32)],
        compiler_params=_params(2),
        name="mlstm",
    )(u, u, u, u, u, conv_w, conv_b.reshape(1, 2 * w), gate_bias)


def _attn_weights(w_in):
    d = w_in.shape[0]
    widths = (512, 768, 24, 512, 512, 64, 64, 256, 64, 4, 512)
    offs = np.concatenate([[0], np.cumsum(widths)])
    a_q, a_kv, a_g, a_z, b_q, b_k, b_v, b_qi, b_ki, b_wi, b_z = [
        w_in[:, offs[k]:offs[k + 1]] for k in range(len(widths))]
    scale = HEAD_DIM ** -0.5 * LOG2E
    zeros = lambda n: jnp.zeros((d, n), w_in.dtype)
    cols = [
        a_q * scale, b_q * scale, a_z, b_z, b_qi, a_kv,
        a_g, zeros(LANES - 24),
        b_k, b_v,
        b_ki, zeros(LANES - IDX_DIM),
        b_wi * (IDX_DIM ** -0.5 * IDX_HEADS ** -0.5), zeros(LANES - IDX_HEADS),
    ]
    w = jnp.concatenate(cols, axis=1)
    assert w.shape[1] == A_TOTAL
    return _mx(w)


def _rec_weights(w_in):
    d = w_in.shape[0]
    widths = (512, 512, 512, 512, 512, 4, 4, 512, 512)
    offs = np.concatenate([[0], np.cumsum(widths)])
    c_x, c_z, d_q, d_k, d_v, d_i, d_f, d_o, d_z = [w_in[:, offs[k]:offs[k + 1]] for k in range(len(widths))]
    w = jnp.concatenate([c_x, c_z, d_q, d_k, d_v, d_o, d_z, d_i, d_f,
                         jnp.zeros((d, R_TOTAL - R_GATE - 2 * MLSTM_HEADS), w_in.dtype)], axis=1)
    assert w.shape[1] == R_TOTAL
    return _mx(w)


def _block_diag(w):
    g, n, _ = w.shape
    eye = jnp.eye(g, dtype=w.dtype)
    return (eye[:, None, :, None] * w[:, :, None, :]).reshape(g * n, g * n)


def _attention_layer(x2d, b, s, norm_g, w_in, cmp_pos_k, cmp_w1_k, cmp_w2_k, cmp_pos_v, cmp_w1_v, cmp_w2_v,
                     t5_table):
    nq = s // TQ
    n16 = s // CMP_STRIDE
    nb = s // SEL_BLOCK
    u = _norm_proj(x2d, norm_g, _attn_weights(w_in)).reshape(b, s, A_TOTAL)

    pos = jnp.stack([cmp_pos_k, cmp_pos_v])
    pos = jnp.concatenate([pos] * NSA_GROUPS, axis=-1).reshape(2, CMP_BLOCK, 1, LANES)
    w1 = jnp.stack([cmp_w1_k, cmp_w1_v]).reshape(2, CMP_BLOCK, HEAD_DIM, CMP_HIDDEN)
    z1 = jnp.zeros_like(w1)
    w1 = _mx(jnp.stack([jnp.concatenate([w1, z1], axis=2), jnp.concatenate([z1, w1], axis=2)], axis=1))
    w2 = jnp.stack([cmp_w2_k, cmp_w2_v])
    zpad = jnp.zeros_like(w2)
    w2 = _mx(jnp.stack([jnp.concatenate([w2, zpad], -1), jnp.concatenate([zpad, w2], -1)], axis=1))
    cmp_kv = _compress(u, pos, w1, w2)

    tbl = t5_table.astype(F32)
    tbl_a, tbl_b = tbl[:, :NSA_HEADS], tbl[:, NSA_HEADS:]
    kj = np.arange(TQ)[:, None]
    qi = np.arange(TQ)[None, :]
    assert nb <= LANES
    e_t = jnp.asarray((np.arange(s)[:, None] // SEL_BLOCK) == np.arange(LANES)[None, :], MXU_DTYPE)
    ci = np.arange(n16)[None, :]
    sj = np.arange(nb)[:, None]
    ov = jnp.asarray((ci * CMP_STRIDE < (sj + 1) * SEL_BLOCK) & (ci * CMP_STRIDE + CMP_BLOCK > sj * SEL_BLOCK)
                     & (ci < n16 - 1), MXU_DTYPE)
    tri = jnp.asarray(qi <= kj, MXU_DTYPE)

    ya = _nsa(u, cmp_kv, _cmp_bias(tbl_a, nq, n16), _near_tiles(tbl_a), e_t, ov, min(N_SEL, nb))
    yb = _dsa(u, _near_tiles(tbl_b), tri, min(DSA_TOPK_MAX, s // 4))
    return ya.reshape(b * s, -1), yb.reshape(b * s, -1)


def _recurrent_layer(x2d, b, s, norm_g, w_in, conv_c_w, conv_c_b, wa, ba, wx, bx, lam,
                     conv_d_w, conv_d_b, b_i, b_f):
    u = _norm_proj(x2d, norm_g, _rec_weights(w_in)).reshape(b, s, R_TOTAL)
    yc = _rglru(u, conv_c_w, conv_c_b, _mx(_block_diag(wa)), ba, _mx(_block_diag(wx)), bx, lam)
    gate_bias = jnp.concatenate([b_i, b_f, jnp.zeros((LANES - 2 * MLSTM_HEADS,), F32)]).reshape(1, LANES)
    yd = _mlstm(u, conv_d_w, conv_d_b, gate_bias)
    return yc.reshape(b * s, -1), yd.reshape(b * s, -1)


def kernel(x, p, norm_g, final_g, ple_w, ple_gate_w, t5_table, attn_w_in, attn_w_out, cmp_pos_k, cmp_w1_k, cmp_w2_k, cmp_pos_v, cmp_w1_v, cmp_w2_v, rec_w_in, rec_w_out, lru_conv_w, lru_conv_b, lru_wa, lru_ba, lru_wx, lru_bx, lru_lambda, mlstm_conv_w, mlstm_conv_b, mlstm_b_i, mlstm_b_f):
    b, s, d = x.shape
    depth = p.shape[0]
    x2d = x.reshape(b * s, d)
    for i in range(depth):
        j = i // 2
        if i % 2 == 0:
            ya, yb = _attention_layer(x2d, b, s, norm_g[i], attn_w_in[j], cmp_pos_k[j], cmp_w1_k[j], cmp_w2_k[j],
                                      cmp_pos_v[j], cmp_w1_v[j], cmp_w2_v[j], t5_table)
            w_out = attn_w_out[j]
        else:
            ya, yb = _recurrent_layer(x2d, b, s, norm_g[i], rec_w_in[j], lru_conv_w[j], lru_conv_b[j],
                                      lru_wa[j], lru_ba[j], lru_wx[j], lru_bx[j], lru_lambda[j],
                                      mlstm_conv_w[j], mlstm_conv_b[j], mlstm_b_i[j], mlstm_b_f[j])
            w_out = rec_w_out[j]
        x2d = _out_proj(x2d, ya, yb, p.reshape(depth, b * s, -1), i, w_out, ple_w[i], ple_gate_w[i], final_g,
                        final=(i == depth - 1))
    return x2d.reshape(b, s, d)
```

```python
import functools
import math

import numpy as np
import jax
import jax.numpy as jnp
from jax import lax
from jax.experimental import pallas as pl
from jax.experimental.pallas import tpu as pltpu

F32 = jnp.float32
MXU_DTYPE = jnp.bfloat16

HEAD_DIM = 64
NSA_HEADS = 8
NSA_GROUPS = 2
NSA_HPG = NSA_HEADS // NSA_GROUPS
CMP_BLOCK = 32
CMP_STRIDE = 16
CMP_HIDDEN = 256
SEL_BLOCK = 64
N_SEL = 8
WINDOW = 512
FORCE_BONUS = 1e4
DSA_HEADS = 8
IDX_HEADS = 4
IDX_DIM = 64
DSA_TOPK_MAX = 256
N_BUCKETS = 32
T5_MAX_DIST = 128
LRU_WIDTH = 512
CONV_WIDTH = 4
LRU_C = 8.0
MLSTM_HEADS = 4
MLSTM_DIM = 128
MLSTM_WIDTH = MLSTM_HEADS * MLSTM_DIM
MLSTM_CHUNK = 64
RMS_EPS = 1e-6
NEG = -1e30

LANES = 128
SUBLANES = 8
COL_BLOCK = 512
TQ = 128
INT_MIN = -2 ** 31
LOG2E = math.log2(math.e)
VMEM_LIMIT = 48 * 1024 * 1024

A_NSA_Q = 0
A_DSA_Q = 512
A_AZ = 1024
A_BZ = 1536
A_IDX_Q = 2048
A_KV = 2304
A_GATE = 3072
A_DSA_KV = 3200
A_IDX_K = 3328
A_IDX_W = 3456
A_TOTAL = 3584

R_CX = 0
R_CZ = 512
R_QK = 1024
R_V = 2048
R_O = 2560
R_Z = 3072
R_GATE = 3584
R_TOTAL = 3840


def _dot(a, b):
    return jnp.dot(a, b, preferred_element_type=F32)


def _dot_nt(a, b):
    return lax.dot_general(a, b, (((1,), (1,)), ((), ())), preferred_element_type=F32)


def _mx(a):
    return a.astype(MXU_DTYPE)


def _params(n_grid):
    return pltpu.CompilerParams(dimension_semantics=("arbitrary",) * n_grid,
                                vmem_limit_bytes=VMEM_LIMIT)


def _norm_proj_kernel(x_ref, g_ref, w_ref, o_ref):
    x = x_ref[...]
    ms = jnp.mean(x * x, axis=-1, keepdims=True)
    y = x * lax.rsqrt(ms + RMS_EPS) * g_ref[...]
    o_ref[...] = _dot(_mx(y), w_ref[...])


def _norm_proj(x2d, g, w, tm=512):
    m, d = x2d.shape
    n = w.shape[1]
    return pl.pallas_call(
        _norm_proj_kernel,
        grid=(m // tm,),
        in_specs=[pl.BlockSpec((tm, d), lambda i: (i, 0)),
                  pl.BlockSpec((1, d), lambda i: (0, 0)),
                  pl.BlockSpec((d, n), lambda i: (0, 0))],
        out_specs=pl.BlockSpec((tm, n), lambda i: (i, 0)),
        out_shape=jax.ShapeDtypeStruct((m, n), F32),
        compiler_params=_params(1),
        name="norm_proj",
    )(x2d, g.reshape(1, d), w)


def _out_proj_kernel(x_ref, ya_ref, yb_ref, p_ref, wo_ref, pw_ref, gw_ref, g_ref, *rest, final):
    half = ya_ref.shape[-1]
    y = _dot(_mx(ya_ref[...]), wo_ref[:half, :]) + _dot(_mx(yb_ref[...]), wo_ref[half:, :])
    x1 = x_ref[...] + y
    gate = jax.nn.sigmoid(_dot(_mx(x1), gw_ref[...]))
    x2 = x1 + _dot(_mx(p_ref[...]), pw_ref[...]) * gate
    normed = x2 * lax.rsqrt(jnp.mean(x2 * x2, axis=-1, keepdims=True) + RMS_EPS) * g_ref[...]
    if final:
        (o_ref,) = rest
        o_ref[...] = normed
    else:
        wn_ref, o_ref, u_ref = rest
        o_ref[...] = x2
        u_ref[...] = _dot(_mx(normed), wn_ref[...])


def _out_proj(x2d, ya, yb, p_all, layer, w_out, ple_w, gate_w, gain, next_w=None, tm=512):
    m, d = x2d.shape
    half = ya.shape[1]
    pd = p_all.shape[2]
    final = next_w is None
    row = lambda i: (i, 0)
    whole = lambda i: (0, 0)
    resident = lambda shape: pl.BlockSpec(shape, whole, pipeline_mode=pl.Buffered(1))
    in_specs = [pl.BlockSpec((tm, d), row), pl.BlockSpec((tm, half), row),
                pl.BlockSpec((tm, half), row), pl.BlockSpec((None, tm, pd), lambda i: (layer, i, 0)),
                resident((2 * half, d)), resident((pd, d)), resident((d, d)), resident((1, d))]
    args = [x2d, ya, yb, p_all, _mx(w_out), _mx(ple_w), _mx(gate_w), gain.reshape(1, d)]
    out_specs = pl.BlockSpec((tm, d), row)
    out_shape = jax.ShapeDtypeStruct((m, d), F32)
    if not final:
        n = next_w.shape[1]
        in_specs.append(resident((d, n)))
        args.append(next_w)
        out_specs = (out_specs, pl.BlockSpec((tm, n), row))
        out_shape = (out_shape, jax.ShapeDtypeStruct((m, n), F32))
    return pl.pallas_call(
        functools.partial(_out_proj_kernel, final=final),
        grid=(m // tm,),
        in_specs=in_specs,
        out_specs=out_specs,
        out_shape=out_shape,
        compiler_params=_params(1),
        name="out_proj",
    )(*args)


def _bucket_np(n):
    n = np.asarray(n)
    max_exact = N_BUCKETS // 2
    nf = np.maximum(n, 1).astype(np.float32)
    large = max_exact + (np.log(nf / np.float32(max_exact)) / np.float32(math.log(T5_MAX_DIST / max_exact))
                         * np.float32(N_BUCKETS - max_exact)).astype(np.int32)
    large = np.minimum(large, N_BUCKETS - 1)
    return np.where(n < max_exact, n, large)


def _bias_index(dist):
    dist = np.asarray(dist)
    return np.where(dist >= 0, _bucket_np(np.maximum(dist, 0)), N_BUCKETS).astype(np.int32)


def _lookup(ext, idx):
    a, b = idx.shape
    rows = jnp.arange(ext.shape[0], dtype=jnp.int32)[:, None]
    onehot = (rows == jnp.asarray(idx.reshape(1, -1), jnp.int32)).astype(F32)
    out = jnp.dot(ext.T, onehot, precision=lax.Precision.HIGHEST)
    return out.reshape(ext.shape[1], a, b).transpose(1, 0, 2).reshape(a, -1)


def _near_tiles(tbl):
    h = tbl.shape[1]
    far = tbl[N_BUCKETS - 1]
    ext = jnp.concatenate([tbl, jnp.full((1, h), NEG, F32)], axis=0)
    kj = np.arange(TQ)[:, None]
    qi = np.arange(TQ)[None, :]
    far = jnp.repeat(far, TQ)[None, :]
    d0 = (_lookup(ext, _bias_index(qi - kj)) - far) * LOG2E
    d1 = (_lookup(ext, _bias_index(TQ + qi - kj)) - far) * LOG2E
    edge = jnp.asarray(np.tile(np.where(kj > qi, 0.0, NEG), (1, h)), F32)
    return jnp.stack([jnp.zeros_like(d0), d1, d0, jnp.full_like(d0, NEG), edge])


def _cmp_bias(tbl, nq, n16):
    h = tbl.shape[1]
    ext = jnp.concatenate([tbl, jnp.full((1, h), NEG, F32)], axis=0)
    off = (TQ // CMP_STRIDE) * (nq - 1)
    cc = np.arange(n16 + off)[:, None] - off
    qi = np.arange(TQ)[None, :]
    return _lookup(ext, _bias_index(qi - CMP_STRIDE * cc - (CMP_BLOCK - 1))) * LOG2E


def _compress_kernel(x_ref, pos_ref, w1_ref, w2_ref, o_ref):
    n16 = o_ref.shape[0]
    acc = None
    for g in range(NSA_GROUPS):
        pa = pb = None
        for l in range(CMP_STRIDE):
            x = x_ref[pl.ds(l, n16, stride=CMP_STRIDE), :]
            a = _dot(_mx(x + pos_ref[l]), w1_ref[g, l])
            c = _dot(_mx(x + pos_ref[CMP_STRIDE + l]), w1_ref[g, CMP_STRIDE + l])
            pa = a if pa is None else pa + a
            pb = c if pb is None else pb + c
        pre = pa + pltpu.roll(pb, n16 - 1, 0)
        t = _dot(_mx(jax.nn.silu(pre)), w2_ref[g])
        acc = t if acc is None else acc + t
    o_ref[...] = acc


def _compress(u, pos, w1, w2):
    b, s, _ = u.shape
    n16 = s // CMP_STRIDE
    return pl.pallas_call(
        _compress_kernel,
        grid=(2, b),
        in_specs=[pl.BlockSpec((None, s, LANES), lambda k, i: (i, 0, A_KV // LANES + k)),
                  pl.BlockSpec((None, CMP_BLOCK, 1, LANES), lambda k, i: (k, 0, 0, 0)),
                  pl.BlockSpec((None, NSA_GROUPS, CMP_BLOCK, LANES, CMP_HIDDEN), lambda k, i: (k, 0, 0, 0, 0)),
                  pl.BlockSpec((None, NSA_GROUPS, CMP_HIDDEN, LANES), lambda k, i: (k, 0, 0, 0))],
        out_specs=pl.BlockSpec((None, None, n16, LANES), lambda k, i: (k, i, 0, 0)),
        out_shape=jax.ShapeDtypeStruct((2, b, n16, LANES), F32),
        compiler_params=_params(2),
        name="compress",
    )(u, pos, w1, w2)


def _softmax_reset(m_s, acc_s):
    m_s[...] = jnp.full(m_s.shape, -jnp.inf, F32)
    acc_s[...] = jnp.zeros(acc_s.shape, F32)


def _softmax_step(score_fn, values, m_s, acc_s, col_block):
    r = m_s.shape[1]
    assert (r // len(values)) % col_block == 0, "a column block must not straddle two value operands"
    s_next = score_fn(0, col_block)
    for c0 in range(0, r, col_block):
        cols = slice(c0, c0 + col_block)
        s = s_next
        if c0 + col_block < r:
            s_next = score_fn(c0 + col_block, c0 + 2 * col_block)
        m_prev = m_s[:, cols]
        m_new = jnp.maximum(m_prev, jnp.max(s, axis=0, keepdims=True))
        alpha = jnp.exp2(m_prev - m_new)
        p = _mx(jnp.exp2(s - m_new))
        v = values[c0 * len(values) // r]
        acc_s[:, cols] = alpha * acc_s[:, cols] + _dot(v, p)
        m_s[:, cols] = m_new


def _store_value_tiles(v_ref, vt_s, ones_hi):
    row = lax.broadcasted_iota(jnp.int32, (LANES, TQ), 0)

    def body(j, carry):
        vt = v_ref[pl.ds(pl.multiple_of(j * TQ, TQ), TQ), :].T
        for k, hi in enumerate(ones_hi):
            vt_s[k, j] = _mx(jnp.where((row >= HEAD_DIM) == hi, 1.0, vt))
        return carry

    lax.fori_loop(0, vt_s.shape[1], body, 0)


def _value_tiles(vt_s, k, j0, ntile):
    tiles = [vt_s[k, j0 + t] for t in range(ntile)]
    return tiles[0] if ntile == 1 else jnp.concatenate(tiles, axis=1)


def _near_bias(d_ref, idx_of_tile, j0, ntile, c0, c1):
    tiles = [d_ref[idx_of_tile(j0 + t), :, c0:c1] for t in range(ntile)]
    return tiles[0] if ntile == 1 else jnp.concatenate(tiles, axis=0)


def _stack_heads(q_ref, hi):
    lane = lax.broadcasted_iota(jnp.int32, (TQ, LANES), 1)
    out = []
    for h, up in enumerate(hi):
        v = q_ref[:, (h // 2) * LANES:(h // 2 + 1) * LANES]
        if (h % 2 == 1) != up:
            v = pltpu.roll(v, HEAD_DIM, 1)
        out.append(jnp.where((lane >= HEAD_DIM) == up, v, 0.0))
    return jnp.concatenate(out, axis=0)


def _last_chunk(chunk_fn, i, last, ch):
    if ch < 2:
        chunk_fn(last, True)
        return
    short = (i - last * ch) < ch // 2

    @pl.when(short)
    def _():
        chunk_fn(last, True, ch // 2)

    @pl.when(jnp.logical_not(short))
    def _():
        chunk_fn(last, True)


def _tiles_per_chunk(nq):
    for ch in (4, 2):
        if nq % ch == 0:
            return ch
    return 1


def _nsa_kernel(q_ref, kc_ref, vc_ref, ks_ref, vs_ref, kw_ref, vw_ref, g_ref, z_ref,
                bc_ref, d_ref, et_ref, ov_ref, o_ref,
                qa_s, m_s, acc_s, osel_s, vct_s, vst_s, vwt_s, *, n_pick, ch, n_win_tiles):
    i = pl.program_id(1)
    rows = NSA_HEADS * TQ
    nb = ov_ref.shape[0]
    q0 = i * TQ
    half = LANES // NSA_GROUPS

    @pl.when(i == 0)
    def _():
        vct_s[...] = _mx(vc_ref[...].T)
        _store_value_tiles(vs_ref, vst_s, (True, False))
        _store_value_tiles(vw_ref, vwt_s, (True, False))

    qs = _mx(_stack_heads(q_ref, [h >= NSA_HPG for h in range(NSA_HEADS)]))
    qa_s[:, :LANES] = qs

    s = _dot_nt(_mx(kc_ref[...]), qs) + bc_ref[...]
    p = jnp.exp2(s - jnp.max(s, axis=0, keepdims=True))
    t_lane = q0 + (lax.broadcasted_iota(jnp.int32, (1, rows), 1) & (TQ - 1))
    any_valid = jnp.where(t_lane >= CMP_BLOCK - 1, 1.0, 0.0)
    p = p * (any_valid / jnp.sum(p, axis=0, keepdims=True))
    oc_t = _dot(vct_s[...], _mx(p))

    blk = lax.broadcasted_iota(jnp.int32, (nb, TQ), 0)
    t = q0 + lax.broadcasted_iota(jnp.int32, (nb, TQ), 1)
    cur = lax.shift_right_logical(t, int(math.log2(SEL_BLOCK)))
    forced = jnp.where(blk == 0, 1.0, jnp.where(blk == cur, 1.0, jnp.where(blk == cur - 1, 1.0, 0.0)))
    admissible = blk * SEL_BLOCK <= t
    ov = ov_ref[...]
    for g in range(NSA_GROUPS):
        c0 = g * NSA_HPG * TQ
        ps = p[:, c0:c0 + TQ]
        for h in range(1, NSA_HPG):
            ps = ps + p[:, c0 + h * TQ:c0 + (h + 1) * TQ]
        hi = _mx(ps)
        lo = _mx(ps - hi.astype(F32))
        imp_t = _dot(ov, hi) + _dot(ov, lo)
        score = jnp.where(admissible, imp_t + FORCE_BONUS * forced, NEG)
        sel = jnp.zeros((nb, TQ), F32)
        for _ in range(n_pick):
            best = jnp.max(score, axis=0, keepdims=True)
            first = jnp.min(jnp.where(score == best, blk, nb), axis=0, keepdims=True)
            hit = blk == first
            sel = jnp.where(hit, 1.0, sel)
            score = jnp.where(hit, -jnp.inf, score)
        if nb < LANES:
            sel = jnp.concatenate([sel, jnp.ones((LANES - nb, TQ), F32)], axis=0)
        block_mask = _mx((sel.T - 1.0) * (-NEG))
        for h in range(NSA_HPG):
            r0 = (g * NSA_HPG + h) * TQ
            qa_s[r0:r0 + TQ, LANES:] = block_mask

    def values(vt_s, j0, ntile):
        return [_value_tiles(vt_s, 0, j0, ntile), _value_tiles(vt_s, 1, j0, ntile)]

    _softmax_reset(m_s, acc_s)

    def sel_chunk(c, near, ntile=ch):
        tk = ntile * TQ
        koff = pl.multiple_of(c * (ch * TQ), ch * TQ)
        k_aug = jnp.concatenate([_mx(ks_ref[pl.ds(koff, tk), :]), et_ref[pl.ds(koff, tk), :]], axis=1)

        def scores(c0, c1):
            s = _dot_nt(k_aug, qa_s[c0:c1, :])
            if near:
                s = s + _near_bias(d_ref, lambda j: jnp.clip(j - (i - 2), 0, 3), c * ch, ntile, c0, c1)
            return s

        _softmax_step(scores, values(vst_s, c * ch, ntile), m_s, acc_s, COL_BLOCK)

    first_near = jnp.maximum(i - 1, 0) // ch
    last = i // ch

    def far_body(c, carry):
        sel_chunk(c, False)
        return carry

    def near_body(c, carry):
        sel_chunk(c, True)
        return carry

    lax.fori_loop(0, first_near, far_body, 0)
    lax.fori_loop(first_near, last, near_body, 0)
    _last_chunk(sel_chunk, i, last, ch)
    osel_s[...] = acc_s[...]

    _softmax_reset(m_s, acc_s)
    n_win = WINDOW // TQ
    jw = jnp.maximum(i - (n_win_tiles - 1), 0)
    koff = pl.multiple_of(jw * TQ, TQ)
    tk = n_win_tiles * TQ

    def win_idx(j):
        r = j - (i - n_win)
        return jnp.where(r == 0, 4, jnp.clip(r - (n_win - 2), 0, 3))

    kw = _mx(kw_ref[pl.ds(koff, tk), :])
    _softmax_step(lambda c0, c1: (_dot_nt(kw, qa_s[c0:c1, :LANES])
                                  + _near_bias(d_ref, win_idx, jw, n_win_tiles, c0, c1)),
                  values(vwt_s, jw, n_win_tiles), m_s, acc_s, COL_BLOCK)
    ow_t = acc_s[...]
    os_t = osel_s[...]

    gate_t = jax.nn.sigmoid(g_ref[...]).T
    for m in range(NSA_HEADS // 2):
        parts = []
        for hh in (2 * m, 2 * m + 1):
            g = hh // NSA_HPG
            r = slice(g * half, (g + 1) * half)
            d = (1 - g) * half
            c = slice(hh * TQ, (hh + 1) * TQ)
            parts.append(gate_t[3 * hh:3 * hh + 1, :] * oc_t[r, c]
                         + gate_t[3 * hh + 1:3 * hh + 2, :] * (os_t[r, c] / os_t[d:d + 1, c])
                         + gate_t[3 * hh + 2:3 * hh + 3, :] * (ow_t[r, c] / ow_t[d:d + 1, c]))
        slab = jnp.concatenate(parts, axis=0).T
        o_ref[:, m * LANES:(m + 1) * LANES] = slab * jax.nn.silu(z_ref[:, m * LANES:(m + 1) * LANES])


def _nsa(u, cmp_kv, bias_c, d_tiles, e_t, ov, n_pick):
    b, s, _ = u.shape
    nq = s // TQ
    n16 = cmp_kv.shape[2]
    nb = ov.shape[0]
    rows = NSA_HEADS * TQ
    n_win_tiles = min(WINDOW // TQ + 1, nq)
    seq = lambda col: pl.BlockSpec((None, s, LANES), lambda bi, i, col=col: (bi, 0, col // LANES))
    return pl.pallas_call(
        functools.partial(_nsa_kernel, n_pick=n_pick, ch=_tiles_per_chunk(nq), n_win_tiles=n_win_tiles),
        grid=(b, nq),
        in_specs=[
            pl.BlockSpec((None, TQ, NSA_HEADS * HEAD_DIM), lambda bi, i: (bi, i, A_NSA_Q // (NSA_HEADS * HEAD_DIM))),
            pl.BlockSpec((None, None, n16, LANES), lambda bi, i: (0, bi, 0, 0)),
            pl.BlockSpec((None, None, n16, LANES), lambda bi, i: (1, bi, 0, 0)),
            seq(A_KV + 2 * LANES), seq(A_KV + 3 * LANES), seq(A_KV + 4 * LANES), seq(A_KV + 5 * LANES),
            pl.BlockSpec((None, TQ, LANES), lambda bi, i: (bi, i, A_GATE // LANES)),
            pl.BlockSpec((None, TQ, 512), lambda bi, i: (bi, i, A_AZ // 512)),
            pl.BlockSpec((pl.Element(n16), pl.Element(rows)),
                         lambda bi, i: ((TQ // CMP_STRIDE) * (nq - 1 - i), 0)),
            pl.BlockSpec(d_tiles.shape, lambda bi, i: (0, 0, 0)),
            pl.BlockSpec((s, LANES), lambda bi, i: (0, 0)),
            pl.BlockSpec((nb, n16), lambda bi, i: (0, 0)),
        ],
        out_specs=pl.BlockSpec((None, TQ, 512), lambda bi, i: (bi, i, 0)),
        out_shape=jax.ShapeDtypeStruct((b, s, 512), F32),
        scratch_shapes=[pltpu.VMEM((rows, 2 * LANES), MXU_DTYPE),
                        pltpu.VMEM((1, rows), F32),
                        pltpu.VMEM((LANES, rows), F32),
                        pltpu.VMEM((LANES, rows), F32),
                        pltpu.VMEM((LANES, n16), MXU_DTYPE),
                        pltpu.VMEM((NSA_GROUPS, nq, LANES, TQ), MXU_DTYPE),
                        pltpu.VMEM((NSA_GROUPS, nq, LANES, TQ), MXU_DTYPE)],
        compiler_params=_params(2),
        name="nsa",
    )(u, cmp_kv, cmp_kv, u, u, u, u, u, u, bias_c, d_tiles, e_t, ov)


def _dsa_kernel(q_ref, kv_ref, qi_ref, ki_ref, wi_ref, z_ref, d_ref, tri_ref, o_ref,
                qs_s, qis_s, sc_s, hi_s, lo_s, m_s, acc_s, seen_s, vt_s, *, k_top, ch):
    i = pl.program_id(1)

    @pl.when(i == 0)
    def _():
        _store_value_tiles(kv_ref, vt_s, (False,))

    qs_s[...] = _mx(_stack_heads(q_ref, [False] * DSA_HEADS))
    qis_s[...] = _mx(_stack_heads(qi_ref, [False] * IDX_HEADS))
    wi_t = wi_ref[...].T
    w_rows = [jnp.broadcast_to(wi_t[h:h + 1, :], (TQ, TQ)) for h in range(IDX_HEADS)]
    key_idx = lax.broadcasted_iota(jnp.int32, (TQ, TQ), 0)
    q_idx = lax.broadcasted_iota(jnp.int32, (TQ, TQ), 1)
    n_chunks = (i + ch) // ch
    pairs = ch // 2

    def score_chunk(c, last):
        koff = pl.multiple_of(c * (ch * TQ), ch * TQ)
        r = jnp.maximum(_dot_nt(_mx(ki_ref[pl.ds(koff, ch * TQ), :]), qis_s[...]), 0.0)
        keys = []
        for t in range(ch):
            jt = c * ch + t
            rt = r[t * TQ:(t + 1) * TQ]
            sc = w_rows[0] * rt[:, 0:TQ]
            for h in range(1, IDX_HEADS):
                sc = sc + w_rows[h] * rt[:, h * TQ:(h + 1) * TQ]
            if last:
                sc = jnp.where((jt == i) & (key_idx > q_idx), NEG, sc)
            bits = pltpu.bitcast(sc, jnp.int32)
            key = bits ^ (lax.shift_right_arithmetic(bits, 31) & jnp.int32(0x7FFFFFFF))
            key = jnp.where(sc == 0.0, 0, key)
            if last:
                key = jnp.where(jt > i, INT_MIN, key)
            keys.append(key)
            sc_s[jt] = key
        for pr in range(pairs):
            a, b = keys[2 * pr], keys[2 * pr + 1]
            hi_s[c * pairs + pr] = (a & jnp.int32(-65536)) | lax.shift_right_logical(b, 16)
            lo_s[c * pairs + pr] = (lax.shift_left((a & 0xFFFF) ^ 0x8000, 16)
                                    | ((b & 0xFFFF) ^ 0x8000))

    def earlier_chunk(c, carry):
        score_chunk(c, False)
        return carry

    lax.fori_loop(0, n_chunks - 1, earlier_chunk, 0)
    score_chunk(n_chunks - 1, True)

    def halves(words):
        return pltpu.bitcast(words, jnp.int16)

    def both_halves(v):
        w = lax.shift_left(v, 16) | (v & 0xFFFF)
        return halves(jnp.broadcast_to(w, (TQ, TQ)))

    def count16(words_s, pred):
        def body(c, acc):
            for pr in range(pairs):
                acc = acc + jnp.where(pred(halves(words_s[c * pairs + pr])), jnp.int16(1), jnp.int16(0))
            return acc
        acc = lax.fori_loop(0, n_chunks, body, jnp.zeros((2 * TQ, TQ), jnp.int16))
        rows = 2 * TQ
        while rows > 2 * SUBLANES:
            rows //= 2
            acc = acc[:rows] + acc[rows:]
        return jnp.sum(acc.astype(jnp.int32), axis=0, keepdims=True)

    def bisect16(words_s, target):
        c0 = count16(words_s, lambda x: x >= jnp.int16(0))
        ok0 = c0 >= target

        def step(it, carry):
            v, n_gt = carry
            cand = v | lax.shift_left(jnp.int32(1), 14 - it)
            cand16 = both_halves(cand)
            c = count16(words_s, lambda x: x >= cand16)
            ok = c >= target
            return jnp.where(ok, cand, v), jnp.where(ok, n_gt, c)

        return lax.fori_loop(0, 15, step, (jnp.where(ok0, 0, -32768), jnp.where(ok0, 0, c0)))

    top, above = bisect16(hi_s, k_top)
    top16 = both_halves(top)

    def keep_ties(c, carry):
        for pr in range(pairs):
            p_ = c * pairs + pr
            tied = halves(hi_s[p_]) == top16
            lo_s[p_] = pltpu.bitcast(jnp.where(tied, halves(lo_s[p_]), jnp.int16(-32768)), jnp.int32)
        return carry

    lax.fori_loop(0, n_chunks, keep_ties, 0)
    low, above_low = bisect16(lo_s, k_top - above)
    thr = lax.shift_left(top, 16) | ((low ^ 0x8000) & 0xFFFF)
    thr = jnp.broadcast_to(thr, (TQ, TQ))
    need = (k_top - above - above_low).astype(F32)

    _softmax_reset(m_s, acc_s)
    seen_s[...] = jnp.zeros((TQ, TQ), F32)

    def att_chunk(c, near, ntile=ch):
        tk = ntile * TQ
        koff = pl.multiple_of(c * (ch * TQ), ch * TQ)
        masks = []
        for t in range(ntile):
            key = sc_s[c * ch + t]
            eq = key == thr
            prefix = _dot(tri_ref[...], _mx(jnp.where(eq, 1.0, 0.0)))
            seen = seen_s[...]
            tie = jnp.where(seen + prefix <= need, 0.0, NEG)
            seen_s[...] = seen + prefix[TQ - 1:TQ, :]
            masks.append(jnp.where(key > thr, 0.0, jnp.where(eq, tie, NEG)))
        mb = masks[0] if ntile == 1 else jnp.concatenate(masks, axis=0)
        mb = jnp.concatenate([mb] * DSA_HEADS, axis=1)
        kvm = _mx(kv_ref[pl.ds(koff, tk), :])

        def scores(c0, c1):
            s = _dot_nt(kvm, qs_s[c0:c1, :]) + mb[:, c0:c1]
            if near:
                s = s + _near_bias(d_ref, lambda j: jnp.clip(j - (i - 2), 0, 3), c * ch, ntile, c0, c1)
            return s

        _softmax_step(scores, [_value_tiles(vt_s, 0, c * ch, ntile)], m_s, acc_s, DSA_HEADS * TQ)

    first_near = jnp.maximum(i - 1, 0) // ch

    def far_body(c, carry):
        att_chunk(c, False)
        return carry

    def near_body(c, carry):
        att_chunk(c, True)
        return carry

    lax.fori_loop(0, first_near, far_body, 0)
    lax.fori_loop(first_near, n_chunks - 1, near_body, 0)
    _last_chunk(att_chunk, i, n_chunks - 1, ch)

    o_t = acc_s[...]
    o_t = o_t / o_t[0:1, :]
    for m in range(DSA_HEADS // 2):
        pair = [o_t[HEAD_DIM:, hh * TQ:(hh + 1) * TQ] for hh in (2 * m, 2 * m + 1)]
        slab = jnp.concatenate(pair, axis=0).T
        o_ref[:, m * LANES:(m + 1) * LANES] = slab * jax.nn.silu(z_ref[:, m * LANES:(m + 1) * LANES])


def _dsa(u, d_tiles, tri, k_top):
    b, s, _ = u.shape
    nq = s // TQ
    assert nq % 2 == 0, "the packed threshold search pairs key tiles"
    rows = DSA_HEADS * TQ
    irows = IDX_HEADS * TQ
    seq = lambda col: pl.BlockSpec((None, s, LANES), lambda bi, i, col=col: (bi, 0, col // LANES))
    return pl.pallas_call(
        functools.partial(_dsa_kernel, k_top=k_top, ch=_tiles_per_chunk(nq)),
        grid=(b, nq),
        in_specs=[
            pl.BlockSpec((None, TQ, DSA_HEADS * HEAD_DIM), lambda bi, i: (bi, i, A_DSA_Q // (DSA_HEADS * HEAD_DIM))),
            seq(A_DSA_KV),
            pl.BlockSpec((None, TQ, IDX_HEADS * IDX_DIM), lambda bi, i: (bi, i, A_IDX_Q // (IDX_HEADS * IDX_DIM))),
            seq(A_IDX_K),
            pl.BlockSpec((None, TQ, LANES), lambda bi, i: (bi, i, A_IDX_W // LANES)),
            pl.BlockSpec((None, TQ, 512), lambda bi, i: (bi, i, A_BZ // 512)),
            pl.BlockSpec(d_tiles.shape, lambda bi, i: (0, 0, 0)),
            pl.BlockSpec((TQ, TQ), lambda bi, i: (0, 0)),
        ],
        out_specs=pl.BlockSpec((None, TQ, 512), lambda bi, i: (bi, i, 0)),
        out_shape=jax.ShapeDtypeStruct((b, s, 512), F32),
        scratch_shapes=[pltpu.VMEM((rows, LANES), MXU_DTYPE),
                        pltpu.VMEM((irows, LANES), MXU_DTYPE),
                        pltpu.VMEM((nq, TQ, TQ), jnp.int32),
                        pltpu.VMEM((nq // 2, TQ, TQ), jnp.int32),
                        pltpu.VMEM((nq // 2, TQ, TQ), jnp.int32),
                        pltpu.VMEM((1, rows), F32),
                        pltpu.VMEM((LANES, rows), F32),
                        pltpu.VMEM((TQ, TQ), F32),
                        pltpu.VMEM((1, nq, LANES, TQ), MXU_DTYPE)],
        compiler_params=_params(2),
        name="dsa",
    )(u, u, u, u, u, u, d_tiles, tri)


def _causal_conv(x, xe_s, w_ref, b_ref):
    t = x.shape[0]
    xe_s[8:8 + t, :] = x
    y = b_ref[...] + w_ref[CONV_WIDTH - 1:CONV_WIDTH, :] * x
    for k in range(1, CONV_WIDTH):
        y = y + w_ref[CONV_WIDTH - 1 - k:CONV_WIDTH - k, :] * xe_s[8 - k:8 - k + t, :]
    xe_s[0:8, :] = x[t - 8:t, :]
    return y


def _rglru_kernel(x_ref, z_ref, cw_ref, cb_ref, wa_ref, ba_ref, wx_ref, bx_ref, lam_ref, o_ref,
                  xe_s, h_s):
    t, c = x_ref.shape

    @pl.when(pl.program_id(1) == 0)
    def _():
        xe_s[0:8, :] = jnp.zeros((8, c), F32)
        h_s[...] = jnp.zeros(h_s.shape, F32)

    xc = _causal_conv(x_ref[...], xe_s, cw_ref, cb_ref)
    xcm = _mx(xc)
    r = jax.nn.sigmoid(_dot(xcm, wa_ref[...]) + ba_ref[...])
    ig = jax.nn.sigmoid(_dot(xcm, wx_ref[...]) + bx_ref[...])
    nl = -lam_ref[...]
    softplus = jnp.maximum(nl, 0.0) + jnp.log1p(jnp.exp(-jnp.abs(nl)))
    log_a = (-LRU_C * r) * softplus
    a = jnp.exp(log_a)
    bb = jnp.sqrt(-jnp.tanh(log_a) * (a * a + 1.0)) * (ig * xc)
    row = lax.broadcasted_iota(jnp.int32, (t, c), 0) & (SUBLANES - 1)
    d = 1
    while d < SUBLANES:
        live = row >= d
        a_sh = jnp.where(live, pltpu.roll(a, d, 0), 1.0)
        b_sh = jnp.where(live, pltpu.roll(bb, d, 0), 0.0)
        bb = a * b_sh + bb
        a = a * a_sh
        d *= 2
    carry = h_s[0:1, :]
    groups = []
    for g in range(t // SUBLANES):
        rows = slice(g * SUBLANES, (g + 1) * SUBLANES)
        groups.append(a[rows] * carry + bb[rows])
        carry = groups[-1][SUBLANES - 1:SUBLANES, :]
    h = jnp.concatenate(groups, axis=0)
    h_s[0:1, :] = carry
    o_ref[...] = h * jax.nn.silu(z_ref[...])


def _rglru(u, conv_w, conv_b, wa_bd, ba, wx_bd, bx, lam, t=1024):
    b, s, _ = u.shape
    c = LRU_WIDTH
    whole = lambda bi, i: (0, 0)
    return pl.pallas_call(
        _rglru_kernel,
        grid=(b, s // t),
        in_specs=[pl.BlockSpec((None, t, c), lambda bi, i: (bi, i, R_CX // c)),
                  pl.BlockSpec((None, t, c), lambda bi, i: (bi, i, R_CZ // c)),
                  pl.BlockSpec((CONV_WIDTH, c), whole), pl.BlockSpec((1, c), whole),
                  pl.BlockSpec((c, c), whole), pl.BlockSpec((1, c), whole),
                  pl.BlockSpec((c, c), whole), pl.BlockSpec((1, c), whole),
                  pl.BlockSpec((1, c), whole)],
        out_specs=pl.BlockSpec((None, t, c), lambda bi, i: (bi, i, 0)),
        out_shape=jax.ShapeDtypeStruct((b, s, c), F32),
        scratch_shapes=[pltpu.VMEM((t + 8, c), F32), pltpu.VMEM((8, c), F32)],
        compiler_params=_params(2),
        name="rglru",
    )(u, u, conv_w, conv_b.reshape(1, c), wa_bd, ba.reshape(1, c), wx_bd, bx.reshape(1, c),
      lam.reshape(1, c))


def _mlstm_kernel(qk_ref, v_ref, g_ref, og_ref, z_ref, cw_ref, cb_ref, gb_ref, o_ref,
                  xe_s, c_s, m_s):
    t = qk_ref.shape[0]
    L = MLSTM_CHUNK
    H = MLSTM_HEADS
    D = MLSTM_DIM

    @pl.when(pl.program_id(1) == 0)
    def _():
        xe_s[0:8, :] = jnp.zeros((8, xe_s.shape[1]), F32)
        c_s[...] = jnp.zeros(c_s.shape, F32)
        m_s[...] = jnp.zeros(m_s.shape, F32)

    qk = jax.nn.silu(_causal_conv(qk_ref[...], xe_s, cw_ref, cb_ref))

    lane = lax.broadcasted_iota(jnp.int32, (t, LANES), 1)
    row_in = lax.broadcasted_iota(jnp.int32, (t, LANES), 0) & (L - 1)
    gs = g_ref[...] + gb_ref[...]
    log_sig = jnp.minimum(gs, 0.0) - jnp.log1p(jnp.exp(-jnp.abs(gs)))
    pre = jnp.where(lane < H, gs, log_sig)
    cum = pre
    d = 1
    while d < L:
        cum = cum + jnp.where(row_in >= d, pltpu.roll(cum, d, 0), 0.0)
        d *= 2
    comb = jnp.where(lane < H, pre, cum)
    comb_t = comb.T

    causal_t = (lax.broadcasted_iota(jnp.int32, (L, L), 0) <= lax.broadcasted_iota(jnp.int32, (L, L), 1))
    pad_rows = jnp.zeros((7, L), F32)
    v_ts = [v_ref[:, h * D:(h + 1) * D].T for h in range(H)]
    states = [c_s[h] for h in range(H)]
    m_prevs = [m_s[h:h + 1, 0:1] for h in range(H)]
    outs = [[] for _ in range(H)]
    for c in range(t // L):
        r = slice(c * L, (c + 1) * L)
        for h in range(H):
            v_t, state, m_prev = v_ts[h], states[h], m_prevs[h]
            q_ = _mx(qk[r, h * D:(h + 1) * D])
            k_ = _mx(qk[r, (H + h) * D:(H + h + 1) * D] * (D ** -0.5))
            li_row = comb_t[h:h + 1, r]
            b_row = comb_t[H + h:H + h + 1, r]
            lb_col = comb[r, h:h + 1] - comb[r, H + h:H + h + 1]
            b_last = b_row[:, L - 1:L]

            dmat_t = jnp.where(causal_t, b_row + lb_col, -jnp.inf)
            decay_row = b_last - b_row + li_row
            m_new = jnp.maximum(b_last + m_prev, jnp.max(decay_row, axis=-1, keepdims=True))
            inter = b_row + m_prev
            m_t = jnp.maximum(inter, jnp.max(dmat_t, axis=0, keepdims=True))
            w_t = _dot_nt(k_, q_) * jnp.exp(dmat_t - m_t)
            prev = jnp.exp(inter - m_t)

            read = _dot_nt(_mx(state), q_)
            num_t = prev * read[:D] + _dot(_mx(v_t[:, r]), _mx(w_t))
            den = prev * read[D:D + 1] + jnp.sum(w_t, axis=0, keepdims=True)
            outs[h].append(num_t / jnp.maximum(jnp.abs(den), jnp.exp(-m_t)))

            wk = jnp.exp(decay_row - m_new)
            inc = jnp.concatenate([v_t[:, r] * wk, wk, pad_rows], axis=0)
            states[h] = jnp.exp(b_last + m_prev - m_new) * state + _dot(_mx(inc), k_)
            m_prevs[h] = m_new

    for h in range(H):
        c_s[h] = states[h]
        m_s[h:h + 1, :] = jnp.broadcast_to(m_prevs[h], (1, LANES))
        out = jnp.concatenate(outs[h], axis=1).T
        cols = slice(h * D, (h + 1) * D)
        o_ref[:, cols] = jax.nn.sigmoid(og_ref[:, cols]) * out * jax.nn.silu(z_ref[:, cols])


def _mlstm(u, conv_w, conv_b, gate_bias, t=128):
    b, s, _ = u.shape
    w = MLSTM_WIDTH
    whole = lambda bi, i: (0, 0)
    return pl.pallas_call(
        _mlstm_kernel,
        grid=(b, s // t),
        in_specs=[pl.BlockSpec((None, t, 2 * w), lambda bi, i: (bi, i, R_QK // (2 * w))),
                  pl.BlockSpec((None, t, w), lambda bi, i: (bi, i, R_V // w)),
                  pl.BlockSpec((None, t, LANES), lambda bi, i: (bi, i, R_GATE // LANES)),
                  pl.BlockSpec((None, t, w), lambda bi, i: (bi, i, R_O // w)),
                  pl.BlockSpec((None, t, w), lambda bi, i: (bi, i, R_Z // w)),
                  pl.BlockSpec((CONV_WIDTH, 2 * w), whole), pl.BlockSpec((1, 2 * w), whole),
                  pl.BlockSpec((1, LANES), whole)],
        out_specs=pl.BlockSpec((None, t, w), lambda bi, i: (bi, i, 0)),
        out_shape=jax.ShapeDtypeStruct((b, s, w), F32),
        scratch_shapes=[pltpu.VMEM((t + 8, 2 * w), F32),
                        pltpu.VMEM((MLSTM_HEADS, MLSTM_DIM + 8, MLSTM_DIM), F32),
                        pltpu.VMEM((8, LANES), F32)],
        compiler_params=_params(2),
        name="mlstm",
    )(u, u, u, u, u, conv_w, conv_b.reshape(1, 2 * w), gate_bias)


def _attn_weights(w_in):
    d = w_in.shape[0]
    widths = (512, 768, 24, 512, 512, 64, 64, 256, 64, 4, 512)
    offs = np.concatenate([[0], np.cumsum(widths)])
    a_q, a_kv, a_g, a_z, b_q, b_k, b_v, b_qi, b_ki, b_wi, b_z = [
        w_in[:, offs[k]:offs[k + 1]] for k in range(len(widths))]
    scale = HEAD_DIM ** -0.5 * LOG2E
    zeros = lambda n: jnp.zeros((d, n), w_in.dtype)
    cols = [
        a_q * scale, b_q * scale, a_z, b_z, b_qi, a_kv,
        a_g, zeros(LANES - 24),
        b_k, b_v,
        b_ki, zeros(LANES - IDX_DIM),
        b_wi * (IDX_DIM ** -0.5 * IDX_HEADS ** -0.5), zeros(LANES - IDX_HEADS),
    ]
    w = jnp.concatenate(cols, axis=1)
    assert w.shape[1] == A_TOTAL
    return _mx(w)


def _rec_weights(w_in):
    d = w_in.shape[0]
    widths = (512, 512, 512, 512, 512, 4, 4, 512, 512)
    offs = np.concatenate([[0], np.cumsum(widths)])
    c_x, c_z, d_q, d_k, d_v, d_i, d_f, d_o, d_z = [w_in[:, offs[k]:offs[k + 1]] for k in range(len(widths))]
    w = jnp.concatenate([c_x, c_z, d_q, d_k, d_v, d_o, d_z, d_i, d_f,
                         jnp.zeros((d, R_TOTAL - R_GATE - 2 * MLSTM_HEADS), w_in.dtype)], axis=1)
    assert w.shape[1] == R_TOTAL
    return _mx(w)


def _block_diag(w):
    g, n, _ = w.shape
    eye = jnp.eye(g, dtype=w.dtype)
    return (eye[:, None, :, None] * w[:, :, None, :]).reshape(g * n, g * n)


def _attention_layer(x2d, b, s, norm_g, w_in, cmp_pos_k, cmp_w1_k, cmp_w2_k, cmp_pos_v, cmp_w1_v, cmp_w2_v,
                     t5_table, u=None):
    nq = s // TQ
    n16 = s // CMP_STRIDE
    nb = s // SEL_BLOCK
    if u is None:
        u = _norm_proj(x2d, norm_g, _attn_weights(w_in))
    u = u.reshape(b, s, A_TOTAL)

    pos = jnp.stack([cmp_pos_k, cmp_pos_v])
    pos = jnp.concatenate([pos] * NSA_GROUPS, axis=-1).reshape(2, CMP_BLOCK, 1, LANES)
    w1 = jnp.stack([cmp_w1_k, cmp_w1_v]).reshape(2, CMP_BLOCK, HEAD_DIM, CMP_HIDDEN)
    z1 = jnp.zeros_like(w1)
    w1 = _mx(jnp.stack([jnp.concatenate([w1, z1], axis=2), jnp.concatenate([z1, w1], axis=2)], axis=1))
    w2 = jnp.stack([cmp_w2_k, cmp_w2_v])
    zpad = jnp.zeros_like(w2)
    w2 = _mx(jnp.stack([jnp.concatenate([w2, zpad], -1), jnp.concatenate([zpad, w2], -1)], axis=1))
    cmp_kv = _compress(u, pos, w1, w2)

    tbl = t5_table.astype(F32)
    tbl_a, tbl_b = tbl[:, :NSA_HEADS], tbl[:, NSA_HEADS:]
    kj = np.arange(TQ)[:, None]
    qi = np.arange(TQ)[None, :]
    assert nb <= LANES
    e_t = jnp.asarray((np.arange(s)[:, None] // SEL_BLOCK) == np.arange(LANES)[None, :], MXU_DTYPE)
    ci = np.arange(n16)[None, :]
    sj = np.arange(nb)[:, None]
    ov = jnp.asarray((ci * CMP_STRIDE < (sj + 1) * SEL_BLOCK) & (ci * CMP_STRIDE + CMP_BLOCK > sj * SEL_BLOCK)
                     & (ci < n16 - 1), MXU_DTYPE)
    tri = jnp.asarray(qi <= kj, MXU_DTYPE)

    ya = _nsa(u, cmp_kv, _cmp_bias(tbl_a, nq, n16), _near_tiles(tbl_a), e_t, ov, min(N_SEL, nb))
    yb = _dsa(u, _near_tiles(tbl_b), tri, min(DSA_TOPK_MAX, s // 4))
    return ya.reshape(b * s, -1), yb.reshape(b * s, -1)


def _recurrent_layer(x2d, b, s, norm_g, w_in, conv_c_w, conv_c_b, wa, ba, wx, bx, lam,
                     conv_d_w, conv_d_b, b_i, b_f, u=None):
    if u is None:
        u = _norm_proj(x2d, norm_g, _rec_weights(w_in))
    u = u.reshape(b, s, R_TOTAL)
    yc = _rglru(u, conv_c_w, conv_c_b, _mx(_block_diag(wa)), ba, _mx(_block_diag(wx)), bx, lam)
    gate_bias = jnp.concatenate([b_i, b_f, jnp.zeros((LANES - 2 * MLSTM_HEADS,), F32)]).reshape(1, LANES)
    yd = _mlstm(u, conv_d_w, conv_d_b, gate_bias)
    return yc.reshape(b * s, -1), yd.reshape(b * s, -1)


def kernel(x, p, norm_g, final_g, ple_w, ple_gate_w, t5_table, attn_w_in, attn_w_out, cmp_pos_k, cmp_w1_k, cmp_w2_k, cmp_pos_v, cmp_w1_v, cmp_w2_v, rec_w_in, rec_w_out, lru_conv_w, lru_conv_b, lru_wa, lru_ba, lru_wx, lru_bx, lru_lambda, mlstm_conv_w, mlstm_conv_b, mlstm_b_i, mlstm_b_f):
    b, s, d = x.shape
    depth = p.shape[0]
    x2d = x.reshape(b * s, d)
    p_all = p.reshape(depth, b * s, -1)
    u = None
    for i in range(depth):
        j = i // 2
        if i % 2 == 0:
            ya, yb = _attention_layer(x2d, b, s, norm_g[i], attn_w_in[j], cmp_pos_k[j], cmp_w1_k[j], cmp_w2_k[j],
                                      cmp_pos_v[j], cmp_w1_v[j], cmp_w2_v[j], t5_table, u=u)
            w_out = attn_w_out[j]
        else:
            ya, yb = _recurrent_layer(x2d, b, s, norm_g[i], rec_w_in[j], lru_conv_w[j], lru_conv_b[j],
                                      lru_wa[j], lru_ba[j], lru_wx[j], lru_bx[j], lru_lambda[j],
                                      mlstm_conv_w[j], mlstm_conv_b[j], mlstm_b_i[j], mlstm_b_f[j], u=u)
            w_out = rec_w_out[j]
        if i == depth - 1:
            x2d = _out_proj(x2d, ya, yb, p_all, i, w_out, ple_w[i], ple_gate_w[i], final_g)
        else:
            jn = (i + 1) // 2
            next_w = _rec_weights(rec_w_in[jn]) if (i + 1) % 2 else _attn_weights(attn_w_in[jn])
            x2d, u = _out_proj(x2d, ya, yb, p_all, i, w_out, ple_w[i], ple_gate_w[i], norm_g[i + 1], next_w)
    return x2d.reshape(b, s, d)
```
